```python
import math
import jax, jax.numpy as jnp
from jax import lax
import numpy as np

D_MODEL = 1024
BATCH = 8
SEQ = 2048
DEPTH = 2
DEC_BATCH = 32
DEC_SEQ = 1
PAST_LEN = 16384
PAGE_SIZE = 128

N_EVEN = (DEPTH + 1) // 2
N_ODD = DEPTH // 2

A_HEADS = 8
A_HD = 64
A_W = A_HEADS * A_HD
A_BRANCHES = ((128, 1), (512, 4), (2048, 16))
A_WIN = max(w for w, _ in A_BRANCHES)
N_BUCKETS = 32
MAX_DIST = A_WIN
B_HEADS = 4
B_HD = 128
B_W = B_HEADS * B_HD
B_CONV = 4
B_CHUNK = 64
C_GROUPS = 4
C_GD = 128
C_W = C_GROUPS * C_GD
C_CHUNK = 128
D_GROUPS = 32
D_GCH = 16
D_W = D_GROUPS * D_GCH
D_STATE = 64
D_FF = 3584
N_EXPERTS = 8
TOP_K = 2

IN_E = 3 * A_W + 4 * B_W + 2 * B_HEADS
IN_O = 2 * C_W + D_W
EPS = 1e-6
NEG = -1e30

kernel_name = "hybrid_dilated_mlstm_gmlp_s5_decoder_step"

F32 = jnp.float32


def _rmsnorm(x, g):
    x32 = x.astype(F32)
    y = x32 * lax.rsqrt(jnp.mean(x32 * x32, axis=-1, keepdims=True) + EPS)
    return (y * g.astype(F32)).astype(x.dtype)


def _layernorm(x, g, b):
    x32 = x.astype(F32)
    mu = jnp.mean(x32, axis=-1, keepdims=True)
    xc = x32 - mu
    var = jnp.mean(xc * xc, axis=-1, keepdims=True)
    return xc * lax.rsqrt(var + EPS) * g.astype(F32) + b.astype(F32)


def _swiglu(h, w1, w3, w2):
    return (jax.nn.silu(h @ w1) * (h @ w3)) @ w2


def _t5_bucket(dist):
    max_exact = N_BUCKETS // 2
    d = jnp.maximum(dist, 1).astype(F32)
    large = max_exact + (jnp.log(d / max_exact) / math.log(MAX_DIST / max_exact)
                         * (N_BUCKETS - max_exact)).astype(jnp.int32)
    return jnp.where(dist < max_exact, dist, jnp.minimum(large, N_BUCKETS - 1))


def _branch_bias(rel_bias, d, steps):
    dist = jnp.arange(steps + 1, dtype=jnp.int32) * d
    return rel_bias[_t5_bucket(dist)].astype(F32)


def _banded_window_attn(q, k, v, bias_m, steps):
    N, n, H, hd = q.shape
    blk = steps
    nb = -(-n // blk)
    pad = nb * blk - n

    def blocks(a):
        return jnp.pad(a, ((0, 0), (0, pad), (0, 0), (0, 0))).reshape(N, nb, blk, H, hd)

    def with_prev(a):
        prev = jnp.pad(a, ((0, 0), (1, 0), (0, 0), (0, 0), (0, 0)))[:, :-1]
        return jnp.concatenate([prev, a], axis=2)

    qb = blocks(q)
    kw, vw = with_prev(blocks(k)), with_prev(blocks(v))
    rel = jnp.arange(blk)[:, None] + blk - jnp.arange(2 * blk)[None, :]
    key_idx = jnp.arange(nb)[:, None] * blk - blk + jnp.arange(2 * blk)[None, :]
    valid = ((rel >= 0) & (rel <= steps))[None] & (key_idx >= 0)[:, None, :]
    bias = jnp.transpose(bias_m[jnp.clip(rel, 0, steps)], (2, 0, 1))
    s = jnp.einsum('nbqhd,nbkhd->nbhqk', qb, kw) * (hd ** -0.5) + bias[None, None]
    s = jnp.where(valid[None, :, None], s, NEG)
    mx = jnp.max(s, axis=-1, keepdims=True)
    p = jnp.exp(s - mx)
    den = jnp.sum(p, axis=-1)
    o = jnp.einsum('nbhqk,nbkhd->nbqhd', p, vw) / jnp.swapaxes(den, 2, 3)[..., None]
    lse = jnp.swapaxes(mx[..., 0] + jnp.log(den), 2, 3)
    return o.reshape(N, nb * blk, H, hd)[:, :n], lse.reshape(N, nb * blk, H)[:, :n]


def _merge_branches(outs, lses):
    w = jax.nn.softmax(jnp.stack(lses), axis=0)
    return jnp.sum(w[..., None] * jnp.stack(outs), axis=0)


def _dilated_attn_prompt(q, k, v, rel_bias):
    B, S, H, hd = q.shape
    outs, lses = [], []
    for window, d in A_BRANCHES:
        steps = window // d
        n = S // d

        def to_res(a):
            return a.reshape(B, n, d, H, hd).transpose(0, 2, 1, 3, 4).reshape(B * d, n, H, hd)

        o, lse = _banded_window_attn(to_res(q), to_res(k), to_res(v), _branch_bias(rel_bias, d, steps), steps)
        outs.append(o.reshape(B, d, n, H, hd).transpose(0, 2, 1, 3, 4).reshape(B, S, H, hd))
        lses.append(lse.reshape(B, d, n, H).transpose(0, 2, 1, 3).reshape(B, S, H))
    return _merge_branches(outs, lses)


def _dilated_attn_sample(q, k_all, v_all, rel_bias, L):
    N, T, H, hd = q.shape
    outs, lses = [], []
    for window, d in A_BRANCHES:
        steps = window // d
        idx = L + jnp.arange(T)[:, None] - d * jnp.arange(steps + 1)[None, :]
        valid = idx >= 0
        idc = jnp.clip(idx, 0)
        kg = k_all[:, idc]
        vg = v_all[:, idc]
        bias = _branch_bias(rel_bias, d, steps).T
        s = jnp.einsum('nthd,ntmhd->nhtm', q, kg) * (hd ** -0.5) + bias[None, :, None, :]
        s = jnp.where(valid[None, None], s, NEG)
        mx = jnp.max(s, axis=-1, keepdims=True)
        p = jnp.exp(s - mx)
        den = jnp.sum(p, axis=-1)
        o = jnp.einsum('nhtm,ntmhd->nthd', p, vg) / jnp.transpose(den, (0, 2, 1))[..., None]
        outs.append(o)
        lses.append(jnp.transpose(mx[..., 0] + jnp.log(den), (0, 2, 1)))
    return _merge_branches(outs, lses)


def _causal_conv(x, buf, w, b):
    T = x.shape[1]
    xp = jnp.concatenate([buf.astype(F32), x], axis=1)
    y = b.astype(F32)
    for j in range(B_CONV):
        y = y + xp[:, j:j + T] * w[j].astype(F32)
    return y, xp[:, -(B_CONV - 1):]


def _mlstm_chunkwise(q, k, v, ig, fg, C0, n0, m0, chunk):
    N, S, H, hd = q.shape
    nc = S // chunk
    k = k * (hd ** -0.5)
    logf = jax.nn.log_sigmoid(fg)

    def split(a):
        return jnp.swapaxes(a.reshape(N, nc, chunk, *a.shape[2:]), 0, 1)

    causal = jnp.tril(jnp.ones((chunk, chunk), bool))

    def step(carry, xs):
        C, n, m = carry
        qc, kc, vc, ic, lfc = xs
        b = jnp.cumsum(lfc, axis=1)
        Dm = b[:, :, None, :] - b[:, None, :, :] + ic[:, None, :, :]
        Dm = jnp.where(causal[None, :, :, None], Dm, -jnp.inf)
        inter = b + m[:, None, :]
        mt = jnp.maximum(inter, jnp.max(Dm, axis=2))
        w_intra = jnp.exp(Dm - mt[:, :, None, :])
        w_inter = jnp.exp(inter - mt)
        a = w_intra * jnp.einsum('nthd,nshd->ntsh', qc, kc)
        num = jnp.einsum('ntsh,nshd->nthd', a, vc) + w_inter[..., None] * jnp.einsum('nthk,nhkv->nthv', qc, C)
        den = jnp.sum(a, axis=2) + w_inter * jnp.einsum('nthk,nhk->nth', qc, n)
        h = num / jnp.maximum(jnp.abs(den), jnp.exp(-mt))[..., None]
        m_new = mt[:, -1]
        g = jnp.exp(b[:, -1:, :] - b + ic - m_new[:, None, :])
        decay = jnp.exp(b[:, -1] + m - m_new)
        C_new = decay[..., None, None] * C + jnp.einsum('nsh,nshk,nshv->nhkv', g, kc, vc)
        n_new = decay[..., None] * n + jnp.einsum('nsh,nshk->nhk', g, kc)
        return (C_new, n_new, m_new), h

    init = (C0.astype(F32), n0.astype(F32), m0.astype(F32))
    (C, n, m), hs = lax.scan(step, init, (split(q), split(k), split(v), split(ig), split(logf)))
    return jnp.swapaxes(hs, 0, 1).reshape(N, S, H, hd), C, n, m


def _even_layer(x, a_k_buf, a_v_buf, c0, n0, m0, conv0, rel_bias, g_mix, g_ffn, w_in, b_if,
                w_conv, b_conv, g_head, w_out, w1, w3, w2):
    N, T, _ = x.shape
    h = _rmsnorm(x, g_mix)
    proj = jnp.einsum('ntd,de->nte', h, w_in).astype(F32)
    qa, ka, va, qkb, vb, ob, gates = jnp.split(
        proj, [A_W, 2 * A_W, 3 * A_W, 3 * A_W + 2 * B_W, 3 * A_W + 3 * B_W, 3 * A_W + 4 * B_W], axis=-1)
    qa = qa.reshape(N, T, A_HEADS, A_HD)
    ka = ka.reshape(N, T, A_HEADS, A_HD)
    va = va.reshape(N, T, A_HEADS, A_HD)
    if a_k_buf is None:
        a_out = _dilated_attn_prompt(qa, ka, va, rel_bias)
        L = min(A_WIN, T)
        new_k, new_v = ka[:, T - L:], va[:, T - L:]
        chunk = B_CHUNK
    else:
        L = a_k_buf.shape[1]
        k_all = jnp.concatenate([a_k_buf.astype(F32), ka], axis=1)
        v_all = jnp.concatenate([a_v_buf.astype(F32), va], axis=1)
        a_out = _dilated_attn_sample(qa, k_all, v_all, rel_bias, L)
        new_k, new_v = k_all[:, -L:], v_all[:, -L:]
        chunk = T
    qk, new_conv = _causal_conv(qkb, conv0, w_conv, b_conv)
    qk = jax.nn.silu(qk)
    qm = qk[..., :B_W].reshape(N, T, B_HEADS, B_HD)
    km = qk[..., B_W:].reshape(N, T, B_HEADS, B_HD)
    vm = vb.reshape(N, T, B_HEADS, B_HD)
    bif = b_if.astype(F32)
    ig = gates[..., :B_HEADS] + bif[:B_HEADS]
    fg = gates[..., B_HEADS:] + bif[B_HEADS:]
    hb, C, n, m = _mlstm_chunkwise(qm, km, vm, ig, fg, c0, n0, m0, chunk)
    hb = hb * lax.rsqrt(jnp.mean(hb * hb, axis=-1, keepdims=True) + EPS)
    hb = hb * g_head.astype(F32).reshape(B_HEADS, B_HD)
    b_out = jax.nn.sigmoid(ob) * hb.reshape(N, T, B_W)
    mix = jnp.concatenate([a_out.reshape(N, T, A_W), b_out], axis=-1).astype(x.dtype)
    x = x + jnp.einsum('nte,ed->ntd', mix, w_out).astype(x.dtype)
    x = x + _swiglu(_rmsnorm(x, g_ffn), w1, w3, w2).astype(x.dtype)
    return x, new_k, new_v, C, n, m, new_conv


def _chunk_spatial_gate(v, w_s, b_s):
    N, T, _ = v.shape
    nc = -(-T // C_CHUNK)
    vp = jnp.pad(v, ((0, 0), (0, nc * C_CHUNK - T), (0, 0))).reshape(N, nc, C_CHUNK, C_GROUPS, C_GD)
    w = jnp.where(jnp.tril(jnp.ones((C_CHUNK, C_CHUNK), bool)), w_s.astype(F32), 0.0)
    s = jnp.einsum('gpr,ncrgd->ncpgd', w, vp) + jnp.transpose(b_s.astype(F32))[None, None, :, :, None]
    return s.reshape(N, nc * C_CHUNK, C_W)[:, :T]


def _cplx_affine_combine(e1, e2):
    a1r, a1i, b1r, b1i = e1
    a2r, a2i, b2r, b2i = e2
    return (a2r * a1r - a2i * a1i, a2r * a1i + a2i * a1r,
            a2r * b1r - a2i * b1i + b2r, a2r * b1i + a2i * b1r + b2i)


def _s5(xd, a_re, a_im, log_dt, bm_re, bm_im, cm_re, cm_im, d_skip, h_re0, h_im0):
    a_re, a_im = a_re.astype(F32), a_im.astype(F32)
    dt = jnp.exp(log_dt.astype(F32))[:, None]
    mag = jnp.exp(a_re * dt)
    ab_re = mag * jnp.cos(a_im * dt)
    ab_im = mag * jnp.sin(a_im * dt)
    inv = 1.0 / (a_re * a_re + a_im * a_im)
    f_re = ((ab_re - 1.0) * a_re + ab_im * a_im) * inv
    f_im = (ab_im * a_re - (ab_re - 1.0) * a_im) * inv
    bm_re, bm_im = bm_re.astype(F32), bm_im.astype(F32)
    bb_re = f_re[..., None] * bm_re - f_im[..., None] * bm_im
    bb_im = f_re[..., None] * bm_im + f_im[..., None] * bm_re
    bu_re = jnp.einsum('gpc,ntgc->ntgp', bb_re, xd)
    bu_im = jnp.einsum('gpc,ntgc->ntgp', bb_im, xd)
    h_re0, h_im0 = h_re0.astype(F32), h_im0.astype(F32)
    bu_re = bu_re.at[:, 0].add(ab_re * h_re0 - ab_im * h_im0)
    bu_im = bu_im.at[:, 0].add(ab_re * h_im0 + ab_im * h_re0)
    ar = jnp.broadcast_to(ab_re, bu_re.shape)
    ai = jnp.broadcast_to(ab_im, bu_im.shape)
    _, _, hr, hi = lax.associative_scan(_cplx_affine_combine, (ar, ai, bu_re, bu_im), axis=1)
    y = (jnp.einsum('gcp,ntgp->ntgc', cm_re.astype(F32), hr)
         - jnp.einsum('gcp,ntgp->ntgc', cm_im.astype(F32), hi) + d_skip.astype(F32) * xd)
    return y, hr[:, -1], hi[:, -1]


def _moe(h, w_router, b_router, w1, w3, w2):
    N, T, D = h.shape
    hf = h.reshape(N * T, D)
    logits = (hf @ w_router).astype(F32)
    _, top_i = lax.top_k(logits + b_router.astype(F32), TOP_K)
    gates = jax.nn.softmax(jnp.take_along_axis(logits, top_i, axis=-1), axis=-1)
    combine = jnp.sum(jax.nn.one_hot(top_i, N_EXPERTS, dtype=F32) * gates[..., None], axis=1)
    out = jnp.zeros((N * T, D), F32)
    for e in range(N_EXPERTS):
        out = out + combine[:, e:e + 1] * _swiglu(hf, w1[e], w3[e], w2[e]).astype(F32)
    return out.reshape(N, T, D).astype(h.dtype)


def _odd_layer(x, d_re0, d_im0, g_mix, g_ffn, w_in, g_cv, b_cv, w_s, b_s, a_re, a_im, log_dt,
               bm_re, bm_im, cm_re, cm_im, d_skip, w_glu, b_glu, w_out, w_router, b_router, w1, w3, w2):
    N, T, _ = x.shape
    h = _rmsnorm(x, g_mix)
    proj = jnp.einsum('ntd,de->nte', h, w_in).astype(F32)
    u, v, xd = jnp.split(proj, [C_W, 2 * C_W], axis=-1)
    v = _layernorm(v, g_cv, b_cv)
    c_out = u * _chunk_spatial_gate(v, w_s, b_s)
    y, hr, hi = _s5(xd.reshape(N, T, D_GROUPS, D_GCH), a_re, a_im, log_dt, bm_re, bm_im,
                    cm_re, cm_im, d_skip, d_re0, d_im0)
    g = jax.nn.gelu(y.reshape(N, T, D_W))
    d_out = g * jax.nn.sigmoid(g @ w_glu.astype(F32) + b_glu.astype(F32))
    mix = jnp.concatenate([c_out, d_out], axis=-1).astype(x.dtype)
    x = x + jnp.einsum('nte,ed->ntd', mix, w_out).astype(x.dtype)
    x = x + _moe(_rmsnorm(x, g_ffn), w_router, b_router, w1, w3, w2)
    return x, v, hr, hi


def setup_inputs(seed: int = 0) -> dict:
    key = jax.random.key(seed)
    ks = iter(jax.random.split(key, 64))
    nrm = lambda shape, s=1.0: s * jax.random.normal(next(ks), shape, F32)
    L_A = min(A_WIN, PAST_LEN)
    inp = {}
    inp['x_prompt'] = nrm((BATCH, SEQ, D_MODEL))
    inp['x_sample'] = nrm((DEC_BATCH, DEC_SEQ, D_MODEL))
    inp['cache_a_k'] = nrm((N_EVEN, DEC_BATCH, L_A, A_HEADS, A_HD))
    inp['cache_a_v'] = nrm((N_EVEN, DEC_BATCH, L_A, A_HEADS, A_HD))
    inp['state_b_c'] = nrm((N_EVEN, DEC_BATCH, B_HEADS, B_HD, B_HD), 0.5)
    inp['state_b_n'] = nrm((N_EVEN, DEC_BATCH, B_HEADS, B_HD), 0.5)
    inp['state_b_m'] = nrm((N_EVEN, DEC_BATCH, B_HEADS))
    inp['state_b_conv'] = nrm((N_EVEN, DEC_BATCH, B_CONV - 1, 2 * B_W))
    inp['state_d_re'] = nrm((N_ODD, DEC_BATCH, D_GROUPS, D_STATE), 0.1)
    inp['state_d_im'] = nrm((N_ODD, DEC_BATCH, D_GROUPS, D_STATE), 0.1)
    inp['rel_bias'] = nrm((N_BUCKETS, A_HEADS), 0.5)
    inp['g_mix'] = 1.0 + nrm((DEPTH, D_MODEL), 0.01)
    inp['g_ffn'] = 1.0 + nrm((DEPTH, D_MODEL), 0.01)
    inp['g_final'] = 1.0 + nrm((D_MODEL,), 0.01)
    inp['w_in_e'] = nrm((N_EVEN, D_MODEL, IN_E), D_MODEL ** -0.5)
    inp['b_if'] = jnp.concatenate([nrm((N_EVEN, B_HEADS), 0.1),
                                   jax.random.uniform(next(ks), (N_EVEN, B_HEADS), F32, 3.0, 6.0)], axis=-1)
    inp['w_conv_b'] = nrm((N_EVEN, B_CONV, 2 * B_W), B_CONV ** -0.5)
    inp['b_conv_b'] = nrm((N_EVEN, 2 * B_W), 0.01)
    inp['g_bhead'] = 1.0 + nrm((N_EVEN, B_W), 0.01)
    inp['w_out_e'] = nrm((N_EVEN, A_W + B_W, D_MODEL), (A_W + B_W) ** -0.5)
    inp['w1_e'] = nrm((N_EVEN, D_MODEL, D_FF), D_MODEL ** -0.5)
    inp['w3_e'] = nrm((N_EVEN, D_MODEL, D_FF), D_MODEL ** -0.5)
    inp['w2_e'] = nrm((N_EVEN, D_FF, D_MODEL), D_FF ** -0.5)
    inp['w_in_o'] = nrm((N_ODD, D_MODEL, IN_O), D_MODEL ** -0.5)
    inp['g_cv'] = 1.0 + nrm((N_ODD, C_W), 0.01)
    inp['b_cv'] = nrm((N_ODD, C_W), 0.01)
    inp['w_s'] = nrm((N_ODD, C_GROUPS, C_CHUNK, C_CHUNK), C_CHUNK ** -0.5)
    inp['b_s'] = 1.0 + nrm((N_ODD, C_GROUPS, C_CHUNK), 0.1)
    inp['a_re'] = -0.5 + nrm((N_ODD, D_GROUPS, D_STATE), 0.01)
    inp['a_im'] = math.pi * jnp.arange(D_STATE, dtype=F32) + nrm((N_ODD, D_GROUPS, D_STATE), 0.01)
    inp['log_dt'] = jax.random.uniform(next(ks), (N_ODD, D_GROUPS), F32, math.log(1e-3), math.log(1e-1))
    inp['bm_re'] = nrm((N_ODD, D_GROUPS, D_STATE, D_GCH), (2 * D_GCH) ** -0.5)
    inp['bm_im'] = nrm((N_ODD, D_GROUPS, D_STATE, D_GCH), (2 * D_GCH) ** -0.5)
    inp['cm_re'] = nrm((N_ODD, D_GROUPS, D_GCH, D_STATE), D_STATE ** -0.5)
    inp['cm_im'] = nrm((N_ODD, D_GROUPS, D_GCH, D_STATE), D_STATE ** -0.5)
    inp['d_skip'] = nrm((N_ODD, D_GROUPS, D_GCH))
    inp['w_glu'] = nrm((N_ODD, D_W, D_W), D_W ** -0.5)
    inp['b_glu'] = nrm((N_ODD, D_W), 0.01)
    inp['w_out_o'] = nrm((N_ODD, C_W + D_W, D_MODEL), (C_W + D_W) ** -0.5)
    inp['w_router'] = nrm((N_ODD, D_MODEL, N_EXPERTS), D_MODEL ** -0.5)
    inp['b_router'] = nrm((N_ODD, N_EXPERTS), 0.01)
    inp['w1_m'] = nrm((N_ODD, N_EXPERTS, D_MODEL, D_FF), D_MODEL ** -0.5)
    inp['w3_m'] = nrm((N_ODD, N_EXPERTS, D_MODEL, D_FF), D_MODEL ** -0.5)
    inp['w2_m'] = nrm((N_ODD, N_EXPERTS, D_FF, D_MODEL), D_FF ** -0.5)
    return inp


def reference(x_prompt, x_sample, cache_a_k, cache_a_v, state_b_c, state_b_n, state_b_m, state_b_conv,
              state_d_re, state_d_im, rel_bias, g_mix, g_ffn, g_final, w_in_e, b_if, w_conv_b, b_conv_b,
              g_bhead, w_out_e, w1_e, w3_e, w2_e, w_in_o, g_cv, b_cv, w_s, b_s, a_re, a_im, log_dt,
              bm_re, bm_im, cm_re, cm_im, d_skip, w_glu, b_glu, w_out_o, w_router, b_router,
              w1_m, w3_m, w2_m):
    hp, hs = x_prompt, x_sample
    Bp = x_prompt.shape[0]
    pa_k, pa_v, sa_k, sa_v = [], [], [], []
    pb_c, sb_c, pb_n, sb_n, pb_m, sb_m, pb_cv, sb_cv = [], [], [], [], [], [], [], []
    sc_v = []
    pd_re, sd_re, pd_im, sd_im = [], [], [], []
    for layer in range(DEPTH):
        l = layer // 2
        if layer % 2 == 0:
            ew = (rel_bias, g_mix[layer], g_ffn[layer], w_in_e[l], b_if[l], w_conv_b[l], b_conv_b[l],
                  g_bhead[l], w_out_e[l], w1_e[l], w3_e[l], w2_e[l])
            hp, k1, v1, c1, n1, m1, cv1 = _even_layer(
                hp, None, None, jnp.zeros((Bp, B_HEADS, B_HD, B_HD), F32), jnp.zeros((Bp, B_HEADS, B_HD), F32),
                jnp.zeros((Bp, B_HEADS), F32), jnp.zeros((Bp, B_CONV - 1, 2 * B_W), F32), *ew)
            hs, k2, v2, c2, n2, m2, cv2 = _even_layer(
                hs, cache_a_k[l], cache_a_v[l], state_b_c[l], state_b_n[l], state_b_m[l], state_b_conv[l], *ew)
            pa_k.append(k1); pa_v.append(v1); sa_k.append(k2); sa_v.append(v2)
            pb_c.append(c1); sb_c.append(c2); pb_n.append(n1); sb_n.append(n2)
            pb_m.append(m1); sb_m.append(m2); pb_cv.append(cv1); sb_cv.append(cv2)
        else:
            ow = (g_mix[layer], g_ffn[layer], w_in_o[l], g_cv[l], b_cv[l], w_s[l], b_s[l], a_re[l], a_im[l],
                  log_dt[l], bm_re[l], bm_im[l], cm_re[l], cm_im[l], d_skip[l], w_glu[l], b_glu[l],
                  w_out_o[l], w_router[l], b_router[l], w1_m[l], w3_m[l], w2_m[l])
            hp, _, r1, i1 = _odd_layer(hp, jnp.zeros((Bp, D_GROUPS, D_STATE), F32),
                                       jnp.zeros((Bp, D_GROUPS, D_STATE), F32), *ow)
            hs, v2, r2, i2 = _odd_layer(hs, state_d_re[l], state_d_im[l], *ow)
            sc_v.append(v2); pd_re.append(r1); sd_re.append(r2); pd_im.append(i1); sd_im.append(i2)
    y_prompt = _rmsnorm(hp, g_final)
    y_sample = _rmsnorm(hs, g_final)
    p_a_k, p_a_v, s_a_k, s_a_v = jnp.stack(pa_k), jnp.stack(pa_v), jnp.stack(sa_k), jnp.stack(sa_v)
    p_b_c, s_b_c, p_b_n, s_b_n = jnp.stack(pb_c), jnp.stack(sb_c), jnp.stack(pb_n), jnp.stack(sb_n)
    p_b_m, s_b_m, p_b_conv, s_b_conv = jnp.stack(pb_m), jnp.stack(sb_m), jnp.stack(pb_cv), jnp.stack(sb_cv)
    s_c_v = jnp.stack(sc_v)
    p_d_re, s_d_re, p_d_im, s_d_im = jnp.stack(pd_re), jnp.stack(sd_re), jnp.stack(pd_im), jnp.stack(sd_im)
    return (y_prompt, y_sample, p_a_k, p_a_v, s_a_k, s_a_v, p_b_c, s_b_c, p_b_n, s_b_n, p_b_m, s_b_m,
            p_b_conv, s_b_conv, s_c_v, p_d_re, s_d_re, p_d_im, s_d_im)
```

```python
import functools
import math

import jax
import jax.numpy as jnp
from jax import lax
from jax.experimental import pallas as pl
from jax.experimental.pallas import tpu as pltpu

F32 = jnp.float32
BF16 = jnp.bfloat16
I32 = jnp.int32
HIGHEST = lax.Precision.HIGHEST

D_MODEL = 1024
A_HEADS, A_HD = 8, 64
A_W = A_HEADS * A_HD
A_BRANCHES = ((128, 1), (512, 4), (2048, 16))
A_WIN = 2048
N_BUCKETS = 32
B_HEADS, B_HD = 4, 128
B_W = B_HEADS * B_HD
B_CONV = 4
C_GROUPS, C_GD = 4, 128
C_W = C_GROUPS * C_GD
C_CHUNK = 128
D_GROUPS, D_GCH, D_STATE = 32, 16, 64
D_W = D_GROUPS * D_GCH
D_FF = 3584
N_EXPERTS = 8
TOP_K = 2
EPS = 1e-6
NEG = -1e30

VMEM_LIMIT = 56 * 1024 * 1024

ROUTE_TILE = 256
FFN_TILE = 512


def _cparams(*sem):
    return pltpu.CompilerParams(dimension_semantics=sem, vmem_limit_bytes=VMEM_LIMIT)


def _moe_plan(ids, n_tok_tiles):
    mt = ids.shape[0]
    n_assign = 2 * mt
    n_ffn_tiles = -(-(n_assign + N_EXPERTS * (FFN_TILE - 1)) // FFN_TILE)
    n_sub = n_ffn_tiles * (FFN_TILE // ROUTE_TILE)
    flat = ids.reshape(n_assign)
    onehot = (flat[:, None] == jnp.arange(N_EXPERTS, dtype=I32)[None, :]).astype(I32)
    csum = jnp.cumsum(onehot, axis=0)
    rank = jnp.sum((csum - onehot) * onehot, axis=1)
    counts = csum[-1]
    seg = ((counts + FFN_TILE - 1) // FFN_TILE) * FFN_TILE
    seg_end = jnp.cumsum(seg)
    seg_off = seg_end - seg
    dest = jnp.where(flat >= 0, seg_off[jnp.clip(flat, 0)] + rank, -1).reshape(mt, 2)
    total = seg_end[-1]
    tile_start = jnp.arange(n_ffn_tiles, dtype=I32) * FFN_TILE
    tile_expert = jnp.minimum(jnp.sum((tile_start[:, None] >= seg_end[None, :]).astype(I32), axis=1),
                              N_EXPERTS - 1).astype(I32)
    n_used = (total // FFN_TILE).astype(I32)
    sub_lo = jnp.arange(n_sub, dtype=I32) * ROUTE_TILE
    d = dest.reshape(n_tok_tiles, ROUTE_TILE * 2)
    sub_of = jnp.where(d >= 0, d // ROUTE_TILE, -1)
    overlap = jnp.any(sub_of[None, :, :] == jnp.arange(n_sub, dtype=I32)[:, None, None], axis=2)
    del sub_lo
    n_pairs = n_sub + n_tok_tiles * N_EXPERTS
    ov_g = overlap | ((~jnp.any(overlap, axis=1))[:, None] & (jnp.arange(n_tok_tiles) == 0)[None, :])
    ov_c = overlap.T | ((~jnp.any(overlap, axis=0))[:, None] & (jnp.arange(n_sub) == 0)[None, :])

    def visit_list(ov):
        n_major, n_minor = ov.shape
        idx = jnp.nonzero(ov.reshape(-1), size=n_pairs, fill_value=-1)[0].astype(I32)
        cnt = jnp.sum(ov).astype(I32)
        last = idx[jnp.maximum(cnt - 1, 0)]
        idx = jnp.where(idx < 0, last, idx)
        major, minor = idx // n_minor, idx % n_minor
        p = jnp.arange(n_pairs, dtype=I32)
        live = p < cnt
        prev_major = jnp.concatenate([jnp.full((1,), -1, I32), major[:-1]])
        next_major = jnp.concatenate([major[1:], jnp.full((1,), -1, I32)])
        first = (live & (major != prev_major)).astype(I32)
        lastf = (live & ((major != next_major) | (p == cnt - 1))).astype(I32)
        return major.astype(I32), minor.astype(I32), first, lastf, live.astype(I32)

    return dest, tile_expert, n_used, visit_list(ov_g), visit_list(ov_c), n_ffn_tiles, n_pairs


def _moe_gather_kernel(ps_ref, pj_ref, first_ref, last_ref, live_ref, h_ref, dt_ref, gt_ref, xs_ref, gs_ref):
    p = pl.program_id(0)

    @pl.when(live_ref[p] == 1)
    def _():
        base = ps_ref[p] * ROUTE_TILE
        rows = lax.broadcasted_iota(I32, (ROUTE_TILE, ROUTE_TILE), 0) + base
        m1 = dt_ref[0:1, :] == rows
        m2 = dt_ref[1:2, :] == rows
        onehot = jnp.where(m1 | m2, 1.0, 0.0).astype(BF16)
        x = jnp.dot(onehot, h_ref[...], preferred_element_type=F32).astype(BF16)
        g = jnp.sum(jnp.where(m1, gt_ref[0:1, :], 0.0) + jnp.where(m2, gt_ref[1:2, :], 0.0),
                    axis=1, keepdims=True)

        @pl.when(first_ref[p] == 1)
        def _():
            xs_ref[...] = x
            gs_ref[...] = g

        @pl.when(first_ref[p] == 0)
        def _():
            xs_ref[...] += x
            gs_ref[...] += g


def _moe_gather(hn, dest_t, gates_t, glist, n_sub, n_pairs):
    ps, pj, first, last, live = glist
    d = hn.shape[1]
    grid_spec = pltpu.PrefetchScalarGridSpec(
        num_scalar_prefetch=5,
        grid=(n_pairs,),
        in_specs=[
            pl.BlockSpec((ROUTE_TILE, d), lambda p, ps, pj, *_: (pj[p], 0)),
            pl.BlockSpec((2, ROUTE_TILE), lambda p, ps, pj, *_: (0, pj[p])),
            pl.BlockSpec((2, ROUTE_TILE), lambda p, ps, pj, *_: (0, pj[p])),
        ],
        out_specs=[
            pl.BlockSpec((ROUTE_TILE, d), lambda p, ps, pj, *_: (ps[p], 0)),
            pl.BlockSpec((ROUTE_TILE, 1), lambda p, ps, pj, *_: (ps[p], 0)),
        ],
    )
    return pl.pallas_call(
        _moe_gather_kernel,
        grid_spec=grid_spec,
        out_shape=[jax.ShapeDtypeStruct((n_sub * ROUTE_TILE, d), BF16),
                   jax.ShapeDtypeStruct((n_sub * ROUTE_TILE, 1), F32)],
        compiler_params=_cparams("arbitrary"),
        name="moe_gather",
    )(ps, pj, first, last, live, hn, dest_t, gates_t)


def _moe_ffn_kernel(te_ref, nu_ref, x_ref, gs_ref, w1_ref, w3_ref, w2_ref, y_ref, acc_ref):
    t, f = pl.program_id(0), pl.program_id(1)
    nf = pl.num_programs(1)

    @pl.when(t < nu_ref[0])
    def _():
        x = x_ref[...]
        a = jnp.dot(x, w1_ref[0], preferred_element_type=F32)
        b = jnp.dot(x, w3_ref[0], preferred_element_type=F32)
        hmid = (a * jax.nn.sigmoid(a) * b).astype(BF16)
        part = jnp.dot(hmid, w2_ref[0], preferred_element_type=F32)

        @pl.when(f == 0)
        def _():
            acc_ref[...] = part

        @pl.when(f > 0)
        def _():
            acc_ref[...] += part

        @pl.when(f == nf - 1)
        def _():
            y_ref[...] = (acc_ref[...] * gs_ref[...]).astype(BF16)

    @pl.when((t >= nu_ref[0]) & (f == nf - 1))
    def _():
        y_ref[...] = jnp.zeros_like(y_ref)


def _moe_ffn(xs, gs, w1, w3, w2, tile_expert, n_used, n_ffn_tiles, tf):
    d = xs.shape[1]
    ff = w1.shape[2]
    nf = ff // tf

    def fidx(t, f, nu):
        return jnp.where(t < nu[0], f, nf - 1)

    grid_spec = pltpu.PrefetchScalarGridSpec(
        num_scalar_prefetch=2,
        grid=(n_ffn_tiles, nf),
        in_specs=[
            pl.BlockSpec((FFN_TILE, d), lambda t, f, te, nu: (t, 0)),
            pl.BlockSpec((FFN_TILE, 1), lambda t, f, te, nu: (t, 0)),
            pl.BlockSpec((1, d, tf), lambda t, f, te, nu: (te[t], 0, fidx(t, f, nu))),
            pl.BlockSpec((1, d, tf), lambda t, f, te, nu: (te[t], 0, fidx(t, f, nu))),
            pl.BlockSpec((1, tf, d), lambda t, f, te, nu: (te[t], fidx(t, f, nu), 0)),
        ],
        out_specs=pl.BlockSpec((FFN_TILE, d), lambda t, f, te, nu: (t, 0)),
        scratch_shapes=[pltpu.VMEM((FFN_TILE, d), F32)],
    )
    return pl.pallas_call(
        _moe_ffn_kernel,
        grid_spec=grid_spec,
        out_shape=jax.ShapeDtypeStruct(xs.shape, BF16),
        compiler_params=_cparams("arbitrary", "arbitrary"),
        name="moe_ffn",
    )(tile_expert, n_used.reshape(1), xs, gs, w1, w3, w2)


def _moe_combine_kernel(pj_ref, ps_ref, first_ref, last_ref, live_ref, x_ref, d_ref, ys_ref, g_ref,
                        y_ref, acc_ref):
    p = pl.program_id(0)

    @pl.when(live_ref[p] == 1)
    def _():
        base = ps_ref[p] * ROUTE_TILE
        cols = lax.broadcasted_iota(I32, (ROUTE_TILE, ROUTE_TILE), 1) + base
        hit = (d_ref[:, 0:1] == cols) | (d_ref[:, 1:2] == cols)
        onehot = jnp.where(hit, 1.0, 0.0).astype(BF16)
        part = jnp.dot(onehot, ys_ref[...], preferred_element_type=F32)

        @pl.when(first_ref[p] == 1)
        def _():
            acc_ref[...] = x_ref[...] + part

        @pl.when(first_ref[p] == 0)
        def _():
            acc_ref[...] += part

        @pl.when(last_ref[p] == 1)
        def _():
            x3 = acc_ref[...]
            r = lax.rsqrt(jnp.mean(x3 * x3, axis=-1, keepdims=True) + EPS)
            y_ref[...] = x3 * r * g_ref[...]


def _moe_combine(x2, dest, ys, g_final, clist, n_pairs):
    pj, ps, first, last, live = clist
    mt, d = x2.shape
    grid_spec = pltpu.PrefetchScalarGridSpec(
        num_scalar_prefetch=5,
        grid=(n_pairs,),
        in_specs=[
            pl.BlockSpec((ROUTE_TILE, d), lambda p, pj, ps, *_: (pj[p], 0)),
            pl.BlockSpec((ROUTE_TILE, 2), lambda p, pj, ps, *_: (pj[p], 0)),
            pl.BlockSpec((ROUTE_TILE, d), lambda p, pj, ps, *_: (ps[p], 0)),
            pl.BlockSpec((1, d), lambda p, *_: (0, 0)),
        ],
        out_specs=pl.BlockSpec((ROUTE_TILE, d), lambda p, pj, ps, *_: (pj[p], 0)),
        scratch_shapes=[pltpu.VMEM((ROUTE_TILE, d), F32)],
    )
    return pl.pallas_call(
        _moe_combine_kernel,
        grid_spec=grid_spec,
        out_shape=jax.ShapeDtypeStruct((mt, d), F32),
        compiler_params=_cparams("arbitrary"),
        name="moe_combine",
    )(pj, ps, first, last, live, x2, dest, ys, g_final.reshape(1, d))


def _moe_block(x2, hn, ids, gates, w1, w3, w2, g_final, tf=512):
    mt = x2.shape[0]
    n_tok_tiles = mt // ROUTE_TILE
    dest, tile_expert, n_used, glist, clist, n_ffn_tiles, n_pairs = _moe_plan(ids, n_tok_tiles)
    n_sub = n_ffn_tiles * (FFN_TILE // ROUTE_TILE)
    xs, gs = _moe_gather(hn, dest.T, gates.T, glist, n_sub, n_pairs)
    ys = _moe_ffn(xs, gs, w1, w3, w2, tile_expert, n_used, n_ffn_tiles, tf)
    return _moe_combine(x2, dest, ys, g_final, clist, n_pairs)


def _rms(x, g):
    return x * lax.rsqrt(jnp.mean(x * x, axis=-1, keepdims=True) + EPS) * g


def _mm(a, b, precise):
    if precise:
        return jnp.dot(a.astype(F32), b.astype(F32), preferred_element_type=F32, precision=HIGHEST)
    return jnp.dot(a.astype(BF16), b.astype(BF16), preferred_element_type=F32)


def _silu(x):
    return x * jax.nn.sigmoid(x)


def _norm_proj_kernel(x_ref, g_ref, w_ref, wt_ref, *out_refs, segs, precise):
    xn = _rms(x_ref[...], g_ref[...])
    xn = xn if precise else xn.astype(BF16)
    for (start, width), o_ref in zip(segs, out_refs[:-1]):
        for c in range(0, width, 512):
            cw = min(512, width - c)
            o_ref[:, c:c + cw] = _mm(xn, w_ref[:, start + c:start + c + cw], precise)
    wt = wt_ref[...]
    out_refs[-1][...] = lax.dot_general(
        wt.astype(xn.dtype), xn, (((1,), (1,)), ((), ())), preferred_element_type=F32,
        precision=HIGHEST if precise else None)


def _norm_proj(x, g, w, wt, segs, tm, precise=False):
    m, d = x.shape
    n = w.shape[1]
    nt = wt.shape[0]
    out_shape = [jax.ShapeDtypeStruct((m, width), F32) for _, width in segs]
    out_shape.append(jax.ShapeDtypeStruct((nt, m), F32))
    out_specs = [pl.BlockSpec((tm, width), lambda i: (i, 0)) for _, width in segs]
    out_specs.append(pl.BlockSpec((nt, tm), lambda i: (0, i)))
    return pl.pallas_call(
        functools.partial(_norm_proj_kernel, segs=tuple(segs), precise=precise),
        grid=(m // tm,),
        in_specs=[
            pl.BlockSpec((tm, d), lambda i: (i, 0)),
            pl.BlockSpec((1, d), lambda i: (0, 0)),
            pl.BlockSpec((d, n), lambda i: (0, 0)),
            pl.BlockSpec((nt, d), lambda i: (0, 0)),
        ],
        out_specs=out_specs,
        out_shape=out_shape,
        compiler_params=_cparams("arbitrary"),
        name="norm_proj",
    )(x, g.reshape(1, d), w, wt)


def _mix_ffn_kernel(a_ref, b_ref, x_ref, wo_ref, g_ref, w1_ref, w3_ref, w2_ref, o_ref,
                    x1_ref, hn_ref, acc_ref, *, precise):
    f = pl.program_id(1)
    wa = a_ref.shape[1]

    @pl.when(f == 0)
    def _():
        x1 = (x_ref[...] + _mm(a_ref[...], wo_ref[0:wa, :], precise)
              + _mm(b_ref[...], wo_ref[wa:, :], precise))
        x1_ref[...] = x1
        hn_ref[...] = _rms(x1, g_ref[...]).astype(hn_ref.dtype)

    hn = hn_ref[...]
    part = _mm(_silu(_mm(hn, w1_ref[...], precise)) * _mm(hn, w3_ref[...], precise), w2_ref[...], precise)

    @pl.when(f == 0)
    def _():
        acc_ref[...] = part

    @pl.when(f > 0)
    def _():
        acc_ref[...] += part

    @pl.when(f == pl.num_programs(1) - 1)
    def _():
        o_ref[...] = x1_ref[...] + acc_ref[...]


def _mix_ffn(a, b, x, w_out, g, w1, w3, w2, tm, tf, precise=False):
    m, d = x.shape
    wa, wb = a.shape[1], b.shape[1]
    ff = w1.shape[1]
    return pl.pallas_call(
        functools.partial(_mix_ffn_kernel, precise=precise),
        grid=(m // tm, ff // tf),
        in_specs=[
            pl.BlockSpec((tm, wa), lambda i, f: (i, 0)),
            pl.BlockSpec((tm, wb), lambda i, f: (i, 0)),
            pl.BlockSpec((tm, d), lambda i, f: (i, 0)),
            pl.BlockSpec((wa + wb, d), lambda i, f: (0, 0)),
            pl.BlockSpec((1, d), lambda i, f: (0, 0)),
            pl.BlockSpec((d, tf), lambda i, f: (0, f)),
            pl.BlockSpec((d, tf), lambda i, f: (0, f)),
            pl.BlockSpec((tf, d), lambda i, f: (f, 0)),
        ],
        out_specs=pl.BlockSpec((tm, d), lambda i, f: (i, 0)),
        out_shape=jax.ShapeDtypeStruct((m, d), F32),
        scratch_shapes=[pltpu.VMEM((tm, d), F32), pltpu.VMEM((tm, d), F32 if precise else BF16),
                        pltpu.VMEM((tm, d), F32)],
        compiler_params=_cparams("arbitrary", "arbitrary"),
        name="mix_ffn",
    )(a, b, x, w_out, g.reshape(1, d), w1, w3, w2)


def _mix_router_kernel(a_ref, b_ref, x_ref, wo_ref, g_ref, wr_ref, br_ref, x2_ref, hn_ref, ids_ref, gate_ref,
                       *, precise):
    wa = a_ref.shape[1]
    x2 = (x_ref[...] + _mm(a_ref[...], wo_ref[0:wa, :], precise)
          + _mm(b_ref[...], wo_ref[wa:, :], precise))
    x2_ref[...] = x2
    hn = _rms(x2, g_ref[...])
    hn_ref[...] = hn.astype(BF16)
    logits = jnp.dot(hn, wr_ref[...], preferred_element_type=F32, precision=HIGHEST)
    lane = lax.broadcasted_iota(I32, logits.shape, 1)
    real = lane < N_EXPERTS
    biased = jnp.where(real, logits + br_ref[...], -jnp.inf)
    m1 = jnp.max(biased, axis=-1, keepdims=True)
    i1 = jnp.min(jnp.where(biased == m1, lane, 128), axis=-1, keepdims=True)
    rest = jnp.where(lane == i1, -jnp.inf, biased)
    m2 = jnp.max(rest, axis=-1, keepdims=True)
    i2 = jnp.min(jnp.where(rest == m2, lane, 128), axis=-1, keepdims=True)
    l1 = jnp.sum(jnp.where(lane == i1, logits, 0.0), axis=-1, keepdims=True)
    l2 = jnp.sum(jnp.where(lane == i2, logits, 0.0), axis=-1, keepdims=True)
    mx = jnp.maximum(l1, l2)
    e1, e2 = jnp.exp(l1 - mx), jnp.exp(l2 - mx)
    two = lax.broadcasted_iota(I32, ids_ref.shape, 1)
    ids_ref[...] = jnp.where(two == 0, i1, i2)
    gate_ref[...] = jnp.where(two == 0, e1, e2) / (e1 + e2)


def _mix_router(a, b, x, w_out, g, wr, br, tm, precise=False):
    m, d = x.shape
    wa, wb = a.shape[1], b.shape[1]
    return pl.pallas_call(
        functools.partial(_mix_router_kernel, precise=precise),
        grid=(m // tm,),
        in_specs=[
            pl.BlockSpec((tm, wa), lambda i: (i, 0)),
            pl.BlockSpec((tm, wb), lambda i: (i, 0)),
            pl.BlockSpec((tm, d), lambda i: (i, 0)),
            pl.BlockSpec((wa + wb, d), lambda i: (0, 0)),
            pl.BlockSpec((1, d), lambda i: (0, 0)),
            pl.BlockSpec((d, 128), lambda i: (0, 0)),
            pl.BlockSpec((1, 128), lambda i: (0, 0)),
        ],
        out_specs=[
            pl.BlockSpec((tm, d), lambda i: (i, 0)),
            pl.BlockSpec((tm, d), lambda i: (i, 0)),
            pl.BlockSpec((tm, 2), lambda i: (i, 0)),
            pl.BlockSpec((tm, 2), lambda i: (i, 0)),
        ],
        out_shape=[jax.ShapeDtypeStruct((m, d), F32), jax.ShapeDtypeStruct((m, d), BF16),
                   jax.ShapeDtypeStruct((m, 2), I32), jax.ShapeDtypeStruct((m, 2), F32)],
        compiler_params=_cparams("arbitrary"),
        name="mix_router",
    )(a, b, x, w_out, g.reshape(1, d), wr, br)


ATT_T = 128


def _t5_bucket(dist):
    max_exact = N_BUCKETS // 2
    d = jnp.maximum(dist, 1).astype(F32)
    large = max_exact + (jnp.log(d / max_exact) / math.log(A_WIN / max_exact)
                         * (N_BUCKETS - max_exact)).astype(I32)
    return jnp.where(dist < max_exact, dist, jnp.minimum(large, N_BUCKETS - 1))


def _distance_logit_table(rel_bias, max_dist):
    dist = jnp.arange(max_dist + 1, dtype=I32)
    mult = jnp.zeros((max_dist + 1,), F32)
    for window, d in A_BRANCHES:
        mult = mult + ((dist % d == 0) & (dist <= window)).astype(F32)
    bias = rel_bias[_t5_bucket(dist)].astype(F32).T
    return jnp.where(mult[None, :] > 0, bias + jnp.log(jnp.maximum(mult, 1.0))[None, :], NEG)


def _attn_kernel(q_ref, k_ref, v_ref, tab_ref, o_ref, kb_ref, vb_ref):
    s_len = q_ref.shape[1]
    nb = s_len // ATT_T
    kb_ref[...] = k_ref[0].astype(BF16)
    vb_ref[...] = v_ref[0].astype(BF16)
    scale = A_HD ** -0.5
    for i in range(nb):
        rows = slice(i * ATT_T, (i + 1) * ATT_T)
        width = (i + 1) * ATT_T
        outs = []
        for hl in range(2):
            cols = slice(hl * A_HD, (hl + 1) * A_HD)
            q = (q_ref[0, rows, cols] * scale).astype(BF16)
            s = lax.dot_general(q, kb_ref[0:width, cols], (((1,), (1,)), ((), ())),
                                preferred_element_type=F32)
            s = s + tab_ref[hl, :, (nb - 1 - i) * ATT_T:nb * ATT_T]
            m = jnp.max(s, axis=-1, keepdims=True)
            p = jnp.exp(s - m)
            den = jnp.sum(p, axis=-1, keepdims=True)
            o = jnp.dot(p.astype(BF16), vb_ref[0:width, cols], preferred_element_type=F32)
            outs.append(o / den)
        o_ref[0, rows, :] = jnp.concatenate(outs, axis=-1)


def _attention_prompt(q, k, v, rel_bias):
    b, s_len, _ = q.shape
    nb = s_len // ATT_T
    tab1 = _distance_logit_table(rel_bias, s_len)
    r = jnp.arange(ATT_T, dtype=I32)[:, None]
    x = jnp.arange(s_len, dtype=I32)[None, :]
    delta = (nb - 1 - x // ATT_T) * ATT_T + r - (x % ATT_T)
    tab = jnp.where(delta[None] >= 0, tab1[:, jnp.clip(delta, 0)], NEG)
    spec = pl.BlockSpec((1, s_len, 2 * A_HD), lambda bi, hp: (bi, 0, hp))
    return pl.pallas_call(
        _attn_kernel,
        grid=(b, A_HEADS // 2),
        in_specs=[spec, spec, spec, pl.BlockSpec((2, ATT_T, s_len), lambda bi, hp: (hp, 0, 0))],
        out_specs=spec,
        out_shape=jax.ShapeDtypeStruct(q.shape, F32),
        scratch_shapes=[pltpu.VMEM((s_len, 2 * A_HD), BF16), pltpu.VMEM((s_len, 2 * A_HD), BF16)],
        compiler_params=_cparams("arbitrary", "arbitrary"),
        name="dilated_attention",
    )(q, k, v, tab)


MLSTM_CHUNK = 256


def _log_sigmoid(x):
    return jnp.minimum(x, 0.0) - jnp.log(1.0 + jnp.exp(-jnp.abs(x)))


def _mlstm_kernel(qk_ref, v_ref, ob_ref, gc_ref, gr_ref, wc_ref, bc_ref, bifc_ref, bifr_ref, gh_ref,
                  o_ref, c_out_ref, n_out_ref, m_out_ref, xbuf_ref, c_ref, n_ref, m_ref):
    ci = pl.program_id(1)
    L = qk_ref.shape[1]

    @pl.when(ci == 0)
    def _():
        xbuf_ref[0:8, :] = jnp.zeros((8, 2 * B_W), F32)
        c_ref[...] = jnp.zeros_like(c_ref)
        n_ref[...] = jnp.zeros_like(n_ref)
        m_ref[...] = jnp.zeros_like(m_ref)

    x = qk_ref[0]
    xbuf_ref[8:8 + L, :] = x
    y = bc_ref[...] + wc_ref[3:4, :] * x
    for j in range(B_CONV - 1):
        y = y + wc_ref[j:j + 1, :] * xbuf_ref[5 + j:5 + j + L, :]
    xbuf_ref[0:8, :] = x[L - 8:L, :]
    y = _silu(y)

    gcol = gc_ref[0] + bifc_ref[...]
    grow = gr_ref[...] + bifr_ref[...]
    ri = lax.broadcasted_iota(I32, (L, L), 0)
    cj = lax.broadcasted_iota(I32, (L, L), 1)
    causal = ri >= cj
    lower = jnp.where(causal, 1.0, 0.0)
    b_col = jnp.dot(lower, _log_sigmoid(gcol), preferred_element_type=F32, precision=HIGHEST)
    b_row = lax.dot_general(_log_sigmoid(grow), lower, (((1,), (1,)), ((), ())),
                            preferred_element_type=F32, precision=HIGHEST)

    for h in range(B_HEADS):
        cols = slice(h * B_HD, (h + 1) * B_HD)
        qf = y[:, cols]
        q = qf.astype(BF16)
        kf = y[:, B_W + h * B_HD:B_W + (h + 1) * B_HD] * (B_HD ** -0.5)
        k = kf.astype(BF16)
        v = v_ref[0, :, cols].astype(BF16)
        bc = b_col[:, B_HEADS + h:B_HEADS + h + 1]
        ic = gcol[:, h:h + 1]
        br = b_row[B_HEADS + h:B_HEADS + h + 1, :]
        ir = grow[h:h + 1, :]
        m_prev = m_ref[h:h + 1, 0:1]
        dm = jnp.where(causal, bc - br + ir, NEG)
        inter = bc + m_prev
        mt = jnp.maximum(inter, jnp.max(dm, axis=1, keepdims=True))
        w_intra = jnp.exp(dm - mt)
        w_inter = jnp.exp(inter - mt)
        a = w_intra * lax.dot_general(q, k, (((1,), (1,)), ((), ())), preferred_element_type=F32)
        c_prev = c_ref[h]
        n_prev = n_ref[h:h + 1, :]
        num = (jnp.dot(a.astype(BF16), v, preferred_element_type=F32)
               + w_inter * jnp.dot(q, c_prev.astype(BF16), preferred_element_type=F32))
        den = (jnp.sum(a, axis=1, keepdims=True)
               + w_inter * jnp.sum(qf * n_prev, axis=1, keepdims=True))
        hb = num / jnp.maximum(jnp.abs(den), jnp.exp(-mt))
        m_new = mt[L - 1:L, :]
        b_last = bc[L - 1:L, :]
        g = jnp.exp(b_last - bc + ic - m_new)
        decay = jnp.exp(b_last + m_prev - m_new)
        kg = kf * g
        c_ref[h] = decay * c_prev + lax.dot_general(kg.astype(BF16), v, (((0,), (0,)), ((), ())),
                                                    preferred_element_type=F32)
        n_ref[h:h + 1, :] = decay * n_prev + jnp.sum(kg, axis=0, keepdims=True)
        m_ref[h:h + 1, :] = jnp.broadcast_to(m_new, (1, 128))
        hb = hb * lax.rsqrt(jnp.mean(hb * hb, axis=-1, keepdims=True) + EPS) * gh_ref[:, cols]
        o_ref[0, :, cols] = jax.nn.sigmoid(ob_ref[0, :, cols]) * hb

    @pl.when(ci == pl.num_programs(1) - 1)
    def _():
        c_out_ref[0] = c_ref[...]
        n_out_ref[0] = n_ref[...]
        m_out_ref[0] = m_ref[...]


def _mlstm_prompt(qk, v, ob, gcol, grow, w_conv, b_conv, b_if, g_head):
    b, s_len, _ = qk.shape
    L = min(MLSTM_CHUNK, s_len)
    nc = s_len // L
    bif_col = jnp.zeros((1, 128), F32).at[0, :2 * B_HEADS].set(b_if)
    bif_row = b_if.reshape(2 * B_HEADS, 1)
    seq = lambda w: pl.BlockSpec((1, L, w), lambda bi, ci: (bi, ci, 0))
    full = lambda shape: pl.BlockSpec(shape, lambda bi, ci: (0,) * len(shape))
    return pl.pallas_call(
        _mlstm_kernel,
        grid=(b, nc),
        in_specs=[seq(2 * B_W), seq(B_W), seq(B_W), seq(128),
                  pl.BlockSpec((2 * B_HEADS, L), lambda bi, ci: (0, bi * nc + ci)),
                  full((B_CONV, 2 * B_W)), full((1, 2 * B_W)), full((1, 128)), full((2 * B_HEADS, 1)),
                  full((1, B_W))],
        out_specs=[seq(B_W),
                   pl.BlockSpec((1, B_HEADS, B_HD, B_HD), lambda bi, ci: (bi, 0, 0, 0)),
                   pl.BlockSpec((1, 8, B_HD), lambda bi, ci: (bi, 0, 0)),
                   pl.BlockSpec((1, 8, 128), lambda bi, ci: (bi, 0, 0))],
        out_shape=[jax.ShapeDtypeStruct((b, s_len, B_W), F32),
                   jax.ShapeDtypeStruct((b, B_HEADS, B_HD, B_HD), F32),
                   jax.ShapeDtypeStruct((b, 8, B_HD), F32),
                   jax.ShapeDtypeStruct((b, 8, 128), F32)],
        scratch_shapes=[pltpu.VMEM((L + 8, 2 * B_W), F32), pltpu.VMEM((B_HEADS, B_HD, B_HD), F32),
                        pltpu.VMEM((8, B_HD), F32), pltpu.VMEM((8, 128), F32)],
        compiler_params=_cparams("arbitrary", "arbitrary"),
        name="mlstm_chunkwise",
    )(qk, v, ob, gcol, grow, w_conv, b_conv.reshape(1, -1), bif_col, bif_row, g_head.reshape(1, -1))


def _odd_in_kernel(x_ref, g_ref, w_ref, gcv_ref, bcv_ref, ws_ref, bs_ref, c_ref, xd_ref):
    tm = x_ref.shape[0]
    xn = _rms(x_ref[...], g_ref[...]).astype(BF16)
    u = jnp.dot(xn, w_ref[:, 0:C_W], preferred_element_type=F32)
    v = jnp.dot(xn, w_ref[:, C_W:2 * C_W], preferred_element_type=F32)
    xd_ref[...] = jnp.dot(xn, w_ref[:, 2 * C_W:], preferred_element_type=F32)
    mu = jnp.mean(v, axis=-1, keepdims=True)
    vc = v - mu
    var = jnp.mean(vc * vc, axis=-1, keepdims=True)
    vl = (vc * lax.rsqrt(var + EPS) * gcv_ref[...] + bcv_ref[...]).astype(BF16)
    ri = lax.broadcasted_iota(I32, (C_CHUNK, C_CHUNK), 0)
    cj = lax.broadcasted_iota(I32, (C_CHUNK, C_CHUNK), 1)
    for g in range(C_GROUPS):
        cols = slice(g * C_GD, (g + 1) * C_GD)
        wg = jnp.where(ri >= cj, ws_ref[g], 0.0).astype(BF16)
        for c in range(tm // C_CHUNK):
            rows = slice(c * C_CHUNK, (c + 1) * C_CHUNK)
            s = jnp.dot(wg, vl[rows, cols], preferred_element_type=F32) + bs_ref[:, g:g + 1]
            c_ref[rows, cols] = u[rows, cols] * s


def _odd_in(x, g, w, g_cv, b_cv, w_s, b_s, tm):
    m, d = x.shape
    n = w.shape[1]
    full = lambda shape: pl.BlockSpec(shape, lambda i: (0,) * len(shape))
    return pl.pallas_call(
        _odd_in_kernel,
        grid=(m // tm,),
        in_specs=[pl.BlockSpec((tm, d), lambda i: (i, 0)), full((1, d)), full((d, n)), full((1, C_W)),
                  full((1, C_W)), full((C_GROUPS, C_CHUNK, C_CHUNK)), full((C_CHUNK, C_GROUPS))],
        out_specs=[pl.BlockSpec((tm, C_W), lambda i: (i, 0)), pl.BlockSpec((tm, D_W), lambda i: (i, 0))],
        out_shape=[jax.ShapeDtypeStruct((m, C_W), F32), jax.ShapeDtypeStruct((m, D_W), F32)],
        compiler_params=_cparams("arbitrary"),
        name="odd_in_proj_gate",
    )(x, g.reshape(1, d), w, g_cv.reshape(1, -1), b_cv.reshape(1, -1), w_s, b_s.T)


S5_TT = 128
S5_PAD = 8
S5_HALF = D_GROUPS * D_STATE // 2


def _s5_discretize(a_re, a_im, log_dt, bm_re, bm_im, cm_re, cm_im):
    a_re, a_im = a_re.astype(F32), a_im.astype(F32)
    dt = jnp.exp(log_dt.astype(F32))[:, None]
    mag = jnp.exp(a_re * dt)
    ab_re = mag * jnp.cos(a_im * dt)
    ab_im = mag * jnp.sin(a_im * dt)
    inv = 1.0 / (a_re * a_re + a_im * a_im)
    f_re = ((ab_re - 1.0) * a_re + ab_im * a_im) * inv
    f_im = (ab_im * a_re - (ab_re - 1.0) * a_im) * inv
    bm_re, bm_im = bm_re.astype(F32), bm_im.astype(F32)
    bb_re = f_re[..., None] * bm_re - f_im[..., None] * bm_im
    bb_im = f_re[..., None] * bm_im + f_im[..., None] * bm_re
    return ab_re, ab_im, bb_re, bb_im


def _s5_matrices(bb_re, bb_im, cm_re, cm_im):
    gh = D_GROUPS // 2
    eye = jnp.eye(gh, dtype=F32)

    def in_half(bb):
        return jnp.einsum('gpc,gh->gchp', bb, eye).reshape(gh * D_GCH, gh * D_STATE)

    def out_half(cm):
        return jnp.einsum('gcp,gh->gphc', cm, eye).reshape(gh * D_STATE, gh * D_GCH)

    bd = jnp.stack([jnp.concatenate([in_half(bb_re[h * gh:(h + 1) * gh]), in_half(bb_im[h * gh:(h + 1) * gh])], axis=1)
                    for h in range(2)])
    cm = jnp.stack([jnp.concatenate([out_half(cm_re[h * gh:(h + 1) * gh].astype(F32)),
                                     -out_half(cm_im[h * gh:(h + 1) * gh].astype(F32))], axis=0)
                    for h in range(2)])
    return bd, cm


def _gelu_tanh(x):
    return 0.5 * x * (1.0 + jnp.tanh(math.sqrt(2.0 / math.pi) * (x + 0.044715 * (x * x * x))))


def _s5_kernel(x_ref, bd_ref, cm_ref, ar_ref, ai_ref, dsk_ref, wg_ref, bg_ref, o_ref, hr_out, hi_out,
               bu_ref, hs_ref, hr_ref, hi_ref):
    i = pl.program_id(0)
    nb, tt, _ = x_ref.shape
    stride = tt + S5_PAD
    half_in = D_W // 2
    nct = S5_HALF // 128

    @pl.when(i == 0)
    def _():
        hr_ref[...] = jnp.zeros_like(hr_ref)
        hi_ref[...] = jnp.zeros_like(hi_ref)

    x = x_ref[...].reshape(nb * tt, D_W)
    ys = []
    for h in range(2):
        lanes = slice(h * S5_HALF, (h + 1) * S5_HALF)
        bu = jnp.dot(x[:, h * half_in:(h + 1) * half_in].astype(BF16), bd_ref[h], preferred_element_type=F32)
        for c in range(2 * nct):
            for b in range(nb):
                bu_ref[c, b * stride:b * stride + tt, :] = bu[b * tt:(b + 1) * tt, c * 128:(c + 1) * 128]
        ar = jnp.broadcast_to(ar_ref[:, lanes], (nb, S5_HALF))
        ai = jnp.broadcast_to(ai_ref[:, lanes], (nb, S5_HALF))

        def step(t, carry):
            hr, hi = carry
            rows = pl.ds(t, nb, stride=stride)
            bur = jnp.concatenate([bu_ref[c, rows, :] for c in range(nct)], axis=1)
            bui = jnp.concatenate([bu_ref[nct + c, rows, :] for c in range(nct)], axis=1)
            nhr = ar * hr - ai * hi + bur
            nhi = ar * hi + ai * hr + bui
            for c in range(nct):
                hs_ref[c, rows, :] = nhr[:, c * 128:(c + 1) * 128]
                hs_ref[nct + c, rows, :] = nhi[:, c * 128:(c + 1) * 128]
            return nhr, nhi

        hr, hi = lax.fori_loop(0, tt, step, (hr_ref[:, lanes], hi_ref[:, lanes]))
        hr_ref[:, lanes] = hr
        hi_ref[:, lanes] = hi
        hs = jnp.concatenate(
            [jnp.concatenate([hs_ref[c, b * stride:b * stride + tt, :] for b in range(nb)], axis=0)
             for c in range(2 * nct)], axis=1)
        ys.append(jnp.dot(hs.astype(BF16), cm_ref[h], preferred_element_type=F32))
    y = jnp.concatenate(ys, axis=-1) + dsk_ref[...] * x
    g = _gelu_tanh(y)
    d_out = g * jax.nn.sigmoid(jnp.dot(g.astype(BF16), wg_ref[...], preferred_element_type=F32) + bg_ref[...])
    o_ref[...] = d_out.reshape(nb, tt, D_W)

    @pl.when(i == pl.num_programs(0) - 1)
    def _():
        hr_out[...] = hr_ref[...]
        hi_out[...] = hi_ref[...]


def _s5_prompt(xd, bd, cm, ab_re, ab_im, d_skip, w_glu, b_glu):
    nb, t_len, _ = xd.shape
    tt = min(S5_TT, t_len)
    n_state = D_GROUPS * D_STATE
    full = lambda shape: pl.BlockSpec(shape, lambda i: (0,) * len(shape))
    return pl.pallas_call(
        _s5_kernel,
        grid=(t_len // tt,),
        in_specs=[pl.BlockSpec((nb, tt, D_W), lambda i: (0, i, 0)), full(bd.shape), full(cm.shape),
                  full((1, n_state)), full((1, n_state)), full((1, D_W)), full((D_W, D_W)), full((1, D_W))],
        out_specs=[pl.BlockSpec((nb, tt, D_W), lambda i: (0, i, 0)), full((nb, n_state)), full((nb, n_state))],
        out_shape=[jax.ShapeDtypeStruct(xd.shape, F32), jax.ShapeDtypeStruct((nb, n_state), F32),
                   jax.ShapeDtypeStruct((nb, n_state), F32)],
        scratch_shapes=[pltpu.VMEM((2 * S5_HALF // 128, nb * (tt + S5_PAD), 128), F32),
                        pltpu.VMEM((2 * S5_HALF // 128, nb * (tt + S5_PAD), 128), F32),
                        pltpu.VMEM((nb, n_state), F32), pltpu.VMEM((nb, n_state), F32)],
        compiler_params=_cparams("arbitrary"),
        name="s5_scan_glu",
    )(xd, bd, cm, ab_re.reshape(1, n_state), ab_im.reshape(1, n_state), d_skip.reshape(1, D_W), w_glu,
      b_glu.reshape(1, D_W))


def _hdot(a, b):
    return jnp.dot(a, b, precision=HIGHEST, preferred_element_type=F32)


def _sample_even_mixers(x, cache_k, cache_v, c0, n0, m0, conv0, rel_bias, g_mix, w_in, b_if, w_conv, b_conv,
                        g_head):
    n = x.shape[0]
    h = _rms(x, g_mix)
    proj = _hdot(h, w_in)
    qa, ka, va, qkb, vb, ob, gates = jnp.split(
        proj, [A_W, 2 * A_W, 3 * A_W, 3 * A_W + 2 * B_W, 3 * A_W + 3 * B_W, 3 * A_W + 4 * B_W], axis=-1)
    L = cache_k.shape[1]
    k_all = jnp.concatenate([cache_k, ka.reshape(n, 1, A_HEADS, A_HD)], axis=1)
    v_all = jnp.concatenate([cache_v, va.reshape(n, 1, A_HEADS, A_HD)], axis=1)
    tab = _distance_logit_table(rel_bias, L)[:, ::-1]
    s = jnp.einsum('nhd,njhd->nhj', qa.reshape(n, A_HEADS, A_HD), k_all, precision=HIGHEST) * (A_HD ** -0.5)
    p = jax.nn.softmax(s + tab[None], axis=-1)
    a_out = jnp.einsum('nhj,njhd->nhd', p, v_all, precision=HIGHEST).reshape(n, A_W)
    new_k, new_v = k_all[:, 1:], v_all[:, 1:]
    xp = jnp.concatenate([conv0, qkb[:, None, :]], axis=1)
    qk = _silu(b_conv + jnp.sum(xp * w_conv[None], axis=1))
    q = qk[:, :B_W].reshape(n, B_HEADS, B_HD)
    k = qk[:, B_W:].reshape(n, B_HEADS, B_HD) * (B_HD ** -0.5)
    v = vb.reshape(n, B_HEADS, B_HD)
    ig = gates[:, :B_HEADS] + b_if[:B_HEADS]
    logf = _log_sigmoid(gates[:, B_HEADS:] + b_if[B_HEADS:])
    inter = logf + m0
    mt = jnp.maximum(inter, ig)
    w_intra = jnp.exp(ig - mt)
    w_inter = jnp.exp(inter - mt)
    a = w_intra * jnp.sum(q * k, axis=-1)
    num = a[..., None] * v + w_inter[..., None] * jnp.einsum('nhk,nhkv->nhv', q, c0, precision=HIGHEST)
    den = a + w_inter * jnp.sum(q * n0, axis=-1)
    hb = num / jnp.maximum(jnp.abs(den), jnp.exp(-mt))[..., None]
    g = jnp.exp(ig - mt)
    decay = jnp.exp(inter - mt)
    c_new = decay[..., None, None] * c0 + (g[..., None] * k)[..., :, None] * v[..., None, :]
    n_new = decay[..., None] * n0 + g[..., None] * k
    hb = hb * lax.rsqrt(jnp.mean(hb * hb, axis=-1, keepdims=True) + EPS) * g_head.reshape(B_HEADS, B_HD)
    b_out = jax.nn.sigmoid(ob) * hb.reshape(n, B_W)
    return a_out, b_out, new_k, new_v, c_new, n_new, mt, xp[:, 1:]


def _sample_odd_mixers(x, h_re0, h_im0, g_mix, w_in, g_cv, b_cv, w_s, b_s, ab_re, ab_im, bb_re, bb_im,
                       cm_re, cm_im, d_skip, w_glu, b_glu):
    n = x.shape[0]
    h = _rms(x, g_mix)
    proj = _hdot(h, w_in)
    u, v, xd = proj[:, :C_W], proj[:, C_W:2 * C_W], proj[:, 2 * C_W:]
    mu = jnp.mean(v, axis=-1, keepdims=True)
    vc = v - mu
    v = vc * lax.rsqrt(jnp.mean(vc * vc, axis=-1, keepdims=True) + EPS) * g_cv + b_cv
    s = (w_s[:, 0, 0][None, :, None] * v.reshape(n, C_GROUPS, C_GD) + b_s[:, 0][None, :, None]).reshape(n, C_W)
    c_out = u * s
    xg = xd.reshape(n, D_GROUPS, D_GCH)
    bu_re = jnp.einsum('gpc,ngc->ngp', bb_re, xg, precision=HIGHEST)
    bu_im = jnp.einsum('gpc,ngc->ngp', bb_im, xg, precision=HIGHEST)
    hr = ab_re * h_re0 - ab_im * h_im0 + bu_re
    hi = ab_re * h_im0 + ab_im * h_re0 + bu_im
    y = (jnp.einsum('gcp,ngp->ngc', cm_re, hr, precision=HIGHEST)
         - jnp.einsum('gcp,ngp->ngc', cm_im, hi, precision=HIGHEST) + d_skip * xg)
    g = _gelu_tanh(y.reshape(n, D_W))
    d_out = g * jax.nn.sigmoid(_hdot(g, w_glu) + b_glu)
    return c_out, d_out, v, hr, hi


_EVEN_SEGS = ((0, A_W), (A_W, A_W), (2 * A_W, A_W), (3 * A_W, 2 * B_W), (3 * A_W + 2 * B_W, B_W),
              (3 * A_W + 3 * B_W, B_W), (3 * A_W + 4 * B_W, 128))


def kernel(x_prompt, x_sample, cache_a_k, cache_a_v, state_b_c, state_b_n, state_b_m, state_b_conv,
           state_d_re, state_d_im, rel_bias, g_mix, g_ffn, g_final, w_in_e, b_if, w_conv_b, b_conv_b,
           g_bhead, w_out_e, w1_e, w3_e, w2_e, w_in_o, g_cv, b_cv, w_s, b_s, a_re, a_im, log_dt,
           bm_re, bm_im, cm_re, cm_im, d_skip, w_glu, b_glu, w_out_o, w_router, b_router,
           w1_m, w3_m, w2_m):
    bp, sp, d = x_prompt.shape
    ns = x_sample.shape[0]
    mp = bp * sp
    xp = x_prompt.reshape(mp, d)
    xs = x_sample.reshape(ns, d)

    w_in = w_in_e[0]
    n_main = 3 * A_W + 4 * B_W
    w_gate = w_in[:, n_main:]
    w_cat = jnp.concatenate([w_in[:, :n_main], jnp.pad(w_gate, ((0, 0), (0, 128 - 2 * B_HEADS)))], axis=1)
    q, k, v, qkb, vb, ob, gcol, grow = _norm_proj(
        xp, g_mix[0], w_cat.astype(BF16), w_gate.T.astype(BF16), _EVEN_SEGS, tm=512)
    shp = lambda a: a.reshape(bp, sp, a.shape[-1])
    a_out = _attention_prompt(shp(q), shp(k), shp(v), rel_bias)
    b_out, pc, pn, pm = _mlstm_prompt(shp(qkb), shp(vb), shp(ob), shp(gcol), grow, w_conv_b[0], b_conv_b[0],
                                      b_if[0], g_bhead[0])
    x1p = _mix_ffn(a_out.reshape(mp, A_W), b_out.reshape(mp, B_W), xp, w_out_e[0].astype(BF16), g_ffn[0],
                   w1_e[0].astype(BF16), w3_e[0].astype(BF16), w2_e[0].astype(BF16), tm=1024, tf=512)

    sa, sb, s_k, s_v, sc, sn, sm, s_conv = _sample_even_mixers(
        xs, cache_a_k[0], cache_a_v[0], state_b_c[0], state_b_n[0], state_b_m[0], state_b_conv[0], rel_bias,
        g_mix[0], w_in, b_if[0], w_conv_b[0], b_conv_b[0], g_bhead[0])
    x1s = _mix_ffn(sa, sb, xs, w_out_e[0], g_ffn[0], w1_e[0], w3_e[0], w2_e[0], tm=ns, tf=512, precise=True)

    ab_re, ab_im, bb_re, bb_im = _s5_discretize(a_re[0], a_im[0], log_dt[0], bm_re[0], bm_im[0], cm_re[0],
                                                cm_im[0])
    bd, cmat = _s5_matrices(bb_re, bb_im, cm_re[0], cm_im[0])
    c_out, xd = _odd_in(x1p, g_mix[1], w_in_o[0].astype(BF16), g_cv[0], b_cv[0], w_s[0], b_s[0], tm=512)
    d_out, p_hr, p_hi = _s5_prompt(xd.reshape(bp, sp, D_W), bd.astype(BF16), cmat.astype(BF16), ab_re, ab_im,
                                   d_skip[0], w_glu[0].astype(BF16), b_glu[0])
    wr = jnp.pad(w_router[0].astype(F32), ((0, 0), (0, 128 - N_EXPERTS)))
    br = jnp.pad(b_router[0].astype(F32), (0, 128 - N_EXPERTS)).reshape(1, 128)
    x2p, hnp, idp, gtp = _mix_router(c_out, d_out.reshape(mp, D_W), x1p, w_out_o[0].astype(BF16), g_ffn[1], wr, br,
                                     tm=512)

    sc_out, sd_out, s_cv, s_hr, s_hi = _sample_odd_mixers(
        x1s, state_d_re[0], state_d_im[0], g_mix[1], w_in_o[0], g_cv[0], b_cv[0], w_s[0], b_s[0], ab_re, ab_im,
        bb_re, bb_im, cm_re[0], cm_im[0], d_skip[0], w_glu[0], b_glu[0])
    x2s, hns, ids_s, gts = _mix_router(sc_out, sd_out, x1s, w_out_o[0], g_ffn[1], wr, br, tm=ns, precise=True)

    mt = -(-(mp + ns) // ROUTE_TILE) * ROUTE_TILE
    pad = mt - mp - ns
    x2 = jnp.concatenate([x2p, x2s, jnp.zeros((pad, d), F32)])
    hn = jnp.concatenate([hnp, hns, jnp.zeros((pad, d), BF16)])
    ids = jnp.concatenate([idp, ids_s, jnp.full((pad, 2), -1, I32)])
    gts_all = jnp.concatenate([gtp, gts, jnp.zeros((pad, 2), F32)])
    y = _moe_block(x2, hn, ids, gts_all, w1_m[0].astype(BF16), w3_m[0].astype(BF16), w2_m[0].astype(BF16),
                   g_final)
    y_prompt = y[:mp].reshape(bp, sp, d)
    y_sample = y[mp:mp + ns].reshape(ns, 1, d)

    e = lambda a: a[None]
    p_k = k.reshape(bp, sp, A_HEADS, A_HD)
    p_v = v.reshape(bp, sp, A_HEADS, A_HD)
    p_conv = qkb.reshape(bp, sp, 2 * B_W)[:, sp - (B_CONV - 1):]
    return (y_prompt, y_sample, e(p_k), e(p_v), e(s_k), e(s_v), e(pc), e(sc), e(pn[:, :B_HEADS]), e(sn),
            e(pm[:, :B_HEADS, 0]), e(sm), e(p_conv), e(s_conv), e(s_cv.reshape(ns, 1, C_W)),
            e(p_hr.reshape(bp, D_GROUPS, D_STATE)), e(s_hr), e(p_hi.reshape(bp, D_GROUPS, D_STATE)), e(s_hi))
```

```python
import functools
import math

import jax
import jax.numpy as jnp
from jax import lax
from jax.experimental import pallas as pl
from jax.experimental.pallas import tpu as pltpu

F32 = jnp.float32
BF16 = jnp.bfloat16
I32 = jnp.int32
HIGHEST = lax.Precision.HIGHEST

D_MODEL = 1024
A_HEADS, A_HD = 8, 64
A_W = A_HEADS * A_HD
A_BRANCHES = ((128, 1), (512, 4), (2048, 16))
A_WIN = 2048
N_BUCKETS = 32
B_HEADS, B_HD = 4, 128
B_W = B_HEADS * B_HD
B_CONV = 4
C_GROUPS, C_GD = 4, 128
C_W = C_GROUPS * C_GD
C_CHUNK = 128
D_GROUPS, D_GCH, D_STATE = 32, 16, 64
D_W = D_GROUPS * D_GCH
D_FF = 3584
N_EXPERTS = 8
TOP_K = 2
EPS = 1e-6
NEG = -1e30

VMEM_LIMIT = 56 * 1024 * 1024

ROUTE_TILE = 256
FFN_TILE = 512
COMBINE_ALIGN = 16
COMBINE_WIN = ROUTE_TILE + COMBINE_ALIGN


def _cparams(*sem):
    return pltpu.CompilerParams(dimension_semantics=sem, vmem_limit_bytes=VMEM_LIMIT)


def _moe_plan(ids, n_tok_tiles):
    mt = ids.shape[0]
    n_assign = 2 * mt
    n_ffn_tiles = -(-(n_assign + N_EXPERTS * (FFN_TILE - 1)) // FFN_TILE) + 1
    n_sub = n_ffn_tiles * (FFN_TILE // ROUTE_TILE)
    flat = ids.reshape(n_assign)
    onehot = (flat[:, None] == jnp.arange(N_EXPERTS, dtype=I32)[None, :]).astype(I32)
    csum = jnp.cumsum(onehot, axis=0)
    rank = jnp.sum((csum - onehot) * onehot, axis=1)
    counts = csum[-1]
    seg = ((counts + FFN_TILE - 1) // FFN_TILE) * FFN_TILE
    seg_end = jnp.cumsum(seg)
    seg_off = seg_end - seg
    dest = jnp.where(flat >= 0, seg_off[jnp.clip(flat, 0)] + rank, -1).reshape(mt, 2)
    total = seg_end[-1]
    tile_start = jnp.arange(n_ffn_tiles, dtype=I32) * FFN_TILE
    tile_expert = jnp.minimum(jnp.sum((tile_start[:, None] >= seg_end[None, :]).astype(I32), axis=1),
                              N_EXPERTS - 1).astype(I32)
    n_used = (total // FFN_TILE).astype(I32)
    per_tile = 2 * ROUTE_TILE
    cnt_tile = jnp.concatenate([jnp.zeros((1, N_EXPERTS), I32), csum[per_tile - 1::per_tile]])
    lo = seg_off[None, :] + cnt_tile[:-1]
    hi = seg_off[None, :] + cnt_tile[1:]
    sub_start = jnp.arange(n_sub, dtype=I32) * ROUTE_TILE
    sub_expert = jnp.minimum(jnp.sum((sub_start[:, None] >= seg_end[None, :]).astype(I32), axis=1),
                             N_EXPERTS - 1)
    lo_e, hi_e = lo.T[sub_expert], hi.T[sub_expert]
    jlo = jnp.sum((hi_e <= sub_start[:, None]).astype(I32), axis=1).astype(I32)
    jhi = (jnp.sum((lo_e < sub_start[:, None] + ROUTE_TILE).astype(I32), axis=1) - 1).astype(I32)
    win = lo // COMBINE_ALIGN
    flat1 = lambda a: a.reshape(-1).astype(I32)
    return dest, tile_expert, n_used, (jlo, jhi), (flat1(win), flat1(lo), flat1(hi)), n_ffn_tiles


def _moe_gather_kernel(jlo_ref, jhi_ref, hp_hbm, hs_hbm, dt_ref, gt_ref, xs_ref, gs_ref,
                       hn_ref, acc_ref, g_ref, sem):
    s = pl.program_id(0)
    mp = hp_hbm.shape[0]

    @pl.when(s == 0)
    def _():
        copies = (pltpu.make_async_copy(hp_hbm, hn_ref.at[0:mp], sem.at[0]),
                  pltpu.make_async_copy(hs_hbm, hn_ref.at[mp:mp + ROUTE_TILE], sem.at[1]))
        for c in copies:
            c.start()
        for c in copies:
            c.wait()

    acc_ref[...] = jnp.zeros_like(acc_ref)
    g_ref[...] = jnp.zeros_like(g_ref)
    rows = lax.broadcasted_iota(I32, (ROUTE_TILE, ROUTE_TILE), 0) + s * ROUTE_TILE

    def body(j, carry):
        off = pl.multiple_of(j * ROUTE_TILE, ROUTE_TILE)
        dt = dt_ref[:, pl.ds(off, ROUTE_TILE)]
        gt = gt_ref[:, pl.ds(off, ROUTE_TILE)]
        m1 = dt[0:1, :] == rows
        m2 = dt[1:2, :] == rows
        onehot = jnp.where(m1 | m2, 1.0, 0.0).astype(BF16)
        acc_ref[...] += jnp.dot(onehot, hn_ref[pl.ds(off, ROUTE_TILE), :], preferred_element_type=F32)
        g_ref[...] += jnp.sum(jnp.where(m1, gt[0:1, :], 0.0) + jnp.where(m2, gt[1:2, :], 0.0),
                              axis=1, keepdims=True)
        return carry

    lax.fori_loop(jlo_ref[s], jhi_ref[s] + 1, body, 0)
    xs_ref[...] = acc_ref[...].astype(BF16)
    gs_ref[...] = g_ref[...]


def _moe_gather(hn_p, hn_s, dest_t, gates_t, glist, n_sub):
    jlo, jhi = glist
    mp, d = hn_p.shape
    mt = dest_t.shape[1]
    grid_spec = pltpu.PrefetchScalarGridSpec(
        num_scalar_prefetch=2,
        grid=(n_sub,),
        in_specs=[
            pl.BlockSpec(memory_space=pl.ANY),
            pl.BlockSpec(memory_space=pl.ANY),
            pl.BlockSpec((2, mt), lambda s, *_: (0, 0)),
            pl.BlockSpec((2, mt), lambda s, *_: (0, 0)),
        ],
        out_specs=[
            pl.BlockSpec((ROUTE_TILE, d), lambda s, *_: (s, 0)),
            pl.BlockSpec((ROUTE_TILE, 1), lambda s, *_: (s, 0)),
        ],
        scratch_shapes=[pltpu.VMEM((mp + ROUTE_TILE, d), BF16), pltpu.VMEM((ROUTE_TILE, d), F32),
                        pltpu.VMEM((ROUTE_TILE, 1), F32), pltpu.SemaphoreType.DMA((2,))],
    )
    return pl.pallas_call(
        _moe_gather_kernel,
        grid_spec=grid_spec,
        out_shape=[jax.ShapeDtypeStruct((n_sub * ROUTE_TILE, d), BF16),
                   jax.ShapeDtypeStruct((n_sub * ROUTE_TILE, 1), F32)],
        compiler_params=_cparams("arbitrary"),
        name="moe_gather",
    )(jlo, jhi, hn_p, hn_s, dest_t, gates_t)


def _moe_ffn_kernel(te_ref, nu_ref, x_ref, gs_ref, w1_ref, w3_ref, w2_ref, y_ref, acc_ref):
    t, f = pl.program_id(0), pl.program_id(1)
    nf = pl.num_programs(1)

    @pl.when(t < nu_ref[0])
    def _():
        x = x_ref[...]
        a = jnp.dot(x, w1_ref[0], preferred_element_type=F32)
        b = jnp.dot(x, w3_ref[0], preferred_element_type=F32)
        hmid = (a * jax.nn.sigmoid(a) * b).astype(BF16)
        part = jnp.dot(hmid, w2_ref[0], preferred_element_type=F32)

        @pl.when(f == 0)
        def _():
            acc_ref[...] = part

        @pl.when(f > 0)
        def _():
            acc_ref[...] += part

        @pl.when(f == nf - 1)
        def _():
            y_ref[...] = (acc_ref[...] * gs_ref[...]).astype(BF16)

    @pl.when((t >= nu_ref[0]) & (f == nf - 1))
    def _():
        y_ref[...] = jnp.zeros_like(y_ref)


def _moe_ffn(xs, gs, w1, w3, w2, tile_expert, n_used, n_ffn_tiles, tf):
    d = xs.shape[1]
    ff = w1.shape[2]
    nf = ff // tf

    def fidx(t, f, nu):
        return jnp.where(t < nu[0], f, nf - 1)

    grid_spec = pltpu.PrefetchScalarGridSpec(
        num_scalar_prefetch=2,
        grid=(n_ffn_tiles, nf),
        in_specs=[
            pl.BlockSpec((FFN_TILE, d), lambda t, f, te, nu: (t, 0)),
            pl.BlockSpec((FFN_TILE, 1), lambda t, f, te, nu: (t, 0)),
            pl.BlockSpec((1, d, tf), lambda t, f, te, nu: (te[t], 0, fidx(t, f, nu))),
            pl.BlockSpec((1, d, tf), lambda t, f, te, nu: (te[t], 0, fidx(t, f, nu))),
            pl.BlockSpec((1, tf, d), lambda t, f, te, nu: (te[t], fidx(t, f, nu), 0)),
        ],
        out_specs=pl.BlockSpec((FFN_TILE, d), lambda t, f, te, nu: (t, 0)),
        scratch_shapes=[pltpu.VMEM((FFN_TILE, d), F32)],
    )
    return pl.pallas_call(
        _moe_ffn_kernel,
        grid_spec=grid_spec,
        out_shape=jax.ShapeDtypeStruct(xs.shape, BF16),
        compiler_params=_cparams("arbitrary", "arbitrary"),
        name="moe_ffn",
    )(tile_expert, n_used.reshape(1), xs, gs, w1, w3, w2)


def _moe_combine_kernel(win_ref, lo_ref, hi_ref, xp_ref, xs_ref, d_ref, g_ref, *rest):
    ys_refs = rest[:N_EXPERTS]
    yp_ref, ysm_ref, acc_ref = rest[N_EXPERTS:]
    j = pl.program_id(0)
    n_prompt_tiles = pl.num_programs(0) - 1

    @pl.when(j < n_prompt_tiles)
    def _():
        acc_ref[...] = xp_ref[...]

    @pl.when(j == n_prompt_tiles)
    def _():
        acc_ref[...] = xs_ref[...]

    for e in range(N_EXPERTS):
        lo, hi = lo_ref[j * N_EXPERTS + e], hi_ref[j * N_EXPERTS + e]

        @pl.when(hi > lo)
        def _():
            d = d_ref[...]
            d = jnp.where((d >= lo) & (d < hi), d, -1)
            cols = (lax.broadcasted_iota(I32, (ROUTE_TILE, COMBINE_WIN), 1)
                    + win_ref[j * N_EXPERTS + e] * COMBINE_ALIGN)
            onehot = jnp.where((d[:, 0:1] == cols) | (d[:, 1:2] == cols), 1.0, 0.0).astype(BF16)
            acc_ref[...] += jnp.dot(onehot, ys_refs[e][...], preferred_element_type=F32)

    y = _rms(acc_ref[...], g_ref[...])

    @pl.when(j < n_prompt_tiles)
    def _():
        yp_ref[...] = y

    @pl.when(j == n_prompt_tiles)
    def _():
        ysm_ref[...] = y


def _moe_combine(x2_p, x2_s, dest, ys, g_final, clist):
    win, lo, hi = clist
    mp, d = x2_p.shape
    n_prompt_tiles = mp // ROUTE_TILE

    def ptile(j, *_):
        return (jnp.minimum(j, n_prompt_tiles - 1), 0)

    def window(e):
        return pl.BlockSpec((pl.Element(COMBINE_WIN), pl.Element(d)),
                            lambda j, win, lo, hi: (win[j * N_EXPERTS + e] * COMBINE_ALIGN, 0))

    grid_spec = pltpu.PrefetchScalarGridSpec(
        num_scalar_prefetch=3,
        grid=(n_prompt_tiles + 1,),
        in_specs=[
            pl.BlockSpec((ROUTE_TILE, d), ptile),
            pl.BlockSpec((ROUTE_TILE, d), lambda j, *_: (0, 0)),
            pl.BlockSpec((ROUTE_TILE, 2), lambda j, *_: (j, 0)),
            pl.BlockSpec((1, d), lambda j, *_: (0, 0)),
        ] + [window(e) for e in range(N_EXPERTS)],
        out_specs=[pl.BlockSpec((ROUTE_TILE, d), ptile), pl.BlockSpec((ROUTE_TILE, d), lambda j, *_: (0, 0))],
        scratch_shapes=[pltpu.VMEM((ROUTE_TILE, d), F32)],
    )
    return pl.pallas_call(
        _moe_combine_kernel,
        grid_spec=grid_spec,
        out_shape=[jax.ShapeDtypeStruct((mp, d), F32), jax.ShapeDtypeStruct((ROUTE_TILE, d), F32)],
        compiler_params=_cparams("arbitrary"),
        name="moe_combine",
    )(win, lo, hi, x2_p, x2_s, dest, g_final.reshape(1, d), *([ys] * N_EXPERTS))


def _moe_block(x2_p, hn_p, ids_p, gates_p, x2_s, hn_s, ids_s, gates_s, w1, w3, w2, g_final, tf=512):
    mp = x2_p.shape[0]
    n_tok_tiles = mp // ROUTE_TILE + 1
    ids = jnp.concatenate([ids_p, ids_s])
    gates = jnp.concatenate([gates_p, gates_s])
    dest, tile_expert, n_used, glist, clist, n_ffn_tiles = _moe_plan(ids, n_tok_tiles)
    n_sub = n_ffn_tiles * (FFN_TILE // ROUTE_TILE)
    xs, gs = _moe_gather(hn_p, hn_s, dest.T, gates.T, glist, n_sub)
    ys = _moe_ffn(xs, gs, w1, w3, w2, tile_expert, n_used, n_ffn_tiles, tf)
    return _moe_combine(x2_p, x2_s, dest, ys, g_final, clist)


def _rms(x, g):
    return x * lax.rsqrt(jnp.mean(x * x, axis=-1, keepdims=True) + EPS) * g


def _mm(a, b, precise):
    if precise:
        return jnp.dot(a.astype(F32), b.astype(F32), preferred_element_type=F32, precision=HIGHEST)
    return jnp.dot(a.astype(BF16), b.astype(BF16), preferred_element_type=F32)


def _silu(x):
    return x * jax.nn.sigmoid(x)


def _norm_proj_kernel(x_ref, g_ref, w_ref, wt_ref, *out_refs, segs, precise):
    xn = _rms(x_ref[...], g_ref[...])
    xn = xn if precise else xn.astype(BF16)
    for (start, width), o_ref in zip(segs, out_refs[:-1]):
        for c in range(0, width, 512):
            cw = min(512, width - c)
            o_ref[:, c:c + cw] = _mm(xn, w_ref[:, start + c:start + c + cw], precise)
    wt = wt_ref[...]
    out_refs[-1][...] = lax.dot_general(
        wt.astype(xn.dtype), xn, (((1,), (1,)), ((), ())), preferred_element_type=F32,
        precision=HIGHEST if precise else None)


def _norm_proj(x, g, w, wt, segs, tm, precise=False):
    m, d = x.shape
    n = w.shape[1]
    nt = wt.shape[0]
    out_shape = [jax.ShapeDtypeStruct((m, width), F32) for _, width in segs]
    out_shape.append(jax.ShapeDtypeStruct((nt, m), F32))
    out_specs = [pl.BlockSpec((tm, width), lambda i: (i, 0)) for _, width in segs]
    out_specs.append(pl.BlockSpec((nt, tm), lambda i: (0, i)))
    return pl.pallas_call(
        functools.partial(_norm_proj_kernel, segs=tuple(segs), precise=precise),
        grid=(m // tm,),
        in_specs=[
            pl.BlockSpec((tm, d), lambda i: (i, 0)),
            pl.BlockSpec((1, d), lambda i: (0, 0)),
            pl.BlockSpec((d, n), lambda i: (0, 0)),
            pl.BlockSpec((nt, d), lambda i: (0, 0)),
        ],
        out_specs=out_specs,
        out_shape=out_shape,
        compiler_params=_cparams("arbitrary"),
        name="norm_proj",
    )(x, g.reshape(1, d), w, wt)


def _mix_ffn_kernel(a_ref, b_ref, x_ref, wo_ref, g_ref, w1_ref, w3_ref, w2_ref, o_ref,
                    x1_ref, hn_ref, acc_ref, *, precise):
    f = pl.program_id(1)
    wa = a_ref.shape[1]

    @pl.when(f == 0)
    def _():
        x1 = (x_ref[...] + _mm(a_ref[...], wo_ref[0:wa, :], precise)
              + _mm(b_ref[...], wo_ref[wa:, :], precise))
        x1_ref[...] = x1
        hn_ref[...] = _rms(x1, g_ref[...]).astype(hn_ref.dtype)

    hn = hn_ref[...]
    part = _mm(_silu(_mm(hn, w1_ref[...], precise)) * _mm(hn, w3_ref[...], precise), w2_ref[...], precise)

    @pl.when(f == 0)
    def _():
        acc_ref[...] = part

    @pl.when(f > 0)
    def _():
        acc_ref[...] += part

    @pl.when(f == pl.num_programs(1) - 1)
    def _():
        o_ref[...] = x1_ref[...] + acc_ref[...]


def _mix_ffn(a, b, x, w_out, g, w1, w3, w2, tm, tf, precise=False):
    m, d = x.shape
    wa, wb = a.shape[1], b.shape[1]
    ff = w1.shape[1]
    return pl.pallas_call(
        functools.partial(_mix_ffn_kernel, precise=precise),
        grid=(m // tm, ff // tf),
        in_specs=[
            pl.BlockSpec((tm, wa), lambda i, f: (i, 0)),
            pl.BlockSpec((tm, wb), lambda i, f: (i, 0)),
            pl.BlockSpec((tm, d), lambda i, f: (i, 0)),
            pl.BlockSpec((wa + wb, d), lambda i, f: (0, 0)),
            pl.BlockSpec((1, d), lambda i, f: (0, 0)),
            pl.BlockSpec((d, tf), lambda i, f: (0, f)),
            pl.BlockSpec((d, tf), lambda i, f: (0, f)),
            pl.BlockSpec((tf, d), lambda i, f: (f, 0)),
        ],
        out_specs=pl.BlockSpec((tm, d), lambda i, f: (i, 0)),
        out_shape=jax.ShapeDtypeStruct((m, d), F32),
        scratch_shapes=[pltpu.VMEM((tm, d), F32), pltpu.VMEM((tm, d), F32 if precise else BF16),
                        pltpu.VMEM((tm, d), F32)],
        compiler_params=_cparams("arbitrary", "arbitrary"),
        name="mix_ffn",
    )(a, b, x, w_out, g.reshape(1, d), w1, w3, w2)


def _mix_router_kernel(a_ref, b_ref, x_ref, wo_ref, g_ref, wr_ref, br_ref, x2_ref, hn_ref, ids_ref, gate_ref,
                       *, precise):
    wa = a_ref.shape[1]
    x2 = (x_ref[...] + _mm(a_ref[...], wo_ref[0:wa, :], precise)
          + _mm(b_ref[...], wo_ref[wa:, :], precise))
    x2_ref[...] = x2
    hn = _rms(x2, g_ref[...])
    hn_ref[...] = hn.astype(BF16)
    logits = jnp.dot(hn, wr_ref[...], preferred_element_type=F32, precision=HIGHEST)
    lane = lax.broadcasted_iota(I32, logits.shape, 1)
    real = lane < N_EXPERTS
    biased = jnp.where(real, logits + br_ref[...], -jnp.inf)
    m1 = jnp.max(biased, axis=-1, keepdims=True)
    i1 = jnp.min(jnp.where(biased == m1, lane, 128), axis=-1, keepdims=True)
    rest = jnp.where(lane == i1, -jnp.inf, biased)
    m2 = jnp.max(rest, axis=-1, keepdims=True)
    i2 = jnp.min(jnp.where(rest == m2, lane, 128), axis=-1, keepdims=True)
    l1 = jnp.sum(jnp.where(lane == i1, logits, 0.0), axis=-1, keepdims=True)
    l2 = jnp.sum(jnp.where(lane == i2, logits, 0.0), axis=-1, keepdims=True)
    mx = jnp.maximum(l1, l2)
    e1, e2 = jnp.exp(l1 - mx), jnp.exp(l2 - mx)
    two = lax.broadcasted_iota(I32, ids_ref.shape, 1)
    ids_ref[...] = jnp.where(two == 0, i1, i2)
    gate_ref[...] = jnp.where(two == 0, e1, e2) / (e1 + e2)


def _mix_router(a, b, x, w_out, g, wr, br, tm, precise=False):
    m, d = x.shape
    wa, wb = a.shape[1], b.shape[1]
    return pl.pallas_call(
        functools.partial(_mix_router_kernel, precise=precise),
        grid=(m // tm,),
        in_specs=[
            pl.BlockSpec((tm, wa), lambda i: (i, 0)),
            pl.BlockSpec((tm, wb), lambda i: (i, 0)),
            pl.BlockSpec((tm, d), lambda i: (i, 0)),
            pl.BlockSpec((wa + wb, d), lambda i: (0, 0)),
            pl.BlockSpec((1, d), lambda i: (0, 0)),
            pl.BlockSpec((d, 128), lambda i: (0, 0)),
            pl.BlockSpec((1, 128), lambda i: (0, 0)),
        ],
        out_specs=[
            pl.BlockSpec((tm, d), lambda i: (i, 0)),
            pl.BlockSpec((tm, d), lambda i: (i, 0)),
            pl.BlockSpec((tm, 2), lambda i: (i, 0)),
            pl.BlockSpec((tm, 2), lambda i: (i, 0)),
        ],
        out_shape=[jax.ShapeDtypeStruct((m, d), F32), jax.ShapeDtypeStruct((m, d), BF16),
                   jax.ShapeDtypeStruct((m, 2), I32), jax.ShapeDtypeStruct((m, 2), F32)],
        compiler_params=_cparams("arbitrary"),
        name="mix_router",
    )(a, b, x, w_out, g.reshape(1, d), wr, br)


ATT_T = 128


def _t5_bucket(dist):
    max_exact = N_BUCKETS // 2
    d = jnp.maximum(dist, 1).astype(F32)
    large = max_exact + (jnp.log(d / max_exact) / math.log(A_WIN / max_exact)
                         * (N_BUCKETS - max_exact)).astype(I32)
    return jnp.where(dist < max_exact, dist, jnp.minimum(large, N_BUCKETS - 1))


def _distance_logit_table(rel_bias, max_dist):
    dist = jnp.arange(max_dist + 1, dtype=I32)
    mult = jnp.zeros((max_dist + 1,), F32)
    for window, d in A_BRANCHES:
        mult = mult + ((dist % d == 0) & (dist <= window)).astype(F32)
    bias = rel_bias[_t5_bucket(dist)].astype(F32).T
    return jnp.where(mult[None, :] > 0, bias + jnp.log(jnp.maximum(mult, 1.0))[None, :], NEG)


def _attn_kernel(q_ref, k_ref, v_ref, tab_ref, o_ref, kb_ref, vb_ref):
    s_len = q_ref.shape[1]
    nb = s_len // ATT_T
    kb_ref[...] = k_ref[0].astype(BF16)
    vb_ref[...] = v_ref[0].astype(BF16)
    scale = A_HD ** -0.5
    for i in range(nb):
        rows = slice(i * ATT_T, (i + 1) * ATT_T)
        width = (i + 1) * ATT_T
        outs = []
        for hl in range(2):
            cols = slice(hl * A_HD, (hl + 1) * A_HD)
            q = (q_ref[0, rows, cols] * scale).astype(BF16)
            s = lax.dot_general(q, kb_ref[0:width, cols], (((1,), (1,)), ((), ())),
                                preferred_element_type=F32)
            s = s + tab_ref[hl, :, (nb - 1 - i) * ATT_T:nb * ATT_T]
            m = jnp.max(s, axis=-1, keepdims=True)
            p = jnp.exp(s - m)
            den = jnp.sum(p, axis=-1, keepdims=True)
            o = jnp.dot(p.astype(BF16), vb_ref[0:width, cols], preferred_element_type=F32)
            outs.append(o / den)
        o_ref[0, rows, :] = jnp.concatenate(outs, axis=-1)


def _attention_prompt(q, k, v, rel_bias):
    b, s_len, _ = q.shape
    nb = s_len // ATT_T
    tab1 = _distance_logit_table(rel_bias, s_len)
    period = s_len + ATT_T
    z = jnp.arange(period, dtype=I32)
    delta = jnp.where(z < s_len, (nb - 1) * ATT_T - z, (nb - 1) * ATT_T + period - z)
    row0 = jnp.where(delta[None] >= 0, tab1[:, jnp.clip(delta, 0, s_len)], NEG)
    tab = jnp.tile(row0, (1, ATT_T))[:, :ATT_T * (period - 1)].reshape(A_HEADS, ATT_T, period - 1)[:, :, :s_len]
    spec = pl.BlockSpec((1, s_len, 2 * A_HD), lambda bi, hp: (bi, 0, hp))
    return pl.pallas_call(
        _attn_kernel,
        grid=(b, A_HEADS // 2),
        in_specs=[spec, spec, spec, pl.BlockSpec((2, ATT_T, s_len), lambda bi, hp: (hp, 0, 0))],
        out_specs=spec,
        out_shape=jax.ShapeDtypeStruct(q.shape, F32),
        scratch_shapes=[pltpu.VMEM((s_len, 2 * A_HD), BF16), pltpu.VMEM((s_len, 2 * A_HD), BF16)],
        compiler_params=_cparams("arbitrary", "arbitrary"),
        name="dilated_attention",
    )(q, k, v, tab)


MLSTM_CHUNK = 256


def _log_sigmoid(x):
    return jnp.minimum(x, 0.0) - jnp.log(1.0 + jnp.exp(-jnp.abs(x)))


def _mlstm_kernel(qk_ref, v_ref, ob_ref, gc_ref, gr_ref, wc_ref, bc_ref, bifc_ref, bifr_ref, gh_ref,
                  o_ref, c_out_ref, n_out_ref, m_out_ref, xbuf_ref, c_ref, n_ref, m_ref):
    ci = pl.program_id(1)
    L = qk_ref.shape[1]

    @pl.when(ci == 0)
    def _():
        xbuf_ref[0:8, :] = jnp.zeros((8, 2 * B_W), F32)
        c_ref[...] = jnp.zeros_like(c_ref)
        n_ref[...] = jnp.zeros_like(n_ref)
        m_ref[...] = jnp.zeros_like(m_ref)

    x = qk_ref[0]
    xbuf_ref[8:8 + L, :] = x
    y = bc_ref[...] + wc_ref[3:4, :] * x
    for j in range(B_CONV - 1):
        y = y + wc_ref[j:j + 1, :] * xbuf_ref[5 + j:5 + j + L, :]
    xbuf_ref[0:8, :] = x[L - 8:L, :]
    y = _silu(y)

    gcol = gc_ref[0] + bifc_ref[...]
    grow = gr_ref[...] + bifr_ref[...]
    ri = lax.broadcasted_iota(I32, (L, L), 0)
    cj = lax.broadcasted_iota(I32, (L, L), 1)
    causal = ri >= cj
    lower = jnp.where(causal, 1.0, 0.0)
    b_col = jnp.dot(lower, _log_sigmoid(gcol), preferred_element_type=F32, precision=HIGHEST)
    b_row = lax.dot_general(_log_sigmoid(grow), lower, (((1,), (1,)), ((), ())),
                            preferred_element_type=F32, precision=HIGHEST)

    for h in range(B_HEADS):
        cols = slice(h * B_HD, (h + 1) * B_HD)
        qf = y[:, cols]
        q = qf.astype(BF16)
        kf = y[:, B_W + h * B_HD:B_W + (h + 1) * B_HD] * (B_HD ** -0.5)
        k = kf.astype(BF16)
        v = v_ref[0, :, cols].astype(BF16)
        bc = b_col[:, B_HEADS + h:B_HEADS + h + 1]
        ic = gcol[:, h:h + 1]
        br = b_row[B_HEADS + h:B_HEADS + h + 1, :]
        ir = grow[h:h + 1, :]
        m_prev = m_ref[h:h + 1, 0:1]
        dm = jnp.where(causal, bc - br + ir, NEG)
        inter = bc + m_prev
        mt = jnp.maximum(inter, jnp.max(dm, axis=1, keepdims=True))
        w_intra = jnp.exp(dm - mt)
        w_inter = jnp.exp(inter - mt)
        a = w_intra * lax.dot_general(q, k, (((1,), (1,)), ((), ())), preferred_element_type=F32)
        c_prev = c_ref[h]
        n_prev = n_ref[h:h + 1, :]
        num = (jnp.dot(a.astype(BF16), v, preferred_element_type=F32)
               + w_inter * jnp.dot(q, c_prev.astype(BF16), preferred_element_type=F32))
        den = (jnp.sum(a, axis=1, keepdims=True)
               + w_inter * jnp.sum(qf * n_prev, axis=1, keepdims=True))
        hb = num / jnp.maximum(jnp.abs(den), jnp.exp(-mt))
        m_new = mt[L - 1:L, :]
        b_last = bc[L - 1:L, :]
        g = jnp.exp(b_last - bc + ic - m_new)
        decay = jnp.exp(b_last + m_prev - m_new)
        kg = kf * g
        c_ref[h] = decay * c_prev + lax.dot_general(kg.astype(BF16), v, (((0,), (0,)), ((), ())),
                                                    preferred_element_type=F32)
        n_ref[h:h + 1, :] = decay * n_prev + jnp.sum(kg, axis=0, keepdims=True)
        m_ref[h:h + 1, :] = jnp.broadcast_to(m_new, (1, 128))
        hb = hb * lax.rsqrt(jnp.mean(hb * hb, axis=-1, keepdims=True) + EPS) * gh_ref[:, cols]
        o_ref[0, :, cols] = jax.nn.sigmoid(ob_ref[0, :, cols]) * hb

    @pl.when(ci == pl.num_programs(1) - 1)
    def _():
        c_out_ref[0] = c_ref[...]
        n_out_ref[0] = n_ref[...]
        m_out_ref[0] = m_ref[...]


def _mlstm_prompt(qk, v, ob, gcol, grow, w_conv, b_conv, b_if, g_head):
    b, s_len, _ = qk.shape
    L = min(MLSTM_CHUNK, s_len)
    nc = s_len // L
    bif_col = jnp.zeros((1, 128), F32).at[0, :2 * B_HEADS].set(b_if)
    bif_row = b_if.reshape(2 * B_HEADS, 1)
    seq = lambda w: pl.BlockSpec((1, L, w), lambda bi, ci: (bi, ci, 0))
    full = lambda shape: pl.BlockSpec(shape, lambda bi, ci: (0,) * len(shape))
    return pl.pallas_call(
        _mlstm_kernel,
        grid=(b, nc),
        in_specs=[seq(2 * B_W), seq(B_W), seq(B_W), seq(128),
                  pl.BlockSpec((2 * B_HEADS, L), lambda bi, ci: (0, bi * nc + ci)),
                  full((B_CONV, 2 * B_W)), full((1, 2 * B_W)), full((1, 128)), full((2 * B_HEADS, 1)),
                  full((1, B_W))],
        out_specs=[seq(B_W),
                   pl.BlockSpec((1, B_HEADS, B_HD, B_HD), lambda bi, ci: (bi, 0, 0, 0)),
                   pl.BlockSpec((1, 8, B_HD), lambda bi, ci: (bi, 0, 0)),
                   pl.BlockSpec((1, 8, 128), lambda bi, ci: (bi, 0, 0))],
        out_shape=[jax.ShapeDtypeStruct((b, s_len, B_W), F32),
                   jax.ShapeDtypeStruct((b, B_HEADS, B_HD, B_HD), F32),
                   jax.ShapeDtypeStruct((b, 8, B_HD), F32),
                   jax.ShapeDtypeStruct((b, 8, 128), F32)],
        scratch_shapes=[pltpu.VMEM((L + 8, 2 * B_W), F32), pltpu.VMEM((B_HEADS, B_HD, B_HD), F32),
                        pltpu.VMEM((8, B_HD), F32), pltpu.VMEM((8, 128), F32)],
        compiler_params=_cparams("arbitrary", "arbitrary"),
        name="mlstm_chunkwise",
    )(qk, v, ob, gcol, grow, w_conv, b_conv.reshape(1, -1), bif_col, bif_row, g_head.reshape(1, -1))


def _odd_in_kernel(x_ref, g_ref, w_ref, gcv_ref, bcv_ref, ws_ref, bs_ref, c_ref, xd_ref):
    tm = x_ref.shape[0]
    xn = _rms(x_ref[...], g_ref[...]).astype(BF16)
    u = jnp.dot(xn, w_ref[:, 0:C_W], preferred_element_type=F32)
    v = jnp.dot(xn, w_ref[:, C_W:2 * C_W], preferred_element_type=F32)
    xd_ref[...] = jnp.dot(xn, w_ref[:, 2 * C_W:], preferred_element_type=F32)
    mu = jnp.mean(v, axis=-1, keepdims=True)
    vc = v - mu
    var = jnp.mean(vc * vc, axis=-1, keepdims=True)
    vl = (vc * lax.rsqrt(var + EPS) * gcv_ref[...] + bcv_ref[...]).astype(BF16)
    ri = lax.broadcasted_iota(I32, (C_CHUNK, C_CHUNK), 0)
    cj = lax.broadcasted_iota(I32, (C_CHUNK, C_CHUNK), 1)
    for g in range(C_GROUPS):
        cols = slice(g * C_GD, (g + 1) * C_GD)
        wg = jnp.where(ri >= cj, ws_ref[g], 0.0).astype(BF16)
        for c in range(tm // C_CHUNK):
            rows = slice(c * C_CHUNK, (c + 1) * C_CHUNK)
            s = jnp.dot(wg, vl[rows, cols], preferred_element_type=F32) + bs_ref[:, g:g + 1]
            c_ref[rows, cols] = u[rows, cols] * s


def _odd_in(x, g, w, g_cv, b_cv, w_s, b_s, tm):
    m, d = x.shape
    n = w.shape[1]
    full = lambda shape: pl.BlockSpec(shape, lambda i: (0,) * len(shape))
    return pl.pallas_call(
        _odd_in_kernel,
        grid=(m // tm,),
        in_specs=[pl.BlockSpec((tm, d), lambda i: (i, 0)), full((1, d)), full((d, n)), full((1, C_W)),
                  full((1, C_W)), full((C_GROUPS, C_CHUNK, C_CHUNK)), full((C_CHUNK, C_GROUPS))],
        out_specs=[pl.BlockSpec((tm, C_W), lambda i: (i, 0)), pl.BlockSpec((tm, D_W), lambda i: (i, 0))],
        out_shape=[jax.ShapeDtypeStruct((m, C_W), F32), jax.ShapeDtypeStruct((m, D_W), F32)],
        compiler_params=_cparams("arbitrary"),
        name="odd_in_proj_gate",
    )(x, g.reshape(1, d), w, g_cv.reshape(1, -1), b_cv.reshape(1, -1), w_s, b_s.T)


S5_TT = 128
S5_PAD = 8
S5_HALF = D_GROUPS * D_STATE // 2


def _s5_discretize(a_re, a_im, log_dt, bm_re, bm_im, cm_re, cm_im):
    a_re, a_im = a_re.astype(F32), a_im.astype(F32)
    dt = jnp.exp(log_dt.astype(F32))[:, None]
    mag = jnp.exp(a_re * dt)
    ab_re = mag * jnp.cos(a_im * dt)
    ab_im = mag * jnp.sin(a_im * dt)
    inv = 1.0 / (a_re * a_re + a_im * a_im)
    f_re = ((ab_re - 1.0) * a_re + ab_im * a_im) * inv
    f_im = (ab_im * a_re - (ab_re - 1.0) * a_im) * inv
    bm_re, bm_im = bm_re.astype(F32), bm_im.astype(F32)
    bb_re = f_re[..., None] * bm_re - f_im[..., None] * bm_im
    bb_im = f_re[..., None] * bm_im + f_im[..., None] * bm_re
    return ab_re, ab_im, bb_re, bb_im


def _s5_matrices(bb_re, bb_im, cm_re, cm_im):
    gh = D_GROUPS // 2
    eye = jnp.eye(gh, dtype=F32)

    def in_half(bb):
        return jnp.einsum('gpc,gh->gchp', bb, eye).reshape(gh * D_GCH, gh * D_STATE)

    def out_half(cm):
        return jnp.einsum('gcp,gh->gphc', cm, eye).reshape(gh * D_STATE, gh * D_GCH)

    bd = jnp.stack([jnp.concatenate([in_half(bb_re[h * gh:(h + 1) * gh]), in_half(bb_im[h * gh:(h + 1) * gh])], axis=1)
                    for h in range(2)])
    cm = jnp.stack([jnp.concatenate([out_half(cm_re[h * gh:(h + 1) * gh].astype(F32)),
                                     -out_half(cm_im[h * gh:(h + 1) * gh].astype(F32))], axis=0)
                    for h in range(2)])
    return bd, cm


def _gelu_tanh(x):
    return 0.5 * x * (1.0 + jnp.tanh(math.sqrt(2.0 / math.pi) * (x + 0.044715 * (x * x * x))))


def _s5_kernel(x_ref, bd_ref, cm_ref, ar_ref, ai_ref, dsk_ref, wg_ref, bg_ref, o_ref, hr_out, hi_out,
               bu_ref, hs_ref, hr_ref, hi_ref):
    i = pl.program_id(0)
    nb, tt, _ = x_ref.shape
    stride = tt + S5_PAD
    half_in = D_W // 2
    nct = S5_HALF // 128

    @pl.when(i == 0)
    def _():
        hr_ref[...] = jnp.zeros_like(hr_ref)
        hi_ref[...] = jnp.zeros_like(hi_ref)

    x = x_ref[...].reshape(nb * tt, D_W)
    ys = []
    for h in range(2):
        lanes = slice(h * S5_HALF, (h + 1) * S5_HALF)
        bu = jnp.dot(x[:, h * half_in:(h + 1) * half_in].astype(BF16), bd_ref[h], preferred_element_type=F32)
        for c in range(2 * nct):
            for b in range(nb):
                bu_ref[c, b * stride:b * stride + tt, :] = bu[b * tt:(b + 1) * tt, c * 128:(c + 1) * 128]
        ar = jnp.broadcast_to(ar_ref[:, lanes], (nb, S5_HALF))
        ai = jnp.broadcast_to(ai_ref[:, lanes], (nb, S5_HALF))

        def step(t, carry):
            hr, hi = carry
            rows = pl.ds(t, nb, stride=stride)
            bur = jnp.concatenate([bu_ref[c, rows, :] for c in range(nct)], axis=1)
            bui = jnp.concatenate([bu_ref[nct + c, rows, :] for c in range(nct)], axis=1)
            nhr = ar * hr - ai * hi + bur
            nhi = ar * hi + ai * hr + bui
            for c in range(nct):
                hs_ref[c, rows, :] = nhr[:, c * 128:(c + 1) * 128]
                hs_ref[nct + c, rows, :] = nhi[:, c * 128:(c + 1) * 128]
            return nhr, nhi

        hr, hi = lax.fori_loop(0, tt, step, (hr_ref[:, lanes], hi_ref[:, lanes]))
        hr_ref[:, lanes] = hr
        hi_ref[:, lanes] = hi
        hs = jnp.concatenate(
            [jnp.concatenate([hs_ref[c, b * stride:b * stride + tt, :] for b in range(nb)], axis=0)
             for c in range(2 * nct)], axis=1)
        ys.append(jnp.dot(hs.astype(BF16), cm_ref[h], preferred_element_type=F32))
    y = jnp.concatenate(ys, axis=-1) + dsk_ref[...] * x
    g = _gelu_tanh(y)
    d_out = g * jax.nn.sigmoid(jnp.dot(g.astype(BF16), wg_ref[...], preferred_element_type=F32) + bg_ref[...])
    o_ref[...] = d_out.reshape(nb, tt, D_W)

    @pl.when(i == pl.num_programs(0) - 1)
    def _():
        hr_out[...] = hr_ref[...]
        hi_out[...] = hi_ref[...]


def _s5_prompt(xd, bd, cm, ab_re, ab_im, d_skip, w_glu, b_glu):
    nb, t_len, _ = xd.shape
    tt = min(S5_TT, t_len)
    n_state = D_GROUPS * D_STATE
    full = lambda shape: pl.BlockSpec(shape, lambda i: (0,) * len(shape))
    return pl.pallas_call(
        _s5_kernel,
        grid=(t_len // tt,),
        in_specs=[pl.BlockSpec((nb, tt, D_W), lambda i: (0, i, 0)), full(bd.shape), full(cm.shape),
                  full((1, n_state)), full((1, n_state)), full((1, D_W)), full((D_W, D_W)), full((1, D_W))],
        out_specs=[pl.BlockSpec((nb, tt, D_W), lambda i: (0, i, 0)), full((nb, n_state)), full((nb, n_state))],
        out_shape=[jax.ShapeDtypeStruct(xd.shape, F32), jax.ShapeDtypeStruct((nb, n_state), F32),
                   jax.ShapeDtypeStruct((nb, n_state), F32)],
        scratch_shapes=[pltpu.VMEM((2 * S5_HALF // 128, nb * (tt + S5_PAD), 128), F32),
                        pltpu.VMEM((2 * S5_HALF // 128, nb * (tt + S5_PAD), 128), F32),
                        pltpu.VMEM((nb, n_state), F32), pltpu.VMEM((nb, n_state), F32)],
        compiler_params=_cparams("arbitrary"),
        name="s5_scan_glu",
    )(xd, bd, cm, ab_re.reshape(1, n_state), ab_im.reshape(1, n_state), d_skip.reshape(1, D_W), w_glu,
      b_glu.reshape(1, D_W))


def _hdot(a, b):
    return jnp.dot(a, b, precision=HIGHEST, preferred_element_type=F32)


SAMPLE_NB = 8
SAMPLE_ROWS = 128


def _sample_attn_kernel(q_ref, kn_ref, vn_ref, tab_ref, e_ref, et_ref, k1, k2, k3, v1, v2, v3, o_ref):
    e = e_ref[...]
    et = et_ref[...]
    seg_sum = lambda a: jnp.dot(a, e, preferred_element_type=F32, precision=HIGHEST)
    expand = lambda a: jnp.dot(a, et, preferred_element_type=F32, precision=HIGHEST)
    row8 = lax.broadcasted_iota(I32, (8, 128), 0)
    for n in range(q_ref.shape[0]):
        q = q_ref[n:n + 1, :] * (A_HD ** -0.5)
        s_blocks = [seg_sum(kb[n] * q) + tab_ref[b] for b, kb in enumerate((k1, k2, k3))]
        s_new = jnp.where(row8 == 0, seg_sum(jnp.broadcast_to(kn_ref[n:n + 1, :] * q, (8, A_W))) + tab_ref[3, 0:8, :],
                          NEG)
        m = jnp.max(s_new, axis=0, keepdims=True)
        for s in s_blocks:
            m = jnp.maximum(m, jnp.max(s, axis=0, keepdims=True))
        p_new = jnp.exp(s_new - m)
        den = jnp.sum(p_new, axis=0, keepdims=True)
        acc = jnp.sum(expand(p_new) * vn_ref[n:n + 1, :], axis=0, keepdims=True)
        for s, vb in zip(s_blocks, (v1, v2, v3)):
            p = jnp.exp(s - m)
            den = den + jnp.sum(p, axis=0, keepdims=True)
            acc = acc + jnp.sum(expand(p) * vb[n], axis=0, keepdims=True)
        o_ref[n:n + 1, :] = acc / expand(jnp.broadcast_to(den, (8, 128)))[0:1, :]


def _sample_attention(q, k_new, v_new, cache_k, cache_v, rel_bias):
    n, L = cache_k.shape[0], cache_k.shape[1]
    assert L == A_WIN and all(w // d == SAMPLE_ROWS for w, d in A_BRANCHES)
    bias = rel_bias[_t5_bucket(jnp.arange(L + 1, dtype=I32))].astype(F32)
    r = jnp.arange(SAMPLE_ROWS, dtype=I32)
    tabs = [bias[d * (SAMPLE_ROWS - r)] for _, d in A_BRANCHES]
    tabs.append(jnp.full((SAMPLE_ROWS, A_HEADS), NEG, F32).at[0].set(bias[0] + math.log(len(A_BRANCHES))))
    tab = jnp.pad(jnp.stack(tabs), ((0, 0), (0, 0), (0, 128 - A_HEADS)))
    lane_head = (jnp.arange(A_W, dtype=I32)[:, None] // A_HD == jnp.arange(128, dtype=I32)[None, :]).astype(F32)
    views, specs = [], []
    for _, d in A_BRANCHES:
        rows = L // d
        views.append(lambda c, rows=rows, d=d: c.reshape(n, rows, d * A_W))
        specs.append(pl.BlockSpec((SAMPLE_NB, SAMPLE_ROWS, A_W),
                                  lambda i, blk=rows // SAMPLE_ROWS - 1: (i, blk, 0)))
    row_spec = pl.BlockSpec((SAMPLE_NB, A_W), lambda i: (i, 0))
    full = lambda shape: pl.BlockSpec(shape, lambda i: (0,) * len(shape))
    return pl.pallas_call(
        _sample_attn_kernel,
        grid=(n // SAMPLE_NB,),
        in_specs=[row_spec, row_spec, row_spec, full(tab.shape), full((A_W, 128)), full((128, A_W))] + specs + specs,
        out_specs=row_spec,
        out_shape=jax.ShapeDtypeStruct((n, A_W), F32),
        compiler_params=_cparams("arbitrary"),
        name="sample_attention",
    )(q, k_new, v_new, tab, lane_head, lane_head.T, *[v(cache_k) for v in views], *[v(cache_v) for v in views])


def _sample_even_mixers(x, cache_k, cache_v, c0, n0, m0, conv0, rel_bias, g_mix, w_in, b_if, w_conv, b_conv,
                        g_head):
    n = x.shape[0]
    n_main = 3 * A_W + 4 * B_W
    segs = _EVEN_SEGS[:-1] + ((n_main, 2 * B_HEADS),)
    qa, ka, va, qkb, vb, ob, gates, _ = _norm_proj(x, g_mix, w_in, w_in[:, n_main:].T, segs, tm=n, precise=True)
    a_out = _sample_attention(qa, ka, va, cache_k, cache_v, rel_bias)
    new_k = jnp.concatenate([cache_k[:, 1:], ka.reshape(n, 1, A_HEADS, A_HD)], axis=1)
    new_v = jnp.concatenate([cache_v[:, 1:], va.reshape(n, 1, A_HEADS, A_HD)], axis=1)
    xp = jnp.concatenate([conv0, qkb[:, None, :]], axis=1)
    qk = _silu(b_conv + jnp.sum(xp * w_conv[None], axis=1))
    q = qk[:, :B_W].reshape(n, B_HEADS, B_HD)
    k = qk[:, B_W:].reshape(n, B_HEADS, B_HD) * (B_HD ** -0.5)
    v = vb.reshape(n, B_HEADS, B_HD)
    ig = gates[:, :B_HEADS] + b_if[:B_HEADS]
    logf = _log_sigmoid(gates[:, B_HEADS:] + b_if[B_HEADS:])
    inter = logf + m0
    mt = jnp.maximum(inter, ig)
    w_intra = jnp.exp(ig - mt)
    w_inter = jnp.exp(inter - mt)
    a = w_intra * jnp.sum(q * k, axis=-1)
    num = a[..., None] * v + w_inter[..., None] * jnp.einsum('nhk,nhkv->nhv', q, c0, precision=HIGHEST)
    den = a + w_inter * jnp.sum(q * n0, axis=-1)
    hb = num / jnp.maximum(jnp.abs(den), jnp.exp(-mt))[..., None]
    g = jnp.exp(ig - mt)
    decay = jnp.exp(inter - mt)
    c_new = decay[..., None, None] * c0 + (g[..., None] * k)[..., :, None] * v[..., None, :]
    n_new = decay[..., None] * n0 + g[..., None] * k
    hb = hb * lax.rsqrt(jnp.mean(hb * hb, axis=-1, keepdims=True) + EPS) * g_head.reshape(B_HEADS, B_HD)
    b_out = jax.nn.sigmoid(ob) * hb.reshape(n, B_W)
    return a_out, b_out, new_k, new_v, c_new, n_new, mt, xp[:, 1:]


def _sample_odd_mixers(x, h_re0, h_im0, g_mix, w_in, g_cv, b_cv, w_s, b_s, ab_re, ab_im, bb_re, bb_im,
                       cm_re, cm_im, d_skip, w_glu, b_glu):
    n = x.shape[0]
    h = _rms(x, g_mix)
    proj = _hdot(h, w_in)
    u, v, xd = proj[:, :C_W], proj[:, C_W:2 * C_W], proj[:, 2 * C_W:]
    mu = jnp.mean(v, axis=-1, keepdims=True)
    vc = v - mu
    v = vc * lax.rsqrt(jnp.mean(vc * vc, axis=-1, keepdims=True) + EPS) * g_cv + b_cv
    s = (w_s[:, 0, 0][None, :, None] * v.reshape(n, C_GROUPS, C_GD) + b_s[:, 0][None, :, None]).reshape(n, C_W)
    c_out = u * s
    xg = xd.reshape(n, D_GROUPS, D_GCH)
    bu_re = jnp.einsum('gpc,ngc->ngp', bb_re, xg, precision=HIGHEST)
    bu_im = jnp.einsum('gpc,ngc->ngp', bb_im, xg, precision=HIGHEST)
    hr = ab_re * h_re0 - ab_im * h_im0 + bu_re
    hi = ab_re * h_im0 + ab_im * h_re0 + bu_im
    y = (jnp.einsum('gcp,ngp->ngc', cm_re, hr, precision=HIGHEST)
         - jnp.einsum('gcp,ngp->ngc', cm_im, hi, precision=HIGHEST) + d_skip * xg)
    g = _gelu_tanh(y.reshape(n, D_W))
    d_out = g * jax.nn.sigmoid(_hdot(g, w_glu) + b_glu)
    return c_out, d_out, v, hr, hi


_EVEN_SEGS = ((0, A_W), (A_W, A_W), (2 * A_W, A_W), (3 * A_W, 2 * B_W), (3 * A_W + 2 * B_W, B_W),
              (3 * A_W + 3 * B_W, B_W), (3 * A_W + 4 * B_W, 128))


def kernel(x_prompt, x_sample, cache_a_k, cache_a_v, state_b_c, state_b_n, state_b_m, state_b_conv,
           state_d_re, state_d_im, rel_bias, g_mix, g_ffn, g_final, w_in_e, b_if, w_conv_b, b_conv_b,
           g_bhead, w_out_e, w1_e, w3_e, w2_e, w_in_o, g_cv, b_cv, w_s, b_s, a_re, a_im, log_dt,
           bm_re, bm_im, cm_re, cm_im, d_skip, w_glu, b_glu, w_out_o, w_router, b_router,
           w1_m, w3_m, w2_m):
    bp, sp, d = x_prompt.shape
    ns = x_sample.shape[0]
    mp = bp * sp
    xp = x_prompt.reshape(mp, d)
    xs = x_sample.reshape(ns, d)

    w_in = w_in_e[0]
    n_main = 3 * A_W + 4 * B_W
    w_gate = w_in[:, n_main:]
    w_cat = jnp.concatenate([w_in[:, :n_main], jnp.pad(w_gate, ((0, 0), (0, 128 - 2 * B_HEADS)))], axis=1)
    q, k, v, qkb, vb, ob, gcol, grow = _norm_proj(
        xp, g_mix[0], w_cat.astype(BF16), w_gate.T.astype(BF16), _EVEN_SEGS, tm=512)
    shp = lambda a: a.reshape(bp, sp, a.shape[-1])
    a_out = _attention_prompt(shp(q), shp(k), shp(v), rel_bias)
    b_out, pc, pn, pm = _mlstm_prompt(shp(qkb), shp(vb), shp(ob), shp(gcol), grow, w_conv_b[0], b_conv_b[0],
                                      b_if[0], g_bhead[0])
    x1p = _mix_ffn(a_out.reshape(mp, A_W), b_out.reshape(mp, B_W), xp, w_out_e[0].astype(BF16), g_ffn[0],
                   w1_e[0].astype(BF16), w3_e[0].astype(BF16), w2_e[0].astype(BF16), tm=1024, tf=512)

    sa, sb, s_k, s_v, sc, sn, sm, s_conv = _sample_even_mixers(
        xs, cache_a_k[0], cache_a_v[0], state_b_c[0], state_b_n[0], state_b_m[0], state_b_conv[0], rel_bias,
        g_mix[0], w_in, b_if[0], w_conv_b[0], b_conv_b[0], g_bhead[0])
    x1s = _mix_ffn(sa, sb, xs, w_out_e[0], g_ffn[0], w1_e[0], w3_e[0], w2_e[0], tm=ns, tf=512, precise=True)

    ab_re, ab_im, bb_re, bb_im = _s5_discretize(a_re[0], a_im[0], log_dt[0], bm_re[0], bm_im[0], cm_re[0],
                                                cm_im[0])
    bd, cmat = _s5_matrices(bb_re, bb_im, cm_re[0], cm_im[0])
    c_out, xd = _odd_in(x1p, g_mix[1], w_in_o[0].astype(BF16), g_cv[0], b_cv[0], w_s[0], b_s[0], tm=512)
    d_out, p_hr, p_hi = _s5_prompt(xd.reshape(bp, sp, D_W), bd.astype(BF16), cmat.astype(BF16), ab_re, ab_im,
                                   d_skip[0], w_glu[0].astype(BF16), b_glu[0])
    wr = jnp.pad(w_router[0].astype(F32), ((0, 0), (0, 128 - N_EXPERTS)))
    br = jnp.pad(b_router[0].astype(F32), (0, 128 - N_EXPERTS)).reshape(1, 128)
    x2p, hnp, idp, gtp = _mix_router(c_out, d_out.reshape(mp, D_W), x1p, w_out_o[0].astype(BF16), g_ffn[1], wr, br,
                                     tm=512)

    sc_out, sd_out, s_cv, s_hr, s_hi = _sample_odd_mixers(
        x1s, state_d_re[0], state_d_im[0], g_mix[1], w_in_o[0], g_cv[0], b_cv[0], w_s[0], b_s[0], ab_re, ab_im,
        bb_re, bb_im, cm_re[0], cm_im[0], d_skip[0], w_glu[0], b_glu[0])
    x2s, hns, ids_s, gts = _mix_router(sc_out, sd_out, x1s, w_out_o[0], g_ffn[1], wr, br, tm=ns, precise=True)

    pad = ((0, ROUTE_TILE - ns), (0, 0))
    y_p, y_s = _moe_block(x2p, hnp, idp, gtp, jnp.pad(x2s, pad), jnp.pad(hns, pad),
                          jnp.pad(ids_s, pad, constant_values=-1), jnp.pad(gts, pad),
                          w1_m[0].astype(BF16), w3_m[0].astype(BF16), w2_m[0].astype(BF16), g_final)
    y_prompt = y_p.reshape(bp, sp, d)
    y_sample = y_s[:ns].reshape(ns, 1, d)

    e = lambda a: a[None]
    p_k = k.reshape(bp, sp, A_HEADS, A_HD)
    p_v = v.reshape(bp, sp, A_HEADS, A_HD)
    p_conv = qkb.reshape(bp, sp, 2 * B_W)[:, sp - (B_CONV - 1):]
    return (y_prompt, y_sample, e(p_k), e(p_v), e(s_k), e(s_v), e(pc), e(sc), e(pn[:, :B_HEADS]), e(sn),
            e(pm[:, :B_HEADS, 0]), e(sm), e(p_conv), e(s_conv), e(s_cv.reshape(ns, 1, C_W)),
            e(p_hr.reshape(bp, D_GROUPS, D_STATE)), e(s_hr), e(p_hi.reshape(bp, D_GROUPS, D_STATE)), e(s_hi))
```

```python
import functools
import math

import jax
import jax.numpy as jnp
from jax import lax
from jax.experimental import pallas as pl
from jax.experimental.pallas import tpu as pltpu

F32 = jnp.float32
BF16 = jnp.bfloat16
I32 = jnp.int32
HIGHEST = lax.Precision.HIGHEST

D_MODEL = 1024
A_HEADS, A_HD = 8, 64
A_W = A_HEADS * A_HD
A_BRANCHES = ((128, 1), (512, 4), (2048, 16))
A_WIN = 2048
N_BUCKETS = 32
B_HEADS, B_HD = 4, 128
B_W = B_HEADS * B_HD
B_CONV = 4
C_GROUPS, C_GD = 4, 128
C_W = C_GROUPS * C_GD
C_CHUNK = 128
D_GROUPS, D_GCH, D_STATE = 32, 16, 64
D_W = D_GROUPS * D_GCH
D_FF = 3584
N_EXPERTS = 8
TOP_K = 2
EPS = 1e-6
NEG = -1e30

VMEM_LIMIT = 56 * 1024 * 1024

ROUTE_TILE = 256
FFN_TILE = 1024
FFN_ROW_CHUNK = 1024
COMBINE_ALIGN = 16
COMBINE_WIN = ROUTE_TILE + COMBINE_ALIGN


def _cparams(*sem):
    return pltpu.CompilerParams(dimension_semantics=sem, vmem_limit_bytes=VMEM_LIMIT)


def _moe_plan(ids, n_tok_tiles):
    mt = ids.shape[0]
    n_assign = 2 * mt
    n_ffn_tiles = -(-(n_assign + N_EXPERTS * (FFN_TILE - 1)) // FFN_TILE) + 1
    n_sub = n_ffn_tiles * (FFN_TILE // ROUTE_TILE)
    flat = ids.reshape(n_assign)
    onehot = (flat[:, None] == jnp.arange(N_EXPERTS, dtype=I32)[None, :]).astype(I32)
    csum = jnp.cumsum(onehot, axis=0)
    rank = jnp.sum((csum - onehot) * onehot, axis=1)
    counts = csum[-1]
    seg = ((counts + FFN_TILE - 1) // FFN_TILE) * FFN_TILE
    seg_end = jnp.cumsum(seg)
    seg_off = seg_end - seg
    dest = jnp.where(flat >= 0, seg_off[jnp.clip(flat, 0)] + rank, -1).reshape(mt, 2)
    total = seg_end[-1]
    tile_start = jnp.arange(n_ffn_tiles, dtype=I32) * FFN_TILE
    tile_expert = jnp.minimum(jnp.sum((tile_start[:, None] >= seg_end[None, :]).astype(I32), axis=1),
                              N_EXPERTS - 1).astype(I32)
    n_used = (total // FFN_TILE).astype(I32)
    per_tile = 2 * ROUTE_TILE
    cnt_tile = jnp.concatenate([jnp.zeros((1, N_EXPERTS), I32), csum[per_tile - 1::per_tile]])
    lo = seg_off[None, :] + cnt_tile[:-1]
    hi = seg_off[None, :] + cnt_tile[1:]
    sub_start = jnp.arange(n_sub, dtype=I32) * ROUTE_TILE
    sub_expert = jnp.minimum(jnp.sum((sub_start[:, None] >= seg_end[None, :]).astype(I32), axis=1),
                             N_EXPERTS - 1)
    lo_e, hi_e = lo.T[sub_expert], hi.T[sub_expert]
    jlo = jnp.sum((hi_e <= sub_start[:, None]).astype(I32), axis=1).astype(I32)
    jhi = (jnp.sum((lo_e < sub_start[:, None] + ROUTE_TILE).astype(I32), axis=1) - 1).astype(I32)
    win = lo // COMBINE_ALIGN
    flat1 = lambda a: a.reshape(-1).astype(I32)
    return dest, tile_expert, n_used, (jlo, jhi), (flat1(win), flat1(lo), flat1(hi)), n_ffn_tiles


def _moe_gather_kernel(jlo_ref, jhi_ref, hp_hbm, hs_hbm, dt_ref, gt_ref, xs_ref, gs_ref,
                       hn_ref, acc_ref, g_ref, sem):
    s = pl.program_id(0)
    mp = hp_hbm.shape[0]

    @pl.when(s == 0)
    def _():
        copies = (pltpu.make_async_copy(hp_hbm, hn_ref.at[0:mp], sem.at[0]),
                  pltpu.make_async_copy(hs_hbm, hn_ref.at[mp:mp + ROUTE_TILE], sem.at[1]))
        for c in copies:
            c.start()
        for c in copies:
            c.wait()

    acc_ref[...] = jnp.zeros_like(acc_ref)
    g_ref[...] = jnp.zeros_like(g_ref)
    rows = lax.broadcasted_iota(I32, (ROUTE_TILE, ROUTE_TILE), 0) + s * ROUTE_TILE

    def body(j, carry):
        off = pl.multiple_of(j * ROUTE_TILE, ROUTE_TILE)
        dt = dt_ref[:, pl.ds(off, ROUTE_TILE)]
        gt = gt_ref[:, pl.ds(off, ROUTE_TILE)]
        m1 = dt[0:1, :] == rows
        m2 = dt[1:2, :] == rows
        onehot = jnp.where(m1 | m2, 1.0, 0.0).astype(BF16)
        acc_ref[...] += jnp.dot(onehot, hn_ref[pl.ds(off, ROUTE_TILE), :], preferred_element_type=F32)
        g_ref[...] += jnp.sum(jnp.where(m1, gt[0:1, :], 0.0) + jnp.where(m2, gt[1:2, :], 0.0),
                              axis=1, keepdims=True)
        return carry

    lax.fori_loop(jlo_ref[s], jhi_ref[s] + 1, body, 0)
    xs_ref[...] = acc_ref[...].astype(BF16)
    gs_ref[...] = g_ref[...]


def _moe_gather(hn_p, hn_s, dest_t, gates_t, glist, n_sub):
    jlo, jhi = glist
    mp, d = hn_p.shape
    mt = dest_t.shape[1]
    grid_spec = pltpu.PrefetchScalarGridSpec(
        num_scalar_prefetch=2,
        grid=(n_sub,),
        in_specs=[
            pl.BlockSpec(memory_space=pl.ANY),
            pl.BlockSpec(memory_space=pl.ANY),
            pl.BlockSpec((2, mt), lambda s, *_: (0, 0)),
            pl.BlockSpec((2, mt), lambda s, *_: (0, 0)),
        ],
        out_specs=[
            pl.BlockSpec((ROUTE_TILE, d), lambda s, *_: (s, 0)),
            pl.BlockSpec((ROUTE_TILE, 1), lambda s, *_: (s, 0)),
        ],
        scratch_shapes=[pltpu.VMEM((mp + ROUTE_TILE, d), BF16), pltpu.VMEM((ROUTE_TILE, d), F32),
                        pltpu.VMEM((ROUTE_TILE, 1), F32), pltpu.SemaphoreType.DMA((2,))],
    )
    return pl.pallas_call(
        _moe_gather_kernel,
        grid_spec=grid_spec,
        out_shape=[jax.ShapeDtypeStruct((n_sub * ROUTE_TILE, d), BF16),
                   jax.ShapeDtypeStruct((n_sub * ROUTE_TILE, 1), F32)],
        compiler_params=_cparams("arbitrary"),
        name="moe_gather",
    )(jlo, jhi, hn_p, hn_s, dest_t, gates_t)


def _moe_ffn_kernel(te_ref, nu_ref, x_ref, gs_ref, w1_ref, w3_ref, w2_ref, y_ref, acc_ref):
    t, f = pl.program_id(0), pl.program_id(1)
    nf = pl.num_programs(1)

    @pl.when(t < nu_ref[0])
    def _():
        @pl.when(f == 0)
        def _():
            acc_ref[...] = jnp.zeros_like(acc_ref)

        w1, w3, w2 = w1_ref[0].astype(BF16), w3_ref[0].astype(BF16), w2_ref[0].astype(BF16)
        for r in range(0, FFN_TILE, FFN_ROW_CHUNK):
            rows = slice(r, r + FFN_ROW_CHUNK)
            x = x_ref[rows, :]
            a = jnp.dot(x, w1, preferred_element_type=F32)
            b = jnp.dot(x, w3, preferred_element_type=F32)
            acc_ref[rows, :] += jnp.dot((_silu(a) * b).astype(BF16), w2, preferred_element_type=F32)

        @pl.when(f == nf - 1)
        def _():
            y_ref[...] = (acc_ref[...] * gs_ref[...]).astype(BF16)

    @pl.when((t >= nu_ref[0]) & (f == nf - 1))
    def _():
        y_ref[...] = jnp.zeros_like(y_ref)


def _moe_ffn(xs, gs, w1, w3, w2, tile_expert, n_used, n_ffn_tiles, tf):
    d = xs.shape[1]
    ff = w1.shape[2]
    nf = ff // tf

    def fidx(t, f, nu):
        return jnp.where(t < nu[0], f, nf - 1)

    grid_spec = pltpu.PrefetchScalarGridSpec(
        num_scalar_prefetch=2,
        grid=(n_ffn_tiles, nf),
        in_specs=[
            pl.BlockSpec((FFN_TILE, d), lambda t, f, te, nu: (t, 0)),
            pl.BlockSpec((FFN_TILE, 1), lambda t, f, te, nu: (t, 0)),
            pl.BlockSpec((1, d, tf), lambda t, f, te, nu: (te[t], 0, fidx(t, f, nu))),
            pl.BlockSpec((1, d, tf), lambda t, f, te, nu: (te[t], 0, fidx(t, f, nu))),
            pl.BlockSpec((1, tf, d), lambda t, f, te, nu: (te[t], fidx(t, f, nu), 0)),
        ],
        out_specs=pl.BlockSpec((FFN_TILE, d), lambda t, f, te, nu: (t, 0)),
        scratch_shapes=[pltpu.VMEM((FFN_TILE, d), F32)],
    )
    return pl.pallas_call(
        _moe_ffn_kernel,
        grid_spec=grid_spec,
        out_shape=jax.ShapeDtypeStruct(xs.shape, BF16),
        compiler_params=_cparams("arbitrary", "arbitrary"),
        name="moe_ffn",
    )(tile_expert, n_used.reshape(1), xs, gs, w1, w3, w2)


def _moe_combine_kernel(win_ref, lo_ref, hi_ref, xp_ref, xs_ref, d_ref, g_ref, *rest):
    ys_refs = rest[:N_EXPERTS]
    yp_ref, ysm_ref, acc_ref = rest[N_EXPERTS:]
    j = pl.program_id(0)
    n_prompt_tiles = pl.num_programs(0) - 1

    @pl.when(j < n_prompt_tiles)
    def _():
        acc_ref[...] = xp_ref[...]

    @pl.when(j == n_prompt_tiles)
    def _():
        acc_ref[...] = xs_ref[...]

    for e in range(N_EXPERTS):
        lo, hi = lo_ref[j * N_EXPERTS + e], hi_ref[j * N_EXPERTS + e]

        @pl.when(hi > lo)
        def _():
            d = d_ref[...]
            d = jnp.where((d >= lo) & (d < hi), d, -1)
            cols = (lax.broadcasted_iota(I32, (ROUTE_TILE, COMBINE_WIN), 1)
                    + win_ref[j * N_EXPERTS + e] * COMBINE_ALIGN)
            onehot = jnp.where((d[:, 0:1] == cols) | (d[:, 1:2] == cols), 1.0, 0.0).astype(BF16)
            acc_ref[...] += jnp.dot(onehot, ys_refs[e][...], preferred_element_type=F32)

    y = _rms(acc_ref[...], g_ref[...])

    @pl.when(j < n_prompt_tiles)
    def _():
        yp_ref[...] = y

    @pl.when(j == n_prompt_tiles)
    def _():
        ysm_ref[...] = y


def _moe_combine(x2_p, x2_s, dest, ys, g_final, clist):
    win, lo, hi = clist
    mp, d = x2_p.shape
    n_prompt_tiles = mp // ROUTE_TILE

    def ptile(j, *_):
        return (jnp.minimum(j, n_prompt_tiles - 1), 0)

    def window(e):
        return pl.BlockSpec((pl.Element(COMBINE_WIN), pl.Element(d)),
                            lambda j, win, lo, hi: (win[j * N_EXPERTS + e] * COMBINE_ALIGN, 0))

    grid_spec = pltpu.PrefetchScalarGridSpec(
        num_scalar_prefetch=3,
        grid=(n_prompt_tiles + 1,),
        in_specs=[
            pl.BlockSpec((ROUTE_TILE, d), ptile),
            pl.BlockSpec((ROUTE_TILE, d), lambda j, *_: (0, 0)),
            pl.BlockSpec((ROUTE_TILE, 2), lambda j, *_: (j, 0)),
            pl.BlockSpec((1, d), lambda j, *_: (0, 0)),
        ] + [window(e) for e in range(N_EXPERTS)],
        out_specs=[pl.BlockSpec((ROUTE_TILE, d), ptile), pl.BlockSpec((ROUTE_TILE, d), lambda j, *_: (0, 0))],
        scratch_shapes=[pltpu.VMEM((ROUTE_TILE, d), F32)],
    )
    return pl.pallas_call(
        _moe_combine_kernel,
        grid_spec=grid_spec,
        out_shape=[jax.ShapeDtypeStruct((mp, d), F32), jax.ShapeDtypeStruct((ROUTE_TILE, d), F32)],
        compiler_params=_cparams("arbitrary"),
        name="moe_combine",
    )(win, lo, hi, x2_p, x2_s, dest, g_final.reshape(1, d), *([ys] * N_EXPERTS))


def _moe_block(x2_p, hn_p, ids_p, gates_p, x2_s, hn_s, ids_s, gates_s, w1, w3, w2, g_final, tf=512):
    mp = x2_p.shape[0]
    n_tok_tiles = mp // ROUTE_TILE + 1
    ids = jnp.concatenate([ids_p, ids_s])
    gates = jnp.concatenate([gates_p, gates_s])
    dest, tile_expert, n_used, glist, clist, n_ffn_tiles = _moe_plan(ids, n_tok_tiles)
    n_sub = n_ffn_tiles * (FFN_TILE // ROUTE_TILE)
    xs, gs = _moe_gather(hn_p, hn_s, dest.T, gates.T, glist, n_sub)
    ys = _moe_ffn(xs, gs, w1, w3, w2, tile_expert, n_used, n_ffn_tiles, tf)
    return _moe_combine(x2_p, x2_s, dest, ys, g_final, clist)


def _rms(x, g):
    return x * lax.rsqrt(jnp.mean(x * x, axis=-1, keepdims=True) + EPS) * g


def _mm(a, b, precise):
    if precise:
        return jnp.dot(a.astype(F32), b.astype(F32), preferred_element_type=F32, precision=HIGHEST)
    return jnp.dot(a.astype(BF16), b.astype(BF16), preferred_element_type=F32)


def _silu(x):
    return x * jax.nn.sigmoid(x)


def _norm_proj_kernel(x_ref, g_ref, w_ref, wt_ref, *out_refs, segs, precise):
    xn = _rms(x_ref[...], g_ref[...])
    xn = xn if precise else xn.astype(BF16)
    for (start, width), o_ref in zip(segs, out_refs[:-1]):
        for c in range(0, width, 512):
            cw = min(512, width - c)
            o_ref[:, c:c + cw] = _mm(xn, w_ref[:, start + c:start + c + cw], precise)
    wt = wt_ref[...]
    out_refs[-1][...] = lax.dot_general(
        wt.astype(xn.dtype), xn, (((1,), (1,)), ((), ())), preferred_element_type=F32,
        precision=HIGHEST if precise else None)


def _norm_proj(x, g, w, wt, segs, tm, precise=False):
    m, d = x.shape
    n = w.shape[1]
    nt = wt.shape[0]
    out_shape = [jax.ShapeDtypeStruct((m, width), F32) for _, width in segs]
    out_shape.append(jax.ShapeDtypeStruct((nt, m), F32))
    out_specs = [pl.BlockSpec((tm, width), lambda i: (i, 0)) for _, width in segs]
    out_specs.append(pl.BlockSpec((nt, tm), lambda i: (0, i)))
    return pl.pallas_call(
        functools.partial(_norm_proj_kernel, segs=tuple(segs), precise=precise),
        grid=(m // tm,),
        in_specs=[
            pl.BlockSpec((tm, d), lambda i: (i, 0)),
            pl.BlockSpec((1, d), lambda i: (0, 0)),
            pl.BlockSpec((d, n), lambda i: (0, 0)),
            pl.BlockSpec((nt, d), lambda i: (0, 0)),
        ],
        out_specs=out_specs,
        out_shape=out_shape,
        compiler_params=_cparams("arbitrary"),
        name="norm_proj",
    )(x, g.reshape(1, d), w, wt)


def _mix_ffn_kernel(a_ref, b_ref, x_ref, wo_ref, g_ref, w1_ref, w3_ref, w2_ref, o_ref, hn_ref, *, precise):
    wa = a_ref.shape[1]

    @pl.when(pl.program_id(1) == 0)
    def _():
        x1 = (x_ref[...] + _mm(a_ref[...], wo_ref[0:wa, :], precise)
              + _mm(b_ref[...], wo_ref[wa:, :], precise))
        o_ref[...] = x1
        hn_ref[...] = _rms(x1, g_ref[...]).astype(hn_ref.dtype)

    wdt = F32 if precise else BF16
    w1, w3, w2 = w1_ref[...].astype(wdt), w3_ref[...].astype(wdt), w2_ref[...].astype(wdt)
    tm = hn_ref.shape[0]
    chunk = min(tm, FFN_ROW_CHUNK)
    for r in range(0, tm, chunk):
        rows = slice(r, r + chunk)
        hn = hn_ref[rows, :]
        o_ref[rows, :] += _mm(_silu(_mm(hn, w1, precise)) * _mm(hn, w3, precise), w2, precise)


def _mix_ffn(a, b, x, w_out, g, w1, w3, w2, tm, tf, precise=False):
    m, d = x.shape
    wa, wb = a.shape[1], b.shape[1]
    ff = w1.shape[1]
    return pl.pallas_call(
        functools.partial(_mix_ffn_kernel, precise=precise),
        grid=(m // tm, ff // tf),
        in_specs=[
            pl.BlockSpec((tm, wa), lambda i, f: (i, 0)),
            pl.BlockSpec((tm, wb), lambda i, f: (i, 0)),
            pl.BlockSpec((tm, d), lambda i, f: (i, 0)),
            pl.BlockSpec((wa + wb, d), lambda i, f: (0, 0)),
            pl.BlockSpec((1, d), lambda i, f: (0, 0)),
            pl.BlockSpec((d, tf), lambda i, f: (0, f)),
            pl.BlockSpec((d, tf), lambda i, f: (0, f)),
            pl.BlockSpec((tf, d), lambda i, f: (f, 0)),
        ],
        out_specs=pl.BlockSpec((tm, d), lambda i, f: (i, 0)),
        out_shape=jax.ShapeDtypeStruct((m, d), F32),
        scratch_shapes=[pltpu.VMEM((tm, d), F32 if precise else BF16)],
        compiler_params=_cparams("arbitrary", "arbitrary"),
        name="mix_ffn",
    )(a, b, x, w_out, g.reshape(1, d), w1, w3, w2)


def _mix_router_kernel(a_ref, b_ref, x_ref, wo_ref, g_ref, wr_ref, br_ref, x2_ref, hn_ref, ids_ref, gate_ref,
                       *, precise):
    wa = a_ref.shape[1]
    x2 = (x_ref[...] + _mm(a_ref[...], wo_ref[0:wa, :], precise)
          + _mm(b_ref[...], wo_ref[wa:, :], precise))
    x2_ref[...] = x2
    hn = _rms(x2, g_ref[...])
    hn_ref[...] = hn.astype(BF16)
    wr = wr_ref[...]
    if precise:
        logits = jnp.dot(hn, wr, preferred_element_type=F32, precision=HIGHEST)
    else:
        hn_hi, wr_hi = hn.astype(BF16), wr.astype(BF16)
        hn_lo, wr_lo = (hn - hn_hi.astype(F32)).astype(BF16), (wr - wr_hi.astype(F32)).astype(BF16)
        logits = (jnp.dot(hn_hi, wr_hi, preferred_element_type=F32)
                  + (jnp.dot(hn_lo, wr_hi, preferred_element_type=F32)
                     + jnp.dot(hn_hi, wr_lo, preferred_element_type=F32)))
    lane = lax.broadcasted_iota(I32, logits.shape, 1)
    real = lane < N_EXPERTS
    biased = jnp.where(real, logits + br_ref[...], -jnp.inf)
    m1 = jnp.max(biased, axis=-1, keepdims=True)
    i1 = jnp.min(jnp.where(biased == m1, lane, 128), axis=-1, keepdims=True)
    rest = jnp.where(lane == i1, -jnp.inf, biased)
    m2 = jnp.max(rest, axis=-1, keepdims=True)
    i2 = jnp.min(jnp.where(rest == m2, lane, 128), axis=-1, keepdims=True)
    l1 = jnp.sum(jnp.where(lane == i1, logits, 0.0), axis=-1, keepdims=True)
    l2 = jnp.sum(jnp.where(lane == i2, logits, 0.0), axis=-1, keepdims=True)
    mx = jnp.maximum(l1, l2)
    e1, e2 = jnp.exp(l1 - mx), jnp.exp(l2 - mx)
    two = lax.broadcasted_iota(I32, ids_ref.shape, 1)
    ids_ref[...] = jnp.where(two == 0, i1, i2)
    gate_ref[...] = jnp.where(two == 0, e1, e2) / (e1 + e2)


def _mix_router(a, b, x, w_out, g, wr, br, tm, precise=False):
    m, d = x.shape
    wa, wb = a.shape[1], b.shape[1]
    return pl.pallas_call(
        functools.partial(_mix_router_kernel, precise=precise),
        grid=(m // tm,),
        in_specs=[
            pl.BlockSpec((tm, wa), lambda i: (i, 0)),
            pl.BlockSpec((tm, wb), lambda i: (i, 0)),
            pl.BlockSpec((tm, d), lambda i: (i, 0)),
            pl.BlockSpec((wa + wb, d), lambda i: (0, 0)),
            pl.BlockSpec((1, d), lambda i: (0, 0)),
            pl.BlockSpec((d, 128), lambda i: (0, 0)),
            pl.BlockSpec((1, 128), lambda i: (0, 0)),
        ],
        out_specs=[
            pl.BlockSpec((tm, d), lambda i: (i, 0)),
            pl.BlockSpec((tm, d), lambda i: (i, 0)),
            pl.BlockSpec((tm, 2), lambda i: (i, 0)),
            pl.BlockSpec((tm, 2), lambda i: (i, 0)),
        ],
        out_shape=[jax.ShapeDtypeStruct((m, d), F32), jax.ShapeDtypeStruct((m, d), BF16),
                   jax.ShapeDtypeStruct((m, 2), I32), jax.ShapeDtypeStruct((m, 2), F32)],
        compiler_params=_cparams("arbitrary"),
        name="mix_router",
    )(a, b, x, w_out, g.reshape(1, d), wr, br)


ATT_T = 128


def _t5_bucket(dist):
    max_exact = N_BUCKETS // 2
    d = jnp.maximum(dist, 1).astype(F32)
    large = max_exact + (jnp.log(d / max_exact) / math.log(A_WIN / max_exact)
                         * (N_BUCKETS - max_exact)).astype(I32)
    return jnp.where(dist < max_exact, dist, jnp.minimum(large, N_BUCKETS - 1))


def _distance_logit_table(rel_bias, max_dist):
    dist = jnp.arange(max_dist + 1, dtype=I32)
    mult = jnp.zeros((max_dist + 1,), F32)
    for window, d in A_BRANCHES:
        mult = mult + ((dist % d == 0) & (dist <= window)).astype(F32)
    bias = rel_bias[_t5_bucket(dist)].astype(F32).T
    return jnp.where(mult[None, :] > 0, bias + jnp.log(jnp.maximum(mult, 1.0))[None, :], NEG)


def _attn_kernel(q_ref, k_ref, v_ref, tab_ref, o_ref, kb_ref, vb_ref):
    s_len = q_ref.shape[1]
    nb = s_len // ATT_T
    kb_ref[...] = k_ref[0].astype(BF16)
    vb_ref[...] = v_ref[0].astype(BF16)
    scale = A_HD ** -0.5
    for i in range(nb):
        rows = slice(i * ATT_T, (i + 1) * ATT_T)
        width = (i + 1) * ATT_T
        outs = []
        for hl in range(2):
            cols = slice(hl * A_HD, (hl + 1) * A_HD)
            q = (q_ref[0, rows, cols] * scale).astype(BF16)
            s = lax.dot_general(q, kb_ref[0:width, cols], (((1,), (1,)), ((), ())),
                                preferred_element_type=F32)
            s = s + tab_ref[hl, :, (nb - 1 - i) * ATT_T:nb * ATT_T]
            m = jnp.max(s, axis=-1, keepdims=True)
            p = jnp.exp(s - m)
            den = jnp.sum(p, axis=-1, keepdims=True)
            o = jnp.dot(p.astype(BF16), vb_ref[0:width, cols], preferred_element_type=F32)
            outs.append(o / den)
        o_ref[0, rows, :] = jnp.concatenate(outs, axis=-1)


def _attention_prompt(q, k, v, rel_bias):
    b, s_len, _ = q.shape
    nb = s_len // ATT_T
    tab1 = _distance_logit_table(rel_bias, s_len)
    period = s_len + ATT_T
    z = jnp.arange(period, dtype=I32)
    delta = jnp.where(z < s_len, (nb - 1) * ATT_T - z, (nb - 1) * ATT_T + period - z)
    row0 = jnp.where(delta[None] >= 0, tab1[:, jnp.clip(delta, 0, s_len)], NEG)
    tab = jnp.tile(row0, (1, ATT_T))[:, :ATT_T * (period - 1)].reshape(A_HEADS, ATT_T, period - 1)[:, :, :s_len]
    spec = pl.BlockSpec((1, s_len, 2 * A_HD), lambda bi, hp: (bi, 0, hp))
    return pl.pallas_call(
        _attn_kernel,
        grid=(b, A_HEADS // 2),
        in_specs=[spec, spec, spec, pl.BlockSpec((2, ATT_T, s_len), lambda bi, hp: (hp, 0, 0))],
        out_specs=spec,
        out_shape=jax.ShapeDtypeStruct(q.shape, F32),
        scratch_shapes=[pltpu.VMEM((s_len, 2 * A_HD), BF16), pltpu.VMEM((s_len, 2 * A_HD), BF16)],
        compiler_params=_cparams("arbitrary", "arbitrary"),
        name="dilated_attention",
    )(q, k, v, tab)


MLSTM_CHUNK = 256


def _log_sigmoid(x):
    return jnp.minimum(x, 0.0) - jnp.log(1.0 + jnp.exp(-jnp.abs(x)))


def _mlstm_kernel(qk_ref, v_ref, ob_ref, gc_ref, gr_ref, wc_ref, bc_ref, bifc_ref, bifr_ref, gh_ref,
                  o_ref, c_out_ref, n_out_ref, m_out_ref, xbuf_ref, c_ref, n_ref, m_ref):
    ci = pl.program_id(1)
    L = qk_ref.shape[1]

    @pl.when(ci == 0)
    def _():
        xbuf_ref[0:8, :] = jnp.zeros((8, 2 * B_W), F32)
        c_ref[...] = jnp.zeros_like(c_ref)
        n_ref[...] = jnp.zeros_like(n_ref)
        m_ref[...] = jnp.zeros_like(m_ref)

    x = qk_ref[0]
    xbuf_ref[8:8 + L, :] = x
    y = bc_ref[...] + wc_ref[3:4, :] * x
    for j in range(B_CONV - 1):
        y = y + wc_ref[j:j + 1, :] * xbuf_ref[5 + j:5 + j + L, :]
    xbuf_ref[0:8, :] = x[L - 8:L, :]
    y = _silu(y)

    gcol = gc_ref[0] + bifc_ref[...]
    grow = gr_ref[...] + bifr_ref[...]
    ri = lax.broadcasted_iota(I32, (L, L), 0)
    cj = lax.broadcasted_iota(I32, (L, L), 1)
    causal = ri >= cj
    lower = jnp.where(causal, 1.0, 0.0)
    b_col = jnp.dot(lower, _log_sigmoid(gcol), preferred_element_type=F32, precision=HIGHEST)
    b_row = lax.dot_general(_log_sigmoid(grow), lower, (((1,), (1,)), ((), ())),
                            preferred_element_type=F32, precision=HIGHEST)

    for h in range(B_HEADS):
        cols = slice(h * B_HD, (h + 1) * B_HD)
        qf = y[:, cols]
        q = qf.astype(BF16)
        kf = y[:, B_W + h * B_HD:B_W + (h + 1) * B_HD] * (B_HD ** -0.5)
        k = kf.astype(BF16)
        v = v_ref[0, :, cols].astype(BF16)
        bc = b_col[:, B_HEADS + h:B_HEADS + h + 1]
        ic = gcol[:, h:h + 1]
        br = b_row[B_HEADS + h:B_HEADS + h + 1, :]
        ir = grow[h:h + 1, :]
        m_prev = m_ref[h:h + 1, 0:1]
        dm = jnp.where(causal, bc - br + ir, NEG)
        inter = bc + m_prev
        mt = jnp.maximum(inter, jnp.max(dm, axis=1, keepdims=True))
        w_intra = jnp.exp(dm - mt)
        w_inter = jnp.exp(inter - mt)
        a = w_intra * lax.dot_general(q, k, (((1,), (1,)), ((), ())), preferred_element_type=F32)
        c_prev = c_ref[h]
        n_prev = n_ref[h:h + 1, :]
        num = (jnp.dot(a.astype(BF16), v, preferred_element_type=F32)
               + w_inter * jnp.dot(q, c_prev.astype(BF16), preferred_element_type=F32))
        den = (jnp.sum(a, axis=1, keepdims=True)
               + w_inter * jnp.sum(qf * n_prev, axis=1, keepdims=True))
        hb = num / jnp.maximum(jnp.abs(den), jnp.exp(-mt))
        m_new = mt[L - 1:L, :]
        b_last = bc[L - 1:L, :]
        g = jnp.exp(b_last - bc + ic - m_new)
        decay = jnp.exp(b_last + m_prev - m_new)
        kg = kf * g
        c_ref[h] = decay * c_prev + lax.dot_general(kg.astype(BF16), v, (((0,), (0,)), ((), ())),
                                                    preferred_element_type=F32)
        n_ref[h:h + 1, :] = decay * n_prev + jnp.sum(kg, axis=0, keepdims=True)
        m_ref[h:h + 1, :] = jnp.broadcast_to(m_new, (1, 128))
        hb = hb * lax.rsqrt(jnp.mean(hb * hb, axis=-1, keepdims=True) + EPS) * gh_ref[:, cols]
        o_ref[0, :, cols] = jax.nn.sigmoid(ob_ref[0, :, cols]) * hb

    @pl.when(ci == pl.num_programs(1) - 1)
    def _():
        c_out_ref[0] = c_ref[...]
        n_out_ref[0] = n_ref[...]
        m_out_ref[0] = m_ref[...]


def _mlstm_prompt(qk, v, ob, gcol, grow, w_conv, b_conv, b_if, g_head):
    b, s_len, _ = qk.shape
    L = min(MLSTM_CHUNK, s_len)
    nc = s_len // L
    bif_col = jnp.zeros((1, 128), F32).at[0, :2 * B_HEADS].set(b_if)
    bif_row = b_if.reshape(2 * B_HEADS, 1)
    seq = lambda w: pl.BlockSpec((1, L, w), lambda bi, ci: (bi, ci, 0))
    full = lambda shape: pl.BlockSpec(shape, lambda bi, ci: (0,) * len(shape))
    return pl.pallas_call(
        _mlstm_kernel,
        grid=(b, nc),
        in_specs=[seq(2 * B_W), seq(B_W), seq(B_W), seq(128),
                  pl.BlockSpec((2 * B_HEADS, L), lambda bi, ci: (0, bi * nc + ci)),
                  full((B_CONV, 2 * B_W)), full((1, 2 * B_W)), full((1, 128)), full((2 * B_HEADS, 1)),
                  full((1, B_W))],
        out_specs=[seq(B_W),
                   pl.BlockSpec((1, B_HEADS, B_HD, B_HD), lambda bi, ci: (bi, 0, 0, 0)),
                   pl.BlockSpec((1, 8, B_HD), lambda bi, ci: (bi, 0, 0)),
                   pl.BlockSpec((1, 8, 128), lambda bi, ci: (bi, 0, 0))],
        out_shape=[jax.ShapeDtypeStruct((b, s_len, B_W), F32),
                   jax.ShapeDtypeStruct((b, B_HEADS, B_HD, B_HD), F32),
                   jax.ShapeDtypeStruct((b, 8, B_HD), F32),
                   jax.ShapeDtypeStruct((b, 8, 128), F32)],
        scratch_shapes=[pltpu.VMEM((L + 8, 2 * B_W), F32), pltpu.VMEM((B_HEADS, B_HD, B_HD), F32),
                        pltpu.VMEM((8, B_HD), F32), pltpu.VMEM((8, 128), F32)],
        compiler_params=_cparams("arbitrary", "arbitrary"),
        name="mlstm_chunkwise",
    )(qk, v, ob, gcol, grow, w_conv, b_conv.reshape(1, -1), bif_col, bif_row, g_head.reshape(1, -1))


def _odd_in_kernel(x_ref, g_ref, w_ref, gcv_ref, bcv_ref, ws_ref, bs_ref, c_ref, xd_ref):
    tm = x_ref.shape[0]
    xn = _rms(x_ref[...], g_ref[...]).astype(BF16)
    u = jnp.dot(xn, w_ref[:, 0:C_W], preferred_element_type=F32)
    v = jnp.dot(xn, w_ref[:, C_W:2 * C_W], preferred_element_type=F32)
    xd_ref[...] = jnp.dot(xn, w_ref[:, 2 * C_W:], preferred_element_type=F32)
    mu = jnp.mean(v, axis=-1, keepdims=True)
    vc = v - mu
    var = jnp.mean(vc * vc, axis=-1, keepdims=True)
    vl = (vc * lax.rsqrt(var + EPS) * gcv_ref[...] + bcv_ref[...]).astype(BF16)
    ri = lax.broadcasted_iota(I32, (C_CHUNK, C_CHUNK), 0)
    cj = lax.broadcasted_iota(I32, (C_CHUNK, C_CHUNK), 1)
    for g in range(C_GROUPS):
        cols = slice(g * C_GD, (g + 1) * C_GD)
        wg = jnp.where(ri >= cj, ws_ref[g], 0.0).astype(BF16)
        for c in range(tm // C_CHUNK):
            rows = slice(c * C_CHUNK, (c + 1) * C_CHUNK)
            s = jnp.dot(wg, vl[rows, cols], preferred_element_type=F32) + bs_ref[:, g:g + 1]
            c_ref[rows, cols] = u[rows, cols] * s


def _odd_in(x, g, w, g_cv, b_cv, w_s, b_s, tm):
    m, d = x.shape
    n = w.shape[1]
    full = lambda shape: pl.BlockSpec(shape, lambda i: (0,) * len(shape))
    return pl.pallas_call(
        _odd_in_kernel,
        grid=(m // tm,),
        in_specs=[pl.BlockSpec((tm, d), lambda i: (i, 0)), full((1, d)), full((d, n)), full((1, C_W)),
                  full((1, C_W)), full((C_GROUPS, C_CHUNK, C_CHUNK)), full((C_CHUNK, C_GROUPS))],
        out_specs=[pl.BlockSpec((tm, C_W), lambda i: (i, 0)), pl.BlockSpec((tm, D_W), lambda i: (i, 0))],
        out_shape=[jax.ShapeDtypeStruct((m, C_W), F32), jax.ShapeDtypeStruct((m, D_W), F32)],
        compiler_params=_cparams("arbitrary"),
        name="odd_in_proj_gate",
    )(x, g.reshape(1, d), w, g_cv.reshape(1, -1), b_cv.reshape(1, -1), w_s, b_s.T)


S5_TT = 128
S5_PAD = 8
S5_HALF = D_GROUPS * D_STATE // 2


def _s5_discretize(a_re, a_im, log_dt, bm_re, bm_im, cm_re, cm_im):
    a_re, a_im = a_re.astype(F32), a_im.astype(F32)
    dt = jnp.exp(log_dt.astype(F32))[:, None]
    mag = jnp.exp(a_re * dt)
    ab_re = mag * jnp.cos(a_im * dt)
    ab_im = mag * jnp.sin(a_im * dt)
    inv = 1.0 / (a_re * a_re + a_im * a_im)
    f_re = ((ab_re - 1.0) * a_re + ab_im * a_im) * inv
    f_im = (ab_im * a_re - (ab_re - 1.0) * a_im) * inv
    bm_re, bm_im = bm_re.astype(F32), bm_im.astype(F32)
    bb_re = f_re[..., None] * bm_re - f_im[..., None] * bm_im
    bb_im = f_re[..., None] * bm_im + f_im[..., None] * bm_re
    return ab_re, ab_im, bb_re, bb_im


def _s5_matrices(bb_re, bb_im, cm_re, cm_im):
    gh = D_GROUPS // 2
    eye = jnp.eye(gh, dtype=F32)

    def in_half(bb):
        return jnp.einsum('gpc,gh->gchp', bb, eye).reshape(gh * D_GCH, gh * D_STATE)

    def out_half(cm):
        return jnp.einsum('gcp,gh->gphc', cm, eye).reshape(gh * D_STATE, gh * D_GCH)

    bd = jnp.stack([jnp.concatenate([in_half(bb_re[h * gh:(h + 1) * gh]), in_half(bb_im[h * gh:(h + 1) * gh])], axis=1)
                    for h in range(2)])
    cm = jnp.stack([jnp.concatenate([out_half(cm_re[h * gh:(h + 1) * gh].astype(F32)),
                                     -out_half(cm_im[h * gh:(h + 1) * gh].astype(F32))], axis=0)
                    for h in range(2)])
    return bd, cm


def _gelu_tanh(x):
    return 0.5 * x * (1.0 + jnp.tanh(math.sqrt(2.0 / math.pi) * (x + 0.044715 * (x * x * x))))


def _s5_kernel(x_ref, bd_ref, cm_ref, ar_ref, ai_ref, dsk_ref, wg_ref, bg_ref, o_ref, hr_out, hi_out,
               bu_ref, hs_ref, hr_ref, hi_ref):
    i = pl.program_id(0)
    nb, tt, _ = x_ref.shape
    stride = tt + S5_PAD
    half_in = D_W // 2
    nct = S5_HALF // 128

    @pl.when(i == 0)
    def _():
        hr_ref[...] = jnp.zeros_like(hr_ref)
        hi_ref[...] = jnp.zeros_like(hi_ref)

    x = x_ref[...].reshape(nb * tt, D_W)
    ys = []
    for h in range(2):
        lanes = slice(h * S5_HALF, (h + 1) * S5_HALF)
        bu = jnp.dot(x[:, h * half_in:(h + 1) * half_in].astype(BF16), bd_ref[h], preferred_element_type=F32)
        for c in range(2 * nct):
            for b in range(nb):
                bu_ref[c, b * stride:b * stride + tt, :] = bu[b * tt:(b + 1) * tt, c * 128:(c + 1) * 128]
        ar = jnp.broadcast_to(ar_ref[:, lanes], (nb, S5_HALF))
        ai = jnp.broadcast_to(ai_ref[:, lanes], (nb, S5_HALF))

        def step(t, carry):
            hr, hi = carry
            rows = pl.ds(t, nb, stride=stride)
            bur = jnp.concatenate([bu_ref[c, rows, :] for c in range(nct)], axis=1)
            bui = jnp.concatenate([bu_ref[nct + c, rows, :] for c in range(nct)], axis=1)
            nhr = ar * hr - ai * hi + bur
            nhi = ar * hi + ai * hr + bui
            for c in range(nct):
                hs_ref[c, rows, :] = nhr[:, c * 128:(c + 1) * 128]
                hs_ref[nct + c, rows, :] = nhi[:, c * 128:(c + 1) * 128]
            return nhr, nhi

        hr, hi = lax.fori_loop(0, tt, step, (hr_ref[:, lanes], hi_ref[:, lanes]))
        hr_ref[:, lanes] = hr
        hi_ref[:, lanes] = hi
        hs = jnp.concatenate(
            [jnp.concatenate([hs_ref[c, b * stride:b * stride + tt, :] for b in range(nb)], axis=0)
             for c in range(2 * nct)], axis=1)
        ys.append(jnp.dot(hs.astype(BF16), cm_ref[h], preferred_element_type=F32))
    y = jnp.concatenate(ys, axis=-1) + dsk_ref[...] * x
    g = _gelu_tanh(y)
    d_out = g * jax.nn.sigmoid(jnp.dot(g.astype(BF16), wg_ref[...], preferred_element_type=F32) + bg_ref[...])
    o_ref[...] = d_out.reshape(nb, tt, D_W)

    @pl.when(i == pl.num_programs(0) - 1)
    def _():
        hr_out[...] = hr_ref[...]
        hi_out[...] = hi_ref[...]


def _s5_prompt(xd, bd, cm, ab_re, ab_im, d_skip, w_glu, b_glu):
    nb, t_len, _ = xd.shape
    tt = min(S5_TT, t_len)
    n_state = D_GROUPS * D_STATE
    full = lambda shape: pl.BlockSpec(shape, lambda i: (0,) * len(shape))
    return pl.pallas_call(
        _s5_kernel,
        grid=(t_len // tt,),
        in_specs=[pl.BlockSpec((nb, tt, D_W), lambda i: (0, i, 0)), full(bd.shape), full(cm.shape),
                  full((1, n_state)), full((1, n_state)), full((1, D_W)), full((D_W, D_W)), full((1, D_W))],
        out_specs=[pl.BlockSpec((nb, tt, D_W), lambda i: (0, i, 0)), full((nb, n_state)), full((nb, n_state))],
        out_shape=[jax.ShapeDtypeStruct(xd.shape, F32), jax.ShapeDtypeStruct((nb, n_state), F32),
                   jax.ShapeDtypeStruct((nb, n_state), F32)],
        scratch_shapes=[pltpu.VMEM((2 * S5_HALF // 128, nb * (tt + S5_PAD), 128), F32),
                        pltpu.VMEM((2 * S5_HALF // 128, nb * (tt + S5_PAD), 128), F32),
                        pltpu.VMEM((nb, n_state), F32), pltpu.VMEM((nb, n_state), F32)],
        compiler_params=_cparams("arbitrary"),
        name="s5_scan_glu",
    )(xd, bd, cm, ab_re.reshape(1, n_state), ab_im.reshape(1, n_state), d_skip.reshape(1, D_W), w_glu,
      b_glu.reshape(1, D_W))


def _hdot(a, b):
    return jnp.dot(a, b, precision=HIGHEST, preferred_element_type=F32)


SAMPLE_NB = 4
SAMPLE_ROWS = 128


def _sample_attn_kernel(q_ref, kn_ref, vn_ref, tab_ref, tabn_ref, k1, k2, k3, v1, v2, v3, o_ref):
    ones = jnp.ones((A_HD, 128), F32)
    lane_sum = lambda a: jnp.dot(a, ones, preferred_element_type=F32, precision=HIGHEST)
    rh = SAMPLE_ROWS * A_HEADS
    for n in range(q_ref.shape[0]):
        q = q_ref[n] * (A_HD ** -0.5)
        qt = jnp.broadcast_to(q[None], (SAMPLE_ROWS, A_HEADS, A_HD)).reshape(rh, A_HD)
        s_new = lane_sum(kn_ref[n] * q) + tabn_ref[...]
        s_blocks = [(lane_sum(kb[n].reshape(rh, A_HD) * qt) + tab_ref[b]).reshape(SAMPLE_ROWS, A_HEADS, 128)
                    for b, kb in enumerate((k1, k2, k3))]
        m = s_new
        for s in s_blocks:
            m = jnp.maximum(m, jnp.max(s, axis=0))
        p_new = jnp.exp(s_new - m)
        den = p_new
        acc = p_new[:, :A_HD] * vn_ref[n]
        for s, vb in zip(s_blocks, (v1, v2, v3)):
            p = jnp.exp(s - m[None])
            den = den + jnp.sum(p, axis=0)
            acc = acc + jnp.sum(p[:, :, :A_HD] * vb[n], axis=0)
        o_ref[n] = acc / den[:, :A_HD]


def _sample_attention(q, k_new, v_new, cache_k, cache_v, rel_bias):
    n, L = cache_k.shape[0], cache_k.shape[1]
    assert L == A_WIN and all(w // d == SAMPLE_ROWS for w, d in A_BRANCHES)
    bias = rel_bias[_t5_bucket(jnp.arange(L + 1, dtype=I32))].astype(F32)
    r = jnp.arange(SAMPLE_ROWS, dtype=I32)
    tab = jnp.stack([bias[d * (SAMPLE_ROWS - r)].reshape(SAMPLE_ROWS * A_HEADS) for _, d in A_BRANCHES])
    tab = jnp.broadcast_to(tab[:, :, None], (len(A_BRANCHES), SAMPLE_ROWS * A_HEADS, 128))
    tab_new = jnp.broadcast_to((bias[0] + math.log(len(A_BRANCHES)))[:, None], (A_HEADS, 128))
    views, specs = [], []
    for _, d in A_BRANCHES:
        rows = L // d
        views.append(lambda c, rows=rows, d=d: c.reshape(n, rows, d, A_HEADS, A_HD))
        specs.append(pl.BlockSpec((SAMPLE_NB, SAMPLE_ROWS, None, A_HEADS, A_HD),
                                  lambda i, blk=rows // SAMPLE_ROWS - 1: (i, blk, 0, 0, 0)))
    row_spec = pl.BlockSpec((SAMPLE_NB, A_HEADS, A_HD), lambda i: (i, 0, 0))
    full = lambda shape: pl.BlockSpec(shape, lambda i: (0,) * len(shape))
    return pl.pallas_call(
        _sample_attn_kernel,
        grid=(n // SAMPLE_NB,),
        in_specs=[row_spec, row_spec, row_spec, full(tab.shape), full(tab_new.shape)] + specs + specs,
        out_specs=row_spec,
        out_shape=jax.ShapeDtypeStruct((n, A_HEADS, A_HD), F32),
        compiler_params=_cparams("arbitrary"),
        name="sample_attention",
    )(q, k_new, v_new, tab, tab_new, *[v(cache_k) for v in views], *[v(cache_v) for v in views])


def _sample_even_mixers(x, cache_k, cache_v, c0, n0, m0, conv0, rel_bias, g_mix, w_in, b_if, w_conv, b_conv,
                        g_head):
    n = x.shape[0]
    n_main = 3 * A_W + 4 * B_W
    segs = _EVEN_SEGS[:-1] + ((n_main, 2 * B_HEADS),)
    qa, ka, va, qkb, vb, ob, gates, _ = _norm_proj(x, g_mix, w_in, w_in[:, n_main:].T, segs, tm=n, precise=True)
    heads = lambda a: a.reshape(n, A_HEADS, A_HD)
    a_out = _sample_attention(heads(qa), heads(ka), heads(va), cache_k, cache_v, rel_bias).reshape(n, A_W)
    new_k = jnp.concatenate([cache_k[:, 1:], heads(ka)[:, None]], axis=1)
    new_v = jnp.concatenate([cache_v[:, 1:], heads(va)[:, None]], axis=1)
    xp = jnp.concatenate([conv0, qkb[:, None, :]], axis=1)
    qk = _silu(b_conv + jnp.sum(xp * w_conv[None], axis=1))
    q = qk[:, :B_W].reshape(n, B_HEADS, B_HD)
    k = qk[:, B_W:].reshape(n, B_HEADS, B_HD) * (B_HD ** -0.5)
    v = vb.reshape(n, B_HEADS, B_HD)
    ig = gates[:, :B_HEADS] + b_if[:B_HEADS]
    logf = _log_sigmoid(gates[:, B_HEADS:] + b_if[B_HEADS:])
    inter = logf + m0
    mt = jnp.maximum(inter, ig)
    w_intra = jnp.exp(ig - mt)
    w_inter = jnp.exp(inter - mt)
    a = w_intra * jnp.sum(q * k, axis=-1)
    num = a[..., None] * v + w_inter[..., None] * jnp.einsum('nhk,nhkv->nhv', q, c0, precision=HIGHEST)
    den = a + w_inter * jnp.sum(q * n0, axis=-1)
    hb = num / jnp.maximum(jnp.abs(den), jnp.exp(-mt))[..., None]
    g = jnp.exp(ig - mt)
    decay = jnp.exp(inter - mt)
    c_new = decay[..., None, None] * c0 + (g[..., None] * k)[..., :, None] * v[..., None, :]
    n_new = decay[..., None] * n0 + g[..., None] * k
    hb = hb * lax.rsqrt(jnp.mean(hb * hb, axis=-1, keepdims=True) + EPS) * g_head.reshape(B_HEADS, B_HD)
    b_out = jax.nn.sigmoid(ob) * hb.reshape(n, B_W)
    return a_out, b_out, new_k, new_v, c_new, n_new, mt, xp[:, 1:]


def _sample_odd_mixers(x, h_re0, h_im0, g_mix, w_in, g_cv, b_cv, w_s, b_s, ab_re, ab_im, bb_re, bb_im,
                       cm_re, cm_im, d_skip, w_glu, b_glu):
    n = x.shape[0]
    h = _rms(x, g_mix)
    proj = _hdot(h, w_in)
    u, v, xd = proj[:, :C_W], proj[:, C_W:2 * C_W], proj[:, 2 * C_W:]
    mu = jnp.mean(v, axis=-1, keepdims=True)
    vc = v - mu
    v = vc * lax.rsqrt(jnp.mean(vc * vc, axis=-1, keepdims=True) + EPS) * g_cv + b_cv
    s = (w_s[:, 0, 0][None, :, None] * v.reshape(n, C_GROUPS, C_GD) + b_s[:, 0][None, :, None]).reshape(n, C_W)
    c_out = u * s
    xg = xd.reshape(n, D_GROUPS, D_GCH)
    bu_re = jnp.einsum('gpc,ngc->ngp', bb_re, xg, precision=HIGHEST)
    bu_im = jnp.einsum('gpc,ngc->ngp', bb_im, xg, precision=HIGHEST)
    hr = ab_re * h_re0 - ab_im * h_im0 + bu_re
    hi = ab_re * h_im0 + ab_im * h_re0 + bu_im
    y = (jnp.einsum('gcp,ngp->ngc', cm_re, hr, precision=HIGHEST)
         - jnp.einsum('gcp,ngp->ngc', cm_im, hi, precision=HIGHEST) + d_skip * xg)
    g = _gelu_tanh(y.reshape(n, D_W))
    d_out = g * jax.nn.sigmoid(_hdot(g, w_glu) + b_glu)
    return c_out, d_out, v, hr, hi


_EVEN_SEGS = ((0, A_W), (A_W, A_W), (2 * A_W, A_W), (3 * A_W, 2 * B_W), (3 * A_W + 2 * B_W, B_W),
              (3 * A_W + 3 * B_W, B_W), (3 * A_W + 4 * B_W, 128))


def kernel(x_prompt, x_sample, cache_a_k, cache_a_v, state_b_c, state_b_n, state_b_m, state_b_conv,
           state_d_re, state_d_im, rel_bias, g_mix, g_ffn, g_final, w_in_e, b_if, w_conv_b, b_conv_b,
           g_bhead, w_out_e, w1_e, w3_e, w2_e, w_in_o, g_cv, b_cv, w_s, b_s, a_re, a_im, log_dt,
           bm_re, bm_im, cm_re, cm_im, d_skip, w_glu, b_glu, w_out_o, w_router, b_router,
           w1_m, w3_m, w2_m):
    bp, sp, d = x_prompt.shape
    ns = x_sample.shape[0]
    mp = bp * sp
    xp = x_prompt.reshape(mp, d)
    xs = x_sample.reshape(ns, d)

    w_in = w_in_e[0]
    n_main = 3 * A_W + 4 * B_W
    w_gate = w_in[:, n_main:]
    w_cat = jnp.concatenate([w_in[:, :n_main], jnp.pad(w_gate, ((0, 0), (0, 128 - 2 * B_HEADS)))], axis=1)
    q, k, v, qkb, vb, ob, gcol, grow = _norm_proj(
        xp, g_mix[0], w_cat.astype(BF16), w_gate.T.astype(BF16), _EVEN_SEGS, tm=512)
    shp = lambda a: a.reshape(bp, sp, a.shape[-1])
    a_out = _attention_prompt(shp(q), shp(k), shp(v), rel_bias)
    b_out, pc, pn, pm = _mlstm_prompt(shp(qkb), shp(vb), shp(ob), shp(gcol), grow, w_conv_b[0], b_conv_b[0],
                                      b_if[0], g_bhead[0])
    x1p = _mix_ffn(a_out.reshape(mp, A_W), b_out.reshape(mp, B_W), xp, w_out_e[0].astype(BF16), g_ffn[0],
                   w1_e[0], w3_e[0], w2_e[0], tm=1024, tf=512)

    sa, sb, s_k, s_v, sc, sn, sm, s_conv = _sample_even_mixers(
        xs, cache_a_k[0], cache_a_v[0], state_b_c[0], state_b_n[0], state_b_m[0], state_b_conv[0], rel_bias,
        g_mix[0], w_in, b_if[0], w_conv_b[0], b_conv_b[0], g_bhead[0])
    x1s = _mix_ffn(sa, sb, xs, w_out_e[0], g_ffn[0], w1_e[0], w3_e[0], w2_e[0], tm=ns, tf=512, precise=True)

    ab_re, ab_im, bb_re, bb_im = _s5_discretize(a_re[0], a_im[0], log_dt[0], bm_re[0], bm_im[0], cm_re[0],
                                                cm_im[0])
    bd, cmat = _s5_matrices(bb_re, bb_im, cm_re[0], cm_im[0])
    c_out, xd = _odd_in(x1p, g_mix[1], w_in_o[0].astype(BF16), g_cv[0], b_cv[0], w_s[0], b_s[0], tm=512)
    d_out, p_hr, p_hi = _s5_prompt(xd.reshape(bp, sp, D_W), bd.astype(BF16), cmat.astype(BF16), ab_re, ab_im,
                                   d_skip[0], w_glu[0].astype(BF16), b_glu[0])
    wr = jnp.pad(w_router[0].astype(F32), ((0, 0), (0, 128 - N_EXPERTS)))
    br = jnp.pad(b_router[0].astype(F32), (0, 128 - N_EXPERTS)).reshape(1, 128)
    x2p, hnp, idp, gtp = _mix_router(c_out, d_out.reshape(mp, D_W), x1p, w_out_o[0].astype(BF16), g_ffn[1], wr, br,
                                     tm=512)

    sc_out, sd_out, s_cv, s_hr, s_hi = _sample_odd_mixers(
        x1s, state_d_re[0], state_d_im[0], g_mix[1], w_in_o[0], g_cv[0], b_cv[0], w_s[0], b_s[0], ab_re, ab_im,
        bb_re, bb_im, cm_re[0], cm_im[0], d_skip[0], w_glu[0], b_glu[0])
    x2s, hns, ids_s, gts = _mix_router(sc_out, sd_out, x1s, w_out_o[0], g_ffn[1], wr, br, tm=ns, precise=True)

    pad = ((0, ROUTE_TILE - ns), (0, 0))
    y_p, y_s = _moe_block(x2p, hnp, idp, gtp, jnp.pad(x2s, pad), jnp.pad(hns, pad),
                          jnp.pad(ids_s, pad, constant_values=-1), jnp.pad(gts, pad),
                          w1_m[0], w3_m[0], w2_m[0], g_final)
    y_prompt = y_p.reshape(bp, sp, d)
    y_sample = y_s[:ns].reshape(ns, 1, d)

    e = lambda a: a[None]
    p_k = k.reshape(bp, sp, A_HEADS, A_HD)
    p_v = v.reshape(bp, sp, A_HEADS, A_HD)
    p_conv = qkb.reshape(bp, sp, 2 * B_W)[:, sp - (B_CONV - 1):]
    return (y_prompt, y_sample, e(p_k), e(p_v), e(s_k), e(s_v), e(pc), e(sc), e(pn[:, :B_HEADS]), e(sn),
            e(pm[:, :B_HEADS, 0]), e(sm), e(p_conv), e(s_conv), e(s_cv.reshape(ns, 1, C_W)),
            e(p_hr.reshape(bp, D_GROUPS, D_STATE)), e(s_hr), e(p_hi.reshape(bp, D_GROUPS, D_STATE)), e(s_hi))
```

```python
import functools
import math

import jax
import jax.numpy as jnp
from jax import lax
from jax.experimental import pallas as pl
from jax.experimental.pallas import tpu as pltpu

F32 = jnp.float32
BF16 = jnp.bfloat16
I32 = jnp.int32
HIGHEST = lax.Precision.HIGHEST

D_MODEL = 1024
A_HEADS, A_HD = 8, 64
A_W = A_HEADS * A_HD
A_BRANCHES = ((128, 1), (512, 4), (2048, 16))
A_WIN = 2048
N_BUCKETS = 32
B_HEADS, B_HD = 4, 128
B_W = B_HEADS * B_HD
B_CONV = 4
C_GROUPS, C_GD = 4, 128
C_W = C_GROUPS * C_GD
C_CHUNK = 128
D_GROUPS, D_GCH, D_STATE = 32, 16, 64
D_W = D_GROUPS * D_GCH
D_FF = 3584
N_EXPERTS = 8
TOP_K = 2
EPS = 1e-6
NEG = -1e30

VMEM_LIMIT = 56 * 1024 * 1024

ROUTE_TILE = 256
FFN_TILE = 1024
FFN_ROW_CHUNK = 1024
GATHER_TOKENS = 2 * ROUTE_TILE
COMBINE_ALIGN = 16
COMBINE_WIN = ROUTE_TILE + COMBINE_ALIGN


def _cparams(*sem):
    return pltpu.CompilerParams(dimension_semantics=sem, vmem_limit_bytes=VMEM_LIMIT)


def _moe_plan(ids, n_tok_tiles):
    mt = ids.shape[0]
    n_assign = 2 * mt
    n_ffn_tiles = -(-(n_assign + N_EXPERTS * (FFN_TILE - 1)) // FFN_TILE) + 1
    n_sub = n_ffn_tiles * (FFN_TILE // ROUTE_TILE)
    flat = ids.reshape(n_assign)
    onehot = (flat[:, None] == jnp.arange(N_EXPERTS, dtype=I32)[None, :]).astype(I32)
    csum = jnp.cumsum(onehot, axis=0)
    rank = jnp.sum((csum - onehot) * onehot, axis=1)
    counts = csum[-1]
    seg = ((counts + FFN_TILE - 1) // FFN_TILE) * FFN_TILE
    seg_end = jnp.cumsum(seg)
    seg_off = seg_end - seg
    dest = jnp.where(flat >= 0, seg_off[jnp.clip(flat, 0)] + rank, -1).reshape(mt, 2)
    total = seg_end[-1]
    tile_start = jnp.arange(n_ffn_tiles, dtype=I32) * FFN_TILE
    tile_expert = jnp.minimum(jnp.sum((tile_start[:, None] >= seg_end[None, :]).astype(I32), axis=1),
                              N_EXPERTS - 1).astype(I32)
    n_used = (total // FFN_TILE).astype(I32)
    per_tile = 2 * ROUTE_TILE
    cnt_tile = jnp.concatenate([jnp.zeros((1, N_EXPERTS), I32), csum[per_tile - 1::per_tile]])
    lo = seg_off[None, :] + cnt_tile[:-1]
    hi = seg_off[None, :] + cnt_tile[1:]
    sub_start = jnp.arange(n_sub, dtype=I32) * ROUTE_TILE
    sub_expert = jnp.minimum(jnp.sum((sub_start[:, None] >= seg_end[None, :]).astype(I32), axis=1),
                             N_EXPERTS - 1)
    lo_e, hi_e = lo.T[sub_expert], hi.T[sub_expert]
    jlo = jnp.sum((hi_e <= sub_start[:, None]).astype(I32), axis=1).astype(I32)
    jhi = (jnp.sum((lo_e < sub_start[:, None] + ROUTE_TILE).astype(I32), axis=1) - 1).astype(I32)
    win = lo // COMBINE_ALIGN
    flat1 = lambda a: a.reshape(-1).astype(I32)
    return dest, tile_expert, n_used, (jlo, jhi), (flat1(win), flat1(lo), flat1(hi)), n_ffn_tiles


def _moe_gather_kernel(jlo_ref, jhi_ref, hp_hbm, hs_hbm, dt_ref, gt_ref, xs_ref, gs_ref,
                       hn_ref, acc_ref, g_ref, sem):
    s = pl.program_id(0)
    mp = hp_hbm.shape[0]

    @pl.when(s == 0)
    def _():
        copies = (pltpu.make_async_copy(hp_hbm, hn_ref.at[0:mp], sem.at[0]),
                  pltpu.make_async_copy(hs_hbm, hn_ref.at[mp:mp + ROUTE_TILE], sem.at[1]))
        for c in copies:
            c.start()
        hn_ref[mp + ROUTE_TILE:, :] = jnp.zeros((ROUTE_TILE, hn_ref.shape[1]), BF16)
        for c in copies:
            c.wait()

    acc_ref[...] = jnp.zeros_like(acc_ref)
    g_ref[...] = jnp.zeros_like(g_ref)
    rows = lax.broadcasted_iota(I32, (ROUTE_TILE, GATHER_TOKENS), 0) + s * ROUTE_TILE
    jlo = jlo_ref[s]

    def body(it, carry):
        off = pl.multiple_of(jlo * ROUTE_TILE + it * GATHER_TOKENS, ROUTE_TILE)
        dt = dt_ref[:, pl.ds(off, GATHER_TOKENS)]
        gt = gt_ref[:, pl.ds(off, GATHER_TOKENS)]
        m1 = dt[0:1, :] == rows
        m2 = dt[1:2, :] == rows
        onehot = jnp.where(m1 | m2, 1.0, 0.0).astype(BF16)
        acc_ref[...] += jnp.dot(onehot, hn_ref[pl.ds(off, GATHER_TOKENS), :], preferred_element_type=F32)
        g_ref[...] += jnp.sum(jnp.where(m1, gt[0:1, :], 0.0) + jnp.where(m2, gt[1:2, :], 0.0),
                              axis=1, keepdims=True)
        return carry

    tiles_per_iter = GATHER_TOKENS // ROUTE_TILE
    lax.fori_loop(0, (jhi_ref[s] - jlo + tiles_per_iter) // tiles_per_iter, body, 0)
    xs_ref[...] = acc_ref[...].astype(BF16)
    gs_ref[...] = g_ref[...]


def _moe_gather(hn_p, hn_s, dest_t, gates_t, glist, n_sub):
    jlo, jhi = glist
    mp, d = hn_p.shape
    mt = dest_t.shape[1]
    grid_spec = pltpu.PrefetchScalarGridSpec(
        num_scalar_prefetch=2,
        grid=(n_sub,),
        in_specs=[
            pl.BlockSpec(memory_space=pl.ANY),
            pl.BlockSpec(memory_space=pl.ANY),
            pl.BlockSpec((2, mt), lambda s, *_: (0, 0)),
            pl.BlockSpec((2, mt), lambda s, *_: (0, 0)),
        ],
        out_specs=[
            pl.BlockSpec((ROUTE_TILE, d), lambda s, *_: (s, 0)),
            pl.BlockSpec((ROUTE_TILE, 1), lambda s, *_: (s, 0)),
        ],
        scratch_shapes=[pltpu.VMEM((mp + 2 * ROUTE_TILE, d), BF16), pltpu.VMEM((ROUTE_TILE, d), F32),
                        pltpu.VMEM((ROUTE_TILE, 1), F32), pltpu.SemaphoreType.DMA((2,))],
    )
    return pl.pallas_call(
        _moe_gather_kernel,
        grid_spec=grid_spec,
        out_shape=[jax.ShapeDtypeStruct((n_sub * ROUTE_TILE, d), BF16),
                   jax.ShapeDtypeStruct((n_sub * ROUTE_TILE, 1), F32)],
        compiler_params=_cparams("arbitrary"),
        name="moe_gather",
    )(jlo, jhi, hn_p, hn_s, dest_t, gates_t)


def _moe_ffn_kernel(te_ref, nu_ref, x_ref, gs_ref, w1_ref, w3_ref, w2_ref, y_ref, acc_ref):
    t, f = pl.program_id(0), pl.program_id(1)
    nf = pl.num_programs(1)

    @pl.when(t < nu_ref[0])
    def _():
        @pl.when(f == 0)
        def _():
            acc_ref[...] = jnp.zeros_like(acc_ref)

        w1, w3, w2 = w1_ref[0].astype(BF16), w3_ref[0].astype(BF16), w2_ref[0].astype(BF16)
        for r in range(0, FFN_TILE, FFN_ROW_CHUNK):
            rows = slice(r, r + FFN_ROW_CHUNK)
            x = x_ref[rows, :]
            a = jnp.dot(x, w1, preferred_element_type=F32)
            b = jnp.dot(x, w3, preferred_element_type=F32)
            acc_ref[rows, :] += jnp.dot((_silu(a) * b).astype(BF16), w2, preferred_element_type=F32)

        @pl.when(f == nf - 1)
        def _():
            y_ref[...] = (acc_ref[...] * gs_ref[...]).astype(BF16)

    @pl.when((t >= nu_ref[0]) & (f == nf - 1))
    def _():
        y_ref[...] = jnp.zeros_like(y_ref)


def _moe_ffn(xs, gs, w1, w3, w2, tile_expert, n_used, n_ffn_tiles, tf):
    d = xs.shape[1]
    ff = w1.shape[2]
    nf = ff // tf

    def fidx(t, f, nu):
        return jnp.where(t < nu[0], f, nf - 1)

    grid_spec = pltpu.PrefetchScalarGridSpec(
        num_scalar_prefetch=2,
        grid=(n_ffn_tiles, nf),
        in_specs=[
            pl.BlockSpec((FFN_TILE, d), lambda t, f, te, nu: (t, 0)),
            pl.BlockSpec((FFN_TILE, 1), lambda t, f, te, nu: (t, 0)),
            pl.BlockSpec((1, d, tf), lambda t, f, te, nu: (te[t], 0, fidx(t, f, nu))),
            pl.BlockSpec((1, d, tf), lambda t, f, te, nu: (te[t], 0, fidx(t, f, nu))),
            pl.BlockSpec((1, tf, d), lambda t, f, te, nu: (te[t], fidx(t, f, nu), 0)),
        ],
        out_specs=pl.BlockSpec((FFN_TILE, d), lambda t, f, te, nu: (t, 0)),
        scratch_shapes=[pltpu.VMEM((FFN_TILE, d), F32)],
    )
    return pl.pallas_call(
        _moe_ffn_kernel,
        grid_spec=grid_spec,
        out_shape=jax.ShapeDtypeStruct(xs.shape, BF16),
        compiler_params=_cparams("arbitrary", "arbitrary"),
        name="moe_ffn",
    )(tile_expert, n_used.reshape(1), xs, gs, w1, w3, w2)


def _moe_combine_kernel(win_ref, lo_ref, hi_ref, xp_ref, xs_ref, d_ref, g_ref, *rest):
    ys_refs = rest[:N_EXPERTS]
    yp_ref, ysm_ref, acc_ref = rest[N_EXPERTS:]
    j = pl.program_id(0)
    n_prompt_tiles = pl.num_programs(0) - 1

    @pl.when(j < n_prompt_tiles)
    def _():
        acc_ref[...] = xp_ref[...]

    @pl.when(j == n_prompt_tiles)
    def _():
        acc_ref[...] = xs_ref[...]

    for e in range(N_EXPERTS):
        lo, hi = lo_ref[j * N_EXPERTS + e], hi_ref[j * N_EXPERTS + e]

        @pl.when(hi > lo)
        def _():
            d = d_ref[...]
            d = jnp.where((d >= lo) & (d < hi), d, -1)
            cols = (lax.broadcasted_iota(I32, (ROUTE_TILE, COMBINE_WIN), 1)
                    + win_ref[j * N_EXPERTS + e] * COMBINE_ALIGN)
            onehot = jnp.where((d[:, 0:1] == cols) | (d[:, 1:2] == cols), 1.0, 0.0).astype(BF16)
            acc_ref[...] += jnp.dot(onehot, ys_refs[e][...], preferred_element_type=F32)

    y = _rms(acc_ref[...], g_ref[...])

    @pl.when(j < n_prompt_tiles)
    def _():
        yp_ref[...] = y

    @pl.when(j == n_prompt_tiles)
    def _():
        ysm_ref[...] = y


def _moe_combine(x2_p, x2_s, dest, ys, g_final, clist):
    win, lo, hi = clist
    mp, d = x2_p.shape
    n_prompt_tiles = mp // ROUTE_TILE

    def ptile(j, *_):
        return (jnp.minimum(j, n_prompt_tiles - 1), 0)

    def window(e):
        return pl.BlockSpec((pl.Element(COMBINE_WIN), pl.Element(d)),
                            lambda j, win, lo, hi: (win[j * N_EXPERTS + e] * COMBINE_ALIGN, 0))

    grid_spec = pltpu.PrefetchScalarGridSpec(
        num_scalar_prefetch=3,
        grid=(n_prompt_tiles + 1,),
        in_specs=[
            pl.BlockSpec((ROUTE_TILE, d), ptile),
            pl.BlockSpec((ROUTE_TILE, d), lambda j, *_: (0, 0)),
            pl.BlockSpec((ROUTE_TILE, 2), lambda j, *_: (j, 0)),
            pl.BlockSpec((1, d), lambda j, *_: (0, 0)),
        ] + [window(e) for e in range(N_EXPERTS)],
        out_specs=[pl.BlockSpec((ROUTE_TILE, d), ptile), pl.BlockSpec((ROUTE_TILE, d), lambda j, *_: (0, 0))],
        scratch_shapes=[pltpu.VMEM((ROUTE_TILE, d), F32)],
    )
    return pl.pallas_call(
        _moe_combine_kernel,
        grid_spec=grid_spec,
        out_shape=[jax.ShapeDtypeStruct((mp, d), F32), jax.ShapeDtypeStruct((ROUTE_TILE, d), F32)],
        compiler_params=_cparams("arbitrary"),
        name="moe_combine",
    )(win, lo, hi, x2_p, x2_s, dest, g_final.reshape(1, d), *([ys] * N_EXPERTS))


def _moe_block(x2_p, hn_p, ids_p, gates_p, x2_s, hn_s, ids_s, gates_s, w1, w3, w2, g_final, tf=512):
    mp = x2_p.shape[0]
    n_tok_tiles = mp // ROUTE_TILE + 1
    ids = jnp.concatenate([ids_p, ids_s])
    gates = jnp.concatenate([gates_p, gates_s])
    dest, tile_expert, n_used, glist, clist, n_ffn_tiles = _moe_plan(ids, n_tok_tiles)
    n_sub = n_ffn_tiles * (FFN_TILE // ROUTE_TILE)
    spare = ((0, 0), (0, GATHER_TOKENS - ROUTE_TILE))
    xs, gs = _moe_gather(hn_p, hn_s, jnp.pad(dest.T, spare, constant_values=-1), jnp.pad(gates.T, spare), glist,
                         n_sub)
    ys = _moe_ffn(xs, gs, w1, w3, w2, tile_expert, n_used, n_ffn_tiles, tf)
    return _moe_combine(x2_p, x2_s, dest, ys, g_final, clist)


def _rms(x, g):
    return x * lax.rsqrt(jnp.mean(x * x, axis=-1, keepdims=True) + EPS) * g


def _mm(a, b, precise):
    if precise:
        return jnp.dot(a.astype(F32), b.astype(F32), preferred_element_type=F32, precision=HIGHEST)
    return jnp.dot(a.astype(BF16), b.astype(BF16), preferred_element_type=F32)


def _silu(x):
    return x * jax.nn.sigmoid(x)


def _norm_proj_kernel(x_ref, g_ref, w_ref, wt_ref, *out_refs, segs, precise):
    xn = _rms(x_ref[...], g_ref[...])
    xn = xn if precise else xn.astype(BF16)
    for (start, width), o_ref in zip(segs, out_refs[:-1]):
        for c in range(0, width, 512):
            cw = min(512, width - c)
            o_ref[:, c:c + cw] = _mm(xn, w_ref[:, start + c:start + c + cw], precise)
    wt = wt_ref[...]
    out_refs[-1][...] = lax.dot_general(
        wt.astype(xn.dtype), xn, (((1,), (1,)), ((), ())), preferred_element_type=F32,
        precision=HIGHEST if precise else None)


def _norm_proj(x, g, w, wt, segs, tm, precise=False):
    m, d = x.shape
    n = w.shape[1]
    nt = wt.shape[0]
    out_shape = [jax.ShapeDtypeStruct((m, width), F32) for _, width in segs]
    out_shape.append(jax.ShapeDtypeStruct((nt, m), F32))
    out_specs = [pl.BlockSpec((tm, width), lambda i: (i, 0)) for _, width in segs]
    out_specs.append(pl.BlockSpec((nt, tm), lambda i: (0, i)))
    return pl.pallas_call(
        functools.partial(_norm_proj_kernel, segs=tuple(segs), precise=precise),
        grid=(m // tm,),
        in_specs=[
            pl.BlockSpec((tm, d), lambda i: (i, 0)),
            pl.BlockSpec((1, d), lambda i: (0, 0)),
            pl.BlockSpec((d, n), lambda i: (0, 0)),
            pl.BlockSpec((nt, d), lambda i: (0, 0)),
        ],
        out_specs=out_specs,
        out_shape=out_shape,
        compiler_params=_cparams("arbitrary"),
        name="norm_proj",
    )(x, g.reshape(1, d), w, wt)


def _mix_ffn_kernel(a_ref, b_ref, x_ref, wo_ref, g_ref, w1_ref, w3_ref, w2_ref, o_ref, hn_ref, *, precise):
    wa = a_ref.shape[1]

    @pl.when(pl.program_id(1) == 0)
    def _():
        x1 = (x_ref[...] + _mm(a_ref[...], wo_ref[0:wa, :], precise)
              + _mm(b_ref[...], wo_ref[wa:, :], precise))
        o_ref[...] = x1
        hn_ref[...] = _rms(x1, g_ref[...]).astype(hn_ref.dtype)

    wdt = F32 if precise else BF16
    w1, w3, w2 = w1_ref[...].astype(wdt), w3_ref[...].astype(wdt), w2_ref[...].astype(wdt)
    tm = hn_ref.shape[0]
    chunk = min(tm, FFN_ROW_CHUNK)
    for r in range(0, tm, chunk):
        rows = slice(r, r + chunk)
        hn = hn_ref[rows, :]
        o_ref[rows, :] += _mm(_silu(_mm(hn, w1, precise)) * _mm(hn, w3, precise), w2, precise)


def _mix_ffn(a, b, x, w_out, g, w1, w3, w2, tm, tf, precise=False):
    m, d = x.shape
    wa, wb = a.shape[1], b.shape[1]
    ff = w1.shape[1]
    return pl.pallas_call(
        functools.partial(_mix_ffn_kernel, precise=precise),
        grid=(m // tm, ff // tf),
        in_specs=[
            pl.BlockSpec((tm, wa), lambda i, f: (i, 0)),
            pl.BlockSpec((tm, wb), lambda i, f: (i, 0)),
            pl.BlockSpec((tm, d), lambda i, f: (i, 0)),
            pl.BlockSpec((wa + wb, d), lambda i, f: (0, 0)),
            pl.BlockSpec((1, d), lambda i, f: (0, 0)),
            pl.BlockSpec((d, tf), lambda i, f: (0, f)),
            pl.BlockSpec((d, tf), lambda i, f: (0, f)),
            pl.BlockSpec((tf, d), lambda i, f: (f, 0)),
        ],
        out_specs=pl.BlockSpec((tm, d), lambda i, f: (i, 0)),
        out_shape=jax.ShapeDtypeStruct((m, d), F32),
        scratch_shapes=[pltpu.VMEM((tm, d), F32 if precise else BF16)],
        compiler_params=_cparams("arbitrary", "arbitrary"),
        name="mix_ffn",
    )(a, b, x, w_out, g.reshape(1, d), w1, w3, w2)


def _mix_router_kernel(a_ref, b_ref, x_ref, wo_ref, g_ref, wr_ref, br_ref, x2_ref, hn_ref, ids_ref, gate_ref,
                       *, precise):
    wa = a_ref.shape[1]
    x2 = (x_ref[...] + _mm(a_ref[...], wo_ref[0:wa, :], precise)
          + _mm(b_ref[...], wo_ref[wa:, :], precise))
    x2_ref[...] = x2
    hn = _rms(x2, g_ref[...])
    hn_ref[...] = hn.astype(BF16)
    wr = wr_ref[...]
    if precise:
        logits = jnp.dot(hn, wr, preferred_element_type=F32, precision=HIGHEST)
    else:
        hn_hi, wr_hi = hn.astype(BF16), wr.astype(BF16)
        hn_lo, wr_lo = (hn - hn_hi.astype(F32)).astype(BF16), (wr - wr_hi.astype(F32)).astype(BF16)
        logits = (jnp.dot(hn_hi, wr_hi, preferred_element_type=F32)
                  + (jnp.dot(hn_lo, wr_hi, preferred_element_type=F32)
                     + jnp.dot(hn_hi, wr_lo, preferred_element_type=F32)))
    lane = lax.broadcasted_iota(I32, logits.shape, 1)
    real = lane < N_EXPERTS
    biased = jnp.where(real, logits + br_ref[...], -jnp.inf)
    m1 = jnp.max(biased, axis=-1, keepdims=True)
    i1 = jnp.min(jnp.where(biased == m1, lane, 128), axis=-1, keepdims=True)
    rest = jnp.where(lane == i1, -jnp.inf, biased)
    m2 = jnp.max(rest, axis=-1, keepdims=True)
    i2 = jnp.min(jnp.where(rest == m2, lane, 128), axis=-1, keepdims=True)
    l1 = jnp.sum(jnp.where(lane == i1, logits, 0.0), axis=-1, keepdims=True)
    l2 = jnp.sum(jnp.where(lane == i2, logits, 0.0), axis=-1, keepdims=True)
    mx = jnp.maximum(l1, l2)
    e1, e2 = jnp.exp(l1 - mx), jnp.exp(l2 - mx)
    two = lax.broadcasted_iota(I32, ids_ref.shape, 1)
    ids_ref[...] = jnp.where(two == 0, i1, i2)
    gate_ref[...] = jnp.where(two == 0, e1, e2) / (e1 + e2)


def _mix_router(a, b, x, w_out, g, wr, br, tm, precise=False):
    m, d = x.shape
    wa, wb = a.shape[1], b.shape[1]
    return pl.pallas_call(
        functools.partial(_mix_router_kernel, precise=precise),
        grid=(m // tm,),
        in_specs=[
            pl.BlockSpec((tm, wa), lambda i: (i, 0)),
            pl.BlockSpec((tm, wb), lambda i: (i, 0)),
            pl.BlockSpec((tm, d), lambda i: (i, 0)),
            pl.BlockSpec((wa + wb, d), lambda i: (0, 0)),
            pl.BlockSpec((1, d), lambda i: (0, 0)),
            pl.BlockSpec((d, 128), lambda i: (0, 0)),
            pl.BlockSpec((1, 128), lambda i: (0, 0)),
        ],
        out_specs=[
            pl.BlockSpec((tm, d), lambda i: (i, 0)),
            pl.BlockSpec((tm, d), lambda i: (i, 0)),
            pl.BlockSpec((tm, 2), lambda i: (i, 0)),
            pl.BlockSpec((tm, 2), lambda i: (i, 0)),
        ],
        out_shape=[jax.ShapeDtypeStruct((m, d), F32), jax.ShapeDtypeStruct((m, d), BF16),
                   jax.ShapeDtypeStruct((m, 2), I32), jax.ShapeDtypeStruct((m, 2), F32)],
        compiler_params=_cparams("arbitrary"),
        name="mix_router",
    )(a, b, x, w_out, g.reshape(1, d), wr, br)


ATT_T = 512


def _t5_bucket(dist):
    max_exact = N_BUCKETS // 2
    d = jnp.maximum(dist, 1).astype(F32)
    large = max_exact + (jnp.log(d / max_exact) / math.log(A_WIN / max_exact)
                         * (N_BUCKETS - max_exact)).astype(I32)
    return jnp.where(dist < max_exact, dist, jnp.minimum(large, N_BUCKETS - 1))


def _distance_logit_table(rel_bias, max_dist):
    dist = jnp.arange(max_dist + 1, dtype=I32)
    mult = jnp.zeros((max_dist + 1,), F32)
    for window, d in A_BRANCHES:
        mult = mult + ((dist % d == 0) & (dist <= window)).astype(F32)
    bias = rel_bias[_t5_bucket(dist)].astype(F32).T
    return jnp.where(mult[None, :] > 0, bias + jnp.log(jnp.maximum(mult, 1.0))[None, :], NEG)


def _window_shift_copies(caches, news, outs, sem):
    copies = []
    for i, (c, nw, o) in enumerate(zip(caches, news, outs)):
        keep = c.shape[1] - 1
        copies.append(pltpu.make_async_copy(c.at[:, pl.ds(1, keep)], o.at[:, pl.ds(0, keep)], sem.at[2 * i]))
        copies.append(pltpu.make_async_copy(nw, o.at[:, pl.ds(keep, 1)], sem.at[2 * i + 1]))
    return copies


def _attn_kernel(q_ref, k_ref, v_ref, tab_ref, ck_hbm, cv_hbm, nk_hbm, nv_hbm, o_ref, ok_hbm, ov_hbm,
                 kb_ref, vb_ref, sem):
    step = pl.program_id(0) * pl.num_programs(1) + pl.program_id(1)
    copies = _window_shift_copies((ck_hbm, cv_hbm), (nk_hbm, nv_hbm), (ok_hbm, ov_hbm), sem)

    @pl.when(step == 0)
    def _():
        for c in copies:
            c.start()

    s_len = q_ref.shape[1]
    nb = s_len // ATT_T
    kb_ref[...] = k_ref[0].astype(BF16)
    vb_ref[...] = v_ref[0].astype(BF16)
    scale = A_HD ** -0.5
    for i in range(nb):
        rows = slice(i * ATT_T, (i + 1) * ATT_T)
        width = (i + 1) * ATT_T
        outs = []
        for hl in range(2):
            cols = slice(hl * A_HD, (hl + 1) * A_HD)
            q = (q_ref[0, rows, cols] * scale).astype(BF16)
            s = lax.dot_general(q, kb_ref[0:width, cols], (((1,), (1,)), ((), ())),
                                preferred_element_type=F32)
            s = s + tab_ref[hl, :, (nb - 1 - i) * ATT_T:nb * ATT_T]
            m = jnp.max(s, axis=-1, keepdims=True)
            p = jnp.exp(s - m)
            den = jnp.sum(p, axis=-1, keepdims=True)
            o = jnp.dot(p.astype(BF16), vb_ref[0:width, cols], preferred_element_type=F32)
            outs.append(o / den)
        o_ref[0, rows, :] = jnp.concatenate(outs, axis=-1)

    @pl.when(step == pl.num_programs(0) * pl.num_programs(1) - 1)
    def _():
        for c in copies:
            c.wait()


def _attention_prompt(q, k, v, rel_bias, cache_k, cache_v, new_k, new_v):
    b, s_len, _ = q.shape
    nb = s_len // ATT_T
    tab1 = _distance_logit_table(rel_bias, s_len)
    period = s_len + ATT_T
    z = jnp.arange(period, dtype=I32)
    delta = jnp.where(z < s_len, (nb - 1) * ATT_T - z, (nb - 1) * ATT_T + period - z)
    row0 = jnp.where(delta[None] >= 0, tab1[:, jnp.clip(delta, 0, s_len)], NEG)
    tab = jnp.tile(row0, (1, ATT_T))[:, :ATT_T * (period - 1)].reshape(A_HEADS, ATT_T, period - 1)[:, :, :s_len]
    spec = pl.BlockSpec((1, s_len, 2 * A_HD), lambda bi, hp: (bi, 0, hp))
    hbm = pl.BlockSpec(memory_space=pl.ANY)
    return pl.pallas_call(
        _attn_kernel,
        grid=(b, A_HEADS // 2),
        in_specs=[spec, spec, spec, pl.BlockSpec((2, ATT_T, s_len), lambda bi, hp: (hp, 0, 0))] + [hbm] * 4,
        out_specs=[spec, hbm, hbm],
        out_shape=[jax.ShapeDtypeStruct(q.shape, F32), jax.ShapeDtypeStruct(cache_k.shape, cache_k.dtype),
                   jax.ShapeDtypeStruct(cache_v.shape, cache_v.dtype)],
        scratch_shapes=[pltpu.VMEM((s_len, 2 * A_HD), BF16), pltpu.VMEM((s_len, 2 * A_HD), BF16),
                        pltpu.SemaphoreType.DMA((4,))],
        compiler_params=_cparams("arbitrary", "arbitrary"),
        name="dilated_attention",
    )(q, k, v, tab, cache_k, cache_v, new_k, new_v)


MLSTM_CHUNK = 256


def _log_sigmoid(x):
    return jnp.minimum(x, 0.0) - jnp.log(1.0 + jnp.exp(-jnp.abs(x)))


def _mlstm_kernel(qk_ref, v_ref, ob_ref, gc_ref, gr_ref, wc_ref, bc_ref, bifc_ref, bifr_ref, gh_ref,
                  o_ref, c_out_ref, n_out_ref, m_out_ref, xbuf_ref, c_ref, n_ref, m_ref):
    ci = pl.program_id(1)
    L = qk_ref.shape[1]

    @pl.when(ci == 0)
    def _():
        xbuf_ref[0:8, :] = jnp.zeros((8, 2 * B_W), F32)
        c_ref[...] = jnp.zeros_like(c_ref)
        n_ref[...] = jnp.zeros_like(n_ref)
        m_ref[...] = jnp.zeros_like(m_ref)

    x = qk_ref[0]
    xbuf_ref[8:8 + L, :] = x
    y = bc_ref[...] + wc_ref[3:4, :] * x
    for j in range(B_CONV - 1):
        y = y + wc_ref[j:j + 1, :] * xbuf_ref[5 + j:5 + j + L, :]
    xbuf_ref[0:8, :] = x[L - 8:L, :]
    y = _silu(y)

    gcol = gc_ref[0] + bifc_ref[...]
    grow = gr_ref[...] + bifr_ref[...]
    ri = lax.broadcasted_iota(I32, (L, L), 0)
    cj = lax.broadcasted_iota(I32, (L, L), 1)
    causal = ri >= cj
    lower = jnp.where(causal, 1.0, 0.0)
    b_col = jnp.dot(lower, _log_sigmoid(gcol), preferred_element_type=F32, precision=HIGHEST)
    b_row = lax.dot_general(_log_sigmoid(grow), lower, (((1,), (1,)), ((), ())),
                            preferred_element_type=F32, precision=HIGHEST)

    for h in range(B_HEADS):
        cols = slice(h * B_HD, (h + 1) * B_HD)
        qf = y[:, cols]
        q = qf.astype(BF16)
        kf = y[:, B_W + h * B_HD:B_W + (h + 1) * B_HD] * (B_HD ** -0.5)
        k = kf.astype(BF16)
        v = v_ref[0, :, cols].astype(BF16)
        bc = b_col[:, B_HEADS + h:B_HEADS + h + 1]
        ic = gcol[:, h:h + 1]
        br = b_row[B_HEADS + h:B_HEADS + h + 1, :]
        ir = grow[h:h + 1, :]
        m_prev = m_ref[h:h + 1, 0:1]
        dm = jnp.where(causal, bc - br + ir, NEG)
        inter = bc + m_prev
        mt = jnp.maximum(inter, jnp.max(dm, axis=1, keepdims=True))
        w_intra = jnp.exp(dm - mt)
        w_inter = jnp.exp(inter - mt)
        a = w_intra * lax.dot_general(q, k, (((1,), (1,)), ((), ())), preferred_element_type=F32)
        c_prev = c_ref[h]
        n_prev = n_ref[h:h + 1, :]
        num = (jnp.dot(a.astype(BF16), v, preferred_element_type=F32)
               + w_inter * jnp.dot(q, c_prev.astype(BF16), preferred_element_type=F32))
        den = (jnp.sum(a, axis=1, keepdims=True)
               + w_inter * jnp.sum(qf * n_prev, axis=1, keepdims=True))
        hb = num / jnp.maximum(jnp.abs(den), jnp.exp(-mt))
        m_new = mt[L - 1:L, :]
        b_last = bc[L - 1:L, :]
        g = jnp.exp(b_last - bc + ic - m_new)
        decay = jnp.exp(b_last + m_prev - m_new)
        kg = kf * g
        c_ref[h] = decay * c_prev + lax.dot_general(kg.astype(BF16), v, (((0,), (0,)), ((), ())),
                                                    preferred_element_type=F32)
        n_ref[h:h + 1, :] = decay * n_prev + jnp.sum(kg, axis=0, keepdims=True)
        m_ref[h:h + 1, :] = jnp.broadcast_to(m_new, (1, 128))
        hb = hb * lax.rsqrt(jnp.mean(hb * hb, axis=-1, keepdims=True) + EPS) * gh_ref[:, cols]
        o_ref[0, :, cols] = jax.nn.sigmoid(ob_ref[0, :, cols]) * hb

    @pl.when(ci == pl.num_programs(1) - 1)
    def _():
        c_out_ref[0] = c_ref[...]
        n_out_ref[0] = n_ref[...]
        m_out_ref[0] = m_ref[...]


def _mlstm_prompt(qk, v, ob, gcol, grow, w_conv, b_conv, b_if, g_head):
    b, s_len, _ = qk.shape
    L = min(MLSTM_CHUNK, s_len)
    nc = s_len // L
    bif_col = jnp.zeros((1, 128), F32).at[0, :2 * B_HEADS].set(b_if)
    bif_row = b_if.reshape(2 * B_HEADS, 1)
    seq = lambda w: pl.BlockSpec((1, L, w), lambda bi, ci: (bi, ci, 0))
    full = lambda shape: pl.BlockSpec(shape, lambda bi, ci: (0,) * len(shape))
    return pl.pallas_call(
        _mlstm_kernel,
        grid=(b, nc),
        in_specs=[seq(2 * B_W), seq(B_W), seq(B_W), seq(128),
                  pl.BlockSpec((2 * B_HEADS, L), lambda bi, ci: (0, bi * nc + ci)),
                  full((B_CONV, 2 * B_W)), full((1, 2 * B_W)), full((1, 128)), full((2 * B_HEADS, 1)),
                  full((1, B_W))],
        out_specs=[seq(B_W),
                   pl.BlockSpec((1, B_HEADS, B_HD, B_HD), lambda bi, ci: (bi, 0, 0, 0)),
                   pl.BlockSpec((1, 8, B_HD), lambda bi, ci: (bi, 0, 0)),
                   pl.BlockSpec((1, 8, 128), lambda bi, ci: (bi, 0, 0))],
        out_shape=[jax.ShapeDtypeStruct((b, s_len, B_W), F32),
                   jax.ShapeDtypeStruct((b, B_HEADS, B_HD, B_HD), F32),
                   jax.ShapeDtypeStruct((b, 8, B_HD), F32),
                   jax.ShapeDtypeStruct((b, 8, 128), F32)],
        scratch_shapes=[pltpu.VMEM((L + 8, 2 * B_W), F32), pltpu.VMEM((B_HEADS, B_HD, B_HD), F32),
                        pltpu.VMEM((8, B_HD), F32), pltpu.VMEM((8, 128), F32)],
        compiler_params=_cparams("arbitrary", "arbitrary"),
        name="mlstm_chunkwise",
    )(qk, v, ob, gcol, grow, w_conv, b_conv.reshape(1, -1), bif_col, bif_row, g_head.reshape(1, -1))


def _odd_in_kernel(x_ref, g_ref, w_ref, gcv_ref, bcv_ref, ws_ref, bs_ref, c_ref, xd_ref):
    tm = x_ref.shape[0]
    xn = _rms(x_ref[...], g_ref[...]).astype(BF16)
    u = jnp.dot(xn, w_ref[:, 0:C_W], preferred_element_type=F32)
    v = jnp.dot(xn, w_ref[:, C_W:2 * C_W], preferred_element_type=F32)
    xd_ref[...] = jnp.dot(xn, w_ref[:, 2 * C_W:], preferred_element_type=F32)
    mu = jnp.mean(v, axis=-1, keepdims=True)
    vc = v - mu
    var = jnp.mean(vc * vc, axis=-1, keepdims=True)
    vl = (vc * lax.rsqrt(var + EPS) * gcv_ref[...] + bcv_ref[...]).astype(BF16)
    ri = lax.broadcasted_iota(I32, (C_CHUNK, C_CHUNK), 0)
    cj = lax.broadcasted_iota(I32, (C_CHUNK, C_CHUNK), 1)
    for g in range(C_GROUPS):
        cols = slice(g * C_GD, (g + 1) * C_GD)
        wg = jnp.where(ri >= cj, ws_ref[g], 0.0).astype(BF16)
        for c in range(tm // C_CHUNK):
            rows = slice(c * C_CHUNK, (c + 1) * C_CHUNK)
            s = jnp.dot(wg, vl[rows, cols], preferred_element_type=F32) + bs_ref[:, g:g + 1]
            c_ref[rows, cols] = u[rows, cols] * s


def _odd_in(x, g, w, g_cv, b_cv, w_s, b_s, tm):
    m, d = x.shape
    n = w.shape[1]
    full = lambda shape: pl.BlockSpec(shape, lambda i: (0,) * len(shape))
    return pl.pallas_call(
        _odd_in_kernel,
        grid=(m // tm,),
        in_specs=[pl.BlockSpec((tm, d), lambda i: (i, 0)), full((1, d)), full((d, n)), full((1, C_W)),
                  full((1, C_W)), full((C_GROUPS, C_CHUNK, C_CHUNK)), full((C_CHUNK, C_GROUPS))],
        out_specs=[pl.BlockSpec((tm, C_W), lambda i: (i, 0)), pl.BlockSpec((tm, D_W), lambda i: (i, 0))],
        out_shape=[jax.ShapeDtypeStruct((m, C_W), F32), jax.ShapeDtypeStruct((m, D_W), F32)],
        compiler_params=_cparams("arbitrary"),
        name="odd_in_proj_gate",
    )(x, g.reshape(1, d), w, g_cv.reshape(1, -1), b_cv.reshape(1, -1), w_s, b_s.T)


S5_TT = 128
S5_PAD = 8
S5_HALF = D_GROUPS * D_STATE // 2


def _s5_discretize(a_re, a_im, log_dt, bm_re, bm_im, cm_re, cm_im):
    a_re, a_im = a_re.astype(F32), a_im.astype(F32)
    dt = jnp.exp(log_dt.astype(F32))[:, None]
    mag = jnp.exp(a_re * dt)
    ab_re = mag * jnp.cos(a_im * dt)
    ab_im = mag * jnp.sin(a_im * dt)
    inv = 1.0 / (a_re * a_re + a_im * a_im)
    f_re = ((ab_re - 1.0) * a_re + ab_im * a_im) * inv
    f_im = (ab_im * a_re - (ab_re - 1.0) * a_im) * inv
    bm_re, bm_im = bm_re.astype(F32), bm_im.astype(F32)
    bb_re = f_re[..., None] * bm_re - f_im[..., None] * bm_im
    bb_im = f_re[..., None] * bm_im + f_im[..., None] * bm_re
    return ab_re, ab_im, bb_re, bb_im


def _s5_matrices(bb_re, bb_im, cm_re, cm_im):
    gh = D_GROUPS // 2
    eye = jnp.eye(gh, dtype=F32)

    def in_half(bb):
        return jnp.einsum('gpc,gh->gchp', bb, eye).reshape(gh * D_GCH, gh * D_STATE)

    def out_half(cm):
        return jnp.einsum('gcp,gh->gphc', cm, eye).reshape(gh * D_STATE, gh * D_GCH)

    bd = jnp.stack([jnp.concatenate([in_half(bb_re[h * gh:(h + 1) * gh]), in_half(bb_im[h * gh:(h + 1) * gh])], axis=1)
                    for h in range(2)])
    cm = jnp.stack([jnp.concatenate([out_half(cm_re[h * gh:(h + 1) * gh].astype(F32)),
                                     -out_half(cm_im[h * gh:(h + 1) * gh].astype(F32))], axis=0)
                    for h in range(2)])
    return bd, cm


def _gelu_tanh(x):
    return 0.5 * x * (1.0 + jnp.tanh(math.sqrt(2.0 / math.pi) * (x + 0.044715 * (x * x * x))))


def _s5_kernel(x_ref, bd_ref, cm_ref, ar_ref, ai_ref, dsk_ref, wg_ref, bg_ref, o_ref, hr_out, hi_out,
               bu_ref, hs_ref, hr_ref, hi_ref):
    i = pl.program_id(0)
    nb, tt, _ = x_ref.shape
    stride = tt + S5_PAD
    half_in = D_W // 2
    nct = S5_HALF // 128

    @pl.when(i == 0)
    def _():
        hr_ref[...] = jnp.zeros_like(hr_ref)
        hi_ref[...] = jnp.zeros_like(hi_ref)

    x = x_ref[...].reshape(nb * tt, D_W)
    ys = []
    for h in range(2):
        lanes = slice(h * S5_HALF, (h + 1) * S5_HALF)
        bu = jnp.dot(x[:, h * half_in:(h + 1) * half_in].astype(BF16), bd_ref[h], preferred_element_type=F32)
        for c in range(2 * nct):
            for b in range(nb):
                bu_ref[c, b * stride:b * stride + tt, :] = bu[b * tt:(b + 1) * tt, c * 128:(c + 1) * 128]
        ar = jnp.broadcast_to(ar_ref[:, lanes], (nb, S5_HALF))
        ai = jnp.broadcast_to(ai_ref[:, lanes], (nb, S5_HALF))

        def step(t, carry):
            hr, hi = carry
            rows = pl.ds(t, nb, stride=stride)
            bur = jnp.concatenate([bu_ref[c, rows, :] for c in range(nct)], axis=1)
            bui = jnp.concatenate([bu_ref[nct + c, rows, :] for c in range(nct)], axis=1)
            nhr = ar * hr - ai * hi + bur
            nhi = ar * hi + ai * hr + bui
            for c in range(nct):
                hs_ref[c, rows, :] = nhr[:, c * 128:(c + 1) * 128]
                hs_ref[nct + c, rows, :] = nhi[:, c * 128:(c + 1) * 128]
            return nhr, nhi

        hr, hi = lax.fori_loop(0, tt, step, (hr_ref[:, lanes], hi_ref[:, lanes]))
        hr_ref[:, lanes] = hr
        hi_ref[:, lanes] = hi
        hs = jnp.concatenate(
            [jnp.concatenate([hs_ref[c, b * stride:b * stride + tt, :] for b in range(nb)], axis=0)
             for c in range(2 * nct)], axis=1)
        ys.append(jnp.dot(hs.astype(BF16), cm_ref[h], preferred_element_type=F32))
    y = jnp.concatenate(ys, axis=-1) + dsk_ref[...] * x
    g = _gelu_tanh(y)
    d_out = g * jax.nn.sigmoid(jnp.dot(g.astype(BF16), wg_ref[...], preferred_element_type=F32) + bg_ref[...])
    o_ref[...] = d_out.reshape(nb, tt, D_W)

    @pl.when(i == pl.num_programs(0) - 1)
    def _():
        hr_out[...] = hr_ref[...]
        hi_out[...] = hi_ref[...]


def _s5_prompt(xd, bd, cm, ab_re, ab_im, d_skip, w_glu, b_glu):
    nb, t_len, _ = xd.shape
    tt = min(S5_TT, t_len)
    n_state = D_GROUPS * D_STATE
    full = lambda shape: pl.BlockSpec(shape, lambda i: (0,) * len(shape))
    return pl.pallas_call(
        _s5_kernel,
        grid=(t_len // tt,),
        in_specs=[pl.BlockSpec((nb, tt, D_W), lambda i: (0, i, 0)), full(bd.shape), full(cm.shape),
                  full((1, n_state)), full((1, n_state)), full((1, D_W)), full((D_W, D_W)), full((1, D_W))],
        out_specs=[pl.BlockSpec((nb, tt, D_W), lambda i: (0, i, 0)), full((nb, n_state)), full((nb, n_state))],
        out_shape=[jax.ShapeDtypeStruct(xd.shape, F32), jax.ShapeDtypeStruct((nb, n_state), F32),
                   jax.ShapeDtypeStruct((nb, n_state), F32)],
        scratch_shapes=[pltpu.VMEM((2 * S5_HALF // 128, nb * (tt + S5_PAD), 128), F32),
                        pltpu.VMEM((2 * S5_HALF // 128, nb * (tt + S5_PAD), 128), F32),
                        pltpu.VMEM((nb, n_state), F32), pltpu.VMEM((nb, n_state), F32)],
        compiler_params=_cparams("arbitrary"),
        name="s5_scan_glu",
    )(xd, bd, cm, ab_re.reshape(1, n_state), ab_im.reshape(1, n_state), d_skip.reshape(1, D_W), w_glu,
      b_glu.reshape(1, D_W))


def _hdot(a, b):
    return jnp.dot(a, b, precision=HIGHEST, preferred_element_type=F32)


SAMPLE_NB = 4
SAMPLE_ROWS = 128


def _sample_attn_kernel(q_ref, kn_ref, vn_ref, tab_ref, tabn_ref, k1, k2, k3, v1, v2, v3, o_ref):
    ones = jnp.ones((A_HD, 128), F32)
    lane_sum = lambda a: jnp.dot(a, ones, preferred_element_type=F32, precision=HIGHEST)
    rh = SAMPLE_ROWS * A_HEADS
    for n in range(q_ref.shape[0]):
        q = q_ref[n] * (A_HD ** -0.5)
        qt = jnp.broadcast_to(q[None], (SAMPLE_ROWS, A_HEADS, A_HD)).reshape(rh, A_HD)
        s_new = lane_sum(kn_ref[n] * q) + tabn_ref[...]
        s_blocks = [(lane_sum(kb[n].reshape(rh, A_HD) * qt) + tab_ref[b]).reshape(SAMPLE_ROWS, A_HEADS, 128)
                    for b, kb in enumerate((k1, k2, k3))]
        m = s_new
        for s in s_blocks:
            m = jnp.maximum(m, jnp.max(s, axis=0))
        p_new = jnp.exp(s_new - m)
        den = p_new
        acc = p_new[:, :A_HD] * vn_ref[n]
        for s, vb in zip(s_blocks, (v1, v2, v3)):
            p = jnp.exp(s - m[None])
            den = den + jnp.sum(p, axis=0)
            acc = acc + jnp.sum(p[:, :, :A_HD] * vb[n], axis=0)
        o_ref[n] = acc / den[:, :A_HD]


def _sample_attention(q, k_new, v_new, cache_k, cache_v, rel_bias):
    n, L = cache_k.shape[0], cache_k.shape[1]
    assert L == A_WIN and all(w // d == SAMPLE_ROWS for w, d in A_BRANCHES)
    bias = rel_bias[_t5_bucket(jnp.arange(L + 1, dtype=I32))].astype(F32)
    r = jnp.arange(SAMPLE_ROWS, dtype=I32)
    tab = jnp.stack([bias[d * (SAMPLE_ROWS - r)].reshape(SAMPLE_ROWS * A_HEADS) for _, d in A_BRANCHES])
    tab = jnp.broadcast_to(tab[:, :, None], (len(A_BRANCHES), SAMPLE_ROWS * A_HEADS, 128))
    tab_new = jnp.broadcast_to((bias[0] + math.log(len(A_BRANCHES)))[:, None], (A_HEADS, 128))
    views, specs = [], []
    for _, d in A_BRANCHES:
        rows = L // d
        views.append(lambda c, rows=rows, d=d: c.reshape(n, rows, d, A_HEADS, A_HD))
        specs.append(pl.BlockSpec((SAMPLE_NB, SAMPLE_ROWS, None, A_HEADS, A_HD),
                                  lambda i, blk=rows // SAMPLE_ROWS - 1: (i, blk, 0, 0, 0)))
    row_spec = pl.BlockSpec((SAMPLE_NB, A_HEADS, A_HD), lambda i: (i, 0, 0))
    full = lambda shape: pl.BlockSpec(shape, lambda i: (0,) * len(shape))
    return pl.pallas_call(
        _sample_attn_kernel,
        grid=(n // SAMPLE_NB,),
        in_specs=[row_spec, row_spec, row_spec, full(tab.shape), full(tab_new.shape)] + specs + specs,
        out_specs=row_spec,
        out_shape=jax.ShapeDtypeStruct((n, A_HEADS, A_HD), F32),
        compiler_params=_cparams("arbitrary"),
        name="sample_attention",
    )(q, k_new, v_new, tab, tab_new, *[v(cache_k) for v in views], *[v(cache_v) for v in views])


def _sample_even_proj(x, g_mix, w_in):
    n_main = 3 * A_W + 4 * B_W
    segs = _EVEN_SEGS[:-1] + ((n_main, 2 * B_HEADS),)
    return _norm_proj(x, g_mix, w_in, w_in[:, n_main:].T, segs, tm=x.shape[0], precise=True)[:-1]


def _sample_even_mixers(proj, cache_k, cache_v, c0, n0, m0, conv0, rel_bias, b_if, w_conv, b_conv, g_head):
    qa, ka, va, qkb, vb, ob, gates = proj
    n = qa.shape[0]
    heads = lambda a: a.reshape(n, A_HEADS, A_HD)
    a_out = _sample_attention(heads(qa), heads(ka), heads(va), cache_k, cache_v, rel_bias).reshape(n, A_W)
    xp = jnp.concatenate([conv0, qkb[:, None, :]], axis=1)
    qk = _silu(b_conv + jnp.sum(xp * w_conv[None], axis=1))
    q = qk[:, :B_W].reshape(n, B_HEADS, B_HD)
    k = qk[:, B_W:].reshape(n, B_HEADS, B_HD) * (B_HD ** -0.5)
    v = vb.reshape(n, B_HEADS, B_HD)
    ig = gates[:, :B_HEADS] + b_if[:B_HEADS]
    logf = _log_sigmoid(gates[:, B_HEADS:] + b_if[B_HEADS:])
    inter = logf + m0
    mt = jnp.maximum(inter, ig)
    w_intra = jnp.exp(ig - mt)
    w_inter = jnp.exp(inter - mt)
    a = w_intra * jnp.sum(q * k, axis=-1)
    num = a[..., None] * v + w_inter[..., None] * jnp.einsum('nhk,nhkv->nhv', q, c0, precision=HIGHEST)
    den = a + w_inter * jnp.sum(q * n0, axis=-1)
    hb = num / jnp.maximum(jnp.abs(den), jnp.exp(-mt))[..., None]
    g = jnp.exp(ig - mt)
    decay = jnp.exp(inter - mt)
    c_new = decay[..., None, None] * c0 + (g[..., None] * k)[..., :, None] * v[..., None, :]
    n_new = decay[..., None] * n0 + g[..., None] * k
    hb = hb * lax.rsqrt(jnp.mean(hb * hb, axis=-1, keepdims=True) + EPS) * g_head.reshape(B_HEADS, B_HD)
    b_out = jax.nn.sigmoid(ob) * hb.reshape(n, B_W)
    return a_out, b_out, c_new, n_new, mt, xp[:, 1:]


def _sample_odd_mixers(x, h_re0, h_im0, g_mix, w_in, g_cv, b_cv, w_s, b_s, ab_re, ab_im, bb_re, bb_im,
                       cm_re, cm_im, d_skip, w_glu, b_glu):
    n = x.shape[0]
    h = _rms(x, g_mix)
    proj = _hdot(h, w_in)
    u, v, xd = proj[:, :C_W], proj[:, C_W:2 * C_W], proj[:, 2 * C_W:]
    mu = jnp.mean(v, axis=-1, keepdims=True)
    vc = v - mu
    v = vc * lax.rsqrt(jnp.mean(vc * vc, axis=-1, keepdims=True) + EPS) * g_cv + b_cv
    s = (w_s[:, 0, 0][None, :, None] * v.reshape(n, C_GROUPS, C_GD) + b_s[:, 0][None, :, None]).reshape(n, C_W)
    c_out = u * s
    xg = xd.reshape(n, D_GROUPS, D_GCH)
    bu_re = jnp.einsum('gpc,ngc->ngp', bb_re, xg, precision=HIGHEST)
    bu_im = jnp.einsum('gpc,ngc->ngp', bb_im, xg, precision=HIGHEST)
    hr = ab_re * h_re0 - ab_im * h_im0 + bu_re
    hi = ab_re * h_im0 + ab_im * h_re0 + bu_im
    y = (jnp.einsum('gcp,ngp->ngc', cm_re, hr, precision=HIGHEST)
         - jnp.einsum('gcp,ngp->ngc', cm_im, hi, precision=HIGHEST) + d_skip * xg)
    g = _gelu_tanh(y.reshape(n, D_W))
    d_out = g * jax.nn.sigmoid(_hdot(g, w_glu) + b_glu)
    return c_out, d_out, v, hr, hi


_EVEN_SEGS = ((0, A_W), (A_W, A_W), (2 * A_W, A_W), (3 * A_W, 2 * B_W), (3 * A_W + 2 * B_W, B_W),
              (3 * A_W + 3 * B_W, B_W), (3 * A_W + 4 * B_W, 128))


def kernel(x_prompt, x_sample, cache_a_k, cache_a_v, state_b_c, state_b_n, state_b_m, state_b_conv,
           state_d_re, state_d_im, rel_bias, g_mix, g_ffn, g_final, w_in_e, b_if, w_conv_b, b_conv_b,
           g_bhead, w_out_e, w1_e, w3_e, w2_e, w_in_o, g_cv, b_cv, w_s, b_s, a_re, a_im, log_dt,
           bm_re, bm_im, cm_re, cm_im, d_skip, w_glu, b_glu, w_out_o, w_router, b_router,
           w1_m, w3_m, w2_m):
    bp, sp, d = x_prompt.shape
    ns = x_sample.shape[0]
    mp = bp * sp
    xp = x_prompt.reshape(mp, d)
    xs = x_sample.reshape(ns, d)

    w_in = w_in_e[0]
    n_main = 3 * A_W + 4 * B_W
    w_gate = w_in[:, n_main:]
    w_cat = jnp.concatenate([w_in[:, :n_main], jnp.pad(w_gate, ((0, 0), (0, 128 - 2 * B_HEADS)))], axis=1)
    q, k, v, qkb, vb, ob, gcol, grow = _norm_proj(
        xp, g_mix[0], w_cat.astype(BF16), w_gate.T.astype(BF16), _EVEN_SEGS, tm=512)
    shp = lambda a: a.reshape(bp, sp, a.shape[-1])
    s_proj = _sample_even_proj(xs, g_mix[0], w_in)
    new_row = lambda a: a.reshape(ns, 1, A_HEADS, A_HD)
    a_out, s_k, s_v = _attention_prompt(shp(q), shp(k), shp(v), rel_bias, cache_a_k[0], cache_a_v[0],
                                        new_row(s_proj[1]), new_row(s_proj[2]))
    b_out, pc, pn, pm = _mlstm_prompt(shp(qkb), shp(vb), shp(ob), shp(gcol), grow, w_conv_b[0], b_conv_b[0],
                                      b_if[0], g_bhead[0])
    x1p = _mix_ffn(a_out.reshape(mp, A_W), b_out.reshape(mp, B_W), xp, w_out_e[0].astype(BF16), g_ffn[0],
                   w1_e[0].astype(BF16), w3_e[0].astype(BF16), w2_e[0].astype(BF16), tm=1024, tf=512)

    sa, sb, sc, sn, sm, s_conv = _sample_even_mixers(
        s_proj, cache_a_k[0], cache_a_v[0], state_b_c[0], state_b_n[0], state_b_m[0], state_b_conv[0], rel_bias,
        b_if[0], w_conv_b[0], b_conv_b[0], g_bhead[0])
    x1s = _mix_ffn(sa, sb, xs, w_out_e[0], g_ffn[0], w1_e[0], w3_e[0], w2_e[0], tm=ns, tf=512, precise=True)

    ab_re, ab_im, bb_re, bb_im = _s5_discretize(a_re[0], a_im[0], log_dt[0], bm_re[0], bm_im[0], cm_re[0],
                                                cm_im[0])
    bd, cmat = _s5_matrices(bb_re, bb_im, cm_re[0], cm_im[0])
    c_out, xd = _odd_in(x1p, g_mix[1], w_in_o[0].astype(BF16), g_cv[0], b_cv[0], w_s[0], b_s[0], tm=512)
    d_out, p_hr, p_hi = _s5_prompt(xd.reshape(bp, sp, D_W), bd.astype(BF16), cmat.astype(BF16), ab_re, ab_im,
                                   d_skip[0], w_glu[0].astype(BF16), b_glu[0])
    wr = jnp.pad(w_router[0].astype(F32), ((0, 0), (0, 128 - N_EXPERTS)))
    br = jnp.pad(b_router[0].astype(F32), (0, 128 - N_EXPERTS)).reshape(1, 128)
    x2p, hnp, idp, gtp = _mix_router(c_out, d_out.reshape(mp, D_W), x1p, w_out_o[0].astype(BF16), g_ffn[1], wr, br,
                                     tm=512)

    sc_out, sd_out, s_cv, s_hr, s_hi = _sample_odd_mixers(
        x1s, state_d_re[0], state_d_im[0], g_mix[1], w_in_o[0], g_cv[0], b_cv[0], w_s[0], b_s[0], ab_re, ab_im,
        bb_re, bb_im, cm_re[0], cm_im[0], d_skip[0], w_glu[0], b_glu[0])
    x2s, hns, ids_s, gts = _mix_router(sc_out, sd_out, x1s, w_out_o[0], g_ffn[1], wr, br, tm=ns, precise=True)

    pad = ((0, ROUTE_TILE - ns), (0, 0))
    y_p, y_s = _moe_block(x2p, hnp, idp, gtp, jnp.pad(x2s, pad), jnp.pad(hns, pad),
                          jnp.pad(ids_s, pad, constant_values=-1), jnp.pad(gts, pad),
                          w1_m[0], w3_m[0], w2_m[0], g_final)
    y_prompt = y_p.reshape(bp, sp, d)
    y_sample = y_s[:ns].reshape(ns, 1, d)

    e = lambda a: a[None]
    p_k = k.reshape(bp, sp, A_HEADS, A_HD)
    p_v = v.reshape(bp, sp, A_HEADS, A_HD)
    p_conv = qkb.reshape(bp, sp, 2 * B_W)[:, sp - (B_CONV - 1):]
    return (y_prompt, y_sample, e(p_k), e(p_v), e(s_k), e(s_v), e(pc), e(sc), e(pn[:, :B_HEADS]), e(sn),
            e(pm[:, :B_HEADS, 0]), e(sm), e(p_conv), e(s_conv), e(s_cv.reshape(ns, 1, C_W)),
            e(p_hr.reshape(bp, D_GROUPS, D_STATE)), e(s_hr), e(p_hi.reshape(bp, D_GROUPS, D_STATE)), e(s_hi))
```

```python
import functools
import math

import jax
import jax.numpy as jnp
from jax import lax
from jax.experimental import pallas as pl
from jax.experimental.pallas import tpu as pltpu

F32 = jnp.float32
BF16 = jnp.bfloat16
I32 = jnp.int32
HIGHEST = lax.Precision.HIGHEST

D_MODEL = 1024
A_HEADS, A_HD = 8, 64
A_W = A_HEADS * A_HD
A_BRANCHES = ((128, 1), (512, 4), (2048, 16))
A_WIN = 2048
N_BUCKETS = 32
B_HEADS, B_HD = 4, 128
B_W = B_HEADS * B_HD
B_CONV = 4
C_GROUPS, C_GD = 4, 128
C_W = C_GROUPS * C_GD
C_CHUNK = 128
D_GROUPS, D_GCH, D_STATE = 32, 16, 64
D_W = D_GROUPS * D_GCH
D_FF = 3584
N_EXPERTS = 8
TOP_K = 2
EPS = 1e-6
NEG = -1e30

VMEM_LIMIT = 56 * 1024 * 1024

ROUTE_TILE = 256
FFN_TILE = 1024
FFN_ROW_CHUNK = 1024
GATHER_TOKENS = 2 * ROUTE_TILE
COMBINE_ALIGN = 16
COMBINE_WIN = ROUTE_TILE + COMBINE_ALIGN


def _cparams(*sem):
    return pltpu.CompilerParams(dimension_semantics=sem, vmem_limit_bytes=VMEM_LIMIT)


def _moe_plan(ids, n_tok_tiles):
    mt = ids.shape[0]
    n_assign = 2 * mt
    n_ffn_tiles = -(-(n_assign + N_EXPERTS * (FFN_TILE - 1)) // FFN_TILE) + 1
    n_sub = n_ffn_tiles * (FFN_TILE // ROUTE_TILE)
    flat = ids.reshape(n_assign)
    onehot = (flat[:, None] == jnp.arange(N_EXPERTS, dtype=I32)[None, :]).astype(I32)
    csum = jnp.cumsum(onehot, axis=0)
    rank = jnp.sum((csum - onehot) * onehot, axis=1)
    counts = csum[-1]
    seg = ((counts + FFN_TILE - 1) // FFN_TILE) * FFN_TILE
    seg_end = jnp.cumsum(seg)
    seg_off = seg_end - seg
    dest = jnp.where(flat >= 0, seg_off[jnp.clip(flat, 0)] + rank, -1).reshape(mt, 2)
    total = seg_end[-1]
    tile_start = jnp.arange(n_ffn_tiles, dtype=I32) * FFN_TILE
    tile_expert = jnp.minimum(jnp.sum((tile_start[:, None] >= seg_end[None, :]).astype(I32), axis=1),
                              N_EXPERTS - 1).astype(I32)
    n_used = (total // FFN_TILE).astype(I32)
    per_tile = 2 * ROUTE_TILE
    cnt_tile = jnp.concatenate([jnp.zeros((1, N_EXPERTS), I32), csum[per_tile - 1::per_tile]])
    lo = seg_off[None, :] + cnt_tile[:-1]
    hi = seg_off[None, :] + cnt_tile[1:]
    sub_start = jnp.arange(n_sub, dtype=I32) * ROUTE_TILE
    sub_expert = jnp.minimum(jnp.sum((sub_start[:, None] >= seg_end[None, :]).astype(I32), axis=1),
                             N_EXPERTS - 1)
    lo_e, hi_e = lo.T[sub_expert], hi.T[sub_expert]
    jlo = jnp.sum((hi_e <= sub_start[:, None]).astype(I32), axis=1).astype(I32)
    jhi = (jnp.sum((lo_e < sub_start[:, None] + ROUTE_TILE).astype(I32), axis=1) - 1).astype(I32)
    win = lo // COMBINE_ALIGN
    flat1 = lambda a: a.reshape(-1).astype(I32)
    return dest, tile_expert, n_used, (jlo, jhi), (flat1(win), flat1(lo), flat1(hi)), n_ffn_tiles


def _moe_gather_kernel(jlo_ref, jhi_ref, hp_hbm, hs_hbm, dt_ref, gt_ref, xs_ref, gs_ref,
                       hn_ref, acc_ref, g_ref, sem):
    s = pl.program_id(0)
    mp = hp_hbm.shape[0]

    @pl.when(s == 0)
    def _():
        copies = (pltpu.make_async_copy(hp_hbm, hn_ref.at[0:mp], sem.at[0]),
                  pltpu.make_async_copy(hs_hbm, hn_ref.at[mp:mp + ROUTE_TILE], sem.at[1]))
        for c in copies:
            c.start()
        hn_ref[mp + ROUTE_TILE:, :] = jnp.zeros((ROUTE_TILE, hn_ref.shape[1]), BF16)
        for c in copies:
            c.wait()

    acc_ref[...] = jnp.zeros_like(acc_ref)
    g_ref[...] = jnp.zeros_like(g_ref)
    rows = lax.broadcasted_iota(I32, (ROUTE_TILE, GATHER_TOKENS), 0) + s * ROUTE_TILE
    jlo = jlo_ref[s]

    def body(it, carry):
        off = pl.multiple_of(jlo * ROUTE_TILE + it * GATHER_TOKENS, ROUTE_TILE)
        dt = dt_ref[:, pl.ds(off, GATHER_TOKENS)]
        gt = gt_ref[:, pl.ds(off, GATHER_TOKENS)]
        m1 = dt[0:1, :] == rows
        m2 = dt[1:2, :] == rows
        onehot = jnp.where(m1 | m2, 1.0, 0.0).astype(BF16)
        acc_ref[...] += jnp.dot(onehot, hn_ref[pl.ds(off, GATHER_TOKENS), :], preferred_element_type=F32)
        g_ref[...] += jnp.sum(jnp.where(m1, gt[0:1, :], 0.0) + jnp.where(m2, gt[1:2, :], 0.0),
                              axis=1, keepdims=True)
        return carry

    tiles_per_iter = GATHER_TOKENS // ROUTE_TILE
    lax.fori_loop(0, (jhi_ref[s] - jlo + tiles_per_iter) // tiles_per_iter, body, 0)
    xs_ref[...] = acc_ref[...].astype(BF16)
    gs_ref[...] = g_ref[...]


def _moe_gather(hn_p, hn_s, dest_t, gates_t, glist, n_sub):
    jlo, jhi = glist
    mp, d = hn_p.shape
    mt = dest_t.shape[1]
    grid_spec = pltpu.PrefetchScalarGridSpec(
        num_scalar_prefetch=2,
        grid=(n_sub,),
        in_specs=[
            pl.BlockSpec(memory_space=pl.ANY),
            pl.BlockSpec(memory_space=pl.ANY),
            pl.BlockSpec((2, mt), lambda s, *_: (0, 0)),
            pl.BlockSpec((2, mt), lambda s, *_: (0, 0)),
        ],
        out_specs=[
            pl.BlockSpec((ROUTE_TILE, d), lambda s, *_: (s, 0)),
            pl.BlockSpec((ROUTE_TILE, 1), lambda s, *_: (s, 0)),
        ],
        scratch_shapes=[pltpu.VMEM((mp + 2 * ROUTE_TILE, d), BF16), pltpu.VMEM((ROUTE_TILE, d), F32),
                        pltpu.VMEM((ROUTE_TILE, 1), F32), pltpu.SemaphoreType.DMA((2,))],
    )
    return pl.pallas_call(
        _moe_gather_kernel,
        grid_spec=grid_spec,
        out_shape=[jax.ShapeDtypeStruct((n_sub * ROUTE_TILE, d), BF16),
                   jax.ShapeDtypeStruct((n_sub * ROUTE_TILE, 1), F32)],
        compiler_params=_cparams("arbitrary"),
        name="moe_gather",
    )(jlo, jhi, hn_p, hn_s, dest_t, gates_t)


def _moe_ffn_kernel(te_ref, nu_ref, x_ref, gs_ref, w1_ref, w3_ref, w2_ref, y_ref, acc_ref):
    t, f = pl.program_id(0), pl.program_id(1)
    nf = pl.num_programs(1)

    @pl.when(t < nu_ref[0])
    def _():
        @pl.when(f == 0)
        def _():
            acc_ref[...] = jnp.zeros_like(acc_ref)

        w1, w3, w2 = w1_ref[0].astype(BF16), w3_ref[0].astype(BF16), w2_ref[0].astype(BF16)
        for r in range(0, FFN_TILE, FFN_ROW_CHUNK):
            rows = slice(r, r + FFN_ROW_CHUNK)
            x = x_ref[rows, :]
            a = jnp.dot(x, w1, preferred_element_type=F32)
            b = jnp.dot(x, w3, preferred_element_type=F32)
            acc_ref[rows, :] += jnp.dot((_silu(a) * b).astype(BF16), w2, preferred_element_type=F32)

        @pl.when(f == nf - 1)
        def _():
            y_ref[...] = (acc_ref[...] * gs_ref[...]).astype(BF16)

    @pl.when((t >= nu_ref[0]) & (f == nf - 1))
    def _():
        y_ref[...] = jnp.zeros_like(y_ref)


def _moe_ffn(xs, gs, w1, w3, w2, tile_expert, n_used, n_ffn_tiles, tf):
    d = xs.shape[1]
    ff = w1.shape[2]
    nf = ff // tf

    def fidx(t, f, nu):
        return jnp.where(t < nu[0], f, nf - 1)

    grid_spec = pltpu.PrefetchScalarGridSpec(
        num_scalar_prefetch=2,
        grid=(n_ffn_tiles, nf),
        in_specs=[
            pl.BlockSpec((FFN_TILE, d), lambda t, f, te, nu: (t, 0)),
            pl.BlockSpec((FFN_TILE, 1), lambda t, f, te, nu: (t, 0)),
            pl.BlockSpec((1, d, tf), lambda t, f, te, nu: (te[t], 0, fidx(t, f, nu))),
            pl.BlockSpec((1, d, tf), lambda t, f, te, nu: (te[t], 0, fidx(t, f, nu))),
            pl.BlockSpec((1, tf, d), lambda t, f, te, nu: (te[t], fidx(t, f, nu), 0)),
        ],
        out_specs=pl.BlockSpec((FFN_TILE, d), lambda t, f, te, nu: (t, 0)),
        scratch_shapes=[pltpu.VMEM((FFN_TILE, d), F32)],
    )
    return pl.pallas_call(
        _moe_ffn_kernel,
        grid_spec=grid_spec,
        out_shape=jax.ShapeDtypeStruct(xs.shape, BF16),
        compiler_params=_cparams("arbitrary", "arbitrary"),
        name="moe_ffn",
    )(tile_expert, n_used.reshape(1), xs, gs, w1, w3, w2)


def _moe_combine_kernel(win_ref, lo_ref, hi_ref, xp_ref, xs_ref, d_ref, g_ref, *rest):
    ys_refs = rest[:N_EXPERTS]
    yp_ref, ysm_ref, acc_ref = rest[N_EXPERTS:]
    j = pl.program_id(0)
    n_prompt_tiles = pl.num_programs(0) - 1

    @pl.when(j < n_prompt_tiles)
    def _():
        acc_ref[...] = xp_ref[...]

    @pl.when(j == n_prompt_tiles)
    def _():
        acc_ref[...] = xs_ref[...]

    for e in range(N_EXPERTS):
        lo, hi = lo_ref[j * N_EXPERTS + e], hi_ref[j * N_EXPERTS + e]

        @pl.when(hi > lo)
        def _():
            d = d_ref[...]
            d = jnp.where((d >= lo) & (d < hi), d, -1)
            cols = (lax.broadcasted_iota(I32, (ROUTE_TILE, COMBINE_WIN), 1)
                    + win_ref[j * N_EXPERTS + e] * COMBINE_ALIGN)
            onehot = jnp.where((d[:, 0:1] == cols) | (d[:, 1:2] == cols), 1.0, 0.0).astype(BF16)
            acc_ref[...] += jnp.dot(onehot, ys_refs[e][...], preferred_element_type=F32)

    y = _rms(acc_ref[...], g_ref[...])

    @pl.when(j < n_prompt_tiles)
    def _():
        yp_ref[...] = y

    @pl.when(j == n_prompt_tiles)
    def _():
        ysm_ref[...] = y


def _moe_combine(x2_p, x2_s, dest, ys, g_final, clist):
    win, lo, hi = clist
    mp, d = x2_p.shape
    n_prompt_tiles = mp // ROUTE_TILE

    def ptile(j, *_):
        return (jnp.minimum(j, n_prompt_tiles - 1), 0)

    def window(e):
        return pl.BlockSpec((pl.Element(COMBINE_WIN), pl.Element(d)),
                            lambda j, win, lo, hi: (win[j * N_EXPERTS + e] * COMBINE_ALIGN, 0))

    grid_spec = pltpu.PrefetchScalarGridSpec(
        num_scalar_prefetch=3,
        grid=(n_prompt_tiles + 1,),
        in_specs=[
            pl.BlockSpec((ROUTE_TILE, d), ptile),
            pl.BlockSpec((ROUTE_TILE, d), lambda j, *_: (0, 0)),
            pl.BlockSpec((ROUTE_TILE, 2), lambda j, *_: (j, 0)),
            pl.BlockSpec((1, d), lambda j, *_: (0, 0)),
        ] + [window(e) for e in range(N_EXPERTS)],
        out_specs=[pl.BlockSpec((ROUTE_TILE, d), ptile), pl.BlockSpec((ROUTE_TILE, d), lambda j, *_: (0, 0))],
        scratch_shapes=[pltpu.VMEM((ROUTE_TILE, d), F32)],
    )
    return pl.pallas_call(
        _moe_combine_kernel,
        grid_spec=grid_spec,
        out_shape=[jax.ShapeDtypeStruct((mp, d), F32), jax.ShapeDtypeStruct((ROUTE_TILE, d), F32)],
        compiler_params=_cparams("arbitrary"),
        name="moe_combine",
    )(win, lo, hi, x2_p, x2_s, dest, g_final.reshape(1, d), *([ys] * N_EXPERTS))


def _moe_block(x2_p, hn_p, ids_p, gates_p, x2_s, hn_s, ids_s, gates_s, w1, w3, w2, g_final, tf=512):
    mp = x2_p.shape[0]
    n_tok_tiles = mp // ROUTE_TILE + 1
    ids = jnp.concatenate([ids_p, ids_s])
    gates = jnp.concatenate([gates_p, gates_s])
    dest, tile_expert, n_used, glist, clist, n_ffn_tiles = _moe_plan(ids, n_tok_tiles)
    n_sub = n_ffn_tiles * (FFN_TILE // ROUTE_TILE)
    spare = ((0, 0), (0, GATHER_TOKENS - ROUTE_TILE))
    xs, gs = _moe_gather(hn_p, hn_s, jnp.pad(dest.T, spare, constant_values=-1), jnp.pad(gates.T, spare), glist,
                         n_sub)
    ys = _moe_ffn(xs, gs, w1, w3, w2, tile_expert, n_used, n_ffn_tiles, tf)
    return _moe_combine(x2_p, x2_s, dest, ys, g_final, clist)


def _rms(x, g):
    return x * lax.rsqrt(jnp.mean(x * x, axis=-1, keepdims=True) + EPS) * g


def _mm(a, b, precise):
    if precise:
        return jnp.dot(a.astype(F32), b.astype(F32), preferred_element_type=F32, precision=HIGHEST)
    return jnp.dot(a.astype(BF16), b.astype(BF16), preferred_element_type=F32)


def _silu(x):
    return x * jax.nn.sigmoid(x)


def _norm_proj_kernel(x_ref, g_ref, w_ref, wt_ref, *out_refs, segs, tsegs, precise):
    xn = _rms(x_ref[...], g_ref[...])
    xn = xn if precise else xn.astype(BF16)
    for (start, width), o_ref in zip(segs, out_refs[:len(segs)]):
        for c in range(0, width, 512):
            cw = min(512, width - c)
            o_ref[:, c:c + cw] = _mm(xn, w_ref[:, start + c:start + c + cw], precise)
    for (start, height), o_ref in zip(tsegs, out_refs[len(segs):]):
        res = lax.dot_general(wt_ref[start:start + height, :].astype(xn.dtype), xn, (((1,), (1,)), ((), ())),
                              preferred_element_type=F32, precision=HIGHEST if precise else None)
        o_ref[...] = res.reshape(o_ref.shape)


def _norm_proj(x, g, w, wt, segs, tsegs, tm, rows_per_seq=None, precise=False):
    m, d = x.shape
    n = w.shape[1]
    nt = wt.shape[0]
    out_shape = [jax.ShapeDtypeStruct((m, width), F32) for _, width in segs]
    out_specs = [pl.BlockSpec((tm, width), lambda i: (i, 0)) for _, width in segs]
    for _, height, per_seq in tsegs:
        if per_seq:
            tps = rows_per_seq // tm
            out_shape.append(jax.ShapeDtypeStruct((m // rows_per_seq, height, rows_per_seq), F32))
            out_specs.append(pl.BlockSpec((1, height, tm), lambda i, tps=tps: (i // tps, 0, i % tps)))
        else:
            out_shape.append(jax.ShapeDtypeStruct((height, m), F32))
            out_specs.append(pl.BlockSpec((height, tm), lambda i: (0, i)))
    return pl.pallas_call(
        functools.partial(_norm_proj_kernel, segs=tuple(segs), tsegs=tuple(t[:2] for t in tsegs), precise=precise),
        grid=(m // tm,),
        in_specs=[
            pl.BlockSpec((tm, d), lambda i: (i, 0)),
            pl.BlockSpec((1, d), lambda i: (0, 0)),
            pl.BlockSpec((d, n), lambda i: (0, 0)),
            pl.BlockSpec((nt, d), lambda i: (0, 0)),
        ],
        out_specs=out_specs,
        out_shape=out_shape,
        compiler_params=_cparams("arbitrary"),
        name="norm_proj",
    )(x, g.reshape(1, d), w, wt)


def _mix_ffn_kernel(a_ref, b_ref, x_ref, wo_ref, g_ref, w1_ref, w3_ref, w2_ref, o_ref, hn_ref, *, precise):
    wa = a_ref.shape[1]

    @pl.when(pl.program_id(1) == 0)
    def _():
        x1 = (x_ref[...] + _mm(a_ref[...], wo_ref[0:wa, :], precise)
              + _mm(b_ref[...], wo_ref[wa:, :], precise))
        o_ref[...] = x1
        hn_ref[...] = _rms(x1, g_ref[...]).astype(hn_ref.dtype)

    wdt = F32 if precise else BF16
    w1, w3, w2 = w1_ref[...].astype(wdt), w3_ref[...].astype(wdt), w2_ref[...].astype(wdt)
    tm = hn_ref.shape[0]
    chunk = min(tm, FFN_ROW_CHUNK)
    for r in range(0, tm, chunk):
        rows = slice(r, r + chunk)
        hn = hn_ref[rows, :]
        o_ref[rows, :] += _mm(_silu(_mm(hn, w1, precise)) * _mm(hn, w3, precise), w2, precise)


def _mix_ffn(a, b, x, w_out, g, w1, w3, w2, tm, tf, precise=False):
    m, d = x.shape
    wa, wb = a.shape[1], b.shape[1]
    ff = w1.shape[1]
    return pl.pallas_call(
        functools.partial(_mix_ffn_kernel, precise=precise),
        grid=(m // tm, ff // tf),
        in_specs=[
            pl.BlockSpec((tm, wa), lambda i, f: (i, 0)),
            pl.BlockSpec((tm, wb), lambda i, f: (i, 0)),
            pl.BlockSpec((tm, d), lambda i, f: (i, 0)),
            pl.BlockSpec((wa + wb, d), lambda i, f: (0, 0)),
            pl.BlockSpec((1, d), lambda i, f: (0, 0)),
            pl.BlockSpec((d, tf), lambda i, f: (0, f)),
            pl.BlockSpec((d, tf), lambda i, f: (0, f)),
            pl.BlockSpec((tf, d), lambda i, f: (f, 0)),
        ],
        out_specs=pl.BlockSpec((tm, d), lambda i, f: (i, 0)),
        out_shape=jax.ShapeDtypeStruct((m, d), F32),
        scratch_shapes=[pltpu.VMEM((tm, d), F32 if precise else BF16)],
        compiler_params=_cparams("arbitrary", "arbitrary"),
        name="mix_ffn",
    )(a, b, x, w_out, g.reshape(1, d), w1, w3, w2)


def _mix_router_kernel(a_ref, b_ref, x_ref, wo_ref, g_ref, wr_ref, br_ref, x2_ref, hn_ref, ids_ref, gate_ref,
                       *, precise):
    wa = a_ref.shape[1]
    x2 = (x_ref[...] + _mm(a_ref[...], wo_ref[0:wa, :], precise)
          + _mm(b_ref[...], wo_ref[wa:, :], precise))
    x2_ref[...] = x2
    hn = _rms(x2, g_ref[...])
    hn_ref[...] = hn.astype(BF16)
    wr = wr_ref[...]
    if precise:
        logits = jnp.dot(hn, wr, preferred_element_type=F32, precision=HIGHEST)
    else:
        hn_hi, wr_hi = hn.astype(BF16), wr.astype(BF16)
        hn_lo, wr_lo = (hn - hn_hi.astype(F32)).astype(BF16), (wr - wr_hi.astype(F32)).astype(BF16)
        logits = (jnp.dot(hn_hi, wr_hi, preferred_element_type=F32)
                  + (jnp.dot(hn_lo, wr_hi, preferred_element_type=F32)
                     + jnp.dot(hn_hi, wr_lo, preferred_element_type=F32)))
    lane = lax.broadcasted_iota(I32, logits.shape, 1)
    real = lane < N_EXPERTS
    biased = jnp.where(real, logits + br_ref[...], -jnp.inf)
    m1 = jnp.max(biased, axis=-1, keepdims=True)
    i1 = jnp.min(jnp.where(biased == m1, lane, 128), axis=-1, keepdims=True)
    rest = jnp.where(lane == i1, -jnp.inf, biased)
    m2 = jnp.max(rest, axis=-1, keepdims=True)
    i2 = jnp.min(jnp.where(rest == m2, lane, 128), axis=-1, keepdims=True)
    l1 = jnp.sum(jnp.where(lane == i1, logits, 0.0), axis=-1, keepdims=True)
    l2 = jnp.sum(jnp.where(lane == i2, logits, 0.0), axis=-1, keepdims=True)
    mx = jnp.maximum(l1, l2)
    e1, e2 = jnp.exp(l1 - mx), jnp.exp(l2 - mx)
    two = lax.broadcasted_iota(I32, ids_ref.shape, 1)
    ids_ref[...] = jnp.where(two == 0, i1, i2)
    gate_ref[...] = jnp.where(two == 0, e1, e2) / (e1 + e2)


def _mix_router(a, b, x, w_out, g, wr, br, tm, precise=False):
    m, d = x.shape
    wa, wb = a.shape[1], b.shape[1]
    return pl.pallas_call(
        functools.partial(_mix_router_kernel, precise=precise),
        grid=(m // tm,),
        in_specs=[
            pl.BlockSpec((tm, wa), lambda i: (i, 0)),
            pl.BlockSpec((tm, wb), lambda i: (i, 0)),
            pl.BlockSpec((tm, d), lambda i: (i, 0)),
            pl.BlockSpec((wa + wb, d), lambda i: (0, 0)),
            pl.BlockSpec((1, d), lambda i: (0, 0)),
            pl.BlockSpec((d, 128), lambda i: (0, 0)),
            pl.BlockSpec((1, 128), lambda i: (0, 0)),
        ],
        out_specs=[
            pl.BlockSpec((tm, d), lambda i: (i, 0)),
            pl.BlockSpec((tm, d), lambda i: (i, 0)),
            pl.BlockSpec((tm, 2), lambda i: (i, 0)),
            pl.BlockSpec((tm, 2), lambda i: (i, 0)),
        ],
        out_shape=[jax.ShapeDtypeStruct((m, d), F32), jax.ShapeDtypeStruct((m, d), BF16),
                   jax.ShapeDtypeStruct((m, 2), I32), jax.ShapeDtypeStruct((m, 2), F32)],
        compiler_params=_cparams("arbitrary"),
        name="mix_router",
    )(a, b, x, w_out, g.reshape(1, d), wr, br)


ATT_T = 512


def _t5_bucket(dist):
    max_exact = N_BUCKETS // 2
    d = jnp.maximum(dist, 1).astype(F32)
    large = max_exact + (jnp.log(d / max_exact) / math.log(A_WIN / max_exact)
                         * (N_BUCKETS - max_exact)).astype(I32)
    return jnp.where(dist < max_exact, dist, jnp.minimum(large, N_BUCKETS - 1))


def _distance_logit_table(rel_bias, max_dist):
    dist = jnp.arange(max_dist + 1, dtype=I32)
    mult = jnp.zeros((max_dist + 1,), F32)
    for window, d in A_BRANCHES:
        mult = mult + ((dist % d == 0) & (dist <= window)).astype(F32)
    bias = rel_bias[_t5_bucket(dist)].astype(F32).T
    return jnp.where(mult[None, :] > 0, bias + jnp.log(jnp.maximum(mult, 1.0))[None, :], NEG)


def _attn_kernel(q_ref, kt_ref, vt_ref, row0_ref, o_ref, kb_ref, vb_ref):
    s_len = q_ref.shape[1]
    nb = s_len // ATT_T
    period = row0_ref.shape[2]
    kb_ref[...] = kt_ref[0].astype(BF16)
    vb_ref[...] = vt_ref[0].astype(BF16)
    scale = A_HD ** -0.5
    for hl in range(2):
        hrows = slice(hl * A_HD, (hl + 1) * A_HD)
        table = pltpu.roll(jnp.broadcast_to(row0_ref[0, hl:hl + 1, :], (ATT_T, period)), 0, 1,
                           stride=1, stride_axis=0)
        for i in range(nb):
            rows = slice(i * ATT_T, (i + 1) * ATT_T)
            width = (i + 1) * ATT_T
            q = (q_ref[0, rows, hrows] * scale).astype(BF16)
            s = jnp.dot(q, kb_ref[hrows, 0:width], preferred_element_type=F32)
            s = s + table[:, (nb - 1 - i) * ATT_T:nb * ATT_T]
            m = jnp.max(s, axis=-1, keepdims=True)
            p = jnp.exp(s - m)
            den = jnp.sum(p, axis=-1, keepdims=True)
            o = lax.dot_general(p.astype(BF16), vb_ref[hrows, 0:width], (((1,), (1,)), ((), ())),
                                preferred_element_type=F32)
            o_ref[0, rows, hrows] = o / den


def _attention_prompt(q, kt, vt, rel_bias):
    b, s_len, _ = q.shape
    nb = s_len // ATT_T
    tab1 = _distance_logit_table(rel_bias, s_len)
    period = s_len + ATT_T
    z = jnp.arange(period, dtype=I32)
    delta = jnp.where(z < s_len, (nb - 1) * ATT_T - z, (nb - 1) * ATT_T + period - z)
    row0 = jnp.where(delta[None] >= 0, tab1[:, jnp.clip(delta, 0, s_len)], NEG)
    return pl.pallas_call(
        _attn_kernel,
        grid=(b, A_HEADS // 2),
        in_specs=[pl.BlockSpec((1, s_len, 2 * A_HD), lambda bi, hp: (bi, 0, hp)),
                  pl.BlockSpec((1, 2 * A_HD, s_len), lambda bi, hp: (bi, hp, 0)),
                  pl.BlockSpec((1, 2 * A_HD, s_len), lambda bi, hp: (bi, hp, 0)),
                  pl.BlockSpec((1, 2, period), lambda bi, hp: (hp, 0, 0))],
        out_specs=pl.BlockSpec((1, s_len, 2 * A_HD), lambda bi, hp: (bi, 0, hp)),
        out_shape=jax.ShapeDtypeStruct(q.shape, F32),
        scratch_shapes=[pltpu.VMEM((2 * A_HD, s_len), BF16), pltpu.VMEM((2 * A_HD, s_len), BF16)],
        compiler_params=_cparams("arbitrary", "arbitrary"),
        name="dilated_attention",
    )(q, kt, vt, row0.reshape(A_HEADS // 2, 2, period))


MLSTM_CHUNK = 256


def _log_sigmoid(x):
    return jnp.minimum(x, 0.0) - jnp.log(1.0 + jnp.exp(-jnp.abs(x)))


def _mlstm_kernel(qk_ref, v_ref, ob_ref, gc_ref, gr_ref, wc_ref, bc_ref, bifc_ref, bifr_ref, gh_ref,
                  o_ref, c_out_ref, n_out_ref, m_out_ref, xbuf_ref, c_ref, n_ref, m_ref):
    ci = pl.program_id(1)
    L = qk_ref.shape[1]

    @pl.when(ci == 0)
    def _():
        xbuf_ref[0:8, :] = jnp.zeros((8, 2 * B_W), F32)
        c_ref[...] = jnp.zeros_like(c_ref)
        n_ref[...] = jnp.zeros_like(n_ref)
        m_ref[...] = jnp.zeros_like(m_ref)

    x = qk_ref[0]
    xbuf_ref[8:8 + L, :] = x
    y = bc_ref[...] + wc_ref[3:4, :] * x
    for j in range(B_CONV - 1):
        y = y + wc_ref[j:j + 1, :] * xbuf_ref[5 + j:5 + j + L, :]
    xbuf_ref[0:8, :] = x[L - 8:L, :]
    y = _silu(y)

    gcol = gc_ref[0] + bifc_ref[...]
    grow = gr_ref[...] + bifr_ref[...]
    ri = lax.broadcasted_iota(I32, (L, L), 0)
    cj = lax.broadcasted_iota(I32, (L, L), 1)
    causal = ri >= cj
    lower = jnp.where(causal, 1.0, 0.0)
    b_col = jnp.dot(lower, _log_sigmoid(gcol), preferred_element_type=F32, precision=HIGHEST)
    b_row = lax.dot_general(_log_sigmoid(grow), lower, (((1,), (1,)), ((), ())),
                            preferred_element_type=F32, precision=HIGHEST)

    for h in range(B_HEADS):
        cols = slice(h * B_HD, (h + 1) * B_HD)
        qf = y[:, cols]
        q = qf.astype(BF16)
        kf = y[:, B_W + h * B_HD:B_W + (h + 1) * B_HD] * (B_HD ** -0.5)
        k = kf.astype(BF16)
        v = v_ref[0, :, cols].astype(BF16)
        bc = b_col[:, B_HEADS + h:B_HEADS + h + 1]
        ic = gcol[:, h:h + 1]
        br = b_row[B_HEADS + h:B_HEADS + h + 1, :]
        ir = grow[h:h + 1, :]
        m_prev = m_ref[h:h + 1, 0:1]
        dm = jnp.where(causal, bc - br + ir, NEG)
        inter = bc + m_prev
        mt = jnp.maximum(inter, jnp.max(dm, axis=1, keepdims=True))
        w_intra = jnp.exp(dm - mt)
        w_inter = jnp.exp(inter - mt)
        a = w_intra * lax.dot_general(q, k, (((1,), (1,)), ((), ())), preferred_element_type=F32)
        c_prev = c_ref[h]
        n_prev = n_ref[h:h + 1, :]
        num = (jnp.dot(a.astype(BF16), v, preferred_element_type=F32)
               + w_inter * jnp.dot(q, c_prev.astype(BF16), preferred_element_type=F32))
        den = (jnp.sum(a, axis=1, keepdims=True)
               + w_inter * jnp.sum(qf * n_prev, axis=1, keepdims=True))
        hb = num / jnp.maximum(jnp.abs(den), jnp.exp(-mt))
        m_new = mt[L - 1:L, :]
        b_last = bc[L - 1:L, :]
        g = jnp.exp(b_last - bc + ic - m_new)
        decay = jnp.exp(b_last + m_prev - m_new)
        kg = kf * g
        c_ref[h] = decay * c_prev + lax.dot_general(kg.astype(BF16), v, (((0,), (0,)), ((), ())),
                                                    preferred_element_type=F32)
        n_ref[h:h + 1, :] = decay * n_prev + jnp.sum(kg, axis=0, keepdims=True)
        m_ref[h:h + 1, :] = jnp.broadcast_to(m_new, (1, 128))
        hb = hb * lax.rsqrt(jnp.mean(hb * hb, axis=-1, keepdims=True) + EPS) * gh_ref[:, cols]
        o_ref[0, :, cols] = jax.nn.sigmoid(ob_ref[0, :, cols]) * hb

    @pl.when(ci == pl.num_programs(1) - 1)
    def _():
        c_out_ref[0] = c_ref[...]
        n_out_ref[0] = n_ref[...]
        m_out_ref[0] = m_ref[...]


def _mlstm_prompt(qk, v, ob, gcol, grow, w_conv, b_conv, b_if, g_head):
    b, s_len, _ = qk.shape
    L = min(MLSTM_CHUNK, s_len)
    nc = s_len // L
    bif_col = jnp.zeros((1, 128), F32).at[0, :2 * B_HEADS].set(b_if)
    bif_row = b_if.reshape(2 * B_HEADS, 1)
    seq = lambda w: pl.BlockSpec((1, L, w), lambda bi, ci: (bi, ci, 0))
    full = lambda shape: pl.BlockSpec(shape, lambda bi, ci: (0,) * len(shape))
    return pl.pallas_call(
        _mlstm_kernel,
        grid=(b, nc),
        in_specs=[seq(2 * B_W), seq(B_W), seq(B_W), seq(128),
                  pl.BlockSpec((2 * B_HEADS, L), lambda bi, ci: (0, bi * nc + ci)),
                  full((B_CONV, 2 * B_W)), full((1, 2 * B_W)), full((1, 128)), full((2 * B_HEADS, 1)),
                  full((1, B_W))],
        out_specs=[seq(B_W),
                   pl.BlockSpec((1, B_HEADS, B_HD, B_HD), lambda bi, ci: (bi, 0, 0, 0)),
                   pl.BlockSpec((1, 8, B_HD), lambda bi, ci: (bi, 0, 0)),
                   pl.BlockSpec((1, 8, 128), lambda bi, ci: (bi, 0, 0))],
        out_shape=[jax.ShapeDtypeStruct((b, s_len, B_W), F32),
                   jax.ShapeDtypeStruct((b, B_HEADS, B_HD, B_HD), F32),
                   jax.ShapeDtypeStruct((b, 8, B_HD), F32),
                   jax.ShapeDtypeStruct((b, 8, 128), F32)],
        scratch_shapes=[pltpu.VMEM((L + 8, 2 * B_W), F32), pltpu.VMEM((B_HEADS, B_HD, B_HD), F32),
                        pltpu.VMEM((8, B_HD), F32), pltpu.VMEM((8, 128), F32)],
        compiler_params=_cparams("arbitrary", "arbitrary"),
        name="mlstm_chunkwise",
    )(qk, v, ob, gcol, grow, w_conv, b_conv.reshape(1, -1), bif_col, bif_row, g_head.reshape(1, -1))


def _odd_in_kernel(x_ref, g_ref, w_ref, gcv_ref, bcv_ref, ws_ref, bs_ref, c_ref, xd_ref):
    tm = x_ref.shape[0]
    xn = _rms(x_ref[...], g_ref[...]).astype(BF16)
    u = jnp.dot(xn, w_ref[:, 0:C_W], preferred_element_type=F32)
    v = jnp.dot(xn, w_ref[:, C_W:2 * C_W], preferred_element_type=F32)
    xd_ref[...] = jnp.dot(xn, w_ref[:, 2 * C_W:], preferred_element_type=F32)
    mu = jnp.mean(v, axis=-1, keepdims=True)
    vc = v - mu
    var = jnp.mean(vc * vc, axis=-1, keepdims=True)
    vl = (vc * lax.rsqrt(var + EPS) * gcv_ref[...] + bcv_ref[...]).astype(BF16)
    ri = lax.broadcasted_iota(I32, (C_CHUNK, C_CHUNK), 0)
    cj = lax.broadcasted_iota(I32, (C_CHUNK, C_CHUNK), 1)
    for g in range(C_GROUPS):
        cols = slice(g * C_GD, (g + 1) * C_GD)
        wg = jnp.where(ri >= cj, ws_ref[g], 0.0).astype(BF16)
        for c in range(tm // C_CHUNK):
            rows = slice(c * C_CHUNK, (c + 1) * C_CHUNK)
            s = jnp.dot(wg, vl[rows, cols], preferred_element_type=F32) + bs_ref[:, g:g + 1]
            c_ref[rows, cols] = u[rows, cols] * s


def _odd_in(x, g, w, g_cv, b_cv, w_s, b_s, tm):
    m, d = x.shape
    n = w.shape[1]
    full = lambda shape: pl.BlockSpec(shape, lambda i: (0,) * len(shape))
    return pl.pallas_call(
        _odd_in_kernel,
        grid=(m // tm,),
        in_specs=[pl.BlockSpec((tm, d), lambda i: (i, 0)), full((1, d)), full((d, n)), full((1, C_W)),
                  full((1, C_W)), full((C_GROUPS, C_CHUNK, C_CHUNK)), full((C_CHUNK, C_GROUPS))],
        out_specs=[pl.BlockSpec((tm, C_W), lambda i: (i, 0)), pl.BlockSpec((tm, D_W), lambda i: (i, 0))],
        out_shape=[jax.ShapeDtypeStruct((m, C_W), F32), jax.ShapeDtypeStruct((m, D_W), F32)],
        compiler_params=_cparams("arbitrary"),
        name="odd_in_proj_gate",
    )(x, g.reshape(1, d), w, g_cv.reshape(1, -1), b_cv.reshape(1, -1), w_s, b_s.T)


S5_TT = 128
S5_PAD = 8
S5_HALF = D_GROUPS * D_STATE // 2


def _s5_discretize(a_re, a_im, log_dt, bm_re, bm_im, cm_re, cm_im):
    a_re, a_im = a_re.astype(F32), a_im.astype(F32)
    dt = jnp.exp(log_dt.astype(F32))[:, None]
    mag = jnp.exp(a_re * dt)
    ab_re = mag * jnp.cos(a_im * dt)
    ab_im = mag * jnp.sin(a_im * dt)
    inv = 1.0 / (a_re * a_re + a_im * a_im)
    f_re = ((ab_re - 1.0) * a_re + ab_im * a_im) * inv
    f_im = (ab_im * a_re - (ab_re - 1.0) * a_im) * inv
    bm_re, bm_im = bm_re.astype(F32), bm_im.astype(F32)
    bb_re = f_re[..., None] * bm_re - f_im[..., None] * bm_im
    bb_im = f_re[..., None] * bm_im + f_im[..., None] * bm_re
    return ab_re, ab_im, bb_re, bb_im


def _s5_matrices(bb_re, bb_im, cm_re, cm_im):
    gh = D_GROUPS // 2
    eye = jnp.eye(gh, dtype=F32)

    def in_half(bb):
        return jnp.einsum('gpc,gh->gchp', bb, eye).reshape(gh * D_GCH, gh * D_STATE)

    def out_half(cm):
        return jnp.einsum('gcp,gh->gphc', cm, eye).reshape(gh * D_STATE, gh * D_GCH)

    bd = jnp.stack([jnp.concatenate([in_half(bb_re[h * gh:(h + 1) * gh]), in_half(bb_im[h * gh:(h + 1) * gh])], axis=1)
                    for h in range(2)])
    cm = jnp.stack([jnp.concatenate([out_half(cm_re[h * gh:(h + 1) * gh].astype(F32)),
                                     -out_half(cm_im[h * gh:(h + 1) * gh].astype(F32))], axis=0)
                    for h in range(2)])
    return bd, cm


def _gelu_tanh(x):
    return 0.5 * x * (1.0 + jnp.tanh(math.sqrt(2.0 / math.pi) * (x + 0.044715 * (x * x * x))))


def _s5_kernel(x_ref, bd_ref, cm_ref, ar_ref, ai_ref, dsk_ref, wg_ref, bg_ref, o_ref, hr_out, hi_out,
               bu_ref, hs_ref, hr_ref, hi_ref):
    i = pl.program_id(0)
    nb, tt, _ = x_ref.shape
    stride = tt + S5_PAD
    half_in = D_W // 2
    nct = S5_HALF // 128

    @pl.when(i == 0)
    def _():
        hr_ref[...] = jnp.zeros_like(hr_ref)
        hi_ref[...] = jnp.zeros_like(hi_ref)

    x = x_ref[...].reshape(nb * tt, D_W)
    ys = []
    for h in range(2):
        lanes = slice(h * S5_HALF, (h + 1) * S5_HALF)
        bu = jnp.dot(x[:, h * half_in:(h + 1) * half_in].astype(BF16), bd_ref[h], preferred_element_type=F32)
        for c in range(2 * nct):
            for b in range(nb):
                bu_ref[c, b * stride:b * stride + tt, :] = bu[b * tt:(b + 1) * tt, c * 128:(c + 1) * 128]
        ar = jnp.broadcast_to(ar_ref[:, lanes], (nb, S5_HALF))
        ai = jnp.broadcast_to(ai_ref[:, lanes], (nb, S5_HALF))

        def step(t, carry):
            hr, hi = carry
            rows = pl.ds(t, nb, stride=stride)
            bur = jnp.concatenate([bu_ref[c, rows, :] for c in range(nct)], axis=1)
            bui = jnp.concatenate([bu_ref[nct + c, rows, :] for c in range(nct)], axis=1)
            nhr = ar * hr - ai * hi + bur
            nhi = ar * hi + ai * hr + bui
            for c in range(nct):
                hs_ref[c, rows, :] = nhr[:, c * 128:(c + 1) * 128]
                hs_ref[nct + c, rows, :] = nhi[:, c * 128:(c + 1) * 128]
            return nhr, nhi

        hr, hi = lax.fori_loop(0, tt, step, (hr_ref[:, lanes], hi_ref[:, lanes]))
        hr_ref[:, lanes] = hr
        hi_ref[:, lanes] = hi
        hs = jnp.concatenate(
            [jnp.concatenate([hs_ref[c, b * stride:b * stride + tt, :] for b in range(nb)], axis=0)
             for c in range(2 * nct)], axis=1)
        ys.append(jnp.dot(hs.astype(BF16), cm_ref[h], preferred_element_type=F32))
    y = jnp.concatenate(ys, axis=-1) + dsk_ref[...] * x
    g = _gelu_tanh(y)
    d_out = g * jax.nn.sigmoid(jnp.dot(g.astype(BF16), wg_ref[...], preferred_element_type=F32) + bg_ref[...])
    o_ref[...] = d_out.reshape(nb, tt, D_W)

    @pl.when(i == pl.num_programs(0) - 1)
    def _():
        hr_out[...] = hr_ref[...]
        hi_out[...] = hi_ref[...]


def _s5_prompt(xd, bd, cm, ab_re, ab_im, d_skip, w_glu, b_glu):
    nb, t_len, _ = xd.shape
    tt = min(S5_TT, t_len)
    n_state = D_GROUPS * D_STATE
    full = lambda shape: pl.BlockSpec(shape, lambda i: (0,) * len(shape))
    return pl.pallas_call(
        _s5_kernel,
        grid=(t_len // tt,),
        in_specs=[pl.BlockSpec((nb, tt, D_W), lambda i: (0, i, 0)), full(bd.shape), full(cm.shape),
                  full((1, n_state)), full((1, n_state)), full((1, D_W)), full((D_W, D_W)), full((1, D_W))],
        out_specs=[pl.BlockSpec((nb, tt, D_W), lambda i: (0, i, 0)), full((nb, n_state)), full((nb, n_state))],
        out_shape=[jax.ShapeDtypeStruct(xd.shape, F32), jax.ShapeDtypeStruct((nb, n_state), F32),
                   jax.ShapeDtypeStruct((nb, n_state), F32)],
        scratch_shapes=[pltpu.VMEM((2 * S5_HALF // 128, nb * (tt + S5_PAD), 128), F32),
                        pltpu.VMEM((2 * S5_HALF // 128, nb * (tt + S5_PAD), 128), F32),
                        pltpu.VMEM((nb, n_state), F32), pltpu.VMEM((nb, n_state), F32)],
        compiler_params=_cparams("arbitrary"),
        name="s5_scan_glu",
    )(xd, bd, cm, ab_re.reshape(1, n_state), ab_im.reshape(1, n_state), d_skip.reshape(1, D_W), w_glu,
      b_glu.reshape(1, D_W))


def _hdot(a, b):
    return jnp.dot(a, b, precision=HIGHEST, preferred_element_type=F32)


def _sample_window_kernel(q_ref, kn_ref, vn_ref, tab_ref, tabn_ref, kt_ref, vt_ref, o_ref, okt_ref, ovt_ref):
    L = kt_ref.shape[3]
    last = lax.broadcasted_iota(I32, (A_HD, L), 1) == L - 1
    for h in range(A_HEADS):
        kh, vh = kt_ref[0, h], vt_ref[0, h]
        kn, vn = kn_ref[0, h], vn_ref[0, h]
        q = q_ref[0, h] * (A_HD ** -0.5)
        s = jnp.sum(kh * q, axis=0, keepdims=True) + tab_ref[h]
        s_new = jnp.sum(kn * q, axis=0, keepdims=True) + tabn_ref[h]
        m = jnp.maximum(jnp.max(s, axis=1, keepdims=True), s_new)
        p = jnp.exp(s - m)
        p_new = jnp.exp(s_new - m)
        den = jnp.sum(p, axis=1, keepdims=True) + p_new
        o_ref[0, h] = (jnp.sum(vh * p, axis=1, keepdims=True) + vn * p_new) / den
        okt_ref[0, h] = jnp.where(last, kn, pltpu.roll(kh, L - 1, 1))
        ovt_ref[0, h] = jnp.where(last, vn, pltpu.roll(vh, L - 1, 1))


def _sample_window_attention(q, k_new, v_new, cache_kt, cache_vt, rel_bias):
    n, _, _, L = cache_kt.shape
    assert L == A_WIN
    tab1 = _distance_logit_table(rel_bias, L)
    tab = tab1[:, :0:-1].reshape(A_HEADS, 1, L)
    tab_new = tab1[:, 0].reshape(A_HEADS, 1, 1)
    col = pl.BlockSpec((1, A_HEADS, A_HD, 1), lambda i: (i, 0, 0, 0))
    win = pl.BlockSpec((1, A_HEADS, A_HD, L), lambda i: (i, 0, 0, 0))
    full = lambda shape: pl.BlockSpec(shape, lambda i: (0,) * len(shape))
    return pl.pallas_call(
        _sample_window_kernel,
        grid=(n,),
        in_specs=[col, col, col, full(tab.shape), full(tab_new.shape), win, win],
        out_specs=[col, win, win],
        out_shape=[jax.ShapeDtypeStruct((n, A_HEADS, A_HD, 1), F32),
                   jax.ShapeDtypeStruct(cache_kt.shape, F32), jax.ShapeDtypeStruct(cache_vt.shape, F32)],
        compiler_params=_cparams("arbitrary"),
        name="sample_window_attention",
    )(q, k_new, v_new, tab, tab_new, cache_kt, cache_vt)


def _sample_even_proj(x, g_mix, w_in):
    n_main = 3 * A_W + 4 * B_W
    segs = ((0, A_W), (A_W, A_W), (2 * A_W, A_W), (3 * A_W, 2 * B_W), (3 * A_W + 2 * B_W, B_W),
            (3 * A_W + 3 * B_W, B_W), (n_main, 2 * B_HEADS))
    return _norm_proj(x, g_mix, w_in, jnp.zeros((8, x.shape[1]), F32), segs, (), tm=x.shape[0], precise=True)


def _sample_even_mixers(proj, cache_kt, cache_vt, c0, n0, m0, conv0, rel_bias, b_if, w_conv, b_conv, g_head):
    qa, ka, va, qkb, vb, ob, gates = proj
    n = qa.shape[0]
    cols = lambda a: a.reshape(n, A_HEADS, A_HD, 1)
    a_out, new_kt, new_vt = _sample_window_attention(cols(qa), cols(ka), cols(va), cache_kt, cache_vt, rel_bias)
    a_out = a_out.reshape(n, A_W)
    xp = jnp.concatenate([conv0, qkb[:, None, :]], axis=1)
    qk = _silu(b_conv + jnp.sum(xp * w_conv[None], axis=1))
    q = qk[:, :B_W].reshape(n, B_HEADS, B_HD)
    k = qk[:, B_W:].reshape(n, B_HEADS, B_HD) * (B_HD ** -0.5)
    v = vb.reshape(n, B_HEADS, B_HD)
    ig = gates[:, :B_HEADS] + b_if[:B_HEADS]
    logf = _log_sigmoid(gates[:, B_HEADS:] + b_if[B_HEADS:])
    inter = logf + m0
    mt = jnp.maximum(inter, ig)
    w_intra = jnp.exp(ig - mt)
    w_inter = jnp.exp(inter - mt)
    a = w_intra * jnp.sum(q * k, axis=-1)
    num = a[..., None] * v + w_inter[..., None] * jnp.einsum('nhk,nhkv->nhv', q, c0, precision=HIGHEST)
    den = a + w_inter * jnp.sum(q * n0, axis=-1)
    hb = num / jnp.maximum(jnp.abs(den), jnp.exp(-mt))[..., None]
    g = jnp.exp(ig - mt)
    decay = jnp.exp(inter - mt)
    c_new = decay[..., None, None] * c0 + (g[..., None] * k)[..., :, None] * v[..., None, :]
    n_new = decay[..., None] * n0 + g[..., None] * k
    hb = hb * lax.rsqrt(jnp.mean(hb * hb, axis=-1, keepdims=True) + EPS) * g_head.reshape(B_HEADS, B_HD)
    b_out = jax.nn.sigmoid(ob) * hb.reshape(n, B_W)
    return a_out, b_out, new_kt, new_vt, c_new, n_new, mt, xp[:, 1:]


def _sample_odd_mixers(x, h_re0, h_im0, g_mix, w_in, g_cv, b_cv, w_s, b_s, ab_re, ab_im, bb_re, bb_im,
                       cm_re, cm_im, d_skip, w_glu, b_glu):
    n = x.shape[0]
    h = _rms(x, g_mix)
    proj = _hdot(h, w_in)
    u, v, xd = proj[:, :C_W], proj[:, C_W:2 * C_W], proj[:, 2 * C_W:]
    mu = jnp.mean(v, axis=-1, keepdims=True)
    vc = v - mu
    v = vc * lax.rsqrt(jnp.mean(vc * vc, axis=-1, keepdims=True) + EPS) * g_cv + b_cv
    s = (w_s[:, 0, 0][None, :, None] * v.reshape(n, C_GROUPS, C_GD) + b_s[:, 0][None, :, None]).reshape(n, C_W)
    c_out = u * s
    xg = xd.reshape(n, D_GROUPS, D_GCH)
    bu_re = jnp.einsum('gpc,ngc->ngp', bb_re, xg, precision=HIGHEST)
    bu_im = jnp.einsum('gpc,ngc->ngp', bb_im, xg, precision=HIGHEST)
    hr = ab_re * h_re0 - ab_im * h_im0 + bu_re
    hi = ab_re * h_im0 + ab_im * h_re0 + bu_im
    y = (jnp.einsum('gcp,ngp->ngc', cm_re, hr, precision=HIGHEST)
         - jnp.einsum('gcp,ngp->ngc', cm_im, hi, precision=HIGHEST) + d_skip * xg)
    g = _gelu_tanh(y.reshape(n, D_W))
    d_out = g * jax.nn.sigmoid(_hdot(g, w_glu) + b_glu)
    return c_out, d_out, v, hr, hi


_EVEN_SEGS = ((0, A_W), (A_W, 2 * B_W), (A_W + 2 * B_W, B_W), (A_W + 3 * B_W, B_W), (A_W + 4 * B_W, 128))


def kernel(x_prompt, x_sample, cache_a_k, cache_a_v, state_b_c, state_b_n, state_b_m, state_b_conv,
           state_d_re, state_d_im, rel_bias, g_mix, g_ffn, g_final, w_in_e, b_if, w_conv_b, b_conv_b,
           g_bhead, w_out_e, w1_e, w3_e, w2_e, w_in_o, g_cv, b_cv, w_s, b_s, a_re, a_im, log_dt,
           bm_re, bm_im, cm_re, cm_im, d_skip, w_glu, b_glu, w_out_o, w_router, b_router,
           w1_m, w3_m, w2_m):
    bp, sp, d = x_prompt.shape
    ns = x_sample.shape[0]
    mp = bp * sp
    xp = x_prompt.reshape(mp, d)
    xs = x_sample.reshape(ns, d)

    w_in = w_in_e[0]
    n_main = 3 * A_W + 4 * B_W
    w_gate = w_in[:, n_main:]
    w_cat = jnp.concatenate([w_in[:, :A_W], w_in[:, 3 * A_W:n_main],
                             jnp.pad(w_gate, ((0, 0), (0, 128 - 2 * B_HEADS)))], axis=1)
    w_t = jnp.concatenate([w_in[:, A_W:3 * A_W], w_gate], axis=1).T
    q, qkb, vb, ob, gcol, kt, vt, grow = _norm_proj(
        xp, g_mix[0], w_cat.astype(BF16), w_t.astype(BF16), _EVEN_SEGS,
        ((0, A_W, True), (A_W, A_W, True), (2 * A_W, 2 * B_HEADS, False)), tm=512, rows_per_seq=sp)
    shp = lambda a: a.reshape(bp, sp, a.shape[-1])
    a_out = _attention_prompt(shp(q), kt, vt, rel_bias)
    b_out, pc, pn, pm = _mlstm_prompt(shp(qkb), shp(vb), shp(ob), shp(gcol), grow, w_conv_b[0], b_conv_b[0],
                                      b_if[0], g_bhead[0])
    x1p = _mix_ffn(a_out.reshape(mp, A_W), b_out.reshape(mp, B_W), xp, w_out_e[0].astype(BF16), g_ffn[0],
                   w1_e[0].astype(BF16), w3_e[0].astype(BF16), w2_e[0].astype(BF16), tm=1024, tf=512)

    pos_minor = lambda c: jnp.transpose(c, (0, 2, 3, 1))
    pos_major = lambda c: jnp.transpose(c, (0, 3, 1, 2))
    s_proj = _sample_even_proj(xs, g_mix[0], w_in)
    sa, sb, s_kt, s_vt, sc, sn, sm, s_conv = _sample_even_mixers(
        s_proj, pos_minor(cache_a_k[0]), pos_minor(cache_a_v[0]), state_b_c[0], state_b_n[0], state_b_m[0],
        state_b_conv[0], rel_bias, b_if[0], w_conv_b[0], b_conv_b[0], g_bhead[0])
    s_k, s_v = pos_major(s_kt), pos_major(s_vt)
    x1s = _mix_ffn(sa, sb, xs, w_out_e[0], g_ffn[0], w1_e[0], w3_e[0], w2_e[0], tm=ns, tf=512, precise=True)

    ab_re, ab_im, bb_re, bb_im = _s5_discretize(a_re[0], a_im[0], log_dt[0], bm_re[0], bm_im[0], cm_re[0],
                                                cm_im[0])
    bd, cmat = _s5_matrices(bb_re, bb_im, cm_re[0], cm_im[0])
    c_out, xd = _odd_in(x1p, g_mix[1], w_in_o[0].astype(BF16), g_cv[0], b_cv[0], w_s[0], b_s[0], tm=512)
    d_out, p_hr, p_hi = _s5_prompt(xd.reshape(bp, sp, D_W), bd.astype(BF16), cmat.astype(BF16), ab_re, ab_im,
                                   d_skip[0], w_glu[0].astype(BF16), b_glu[0])
    wr = jnp.pad(w_router[0].astype(F32), ((0, 0), (0, 128 - N_EXPERTS)))
    br = jnp.pad(b_router[0].astype(F32), (0, 128 - N_EXPERTS)).reshape(1, 128)
    x2p, hnp, idp, gtp = _mix_router(c_out, d_out.reshape(mp, D_W), x1p, w_out_o[0].astype(BF16), g_ffn[1], wr, br,
                                     tm=512)

    sc_out, sd_out, s_cv, s_hr, s_hi = _sample_odd_mixers(
        x1s, state_d_re[0], state_d_im[0], g_mix[1], w_in_o[0], g_cv[0], b_cv[0], w_s[0], b_s[0], ab_re, ab_im,
        bb_re, bb_im, cm_re[0], cm_im[0], d_skip[0], w_glu[0], b_glu[0])
    x2s, hns, ids_s, gts = _mix_router(sc_out, sd_out, x1s, w_out_o[0], g_ffn[1], wr, br, tm=ns, precise=True)

    pad = ((0, ROUTE_TILE - ns), (0, 0))
    y_p, y_s = _moe_block(x2p, hnp, idp, gtp, jnp.pad(x2s, pad), jnp.pad(hns, pad),
                          jnp.pad(ids_s, pad, constant_values=-1), jnp.pad(gts, pad),
                          w1_m[0], w3_m[0], w2_m[0], g_final)
    y_prompt = y_p.reshape(bp, sp, d)
    y_sample = y_s[:ns].reshape(ns, 1, d)

    e = lambda a: a[None]
    p_k = pos_major(kt.reshape(bp, A_HEADS, A_HD, sp))
    p_v = pos_major(vt.reshape(bp, A_HEADS, A_HD, sp))
    p_conv = qkb.reshape(bp, sp, 2 * B_W)[:, sp - (B_CONV - 1):]
    return (y_prompt, y_sample, e(p_k), e(p_v), e(s_k), e(s_v), e(pc), e(sc), e(pn[:, :B_HEADS]), e(sn),
            e(pm[:, :B_HEADS, 0]), e(sm), e(p_conv), e(s_conv), e(s_cv.reshape(ns, 1, C_W)),
            e(p_hr.reshape(bp, D_GROUPS, D_STATE)), e(s_hr), e(p_hi.reshape(bp, D_GROUPS, D_STATE)), e(s_hi))
```

```python
import functools
import math

import jax
import jax.numpy as jnp
from jax import lax
from jax.experimental import pallas as pl
from jax.experimental.pallas import tpu as pltpu

F32 = jnp.float32
BF16 = jnp.bfloat16
I32 = jnp.int32
HIGHEST = lax.Precision.HIGHEST

D_MODEL = 1024
A_HEADS, A_HD = 8, 64
A_W = A_HEADS * A_HD
A_BRANCHES = ((128, 1), (512, 4), (2048, 16))
A_WIN = 2048
N_BUCKETS = 32
B_HEADS, B_HD = 4, 128
B_W = B_HEADS * B_HD
B_CONV = 4
C_GROUPS, C_GD = 4, 128
C_W = C_GROUPS * C_GD
C_CHUNK = 128
D_GROUPS, D_GCH, D_STATE = 32, 16, 64
D_W = D_GROUPS * D_GCH
D_FF = 3584
N_EXPERTS = 8
TOP_K = 2
EPS = 1e-6
NEG = -1e30

VMEM_LIMIT = 56 * 1024 * 1024

ROUTE_TILE = 256
FFN_TILE = 1024
FFN_ROW_CHUNK = 256
GATHER_TOKENS = 2 * ROUTE_TILE
COMBINE_ALIGN = 16
COMBINE_WIN = ROUTE_TILE + COMBINE_ALIGN


def _cparams(*sem):
    return pltpu.CompilerParams(dimension_semantics=sem, vmem_limit_bytes=VMEM_LIMIT)


def _moe_plan(ids, rank, cnt_tile):
    mt = ids.shape[0]
    n_assign = 2 * mt
    n_ffn_tiles = -(-(n_assign + N_EXPERTS * (FFN_TILE - 1)) // FFN_TILE) + 1
    n_sub = n_ffn_tiles * (FFN_TILE // ROUTE_TILE)
    counts = cnt_tile[-1]
    seg = ((counts + FFN_TILE - 1) // FFN_TILE) * FFN_TILE
    seg_end = jnp.cumsum(seg)
    seg_off = seg_end - seg
    dest = jnp.where(ids >= 0, seg_off[jnp.clip(ids, 0)] + rank, -1)
    total = seg_end[-1]
    tile_start = jnp.arange(n_ffn_tiles, dtype=I32) * FFN_TILE
    tile_expert = jnp.minimum(jnp.sum((tile_start[:, None] >= seg_end[None, :]).astype(I32), axis=1),
                              N_EXPERTS - 1).astype(I32)
    rows_left = (seg_off + counts)[tile_expert] - tile_start
    tile_chunks = jnp.where(tile_start < total,
                            jnp.clip(-(-rows_left // FFN_ROW_CHUNK), 0, FFN_TILE // FFN_ROW_CHUNK), 0).astype(I32)
    lo = seg_off[None, :] + cnt_tile[:-1]
    hi = seg_off[None, :] + cnt_tile[1:]
    sub_start = jnp.arange(n_sub, dtype=I32) * ROUTE_TILE
    sub_expert = jnp.minimum(jnp.sum((sub_start[:, None] >= seg_end[None, :]).astype(I32), axis=1),
                             N_EXPERTS - 1)
    lo_e, hi_e = lo.T[sub_expert], hi.T[sub_expert]
    jlo = jnp.sum((hi_e <= sub_start[:, None]).astype(I32), axis=1).astype(I32)
    jhi = (jnp.sum((lo_e < sub_start[:, None] + ROUTE_TILE).astype(I32), axis=1) - 1).astype(I32)
    win = lo // COMBINE_ALIGN
    flat1 = lambda a: a.reshape(-1).astype(I32)
    return dest, tile_expert, tile_chunks, (jlo, jhi), (flat1(win), flat1(lo), flat1(hi)), n_ffn_tiles


def _moe_gather_kernel(jlo_ref, jhi_ref, hp_hbm, hs_hbm, dt_ref, gt_ref, xs_ref, gs_ref,
                       hn_ref, acc_ref, g_ref, sem):
    s = pl.program_id(0)
    mp = hp_hbm.shape[0]

    @pl.when(s == 0)
    def _():
        copies = (pltpu.make_async_copy(hp_hbm, hn_ref.at[0:mp], sem.at[0]),
                  pltpu.make_async_copy(hs_hbm, hn_ref.at[mp:mp + ROUTE_TILE], sem.at[1]))
        for c in copies:
            c.start()
        hn_ref[mp + ROUTE_TILE:, :] = jnp.zeros((ROUTE_TILE, hn_ref.shape[1]), BF16)
        for c in copies:
            c.wait()

    acc_ref[...] = jnp.zeros_like(acc_ref)
    g_ref[...] = jnp.zeros_like(g_ref)
    rows = lax.broadcasted_iota(I32, (ROUTE_TILE, GATHER_TOKENS), 0) + s * ROUTE_TILE
    jlo = jlo_ref[s]

    def body(it, carry):
        off = pl.multiple_of(jlo * ROUTE_TILE + it * GATHER_TOKENS, ROUTE_TILE)
        dt = dt_ref[:, pl.ds(off, GATHER_TOKENS)]
        gt = gt_ref[:, pl.ds(off, GATHER_TOKENS)]
        m1 = dt[0:1, :] == rows
        m2 = dt[1:2, :] == rows
        onehot = jnp.where(m1 | m2, 1.0, 0.0).astype(BF16)
        acc_ref[...] += jnp.dot(onehot, hn_ref[pl.ds(off, GATHER_TOKENS), :], preferred_element_type=F32)
        g_ref[...] += jnp.sum(jnp.where(m1, gt[0:1, :], 0.0) + jnp.where(m2, gt[1:2, :], 0.0),
                              axis=1, keepdims=True)
        return carry

    tiles_per_iter = GATHER_TOKENS // ROUTE_TILE
    lax.fori_loop(0, (jhi_ref[s] - jlo + tiles_per_iter) // tiles_per_iter, body, 0)
    xs_ref[...] = acc_ref[...].astype(BF16)
    gs_ref[...] = g_ref[...]


def _moe_gather(hn_p, hn_s, dest_t, gates_t, glist, n_sub):
    jlo, jhi = glist
    mp, d = hn_p.shape
    mt = dest_t.shape[1]
    grid_spec = pltpu.PrefetchScalarGridSpec(
        num_scalar_prefetch=2,
        grid=(n_sub,),
        in_specs=[
            pl.BlockSpec(memory_space=pl.ANY),
            pl.BlockSpec(memory_space=pl.ANY),
            pl.BlockSpec((2, mt), lambda s, *_: (0, 0)),
            pl.BlockSpec((2, mt), lambda s, *_: (0, 0)),
        ],
        out_specs=[
            pl.BlockSpec((ROUTE_TILE, d), lambda s, *_: (s, 0)),
            pl.BlockSpec((ROUTE_TILE, 1), lambda s, *_: (s, 0)),
        ],
        scratch_shapes=[pltpu.VMEM((mp + 2 * ROUTE_TILE, d), BF16), pltpu.VMEM((ROUTE_TILE, d), F32),
                        pltpu.VMEM((ROUTE_TILE, 1), F32), pltpu.SemaphoreType.DMA((2,))],
    )
    return pl.pallas_call(
        _moe_gather_kernel,
        grid_spec=grid_spec,
        out_shape=[jax.ShapeDtypeStruct((n_sub * ROUTE_TILE, d), BF16),
                   jax.ShapeDtypeStruct((n_sub * ROUTE_TILE, 1), F32)],
        compiler_params=_cparams("arbitrary"),
        name="moe_gather",
    )(jlo, jhi, hn_p, hn_s, dest_t, gates_t)


def _moe_ffn_kernel(te_ref, nc_ref, x_ref, gs_ref, w1_ref, w3_ref, w2_ref, y_ref, acc_ref):
    t, f = pl.program_id(0), pl.program_id(1)
    nf = pl.num_programs(1)
    n_chunks = nc_ref[t]
    full = FFN_TILE // FFN_ROW_CHUNK

    @pl.when(n_chunks > 0)
    def _():
        @pl.when(f == 0)
        def _():
            acc_ref[...] = jnp.zeros_like(acc_ref)

        w1, w3, w2 = w1_ref[0].astype(BF16), w3_ref[0].astype(BF16), w2_ref[0].astype(BF16)

        def ffn(rows):
            x = x_ref[rows, :]
            a = jnp.dot(x, w1, preferred_element_type=F32)
            b = jnp.dot(x, w3, preferred_element_type=F32)
            acc_ref[rows, :] += jnp.dot((_silu(a) * b).astype(BF16), w2, preferred_element_type=F32)

        @pl.when(n_chunks == full)
        def _():
            ffn(slice(0, FFN_TILE))

        for c in range(full - 1):
            @pl.when((n_chunks < full) & (c < n_chunks))
            def _():
                ffn(slice(c * FFN_ROW_CHUNK, (c + 1) * FFN_ROW_CHUNK))

        @pl.when(f == nf - 1)
        def _():
            y_ref[...] = (acc_ref[...] * gs_ref[...]).astype(BF16)

    @pl.when((n_chunks == 0) & (f == nf - 1))
    def _():
        y_ref[...] = jnp.zeros_like(y_ref)


def _moe_ffn(xs, gs, w1, w3, w2, tile_expert, tile_chunks, n_ffn_tiles, tf):
    d = xs.shape[1]
    ff = w1.shape[2]
    nf = ff // tf

    def fidx(t, f, nc):
        return jnp.where(nc[t] > 0, f, nf - 1)

    grid_spec = pltpu.PrefetchScalarGridSpec(
        num_scalar_prefetch=2,
        grid=(n_ffn_tiles, nf),
        in_specs=[
            pl.BlockSpec((FFN_TILE, d), lambda t, f, te, nu: (t, 0)),
            pl.BlockSpec((FFN_TILE, 1), lambda t, f, te, nu: (t, 0)),
            pl.BlockSpec((1, d, tf), lambda t, f, te, nu: (te[t], 0, fidx(t, f, nu))),
            pl.BlockSpec((1, d, tf), lambda t, f, te, nu: (te[t], 0, fidx(t, f, nu))),
            pl.BlockSpec((1, tf, d), lambda t, f, te, nu: (te[t], fidx(t, f, nu), 0)),
        ],
        out_specs=pl.BlockSpec((FFN_TILE, d), lambda t, f, te, nu: (t, 0)),
        scratch_shapes=[pltpu.VMEM((FFN_TILE, d), F32)],
    )
    return pl.pallas_call(
        _moe_ffn_kernel,
        grid_spec=grid_spec,
        out_shape=jax.ShapeDtypeStruct(xs.shape, BF16),
        compiler_params=_cparams("arbitrary", "arbitrary"),
        name="moe_ffn",
    )(tile_expert, tile_chunks, xs, gs, w1, w3, w2)


def _moe_combine_kernel(win_ref, lo_ref, hi_ref, xp_ref, xs_ref, d_ref, g_ref, *rest):
    ys_refs = rest[:N_EXPERTS]
    yp_ref, ysm_ref, acc_ref = rest[N_EXPERTS:]
    j = pl.program_id(0)
    n_prompt_tiles = pl.num_programs(0) - 1

    @pl.when(j < n_prompt_tiles)
    def _():
        acc_ref[...] = xp_ref[...]

    @pl.when(j == n_prompt_tiles)
    def _():
        acc_ref[...] = xs_ref[...]

    for e in range(N_EXPERTS):
        lo, hi = lo_ref[j * N_EXPERTS + e], hi_ref[j * N_EXPERTS + e]

        @pl.when(hi > lo)
        def _():
            d = d_ref[...]
            d = jnp.where((d >= lo) & (d < hi), d, -1)
            cols = (lax.broadcasted_iota(I32, (ROUTE_TILE, COMBINE_WIN), 1)
                    + win_ref[j * N_EXPERTS + e] * COMBINE_ALIGN)
            onehot = jnp.where((d[:, 0:1] == cols) | (d[:, 1:2] == cols), 1.0, 0.0).astype(BF16)
            acc_ref[...] += jnp.dot(onehot, ys_refs[e][...], preferred_element_type=F32)

    y = _rms(acc_ref[...], g_ref[...])

    @pl.when(j < n_prompt_tiles)
    def _():
        yp_ref[...] = y

    @pl.when(j == n_prompt_tiles)
    def _():
        ysm_ref[...] = y


def _moe_combine(x2_p, x2_s, dest, ys, g_final, clist):
    win, lo, hi = clist
    mp, d = x2_p.shape
    n_prompt_tiles = mp // ROUTE_TILE

    def ptile(j, *_):
        return (jnp.minimum(j, n_prompt_tiles - 1), 0)

    def window(e):
        return pl.BlockSpec((pl.Element(COMBINE_WIN), pl.Element(d)),
                            lambda j, win, lo, hi: (win[j * N_EXPERTS + e] * COMBINE_ALIGN, 0))

    grid_spec = pltpu.PrefetchScalarGridSpec(
        num_scalar_prefetch=3,
        grid=(n_prompt_tiles + 1,),
        in_specs=[
            pl.BlockSpec((ROUTE_TILE, d), ptile),
            pl.BlockSpec((ROUTE_TILE, d), lambda j, *_: (0, 0)),
            pl.BlockSpec((ROUTE_TILE, 2), lambda j, *_: (j, 0)),
            pl.BlockSpec((1, d), lambda j, *_: (0, 0)),
        ] + [window(e) for e in range(N_EXPERTS)],
        out_specs=[pl.BlockSpec((ROUTE_TILE, d), ptile), pl.BlockSpec((ROUTE_TILE, d), lambda j, *_: (0, 0))],
        scratch_shapes=[pltpu.VMEM((ROUTE_TILE, d), F32)],
    )
    return pl.pallas_call(
        _moe_combine_kernel,
        grid_spec=grid_spec,
        out_shape=[jax.ShapeDtypeStruct((mp, d), F32), jax.ShapeDtypeStruct((ROUTE_TILE, d), F32)],
        compiler_params=_cparams("arbitrary"),
        name="moe_combine",
    )(win, lo, hi, x2_p, x2_s, dest, g_final.reshape(1, d), *([ys] * N_EXPERTS))


def _moe_block(x2_p, hn_p, ids_p, gates_p, rank_p, x2_s, hn_s, ids_s, gates_s, rank_s, cnt_tile, w1, w3, w2,
               g_final, tf=512):
    ids = jnp.concatenate([ids_p, ids_s])
    gates = jnp.concatenate([gates_p, gates_s])
    rank = jnp.concatenate([rank_p, rank_s])
    dest, tile_expert, tile_chunks, glist, clist, n_ffn_tiles = _moe_plan(ids, rank, cnt_tile)
    n_sub = n_ffn_tiles * (FFN_TILE // ROUTE_TILE)
    spare = ((0, 0), (0, GATHER_TOKENS - ROUTE_TILE))
    xs, gs = _moe_gather(hn_p, hn_s, jnp.pad(dest.T, spare, constant_values=-1), jnp.pad(gates.T, spare), glist,
                         n_sub)
    ys = _moe_ffn(xs, gs, w1, w3, w2, tile_expert, tile_chunks, n_ffn_tiles, tf)
    return _moe_combine(x2_p, x2_s, dest, ys, g_final, clist)


def _rms(x, g):
    return x * lax.rsqrt(jnp.mean(x * x, axis=-1, keepdims=True) + EPS) * g


def _mm(a, b, precise):
    if precise:
        return jnp.dot(a.astype(F32), b.astype(F32), preferred_element_type=F32, precision=HIGHEST)
    return jnp.dot(a.astype(BF16), b.astype(BF16), preferred_element_type=F32)


def _silu(x):
    return x * jax.nn.sigmoid(x)


def _norm_proj_kernel(x_ref, g_ref, w_ref, wt_ref, *out_refs, segs, tsegs, precise):
    xn = _rms(x_ref[...], g_ref[...])
    xn = xn if precise else xn.astype(BF16)
    for (start, width), o_ref in zip(segs, out_refs[:len(segs)]):
        for c in range(0, width, 512):
            cw = min(512, width - c)
            o_ref[:, c:c + cw] = _mm(xn, w_ref[:, start + c:start + c + cw], precise)
    for (start, height), o_ref in zip(tsegs, out_refs[len(segs):]):
        res = lax.dot_general(wt_ref[start:start + height, :].astype(xn.dtype), xn, (((1,), (1,)), ((), ())),
                              preferred_element_type=F32, precision=HIGHEST if precise else None)
        o_ref[...] = res.reshape(o_ref.shape)


def _norm_proj(x, g, w, wt, segs, tsegs, tm, rows_per_seq=None, precise=False):
    m, d = x.shape
    n = w.shape[1]
    nt = wt.shape[0]
    out_shape = [jax.ShapeDtypeStruct((m, width), F32) for _, width in segs]
    out_specs = [pl.BlockSpec((tm, width), lambda i: (i, 0)) for _, width in segs]
    for _, height, per_seq in tsegs:
        if per_seq:
            tps = rows_per_seq // tm
            out_shape.append(jax.ShapeDtypeStruct((m // rows_per_seq, height, rows_per_seq), F32))
            out_specs.append(pl.BlockSpec((1, height, tm), lambda i, tps=tps: (i // tps, 0, i % tps)))
        else:
            out_shape.append(jax.ShapeDtypeStruct((height, m), F32))
            out_specs.append(pl.BlockSpec((height, tm), lambda i: (0, i)))
    return pl.pallas_call(
        functools.partial(_norm_proj_kernel, segs=tuple(segs), tsegs=tuple(t[:2] for t in tsegs), precise=precise),
        grid=(m // tm,),
        in_specs=[
            pl.BlockSpec((tm, d), lambda i: (i, 0)),
            pl.BlockSpec((1, d), lambda i: (0, 0)),
            pl.BlockSpec((d, n), lambda i: (0, 0)),
            pl.BlockSpec((nt, d), lambda i: (0, 0)),
        ],
        out_specs=out_specs,
        out_shape=out_shape,
        compiler_params=_cparams("arbitrary"),
        name="norm_proj",
    )(x, g.reshape(1, d), w, wt)


def _mix_ffn_kernel(a_ref, b_ref, x_ref, wo_ref, g_ref, w1_ref, w3_ref, w2_ref, o_ref, hn_ref, *, precise):
    wa = a_ref.shape[1]

    @pl.when(pl.program_id(1) == 0)
    def _():
        x1 = (x_ref[...] + _mm(a_ref[...], wo_ref[0:wa, :], precise)
              + _mm(b_ref[...], wo_ref[wa:, :], precise))
        o_ref[...] = x1
        hn_ref[...] = _rms(x1, g_ref[...]).astype(hn_ref.dtype)

    wdt = F32 if precise else BF16
    w1, w3, w2 = w1_ref[...].astype(wdt), w3_ref[...].astype(wdt), w2_ref[...].astype(wdt)
    hn = hn_ref[...]
    o_ref[...] += _mm(_silu(_mm(hn, w1, precise)) * _mm(hn, w3, precise), w2, precise)


def _mix_ffn(a, b, x, w_out, g, w1, w3, w2, tm, tf, precise=False):
    m, d = x.shape
    wa, wb = a.shape[1], b.shape[1]
    ff = w1.shape[1]
    return pl.pallas_call(
        functools.partial(_mix_ffn_kernel, precise=precise),
        grid=(m // tm, ff // tf),
        in_specs=[
            pl.BlockSpec((tm, wa), lambda i, f: (i, 0)),
            pl.BlockSpec((tm, wb), lambda i, f: (i, 0)),
            pl.BlockSpec((tm, d), lambda i, f: (i, 0)),
            pl.BlockSpec((wa + wb, d), lambda i, f: (0, 0)),
            pl.BlockSpec((1, d), lambda i, f: (0, 0)),
            pl.BlockSpec((d, tf), lambda i, f: (0, f)),
            pl.BlockSpec((d, tf), lambda i, f: (0, f)),
            pl.BlockSpec((tf, d), lambda i, f: (f, 0)),
        ],
        out_specs=pl.BlockSpec((tm, d), lambda i, f: (i, 0)),
        out_shape=jax.ShapeDtypeStruct((m, d), F32),
        scratch_shapes=[pltpu.VMEM((tm, d), F32 if precise else BF16)],
        compiler_params=_cparams("arbitrary", "arbitrary"),
        name="mix_ffn",
    )(a, b, x, w_out, g.reshape(1, d), w1, w3, w2)


def _mix_router_kernel(a_ref, b_ref, x_ref, wo_ref, g_ref, wr_ref, br_ref, cnt0_ref, x2_ref, hn_ref, ids_ref,
                       gate_ref, rank_ref, cb_ref, ca_ref, cnt_ref, *, precise):
    @pl.when(pl.program_id(0) == 0)
    def _():
        cnt_ref[...] = cnt0_ref[...]

    wa = a_ref.shape[1]
    x2 = (x_ref[...] + _mm(a_ref[...], wo_ref[0:wa, :], precise)
          + _mm(b_ref[...], wo_ref[wa:, :], precise))
    x2_ref[...] = x2
    hn = _rms(x2, g_ref[...])
    hn_ref[...] = hn.astype(BF16)
    wr = wr_ref[...]
    if precise:
        logits = jnp.dot(hn, wr, preferred_element_type=F32, precision=HIGHEST)
    else:
        hn_hi, wr_hi = hn.astype(BF16), wr.astype(BF16)
        hn_lo, wr_lo = (hn - hn_hi.astype(F32)).astype(BF16), (wr - wr_hi.astype(F32)).astype(BF16)
        logits = (jnp.dot(hn_hi, wr_hi, preferred_element_type=F32)
                  + (jnp.dot(hn_lo, wr_hi, preferred_element_type=F32)
                     + jnp.dot(hn_hi, wr_lo, preferred_element_type=F32)))
    lane = lax.broadcasted_iota(I32, logits.shape, 1)
    real = lane < N_EXPERTS
    biased = jnp.where(real, logits + br_ref[...], -jnp.inf)
    m1 = jnp.max(biased, axis=-1, keepdims=True)
    i1 = jnp.min(jnp.where(biased == m1, lane, 128), axis=-1, keepdims=True)
    rest = jnp.where(lane == i1, -jnp.inf, biased)
    m2 = jnp.max(rest, axis=-1, keepdims=True)
    i2 = jnp.min(jnp.where(rest == m2, lane, 128), axis=-1, keepdims=True)
    l1 = jnp.sum(jnp.where(lane == i1, logits, 0.0), axis=-1, keepdims=True)
    l2 = jnp.sum(jnp.where(lane == i2, logits, 0.0), axis=-1, keepdims=True)
    mx = jnp.maximum(l1, l2)
    e1, e2 = jnp.exp(l1 - mx), jnp.exp(l2 - mx)
    two = lax.broadcasted_iota(I32, ids_ref.shape, 1)
    ids_ref[...] = jnp.where(two == 0, i1, i2)
    gate_ref[...] = jnp.where(two == 0, e1, e2) / (e1 + e2)
    tm = logits.shape[0]
    chosen = jnp.where((lane == i1) | (lane == i2), 1.0, 0.0)
    earlier = (lax.broadcasted_iota(I32, (tm, tm), 0) > lax.broadcasted_iota(I32, (tm, tm), 1))
    before = cnt_ref[...] + jnp.dot(jnp.where(earlier, 1.0, 0.0).astype(BF16), chosen.astype(BF16),
                                    preferred_element_type=F32)
    r1 = jnp.sum(jnp.where(lane == i1, before, 0.0), axis=-1, keepdims=True)
    r2 = jnp.sum(jnp.where(lane == i2, before, 0.0), axis=-1, keepdims=True)
    rank_ref[...] = jnp.where(two == 0, r1, r2).astype(I32)
    for c in range(cb_ref.shape[1]):
        cb_ref[0, c:c + 1, :] = before[c * ROUTE_TILE:c * ROUTE_TILE + 1, :]
    cnt_ref[...] += jnp.sum(chosen, axis=0, keepdims=True)
    ca_ref[...] = cnt_ref[...]


def _mix_router(a, b, x, w_out, g, wr, br, cnt0, tm, precise=False):
    m, d = x.shape
    wa, wb = a.shape[1], b.shape[1]
    n_marks = max(1, tm // ROUTE_TILE)
    return pl.pallas_call(
        functools.partial(_mix_router_kernel, precise=precise),
        grid=(m // tm,),
        in_specs=[
            pl.BlockSpec((tm, wa), lambda i: (i, 0)),
            pl.BlockSpec((tm, wb), lambda i: (i, 0)),
            pl.BlockSpec((tm, d), lambda i: (i, 0)),
            pl.BlockSpec((wa + wb, d), lambda i: (0, 0)),
            pl.BlockSpec((1, d), lambda i: (0, 0)),
            pl.BlockSpec((d, 128), lambda i: (0, 0)),
            pl.BlockSpec((1, 128), lambda i: (0, 0)),
            pl.BlockSpec((1, 128), lambda i: (0, 0)),
        ],
        out_specs=[
            pl.BlockSpec((tm, d), lambda i: (i, 0)),
            pl.BlockSpec((tm, d), lambda i: (i, 0)),
            pl.BlockSpec((tm, 2), lambda i: (i, 0)),
            pl.BlockSpec((tm, 2), lambda i: (i, 0)),
            pl.BlockSpec((tm, 2), lambda i: (i, 0)),
            pl.BlockSpec((1, n_marks, 128), lambda i: (i, 0, 0)),
            pl.BlockSpec((1, 128), lambda i: (0, 0)),
        ],
        out_shape=[jax.ShapeDtypeStruct((m, d), F32), jax.ShapeDtypeStruct((m, d), BF16),
                   jax.ShapeDtypeStruct((m, 2), I32), jax.ShapeDtypeStruct((m, 2), F32),
                   jax.ShapeDtypeStruct((m, 2), I32), jax.ShapeDtypeStruct((m // tm, n_marks, 128), F32),
                   jax.ShapeDtypeStruct((1, 128), F32)],
        scratch_shapes=[pltpu.VMEM((1, 128), F32)],
        compiler_params=_cparams("arbitrary"),
        name="mix_router",
    )(a, b, x, w_out, g.reshape(1, d), wr, br, cnt0)


ATT_T = 512


def _t5_bucket(dist):
    max_exact = N_BUCKETS // 2
    d = jnp.maximum(dist, 1).astype(F32)
    large = max_exact + (jnp.log(d / max_exact) / math.log(A_WIN / max_exact)
                         * (N_BUCKETS - max_exact)).astype(I32)
    return jnp.where(dist < max_exact, dist, jnp.minimum(large, N_BUCKETS - 1))


def _distance_logit_table(rel_bias, max_dist):
    dist = jnp.arange(max_dist + 1, dtype=I32)
    mult = jnp.zeros((max_dist + 1,), F32)
    for window, d in A_BRANCHES:
        mult = mult + ((dist % d == 0) & (dist <= window)).astype(F32)
    bias = rel_bias[_t5_bucket(dist)].astype(F32).T
    return jnp.where(mult[None, :] > 0, bias + jnp.log(jnp.maximum(mult, 1.0))[None, :], NEG)


def _attn_kernel(q_ref, kt_ref, vt_ref, row0_ref, o_ref, kb_ref, vb_ref):
    s_len = q_ref.shape[1]
    nb = s_len // ATT_T
    period = row0_ref.shape[2]
    kb_ref[...] = kt_ref[0].astype(BF16)
    vb_ref[...] = vt_ref[0].astype(BF16)
    scale = A_HD ** -0.5
    for hl in range(2):
        hrows = slice(hl * A_HD, (hl + 1) * A_HD)
        table = pltpu.roll(jnp.broadcast_to(row0_ref[0, hl:hl + 1, :], (ATT_T, period)), 0, 1,
                           stride=1, stride_axis=0)
        for i in range(nb):
            rows = slice(i * ATT_T, (i + 1) * ATT_T)
            width = (i + 1) * ATT_T
            q = (q_ref[0, rows, hrows] * scale).astype(BF16)
            s = jnp.dot(q, kb_ref[hrows, 0:width], preferred_element_type=F32)
            s = s + table[:, (nb - 1 - i) * ATT_T:nb * ATT_T]
            m = jnp.max(s, axis=-1, keepdims=True)
            p = jnp.exp(s - m)
            den = jnp.sum(p, axis=-1, keepdims=True)
            o = lax.dot_general(p.astype(BF16), vb_ref[hrows, 0:width], (((1,), (1,)), ((), ())),
                                preferred_element_type=F32)
            o_ref[0, rows, hrows] = o / den


def _attention_prompt(q, kt, vt, rel_bias):
    b, s_len, _ = q.shape
    nb = s_len // ATT_T
    tab1 = _distance_logit_table(rel_bias, s_len)
    period = s_len + ATT_T
    z = jnp.arange(period, dtype=I32)
    delta = jnp.where(z < s_len, (nb - 1) * ATT_T - z, (nb - 1) * ATT_T + period - z)
    row0 = jnp.where(delta[None] >= 0, tab1[:, jnp.clip(delta, 0, s_len)], NEG)
    return pl.pallas_call(
        _attn_kernel,
        grid=(b, A_HEADS // 2),
        in_specs=[pl.BlockSpec((1, s_len, 2 * A_HD), lambda bi, hp: (bi, 0, hp)),
                  pl.BlockSpec((1, 2 * A_HD, s_len), lambda bi, hp: (bi, hp, 0)),
                  pl.BlockSpec((1, 2 * A_HD, s_len), lambda bi, hp: (bi, hp, 0)),
                  pl.BlockSpec((1, 2, period), lambda bi, hp: (hp, 0, 0))],
        out_specs=pl.BlockSpec((1, s_len, 2 * A_HD), lambda bi, hp: (bi, 0, hp)),
        out_shape=jax.ShapeDtypeStruct(q.shape, F32),
        scratch_shapes=[pltpu.VMEM((2 * A_HD, s_len), BF16), pltpu.VMEM((2 * A_HD, s_len), BF16)],
        compiler_params=_cparams("arbitrary", "arbitrary"),
        name="dilated_attention",
    )(q, kt, vt, row0.reshape(A_HEADS // 2, 2, period))


MLSTM_CHUNK = 256


def _log_sigmoid(x):
    return jnp.minimum(x, 0.0) - jnp.log(1.0 + jnp.exp(-jnp.abs(x)))


def _mlstm_kernel(qk_ref, v_ref, ob_ref, gc_ref, gr_ref, wc_ref, bc_ref, bifc_ref, bifr_ref, gh_ref,
                  o_ref, c_out_ref, n_out_ref, m_out_ref, xbuf_ref, c_ref, n_ref, m_ref):
    ci = pl.program_id(1)
    L = qk_ref.shape[1]

    @pl.when(ci == 0)
    def _():
        xbuf_ref[0:8, :] = jnp.zeros((8, 2 * B_W), F32)
        c_ref[...] = jnp.zeros_like(c_ref)
        n_ref[...] = jnp.zeros_like(n_ref)
        m_ref[...] = jnp.zeros_like(m_ref)

    x = qk_ref[0]
    xbuf_ref[8:8 + L, :] = x
    y = bc_ref[...] + wc_ref[3:4, :] * x
    for j in range(B_CONV - 1):
        y = y + wc_ref[j:j + 1, :] * xbuf_ref[5 + j:5 + j + L, :]
    xbuf_ref[0:8, :] = x[L - 8:L, :]
    y = _silu(y)

    gcol = gc_ref[0] + bifc_ref[...]
    grow = gr_ref[...] + bifr_ref[...]
    ri = lax.broadcasted_iota(I32, (L, L), 0)
    cj = lax.broadcasted_iota(I32, (L, L), 1)
    causal = ri >= cj
    lower = jnp.where(causal, 1.0, 0.0)
    b_col = jnp.dot(lower, _log_sigmoid(gcol), preferred_element_type=F32, precision=HIGHEST)
    b_row = lax.dot_general(_log_sigmoid(grow), lower, (((1,), (1,)), ((), ())),
                            preferred_element_type=F32, precision=HIGHEST)

    for h in range(B_HEADS):
        cols = slice(h * B_HD, (h + 1) * B_HD)
        qf = y[:, cols]
        q = qf.astype(BF16)
        kf = y[:, B_W + h * B_HD:B_W + (h + 1) * B_HD] * (B_HD ** -0.5)
        k = kf.astype(BF16)
        v = v_ref[0, :, cols].astype(BF16)
        bc = b_col[:, B_HEADS + h:B_HEADS + h + 1]
        ic = gcol[:, h:h + 1]
        br = b_row[B_HEADS + h:B_HEADS + h + 1, :]
        ir = grow[h:h + 1, :]
        m_prev = m_ref[h:h + 1, 0:1]
        dm = jnp.where(causal, bc - br + ir, NEG)
        inter = bc + m_prev
        mt = jnp.maximum(inter, jnp.max(dm, axis=1, keepdims=True))
        w_intra = jnp.exp(dm - mt)
        w_inter = jnp.exp(inter - mt)
        a = w_intra * lax.dot_general(q, k, (((1,), (1,)), ((), ())), preferred_element_type=F32)
        c_prev = c_ref[h]
        n_prev = n_ref[h:h + 1, :]
        num = (jnp.dot(a.astype(BF16), v, preferred_element_type=F32)
               + w_inter * jnp.dot(q, c_prev.astype(BF16), preferred_element_type=F32))
        den = (jnp.sum(a, axis=1, keepdims=True)
               + w_inter * jnp.sum(qf * n_prev, axis=1, keepdims=True))
        hb = num / jnp.maximum(jnp.abs(den), jnp.exp(-mt))
        m_new = mt[L - 1:L, :]
        b_last = bc[L - 1:L, :]
        g = jnp.exp(b_last - bc + ic - m_new)
        decay = jnp.exp(b_last + m_prev - m_new)
        kg = kf * g
        c_ref[h] = decay * c_prev + lax.dot_general(kg.astype(BF16), v, (((0,), (0,)), ((), ())),
                                                    preferred_element_type=F32)
        n_ref[h:h + 1, :] = decay * n_prev + jnp.sum(kg, axis=0, keepdims=True)
        m_ref[h:h + 1, :] = jnp.broadcast_to(m_new, (1, 128))
        hb = hb * lax.rsqrt(jnp.mean(hb * hb, axis=-1, keepdims=True) + EPS) * gh_ref[:, cols]
        o_ref[0, :, cols] = jax.nn.sigmoid(ob_ref[0, :, cols]) * hb

    @pl.when(ci == pl.num_programs(1) - 1)
    def _():
        c_out_ref[0] = c_ref[...]
        n_out_ref[0] = n_ref[...]
        m_out_ref[0] = m_ref[...]


def _mlstm_prompt(qk, v, ob, gcol, grow, w_conv, b_conv, b_if, g_head):
    b, s_len, _ = qk.shape
    L = min(MLSTM_CHUNK, s_len)
    nc = s_len // L
    bif_col = jnp.zeros((1, 128), F32).at[0, :2 * B_HEADS].set(b_if)
    bif_row = b_if.reshape(2 * B_HEADS, 1)
    seq = lambda w: pl.BlockSpec((1, L, w), lambda bi, ci: (bi, ci, 0))
    full = lambda shape: pl.BlockSpec(shape, lambda bi, ci: (0,) * len(shape))
    return pl.pallas_call(
        _mlstm_kernel,
        grid=(b, nc),
        in_specs=[seq(2 * B_W), seq(B_W), seq(B_W), seq(128),
                  pl.BlockSpec((2 * B_HEADS, L), lambda bi, ci: (0, bi * nc + ci)),
                  full((B_CONV, 2 * B_W)), full((1, 2 * B_W)), full((1, 128)), full((2 * B_HEADS, 1)),
                  full((1, B_W))],
        out_specs=[seq(B_W),
                   pl.BlockSpec((1, B_HEADS, B_HD, B_HD), lambda bi, ci: (bi, 0, 0, 0)),
                   pl.BlockSpec((1, 8, B_HD), lambda bi, ci: (bi, 0, 0)),
                   pl.BlockSpec((1, 8, 128), lambda bi, ci: (bi, 0, 0))],
        out_shape=[jax.ShapeDtypeStruct((b, s_len, B_W), F32),
                   jax.ShapeDtypeStruct((b, B_HEADS, B_HD, B_HD), F32),
                   jax.ShapeDtypeStruct((b, 8, B_HD), F32),
                   jax.ShapeDtypeStruct((b, 8, 128), F32)],
        scratch_shapes=[pltpu.VMEM((L + 8, 2 * B_W), F32), pltpu.VMEM((B_HEADS, B_HD, B_HD), F32),
                        pltpu.VMEM((8, B_HD), F32), pltpu.VMEM((8, 128), F32)],
        compiler_params=_cparams("arbitrary", "arbitrary"),
        name="mlstm_chunkwise",
    )(qk, v, ob, gcol, grow, w_conv, b_conv.reshape(1, -1), bif_col, bif_row, g_head.reshape(1, -1))


def _odd_in_kernel(x_ref, g_ref, w_ref, gcv_ref, bcv_ref, ws_ref, bs_ref, c_ref, xd_ref):
    tm = x_ref.shape[0]
    xn = _rms(x_ref[...], g_ref[...]).astype(BF16)
    u = jnp.dot(xn, w_ref[:, 0:C_W], preferred_element_type=F32)
    v = jnp.dot(xn, w_ref[:, C_W:2 * C_W], preferred_element_type=F32)
    xd_ref[...] = jnp.dot(xn, w_ref[:, 2 * C_W:], preferred_element_type=F32)
    mu = jnp.mean(v, axis=-1, keepdims=True)
    vc = v - mu
    var = jnp.mean(vc * vc, axis=-1, keepdims=True)
    vl = (vc * lax.rsqrt(var + EPS) * gcv_ref[...] + bcv_ref[...]).astype(BF16)
    ri = lax.broadcasted_iota(I32, (C_CHUNK, C_CHUNK), 0)
    cj = lax.broadcasted_iota(I32, (C_CHUNK, C_CHUNK), 1)
    for g in range(C_GROUPS):
        cols = slice(g * C_GD, (g + 1) * C_GD)
        wg = jnp.where(ri >= cj, ws_ref[g], 0.0).astype(BF16)
        for c in range(tm // C_CHUNK):
            rows = slice(c * C_CHUNK, (c + 1) * C_CHUNK)
            s = jnp.dot(wg, vl[rows, cols], preferred_element_type=F32) + bs_ref[:, g:g + 1]
            c_ref[rows, cols] = u[rows, cols] * s


def _odd_in(x, g, w, g_cv, b_cv, w_s, b_s, tm):
    m, d = x.shape
    n = w.shape[1]
    full = lambda shape: pl.BlockSpec(shape, lambda i: (0,) * len(shape))
    return pl.pallas_call(
        _odd_in_kernel,
        grid=(m // tm,),
        in_specs=[pl.BlockSpec((tm, d), lambda i: (i, 0)), full((1, d)), full((d, n)), full((1, C_W)),
                  full((1, C_W)), full((C_GROUPS, C_CHUNK, C_CHUNK)), full((C_CHUNK, C_GROUPS))],
        out_specs=[pl.BlockSpec((tm, C_W), lambda i: (i, 0)), pl.BlockSpec((tm, D_W), lambda i: (i, 0))],
        out_shape=[jax.ShapeDtypeStruct((m, C_W), F32), jax.ShapeDtypeStruct((m, D_W), F32)],
        compiler_params=_cparams("arbitrary"),
        name="odd_in_proj_gate",
    )(x, g.reshape(1, d), w, g_cv.reshape(1, -1), b_cv.reshape(1, -1), w_s, b_s.T)


S5_TT = 128
S5_PAD = 8
S5_HALF = D_GROUPS * D_STATE // 2


def _s5_discretize(a_re, a_im, log_dt, bm_re, bm_im, cm_re, cm_im):
    a_re, a_im = a_re.astype(F32), a_im.astype(F32)
    dt = jnp.exp(log_dt.astype(F32))[:, None]
    mag = jnp.exp(a_re * dt)
    ab_re = mag * jnp.cos(a_im * dt)
    ab_im = mag * jnp.sin(a_im * dt)
    inv = 1.0 / (a_re * a_re + a_im * a_im)
    f_re = ((ab_re - 1.0) * a_re + ab_im * a_im) * inv
    f_im = (ab_im * a_re - (ab_re - 1.0) * a_im) * inv
    bm_re, bm_im = bm_re.astype(F32), bm_im.astype(F32)
    bb_re = f_re[..., None] * bm_re - f_im[..., None] * bm_im
    bb_im = f_re[..., None] * bm_im + f_im[..., None] * bm_re
    return ab_re, ab_im, bb_re, bb_im


def _s5_matrices(bb_re, bb_im, cm_re, cm_im):
    gh = D_GROUPS // 2
    eye = jnp.eye(gh, dtype=F32)

    def in_half(bb):
        return jnp.einsum('gpc,gh->gchp', bb, eye).reshape(gh * D_GCH, gh * D_STATE)

    def out_half(cm):
        return jnp.einsum('gcp,gh->gphc', cm, eye).reshape(gh * D_STATE, gh * D_GCH)

    bd = jnp.stack([jnp.concatenate([in_half(bb_re[h * gh:(h + 1) * gh]), in_half(bb_im[h * gh:(h + 1) * gh])], axis=1)
                    for h in range(2)])
    cm = jnp.stack([jnp.concatenate([out_half(cm_re[h * gh:(h + 1) * gh].astype(F32)),
                                     -out_half(cm_im[h * gh:(h + 1) * gh].astype(F32))], axis=0)
                    for h in range(2)])
    return bd, cm


def _gelu_tanh(x):
    return 0.5 * x * (1.0 + jnp.tanh(math.sqrt(2.0 / math.pi) * (x + 0.044715 * (x * x * x))))


def _s5_kernel(x_ref, bd_ref, cm_ref, ar_ref, ai_ref, dsk_ref, wg_ref, bg_ref, o_ref, hr_out, hi_out,
               bu_ref, hs_ref, hr_ref, hi_ref):
    i = pl.program_id(0)
    nb, tt, _ = x_ref.shape
    stride = tt + S5_PAD
    half_in = D_W // 2
    nct = S5_HALF // 128

    @pl.when(i == 0)
    def _():
        hr_ref[...] = jnp.zeros_like(hr_ref)
        hi_ref[...] = jnp.zeros_like(hi_ref)

    x = x_ref[...].reshape(nb * tt, D_W)
    ys = []
    for h in range(2):
        lanes = slice(h * S5_HALF, (h + 1) * S5_HALF)
        bu = jnp.dot(x[:, h * half_in:(h + 1) * half_in].astype(BF16), bd_ref[h], preferred_element_type=F32)
        for c in range(2 * nct):
            for b in range(nb):
                bu_ref[c, b * stride:b * stride + tt, :] = bu[b * tt:(b + 1) * tt, c * 128:(c + 1) * 128]
        ar = jnp.broadcast_to(ar_ref[:, lanes], (nb, S5_HALF))
        ai = jnp.broadcast_to(ai_ref[:, lanes], (nb, S5_HALF))

        def step(t, carry):
            hr, hi = carry
            rows = pl.ds(t, nb, stride=stride)
            bur = jnp.concatenate([bu_ref[c, rows, :] for c in range(nct)], axis=1)
            bui = jnp.concatenate([bu_ref[nct + c, rows, :] for c in range(nct)], axis=1)
            nhr = ar * hr - ai * hi + bur
            nhi = ar * hi + ai * hr + bui
            for c in range(nct):
                hs_ref[c, rows, :] = nhr[:, c * 128:(c + 1) * 128]
                hs_ref[nct + c, rows, :] = nhi[:, c * 128:(c + 1) * 128]
            return nhr, nhi

        hr, hi = lax.fori_loop(0, tt, step, (hr_ref[:, lanes], hi_ref[:, lanes]))
        hr_ref[:, lanes] = hr
        hi_ref[:, lanes] = hi
        hs = jnp.concatenate(
            [jnp.concatenate([hs_ref[c, b * stride:b * stride + tt, :] for b in range(nb)], axis=0)
             for c in range(2 * nct)], axis=1)
        ys.append(jnp.dot(hs.astype(BF16), cm_ref[h], preferred_element_type=F32))
    y = jnp.concatenate(ys, axis=-1) + dsk_ref[...] * x
    g = _gelu_tanh(y)
    d_out = g * jax.nn.sigmoid(jnp.dot(g.astype(BF16), wg_ref[...], preferred_element_type=F32) + bg_ref[...])
    o_ref[...] = d_out.reshape(nb, tt, D_W)

    @pl.when(i == pl.num_programs(0) - 1)
    def _():
        hr_out[...] = hr_ref[...]
        hi_out[...] = hi_ref[...]


def _s5_prompt(xd, bd, cm, ab_re, ab_im, d_skip, w_glu, b_glu):
    nb, t_len, _ = xd.shape
    tt = min(S5_TT, t_len)
    n_state = D_GROUPS * D_STATE
    full = lambda shape: pl.BlockSpec(shape, lambda i: (0,) * len(shape))
    return pl.pallas_call(
        _s5_kernel,
        grid=(t_len // tt,),
        in_specs=[pl.BlockSpec((nb, tt, D_W), lambda i: (0, i, 0)), full(bd.shape), full(cm.shape),
                  full((1, n_state)), full((1, n_state)), full((1, D_W)), full((D_W, D_W)), full((1, D_W))],
        out_specs=[pl.BlockSpec((nb, tt, D_W), lambda i: (0, i, 0)), full((nb, n_state)), full((nb, n_state))],
        out_shape=[jax.ShapeDtypeStruct(xd.shape, F32), jax.ShapeDtypeStruct((nb, n_state), F32),
                   jax.ShapeDtypeStruct((nb, n_state), F32)],
        scratch_shapes=[pltpu.VMEM((2 * S5_HALF // 128, nb * (tt + S5_PAD), 128), F32),
                        pltpu.VMEM((2 * S5_HALF // 128, nb * (tt + S5_PAD), 128), F32),
                        pltpu.VMEM((nb, n_state), F32), pltpu.VMEM((nb, n_state), F32)],
        compiler_params=_cparams("arbitrary"),
        name="s5_scan_glu",
    )(xd, bd, cm, ab_re.reshape(1, n_state), ab_im.reshape(1, n_state), d_skip.reshape(1, D_W), w_glu,
      b_glu.reshape(1, D_W))


def _hdot(a, b):
    return jnp.dot(a, b, precision=HIGHEST, preferred_element_type=F32)


def _sample_window_kernel(q_ref, kn_ref, vn_ref, tab_ref, tabn_ref, kt_ref, vt_ref, o_ref, okt_ref, ovt_ref):
    L = kt_ref.shape[3]
    last = lax.broadcasted_iota(I32, (A_HD, L), 1) == L - 1
    for h in range(A_HEADS):
        kh, vh = kt_ref[0, h], vt_ref[0, h]
        kn, vn = kn_ref[0, h], vn_ref[0, h]
        q = q_ref[0, h] * (A_HD ** -0.5)
        s = jnp.sum(kh * q, axis=0, keepdims=True) + tab_ref[h]
        s_new = jnp.sum(kn * q, axis=0, keepdims=True) + tabn_ref[h]
        m = jnp.maximum(jnp.max(s, axis=1, keepdims=True), s_new)
        p = jnp.exp(s - m)
        p_new = jnp.exp(s_new - m)
        den = jnp.sum(p, axis=1, keepdims=True) + p_new
        o_ref[0, h] = (jnp.sum(vh * p, axis=1, keepdims=True) + vn * p_new) / den
        okt_ref[0, h] = jnp.where(last, kn, pltpu.roll(kh, L - 1, 1))
        ovt_ref[0, h] = jnp.where(last, vn, pltpu.roll(vh, L - 1, 1))


def _sample_window_attention(q, k_new, v_new, cache_kt, cache_vt, rel_bias):
    n, _, _, L = cache_kt.shape
    assert L == A_WIN
    tab1 = _distance_logit_table(rel_bias, L)
    tab = tab1[:, :0:-1].reshape(A_HEADS, 1, L)
    tab_new = tab1[:, 0].reshape(A_HEADS, 1, 1)
    col = pl.BlockSpec((1, A_HEADS, A_HD, 1), lambda i: (i, 0, 0, 0))
    win = pl.BlockSpec((1, A_HEADS, A_HD, L), lambda i: (i, 0, 0, 0))
    full = lambda shape: pl.BlockSpec(shape, lambda i: (0,) * len(shape))
    return pl.pallas_call(
        _sample_window_kernel,
        grid=(n,),
        in_specs=[col, col, col, full(tab.shape), full(tab_new.shape), win, win],
        out_specs=[col, win, win],
        out_shape=[jax.ShapeDtypeStruct((n, A_HEADS, A_HD, 1), F32),
                   jax.ShapeDtypeStruct(cache_kt.shape, F32), jax.ShapeDtypeStruct(cache_vt.shape, F32)],
        compiler_params=_cparams("arbitrary"),
        name="sample_window_attention",
    )(q, k_new, v_new, tab, tab_new, cache_kt, cache_vt)


def _sample_even_proj(x, g_mix, w_in):
    n_main = 3 * A_W + 4 * B_W
    segs = ((0, A_W), (A_W, A_W), (2 * A_W, A_W), (3 * A_W, 2 * B_W), (3 * A_W + 2 * B_W, B_W),
            (3 * A_W + 3 * B_W, B_W), (n_main, 2 * B_HEADS))
    return _norm_proj(x, g_mix, w_in, jnp.zeros((8, x.shape[1]), F32), segs, (), tm=x.shape[0], precise=True)


def _sample_even_mixers(proj, cache_kt, cache_vt, c0, n0, m0, conv0, rel_bias, b_if, w_conv, b_conv, g_head):
    qa, ka, va, qkb, vb, ob, gates = proj
    n = qa.shape[0]
    cols = lambda a: a.reshape(n, A_HEADS, A_HD, 1)
    a_out, new_kt, new_vt = _sample_window_attention(cols(qa), cols(ka), cols(va), cache_kt, cache_vt, rel_bias)
    a_out = a_out.reshape(n, A_W)
    xp = jnp.concatenate([conv0, qkb[:, None, :]], axis=1)
    qk = _silu(b_conv + jnp.sum(xp * w_conv[None], axis=1))
    q = qk[:, :B_W].reshape(n, B_HEADS, B_HD)
    k = qk[:, B_W:].reshape(n, B_HEADS, B_HD) * (B_HD ** -0.5)
    v = vb.reshape(n, B_HEADS, B_HD)
    ig = gates[:, :B_HEADS] + b_if[:B_HEADS]
    logf = _log_sigmoid(gates[:, B_HEADS:] + b_if[B_HEADS:])
    inter = logf + m0
    mt = jnp.maximum(inter, ig)
    w_intra = jnp.exp(ig - mt)
    w_inter = jnp.exp(inter - mt)
    a = w_intra * jnp.sum(q * k, axis=-1)
    num = a[..., None] * v + w_inter[..., None] * jnp.einsum('nhk,nhkv->nhv', q, c0, precision=HIGHEST)
    den = a + w_inter * jnp.sum(q * n0, axis=-1)
    hb = num / jnp.maximum(jnp.abs(den), jnp.exp(-mt))[..., None]
    g = jnp.exp(ig - mt)
    decay = jnp.exp(inter - mt)
    c_new = decay[..., None, None] * c0 + (g[..., None] * k)[..., :, None] * v[..., None, :]
    n_new = decay[..., None] * n0 + g[..., None] * k
    hb = hb * lax.rsqrt(jnp.mean(hb * hb, axis=-1, keepdims=True) + EPS) * g_head.reshape(B_HEADS, B_HD)
    b_out = jax.nn.sigmoid(ob) * hb.reshape(n, B_W)
    return a_out, b_out, new_kt, new_vt, c_new, n_new, mt, xp[:, 1:]


def _sample_odd_mixers(x, h_re0, h_im0, g_mix, w_in, g_cv, b_cv, w_s, b_s, ab_re, ab_im, bb_re, bb_im,
                       cm_re, cm_im, d_skip, w_glu, b_glu):
    n = x.shape[0]
    h = _rms(x, g_mix)
    proj = _hdot(h, w_in)
    u, v, xd = proj[:, :C_W], proj[:, C_W:2 * C_W], proj[:, 2 * C_W:]
    mu = jnp.mean(v, axis=-1, keepdims=True)
    vc = v - mu
    v = vc * lax.rsqrt(jnp.mean(vc * vc, axis=-1, keepdims=True) + EPS) * g_cv + b_cv
    s = (w_s[:, 0, 0][None, :, None] * v.reshape(n, C_GROUPS, C_GD) + b_s[:, 0][None, :, None]).reshape(n, C_W)
    c_out = u * s
    xg = xd.reshape(n, D_GROUPS, D_GCH)
    bu_re = jnp.einsum('gpc,ngc->ngp', bb_re, xg, precision=HIGHEST)
    bu_im = jnp.einsum('gpc,ngc->ngp', bb_im, xg, precision=HIGHEST)
    hr = ab_re * h_re0 - ab_im * h_im0 + bu_re
    hi = ab_re * h_im0 + ab_im * h_re0 + bu_im
    y = (jnp.einsum('gcp,ngp->ngc', cm_re, hr, precision=HIGHEST)
         - jnp.einsum('gcp,ngp->ngc', cm_im, hi, precision=HIGHEST) + d_skip * xg)
    g = _gelu_tanh(y.reshape(n, D_W))
    d_out = g * jax.nn.sigmoid(_hdot(g, w_glu) + b_glu)
    return c_out, d_out, v, hr, hi


_EVEN_SEGS = ((0, A_W), (A_W, 2 * B_W), (A_W + 2 * B_W, B_W), (A_W + 3 * B_W, B_W), (A_W + 4 * B_W, 128))


def kernel(x_prompt, x_sample, cache_a_k, cache_a_v, state_b_c, state_b_n, state_b_m, state_b_conv,
           state_d_re, state_d_im, rel_bias, g_mix, g_ffn, g_final, w_in_e, b_if, w_conv_b, b_conv_b,
           g_bhead, w_out_e, w1_e, w3_e, w2_e, w_in_o, g_cv, b_cv, w_s, b_s, a_re, a_im, log_dt,
           bm_re, bm_im, cm_re, cm_im, d_skip, w_glu, b_glu, w_out_o, w_router, b_router,
           w1_m, w3_m, w2_m):
    bp, sp, d = x_prompt.shape
    ns = x_sample.shape[0]
    mp = bp * sp
    xp = x_prompt.reshape(mp, d)
    xs = x_sample.reshape(ns, d)

    w_in = w_in_e[0]
    n_main = 3 * A_W + 4 * B_W
    w_gate = w_in[:, n_main:]
    w_cat = jnp.concatenate([w_in[:, :A_W], w_in[:, 3 * A_W:n_main],
                             jnp.pad(w_gate, ((0, 0), (0, 128 - 2 * B_HEADS)))], axis=1)
    w_t = jnp.concatenate([w_in[:, A_W:3 * A_W], w_gate], axis=1).T
    q, qkb, vb, ob, gcol, kt, vt, grow = _norm_proj(
        xp, g_mix[0], w_cat.astype(BF16), w_t.astype(BF16), _EVEN_SEGS,
        ((0, A_W, True), (A_W, A_W, True), (2 * A_W, 2 * B_HEADS, False)), tm=512, rows_per_seq=sp)
    shp = lambda a: a.reshape(bp, sp, a.shape[-1])
    a_out = _attention_prompt(shp(q), kt, vt, rel_bias)
    b_out, pc, pn, pm = _mlstm_prompt(shp(qkb), shp(vb), shp(ob), shp(gcol), grow, w_conv_b[0], b_conv_b[0],
                                      b_if[0], g_bhead[0])
    x1p = _mix_ffn(a_out.reshape(mp, A_W), b_out.reshape(mp, B_W), xp, w_out_e[0].astype(BF16), g_ffn[0],
                   w1_e[0].astype(BF16), w3_e[0].astype(BF16), w2_e[0].astype(BF16), tm=1024, tf=512)

    pos_minor = lambda c: jnp.transpose(c, (0, 2, 3, 1))
    pos_major = lambda c: jnp.transpose(c, (0, 3, 1, 2))
    s_proj = _sample_even_proj(xs, g_mix[0], w_in)
    sa, sb, s_kt, s_vt, sc, sn, sm, s_conv = _sample_even_mixers(
        s_proj, pos_minor(cache_a_k[0]), pos_minor(cache_a_v[0]), state_b_c[0], state_b_n[0], state_b_m[0],
        state_b_conv[0], rel_bias, b_if[0], w_conv_b[0], b_conv_b[0], g_bhead[0])
    s_k, s_v = pos_major(s_kt), pos_major(s_vt)
    x1s = _mix_ffn(sa, sb, xs, w_out_e[0], g_ffn[0], w1_e[0], w3_e[0], w2_e[0], tm=ns, tf=512, precise=True)

    ab_re, ab_im, bb_re, bb_im = _s5_discretize(a_re[0], a_im[0], log_dt[0], bm_re[0], bm_im[0], cm_re[0],
                                                cm_im[0])
    bd, cmat = _s5_matrices(bb_re, bb_im, cm_re[0], cm_im[0])
    c_out, xd = _odd_in(x1p, g_mix[1], w_in_o[0].astype(BF16), g_cv[0], b_cv[0], w_s[0], b_s[0], tm=512)
    d_out, p_hr, p_hi = _s5_prompt(xd.reshape(bp, sp, D_W), bd.astype(BF16), cmat.astype(BF16), ab_re, ab_im,
                                   d_skip[0], w_glu[0].astype(BF16), b_glu[0])
    wr = jnp.pad(w_router[0].astype(F32), ((0, 0), (0, 128 - N_EXPERTS)))
    br = jnp.pad(b_router[0].astype(F32), (0, 128 - N_EXPERTS)).reshape(1, 128)
    x2p, hnp, idp, gtp, rkp, cbp, cap = _mix_router(c_out, d_out.reshape(mp, D_W), x1p, w_out_o[0].astype(BF16),
                                                    g_ffn[1], wr, br, jnp.zeros((1, 128), F32), tm=512)

    sc_out, sd_out, s_cv, s_hr, s_hi = _sample_odd_mixers(
        x1s, state_d_re[0], state_d_im[0], g_mix[1], w_in_o[0], g_cv[0], b_cv[0], w_s[0], b_s[0], ab_re, ab_im,
        bb_re, bb_im, cm_re[0], cm_im[0], d_skip[0], w_glu[0], b_glu[0])
    x2s, hns, ids_s, gts, rks, cbs, cas = _mix_router(sc_out, sd_out, x1s, w_out_o[0], g_ffn[1], wr, br, cap,
                                                      tm=ns, precise=True)

    pad = ((0, ROUTE_TILE - ns), (0, 0))
    cnt_tile = jnp.concatenate([cbp.reshape(-1, 128), cbs.reshape(-1, 128), cas])[:, :N_EXPERTS].astype(I32)
    y_p, y_s = _moe_block(x2p, hnp, idp, gtp, rkp, jnp.pad(x2s, pad), jnp.pad(hns, pad),
                          jnp.pad(ids_s, pad, constant_values=-1), jnp.pad(gts, pad), jnp.pad(rks, pad),
                          cnt_tile, w1_m[0], w3_m[0], w2_m[0], g_final)
    y_prompt = y_p.reshape(bp, sp, d)
    y_sample = y_s[:ns].reshape(ns, 1, d)

    e = lambda a: a[None]
    p_k = pos_major(kt.reshape(bp, A_HEADS, A_HD, sp))
    p_v = pos_major(vt.reshape(bp, A_HEADS, A_HD, sp))
    p_conv = qkb.reshape(bp, sp, 2 * B_W)[:, sp - (B_CONV - 1):]
    return (y_prompt, y_sample, e(p_k), e(p_v), e(s_k), e(s_v), e(pc), e(sc), e(pn[:, :B_HEADS]), e(sn),
            e(pm[:, :B_HEADS, 0]), e(sm), e(p_conv), e(s_conv), e(s_cv.reshape(ns, 1, C_W)),
            e(p_hr.reshape(bp, D_GROUPS, D_STATE)), e(s_hr), e(p_hi.reshape(bp, D_GROUPS, D_STATE)), e(s_hi))
```

```python
import functools
import math

import jax
import jax.numpy as jnp
from jax import lax
from jax.experimental import pallas as pl
from jax.experimental.pallas import tpu as pltpu

F32 = jnp.float32
BF16 = jnp.bfloat16
I32 = jnp.int32
HIGHEST = lax.Precision.HIGHEST

D_MODEL = 1024
A_HEADS, A_HD = 8, 64
A_W = A_HEADS * A_HD
A_BRANCHES = ((128, 1), (512, 4), (2048, 16))
A_WIN = 2048
N_BUCKETS = 32
B_HEADS, B_HD = 4, 128
B_W = B_HEADS * B_HD
B_CONV = 4
C_GROUPS, C_GD = 4, 128
C_W = C_GROUPS * C_GD
C_CHUNK = 128
D_GROUPS, D_GCH, D_STATE = 32, 16, 64
D_W = D_GROUPS * D_GCH
D_FF = 3584
N_EXPERTS = 8
TOP_K = 2
EPS = 1e-6
NEG = -1e30

VMEM_LIMIT = 56 * 1024 * 1024

ROUTE_TILE = 256
FFN_TILE = 1024
FFN_ROW_CHUNK = 256
GATHER_TOKENS = 2 * ROUTE_TILE
COMBINE_ALIGN = 16
COMBINE_WIN = ROUTE_TILE + COMBINE_ALIGN


def _cparams(*sem):
    return pltpu.CompilerParams(dimension_semantics=sem, vmem_limit_bytes=VMEM_LIMIT)


def _moe_plan(ids, rank, cnt_tile):
    mt = ids.shape[0]
    n_assign = 2 * mt
    n_ffn_tiles = -(-(n_assign + N_EXPERTS * (FFN_TILE - 1)) // FFN_TILE) + 1
    n_sub = n_ffn_tiles * (FFN_TILE // ROUTE_TILE)
    counts = cnt_tile[-1]
    seg = ((counts + FFN_TILE - 1) // FFN_TILE) * FFN_TILE
    seg_end = jnp.cumsum(seg)
    seg_off = seg_end - seg
    off_of = jnp.sum(jnp.where(ids[..., None] == jnp.arange(N_EXPERTS, dtype=I32), seg_off, 0), axis=-1)
    dest = jnp.where(ids >= 0, off_of + rank, -1)
    total = seg_end[-1]
    tile_start = jnp.arange(n_ffn_tiles, dtype=I32) * FFN_TILE
    tile_expert = jnp.minimum(jnp.sum((tile_start[:, None] >= seg_end[None, :]).astype(I32), axis=1),
                              N_EXPERTS - 1).astype(I32)
    rows_left = (seg_off + counts)[tile_expert] - tile_start
    tile_chunks = jnp.where(tile_start < total,
                            jnp.clip(-(-rows_left // FFN_ROW_CHUNK), 0, FFN_TILE // FFN_ROW_CHUNK), 0).astype(I32)
    lo = seg_off[None, :] + cnt_tile[:-1]
    hi = seg_off[None, :] + cnt_tile[1:]
    sub_start = jnp.arange(n_sub, dtype=I32) * ROUTE_TILE
    sub_expert = jnp.minimum(jnp.sum((sub_start[:, None] >= seg_end[None, :]).astype(I32), axis=1),
                             N_EXPERTS - 1)
    lo_e, hi_e = lo.T[sub_expert], hi.T[sub_expert]
    jlo = jnp.sum((hi_e <= sub_start[:, None]).astype(I32), axis=1).astype(I32)
    jhi = (jnp.sum((lo_e < sub_start[:, None] + ROUTE_TILE).astype(I32), axis=1) - 1).astype(I32)
    win = lo // COMBINE_ALIGN
    flat1 = lambda a: a.reshape(-1).astype(I32)
    return dest, tile_expert, tile_chunks, (jlo, jhi), (flat1(win), flat1(lo), flat1(hi)), n_ffn_tiles


def _moe_gather_kernel(jlo_ref, jhi_ref, hp_hbm, hs_hbm, dt_ref, gt_ref, xs_ref, gs_ref,
                       hn_ref, acc_ref, g_ref, sem):
    s = pl.program_id(0)
    mp = hp_hbm.shape[0]

    @pl.when(s == 0)
    def _():
        copies = (pltpu.make_async_copy(hp_hbm, hn_ref.at[0:mp], sem.at[0]),
                  pltpu.make_async_copy(hs_hbm, hn_ref.at[mp:mp + ROUTE_TILE], sem.at[1]))
        for c in copies:
            c.start()
        hn_ref[mp + ROUTE_TILE:, :] = jnp.zeros((ROUTE_TILE, hn_ref.shape[1]), BF16)
        for c in copies:
            c.wait()

    acc_ref[...] = jnp.zeros_like(acc_ref)
    g_ref[...] = jnp.zeros_like(g_ref)
    rows = lax.broadcasted_iota(I32, (ROUTE_TILE, GATHER_TOKENS), 0) + s * ROUTE_TILE
    jlo = jlo_ref[s]

    def body(it, carry):
        off = pl.multiple_of(jlo * ROUTE_TILE + it * GATHER_TOKENS, ROUTE_TILE)
        dt = dt_ref[:, pl.ds(off, GATHER_TOKENS)]
        gt = gt_ref[:, pl.ds(off, GATHER_TOKENS)]
        m1 = dt[0:1, :] == rows
        m2 = dt[1:2, :] == rows
        onehot = jnp.where(m1 | m2, 1.0, 0.0).astype(BF16)
        acc_ref[...] += jnp.dot(onehot, hn_ref[pl.ds(off, GATHER_TOKENS), :], preferred_element_type=F32)
        g_ref[...] += jnp.sum(jnp.where(m1, gt[0:1, :], 0.0) + jnp.where(m2, gt[1:2, :], 0.0),
                              axis=1, keepdims=True)
        return carry

    tiles_per_iter = GATHER_TOKENS // ROUTE_TILE
    lax.fori_loop(0, (jhi_ref[s] - jlo + tiles_per_iter) // tiles_per_iter, body, 0)
    xs_ref[...] = acc_ref[...].astype(BF16)
    gs_ref[...] = g_ref[...]


def _moe_gather(hn_p, hn_s, dest_t, gates_t, glist, n_sub):
    jlo, jhi = glist
    mp, d = hn_p.shape
    mt = dest_t.shape[1]
    grid_spec = pltpu.PrefetchScalarGridSpec(
        num_scalar_prefetch=2,
        grid=(n_sub,),
        in_specs=[
            pl.BlockSpec(memory_space=pl.ANY),
            pl.BlockSpec(memory_space=pl.ANY),
            pl.BlockSpec((2, mt), lambda s, *_: (0, 0)),
            pl.BlockSpec((2, mt), lambda s, *_: (0, 0)),
        ],
        out_specs=[
            pl.BlockSpec((ROUTE_TILE, d), lambda s, *_: (s, 0)),
            pl.BlockSpec((ROUTE_TILE, 1), lambda s, *_: (s, 0)),
        ],
        scratch_shapes=[pltpu.VMEM((mp + 2 * ROUTE_TILE, d), BF16), pltpu.VMEM((ROUTE_TILE, d), F32),
                        pltpu.VMEM((ROUTE_TILE, 1), F32), pltpu.SemaphoreType.DMA((2,))],
    )
    return pl.pallas_call(
        _moe_gather_kernel,
        grid_spec=grid_spec,
        out_shape=[jax.ShapeDtypeStruct((n_sub * ROUTE_TILE, d), BF16),
                   jax.ShapeDtypeStruct((n_sub * ROUTE_TILE, 1), F32)],
        compiler_params=_cparams("arbitrary"),
        name="moe_gather",
    )(jlo, jhi, hn_p, hn_s, dest_t, gates_t)


def _moe_ffn_kernel(te_ref, nc_ref, x_ref, gs_ref, w1_ref, w3_ref, w2_ref, y_ref, acc_ref):
    t, f = pl.program_id(0), pl.program_id(1)
    nf = pl.num_programs(1)
    n_chunks = nc_ref[t]
    full = FFN_TILE // FFN_ROW_CHUNK

    @pl.when(n_chunks > 0)
    def _():
        @pl.when(f == 0)
        def _():
            acc_ref[...] = jnp.zeros_like(acc_ref)

        w1, w3, w2 = w1_ref[0].astype(BF16), w3_ref[0].astype(BF16), w2_ref[0].astype(BF16)

        def ffn(rows):
            x = x_ref[rows, :]
            a = jnp.dot(x, w1, preferred_element_type=F32)
            b = jnp.dot(x, w3, preferred_element_type=F32)
            acc_ref[rows, :] += jnp.dot((_silu(a) * b).astype(BF16), w2, preferred_element_type=F32)

        @pl.when(n_chunks == full)
        def _():
            ffn(slice(0, FFN_TILE))

        for c in range(full - 1):
            @pl.when((n_chunks < full) & (c < n_chunks))
            def _():
                ffn(slice(c * FFN_ROW_CHUNK, (c + 1) * FFN_ROW_CHUNK))

        @pl.when(f == nf - 1)
        def _():
            y_ref[...] = (acc_ref[...] * gs_ref[...]).astype(BF16)

    @pl.when((n_chunks == 0) & (f == nf - 1))
    def _():
        y_ref[...] = jnp.zeros_like(y_ref)


def _moe_ffn(xs, gs, w1, w3, w2, tile_expert, tile_chunks, n_ffn_tiles, tf):
    d = xs.shape[1]
    ff = w1.shape[2]
    nf = ff // tf

    def fidx(t, f, nc):
        return jnp.where(nc[t] > 0, f, nf - 1)

    grid_spec = pltpu.PrefetchScalarGridSpec(
        num_scalar_prefetch=2,
        grid=(n_ffn_tiles, nf),
        in_specs=[
            pl.BlockSpec((FFN_TILE, d), lambda t, f, te, nu: (t, 0)),
            pl.BlockSpec((FFN_TILE, 1), lambda t, f, te, nu: (t, 0)),
            pl.BlockSpec((1, d, tf), lambda t, f, te, nu: (te[t], 0, fidx(t, f, nu))),
            pl.BlockSpec((1, d, tf), lambda t, f, te, nu: (te[t], 0, fidx(t, f, nu))),
            pl.BlockSpec((1, tf, d), lambda t, f, te, nu: (te[t], fidx(t, f, nu), 0)),
        ],
        out_specs=pl.BlockSpec((FFN_TILE, d), lambda t, f, te, nu: (t, 0)),
        scratch_shapes=[pltpu.VMEM((FFN_TILE, d), F32)],
    )
    return pl.pallas_call(
        _moe_ffn_kernel,
        grid_spec=grid_spec,
        out_shape=jax.ShapeDtypeStruct(xs.shape, BF16),
        compiler_params=_cparams("arbitrary", "arbitrary"),
        name="moe_ffn",
    )(tile_expert, tile_chunks, xs, gs, w1, w3, w2)


def _moe_combine_kernel(win_ref, lo_ref, hi_ref, xp_ref, xs_ref, d_ref, g_ref, *rest):
    ys_refs = rest[:N_EXPERTS]
    yp_ref, ysm_ref, acc_ref = rest[N_EXPERTS:]
    j = pl.program_id(0)
    n_prompt_tiles = pl.num_programs(0) - 1

    @pl.when(j < n_prompt_tiles)
    def _():
        acc_ref[...] = xp_ref[...]

    @pl.when(j == n_prompt_tiles)
    def _():
        acc_ref[...] = xs_ref[...]

    for e in range(N_EXPERTS):
        lo, hi = lo_ref[j * N_EXPERTS + e], hi_ref[j * N_EXPERTS + e]

        @pl.when(hi > lo)
        def _():
            d = d_ref[...]
            d = jnp.where((d >= lo) & (d < hi), d, -1)
            cols = (lax.broadcasted_iota(I32, (ROUTE_TILE, COMBINE_WIN), 1)
                    + win_ref[j * N_EXPERTS + e] * COMBINE_ALIGN)
            onehot = jnp.where((d[:, 0:1] == cols) | (d[:, 1:2] == cols), 1.0, 0.0).astype(BF16)
            acc_ref[...] += jnp.dot(onehot, ys_refs[e][...], preferred_element_type=F32)

    y = _rms(acc_ref[...], g_ref[...])

    @pl.when(j < n_prompt_tiles)
    def _():
        yp_ref[...] = y

    @pl.when(j == n_prompt_tiles)
    def _():
        ysm_ref[...] = y


def _moe_combine(x2_p, x2_s, dest, ys, g_final, clist):
    win, lo, hi = clist
    mp, d = x2_p.shape
    n_prompt_tiles = mp // ROUTE_TILE

    def ptile(j, *_):
        return (jnp.minimum(j, n_prompt_tiles - 1), 0)

    def window(e):
        return pl.BlockSpec((pl.Element(COMBINE_WIN), pl.Element(d)),
                            lambda j, win, lo, hi: (win[j * N_EXPERTS + e] * COMBINE_ALIGN, 0))

    grid_spec = pltpu.PrefetchScalarGridSpec(
        num_scalar_prefetch=3,
        grid=(n_prompt_tiles + 1,),
        in_specs=[
            pl.BlockSpec((ROUTE_TILE, d), ptile),
            pl.BlockSpec((ROUTE_TILE, d), lambda j, *_: (0, 0)),
            pl.BlockSpec((ROUTE_TILE, 2), lambda j, *_: (j, 0)),
            pl.BlockSpec((1, d), lambda j, *_: (0, 0)),
        ] + [window(e) for e in range(N_EXPERTS)],
        out_specs=[pl.BlockSpec((ROUTE_TILE, d), ptile), pl.BlockSpec((ROUTE_TILE, d), lambda j, *_: (0, 0))],
        scratch_shapes=[pltpu.VMEM((ROUTE_TILE, d), F32)],
    )
    return pl.pallas_call(
        _moe_combine_kernel,
        grid_spec=grid_spec,
        out_shape=[jax.ShapeDtypeStruct((mp, d), F32), jax.ShapeDtypeStruct((ROUTE_TILE, d), F32)],
        compiler_params=_cparams("arbitrary"),
        name="moe_combine",
    )(win, lo, hi, x2_p, x2_s, dest, g_final.reshape(1, d), *([ys] * N_EXPERTS))


def _moe_block(x2_p, hn_p, ids_p, gates_p, rank_p, x2_s, hn_s, ids_s, gates_s, rank_s, cnt_tile, w1, w3, w2,
               g_final, tf=512):
    ids = jnp.concatenate([ids_p, ids_s])
    gates = jnp.concatenate([gates_p, gates_s])
    rank = jnp.concatenate([rank_p, rank_s])
    dest, tile_expert, tile_chunks, glist, clist, n_ffn_tiles = _moe_plan(ids, rank, cnt_tile)
    n_sub = n_ffn_tiles * (FFN_TILE // ROUTE_TILE)
    spare = ((0, 0), (0, GATHER_TOKENS - ROUTE_TILE))
    xs, gs = _moe_gather(hn_p, hn_s, jnp.pad(dest.T, spare, constant_values=-1), jnp.pad(gates.T, spare), glist,
                         n_sub)
    ys = _moe_ffn(xs, gs, w1, w3, w2, tile_expert, tile_chunks, n_ffn_tiles, tf)
    return _moe_combine(x2_p, x2_s, dest, ys, g_final, clist)


def _rms(x, g):
    return x * lax.rsqrt(jnp.mean(x * x, axis=-1, keepdims=True) + EPS) * g


def _mm(a, b, precise):
    if precise:
        return jnp.dot(a.astype(F32), b.astype(F32), preferred_element_type=F32, precision=HIGHEST)
    return jnp.dot(a.astype(BF16), b.astype(BF16), preferred_element_type=F32)


def _silu(x):
    return x * jax.nn.sigmoid(x)


def _norm_proj_kernel(x_ref, g_ref, w_ref, wt_ref, *out_refs, segs, tsegs, precise):
    xn = _rms(x_ref[...], g_ref[...])
    xn = xn if precise else xn.astype(BF16)
    for (start, width), o_ref in zip(segs, out_refs[:len(segs)]):
        for c in range(0, width, 512):
            cw = min(512, width - c)
            o_ref[:, c:c + cw] = _mm(xn, w_ref[:, start + c:start + c + cw], precise)
    for (start, height), o_ref in zip(tsegs, out_refs[len(segs):]):
        res = lax.dot_general(wt_ref[start:start + height, :].astype(xn.dtype), xn, (((1,), (1,)), ((), ())),
                              preferred_element_type=F32, precision=HIGHEST if precise else None)
        o_ref[...] = res.reshape(o_ref.shape)


def _norm_proj(x, g, w, wt, segs, tsegs, tm, rows_per_seq=None, precise=False):
    m, d = x.shape
    n = w.shape[1]
    nt = wt.shape[0]
    out_shape = [jax.ShapeDtypeStruct((m, width), F32) for _, width in segs]
    out_specs = [pl.BlockSpec((tm, width), lambda i: (i, 0)) for _, width in segs]
    for _, height, per_seq in tsegs:
        if per_seq:
            tps = rows_per_seq // tm
            out_shape.append(jax.ShapeDtypeStruct((m // rows_per_seq, height, rows_per_seq), F32))
            out_specs.append(pl.BlockSpec((1, height, tm), lambda i, tps=tps: (i // tps, 0, i % tps)))
        else:
            out_shape.append(jax.ShapeDtypeStruct((height, m), F32))
            out_specs.append(pl.BlockSpec((height, tm), lambda i: (0, i)))
    return pl.pallas_call(
        functools.partial(_norm_proj_kernel, segs=tuple(segs), tsegs=tuple(t[:2] for t in tsegs), precise=precise),
        grid=(m // tm,),
        in_specs=[
            pl.BlockSpec((tm, d), lambda i: (i, 0)),
            pl.BlockSpec((1, d), lambda i: (0, 0)),
            pl.BlockSpec((d, n), lambda i: (0, 0)),
            pl.BlockSpec((nt, d), lambda i: (0, 0)),
        ],
        out_specs=out_specs,
        out_shape=out_shape,
        compiler_params=_cparams("arbitrary"),
        name="norm_proj",
    )(x, g.reshape(1, d), w, wt)


def _mix_ffn_prologue(a_ref, b_ref, x_ref, wo_ref, g_ref, o_ref, hn_ref, precise):
    wa = a_ref.shape[1]

    @pl.when(pl.program_id(1) == 0)
    def _():
        x1 = (x_ref[...] + _mm(a_ref[...], wo_ref[0:wa, :], precise)
              + _mm(b_ref[...], wo_ref[wa:, :], precise))
        o_ref[...] = x1
        hn_ref[...] = _rms(x1, g_ref[...]).astype(hn_ref.dtype)


def _mix_ffn_step(w1_ref, w3_ref, w2_ref, o_ref, hn_ref, precise):
    wdt = F32 if precise else BF16
    w1, w3, w2 = w1_ref[...].astype(wdt), w3_ref[...].astype(wdt), w2_ref[...].astype(wdt)
    hn = hn_ref[...]
    o_ref[...] += _mm(_silu(_mm(hn, w1, precise)) * _mm(hn, w3, precise), w2, precise)


def _mix_ffn_kernel(a_ref, b_ref, x_ref, wo_ref, g_ref, w1_ref, w3_ref, w2_ref, o_ref, hn_ref, *, precise):
    _mix_ffn_prologue(a_ref, b_ref, x_ref, wo_ref, g_ref, o_ref, hn_ref, precise)
    _mix_ffn_step(w1_ref, w3_ref, w2_ref, o_ref, hn_ref, precise)


WINDOW_HEAD_BLOCKS = 2
WINDOW_STEPS = 4
N_MIX_FFN_IN, N_WINDOW_IN, N_WINDOW_OUT = 8, 7, 3


def _mix_ffn_window_kernel(*refs, precise):
    (a_ref, b_ref, x_ref, wo_ref, g_ref, w1_ref, w3_ref, w2_ref), refs = refs[:N_MIX_FFN_IN], refs[N_MIX_FFN_IN:]
    w_ins, refs = refs[:N_WINDOW_IN], refs[N_WINDOW_IN:]
    o_ref, w_outs, hn_ref = refs[0], refs[1:1 + N_WINDOW_OUT], refs[1 + N_WINDOW_OUT]
    f = pl.program_id(1)
    _mix_ffn_prologue(a_ref, b_ref, x_ref, wo_ref, g_ref, o_ref, hn_ref, precise)

    @pl.when(f < WINDOW_STEPS)
    def _():
        _sample_window_kernel(*w_ins, *w_outs, head0=(f % WINDOW_HEAD_BLOCKS) * (A_HEADS // WINDOW_HEAD_BLOCKS))
        _mix_ffn_step(w1_ref, w3_ref, w2_ref, o_ref, hn_ref, precise)

    @pl.when(f >= WINDOW_STEPS)
    def _():
        _mix_ffn_step(w1_ref, w3_ref, w2_ref, o_ref, hn_ref, precise)


def _mix_ffn(a, b, x, w_out, g, w1, w3, w2, tm, tf, precise=False, window=None, rel_bias=None):
    m, d = x.shape
    wa, wb = a.shape[1], b.shape[1]
    ff = w1.shape[1]
    in_specs = [
        pl.BlockSpec((tm, wa), lambda i, f: (i, 0)),
        pl.BlockSpec((tm, wb), lambda i, f: (i, 0)),
        pl.BlockSpec((tm, d), lambda i, f: (i, 0)),
        pl.BlockSpec((wa + wb, d), lambda i, f: (0, 0)),
        pl.BlockSpec((1, d), lambda i, f: (0, 0)),
        pl.BlockSpec((d, tf), lambda i, f: (0, f)),
        pl.BlockSpec((d, tf), lambda i, f: (0, f)),
        pl.BlockSpec((tf, d), lambda i, f: (f, 0)),
    ]
    out_specs = [pl.BlockSpec((tm, d), lambda i, f: (i, 0))]
    out_shape = [jax.ShapeDtypeStruct((m, d), F32)]
    operands = [a, b, x, w_out, g.reshape(1, d), w1, w3, w2]
    body = _mix_ffn_kernel
    if window is not None:
        assert ff // tf >= WINDOW_STEPS and window[0].shape[0] * WINDOW_HEAD_BLOCKS == (m // tm) * WINDOW_STEPS

        def block_of_step(i, f):
            p = i * WINDOW_STEPS + jnp.minimum(f, WINDOW_STEPS - 1)
            return p // WINDOW_HEAD_BLOCKS, p % WINDOW_HEAD_BLOCKS

        w_operands, w_in, w_out_specs, w_shape = _sample_window_specs(*window, rel_bias, block_of_step,
                                                                      A_HEADS // WINDOW_HEAD_BLOCKS)
        in_specs, out_specs, out_shape = in_specs + w_in, out_specs + w_out_specs, out_shape + w_shape
        operands, body = operands + w_operands, _mix_ffn_window_kernel
    res = pl.pallas_call(
        functools.partial(body, precise=precise),
        grid=(m // tm, ff // tf),
        in_specs=in_specs,
        out_specs=out_specs,
        out_shape=out_shape,
        scratch_shapes=[pltpu.VMEM((tm, d), F32 if precise else BF16)],
        compiler_params=_cparams("arbitrary", "arbitrary"),
        name="mix_ffn",
    )(*operands)
    return res[0] if window is None else res


def _mix_router_kernel(a_ref, b_ref, x_ref, wo_ref, g_ref, wr_ref, br_ref, cnt0_ref, x2_ref, hn_ref, ids_ref,
                       gate_ref, rank_ref, cb_ref, ca_ref, cnt_ref, *, precise):
    @pl.when(pl.program_id(0) == 0)
    def _():
        cnt_ref[...] = cnt0_ref[...]

    wa = a_ref.shape[1]
    x2 = (x_ref[...] + _mm(a_ref[...], wo_ref[0:wa, :], precise)
          + _mm(b_ref[...], wo_ref[wa:, :], precise))
    x2_ref[...] = x2
    hn = _rms(x2, g_ref[...])
    hn_ref[...] = hn.astype(BF16)
    wr = wr_ref[...]
    if precise:
        logits = jnp.dot(hn, wr, preferred_element_type=F32, precision=HIGHEST)
    else:
        hn_hi, wr_hi = hn.astype(BF16), wr.astype(BF16)
        hn_lo, wr_lo = (hn - hn_hi.astype(F32)).astype(BF16), (wr - wr_hi.astype(F32)).astype(BF16)
        logits = (jnp.dot(hn_hi, wr_hi, preferred_element_type=F32)
                  + (jnp.dot(hn_lo, wr_hi, preferred_element_type=F32)
                     + jnp.dot(hn_hi, wr_lo, preferred_element_type=F32)))
    lane = lax.broadcasted_iota(I32, logits.shape, 1)
    real = lane < N_EXPERTS
    biased = jnp.where(real, logits + br_ref[...], -jnp.inf)
    m1 = jnp.max(biased, axis=-1, keepdims=True)
    i1 = jnp.min(jnp.where(biased == m1, lane, 128), axis=-1, keepdims=True)
    rest = jnp.where(lane == i1, -jnp.inf, biased)
    m2 = jnp.max(rest, axis=-1, keepdims=True)
    i2 = jnp.min(jnp.where(rest == m2, lane, 128), axis=-1, keepdims=True)
    l1 = jnp.sum(jnp.where(lane == i1, logits, 0.0), axis=-1, keepdims=True)
    l2 = jnp.sum(jnp.where(lane == i2, logits, 0.0), axis=-1, keepdims=True)
    mx = jnp.maximum(l1, l2)
    e1, e2 = jnp.exp(l1 - mx), jnp.exp(l2 - mx)
    two = lax.broadcasted_iota(I32, ids_ref.shape, 1)
    ids_ref[...] = jnp.where(two == 0, i1, i2)
    gate_ref[...] = jnp.where(two == 0, e1, e2) / (e1 + e2)
    tm = logits.shape[0]
    chosen = jnp.where((lane == i1) | (lane == i2), 1.0, 0.0)
    earlier = (lax.broadcasted_iota(I32, (tm, tm), 0) > lax.broadcasted_iota(I32, (tm, tm), 1))
    before = cnt_ref[...] + jnp.dot(jnp.where(earlier, 1.0, 0.0).astype(BF16), chosen.astype(BF16),
                                    preferred_element_type=F32)
    r1 = jnp.sum(jnp.where(lane == i1, before, 0.0), axis=-1, keepdims=True)
    r2 = jnp.sum(jnp.where(lane == i2, before, 0.0), axis=-1, keepdims=True)
    rank_ref[...] = jnp.where(two == 0, r1, r2).astype(I32)
    for c in range(cb_ref.shape[1]):
        cb_ref[0, c:c + 1, :] = before[c * ROUTE_TILE:c * ROUTE_TILE + 1, :]
    cnt_ref[...] += jnp.sum(chosen, axis=0, keepdims=True)
    ca_ref[...] = cnt_ref[...]


def _mix_router(a, b, x, w_out, g, wr, br, cnt0, tm, precise=False):
    m, d = x.shape
    wa, wb = a.shape[1], b.shape[1]
    n_marks = max(1, tm // ROUTE_TILE)
    return pl.pallas_call(
        functools.partial(_mix_router_kernel, precise=precise),
        grid=(m // tm,),
        in_specs=[
            pl.BlockSpec((tm, wa), lambda i: (i, 0)),
            pl.BlockSpec((tm, wb), lambda i: (i, 0)),
            pl.BlockSpec((tm, d), lambda i: (i, 0)),
            pl.BlockSpec((wa + wb, d), lambda i: (0, 0)),
            pl.BlockSpec((1, d), lambda i: (0, 0)),
            pl.BlockSpec((d, 128), lambda i: (0, 0)),
            pl.BlockSpec((1, 128), lambda i: (0, 0)),
            pl.BlockSpec((1, 128), lambda i: (0, 0)),
        ],
        out_specs=[
            pl.BlockSpec((tm, d), lambda i: (i, 0)),
            pl.BlockSpec((tm, d), lambda i: (i, 0)),
            pl.BlockSpec((tm, 2), lambda i: (i, 0)),
            pl.BlockSpec((tm, 2), lambda i: (i, 0)),
            pl.BlockSpec((tm, 2), lambda i: (i, 0)),
            pl.BlockSpec((1, n_marks, 128), lambda i: (i, 0, 0)),
            pl.BlockSpec((1, 128), lambda i: (0, 0)),
        ],
        out_shape=[jax.ShapeDtypeStruct((m, d), F32), jax.ShapeDtypeStruct((m, d), BF16),
                   jax.ShapeDtypeStruct((m, 2), I32), jax.ShapeDtypeStruct((m, 2), F32),
                   jax.ShapeDtypeStruct((m, 2), I32), jax.ShapeDtypeStruct((m // tm, n_marks, 128), F32),
                   jax.ShapeDtypeStruct((1, 128), F32)],
        scratch_shapes=[pltpu.VMEM((1, 128), F32)],
        compiler_params=_cparams("arbitrary"),
        name="mix_router",
    )(a, b, x, w_out, g.reshape(1, d), wr, br, cnt0)


ATT_T = 512


def _t5_bucket(dist):
    max_exact = N_BUCKETS // 2
    d = jnp.maximum(dist, 1).astype(F32)
    large = max_exact + (jnp.log(d / max_exact) / math.log(A_WIN / max_exact)
                         * (N_BUCKETS - max_exact)).astype(I32)
    return jnp.where(dist < max_exact, dist, jnp.minimum(large, N_BUCKETS - 1))


def _distance_logit_table(rel_bias, max_dist):
    dist = jnp.arange(max_dist + 1, dtype=I32)
    mult = jnp.zeros((max_dist + 1,), F32)
    for window, d in A_BRANCHES:
        mult = mult + ((dist % d == 0) & (dist <= window)).astype(F32)
    bias = rel_bias[_t5_bucket(dist)].astype(F32).T
    return jnp.where(mult[None, :] > 0, bias + jnp.log(jnp.maximum(mult, 1.0))[None, :], NEG)


def _attn_kernel(q_ref, kt_ref, vt_ref, row0_ref, o_ref, kb_ref, vb_ref):
    s_len = q_ref.shape[1]
    nb = s_len // ATT_T
    period = row0_ref.shape[2]
    kb_ref[...] = kt_ref[0].astype(BF16)
    vb_ref[...] = vt_ref[0].astype(BF16)
    scale = A_HD ** -0.5
    for hl in range(2):
        hrows = slice(hl * A_HD, (hl + 1) * A_HD)
        table = pltpu.roll(jnp.broadcast_to(row0_ref[0, hl:hl + 1, :], (ATT_T, period)), 0, 1,
                           stride=1, stride_axis=0)
        for i in range(nb):
            rows = slice(i * ATT_T, (i + 1) * ATT_T)
            width = (i + 1) * ATT_T
            q = (q_ref[0, rows, hrows] * scale).astype(BF16)
            s = jnp.dot(q, kb_ref[hrows, 0:width], preferred_element_type=F32)
            s = s + table[:, (nb - 1 - i) * ATT_T:nb * ATT_T]
            m = jnp.max(s, axis=-1, keepdims=True)
            p = jnp.exp(s - m)
            den = jnp.sum(p, axis=-1, keepdims=True)
            o = lax.dot_general(p.astype(BF16), vb_ref[hrows, 0:width], (((1,), (1,)), ((), ())),
                                preferred_element_type=F32)
            o_ref[0, rows, hrows] = o / den


def _attention_prompt(q, kt, vt, rel_bias):
    b, s_len, _ = q.shape
    nb = s_len // ATT_T
    tab1 = _distance_logit_table(rel_bias, s_len)
    period = s_len + ATT_T
    z = jnp.arange(period, dtype=I32)
    delta = jnp.where(z < s_len, (nb - 1) * ATT_T - z, (nb - 1) * ATT_T + period - z)
    row0 = jnp.where(delta[None] >= 0, tab1[:, jnp.clip(delta, 0, s_len)], NEG)
    return pl.pallas_call(
        _attn_kernel,
        grid=(b, A_HEADS // 2),
        in_specs=[pl.BlockSpec((1, s_len, 2 * A_HD), lambda bi, hp: (bi, 0, hp)),
                  pl.BlockSpec((1, 2 * A_HD, s_len), lambda bi, hp: (bi, hp, 0)),
                  pl.BlockSpec((1, 2 * A_HD, s_len), lambda bi, hp: (bi, hp, 0)),
                  pl.BlockSpec((1, 2, period), lambda bi, hp: (hp, 0, 0))],
        out_specs=pl.BlockSpec((1, s_len, 2 * A_HD), lambda bi, hp: (bi, 0, hp)),
        out_shape=jax.ShapeDtypeStruct(q.shape, F32),
        scratch_shapes=[pltpu.VMEM((2 * A_HD, s_len), BF16), pltpu.VMEM((2 * A_HD, s_len), BF16)],
        compiler_params=_cparams("arbitrary", "arbitrary"),
        name="dilated_attention",
    )(q, kt, vt, row0.reshape(A_HEADS // 2, 2, period))


MLSTM_CHUNK = 256


def _log_sigmoid(x):
    return jnp.minimum(x, 0.0) - jnp.log(1.0 + jnp.exp(-jnp.abs(x)))


def _mlstm_kernel(qk_ref, v_ref, ob_ref, gc_ref, gr_ref, wc_ref, bc_ref, bifc_ref, bifr_ref, gh_ref,
                  o_ref, c_out_ref, n_out_ref, m_out_ref, xbuf_ref, c_ref, n_ref, m_ref):
    ci = pl.program_id(1)
    L = qk_ref.shape[1]

    @pl.when(ci == 0)
    def _():
        xbuf_ref[0:8, :] = jnp.zeros((8, 2 * B_W), F32)
        c_ref[...] = jnp.zeros_like(c_ref)
        n_ref[...] = jnp.zeros_like(n_ref)
        m_ref[...] = jnp.zeros_like(m_ref)

    x = qk_ref[0]
    xbuf_ref[8:8 + L, :] = x
    y = bc_ref[...] + wc_ref[3:4, :] * x
    for j in range(B_CONV - 1):
        y = y + wc_ref[j:j + 1, :] * xbuf_ref[5 + j:5 + j + L, :]
    xbuf_ref[0:8, :] = x[L - 8:L, :]
    y = _silu(y)

    gcol = gc_ref[0] + bifc_ref[...]
    grow = gr_ref[...] + bifr_ref[...]
    ri = lax.broadcasted_iota(I32, (L, L), 0)
    cj = lax.broadcasted_iota(I32, (L, L), 1)
    causal = ri >= cj
    lower = jnp.where(causal, 1.0, 0.0)
    b_col = jnp.dot(lower, _log_sigmoid(gcol), preferred_element_type=F32, precision=HIGHEST)
    b_row = lax.dot_general(_log_sigmoid(grow), lower, (((1,), (1,)), ((), ())),
                            preferred_element_type=F32, precision=HIGHEST)

    for h in range(B_HEADS):
        cols = slice(h * B_HD, (h + 1) * B_HD)
        qf = y[:, cols]
        q = qf.astype(BF16)
        kf = y[:, B_W + h * B_HD:B_W + (h + 1) * B_HD] * (B_HD ** -0.5)
        k = kf.astype(BF16)
        v = v_ref[0, :, cols].astype(BF16)
        bc = b_col[:, B_HEADS + h:B_HEADS + h + 1]
        ic = gcol[:, h:h + 1]
        br = b_row[B_HEADS + h:B_HEADS + h + 1, :]
        ir = grow[h:h + 1, :]
        m_prev = m_ref[h:h + 1, 0:1]
        dm = jnp.where(causal, bc - br + ir, NEG)
        inter = bc + m_prev
        mt = jnp.maximum(inter, jnp.max(dm, axis=1, keepdims=True))
        w_intra = jnp.exp(dm - mt)
        w_inter = jnp.exp(inter - mt)
        a = w_intra * lax.dot_general(q, k, (((1,), (1,)), ((), ())), preferred_element_type=F32)
        c_prev = c_ref[h]
        n_prev = n_ref[h:h + 1, :]
        num = (jnp.dot(a.astype(BF16), v, preferred_element_type=F32)
               + w_inter * jnp.dot(q, c_prev.astype(BF16), preferred_element_type=F32))
        den = (jnp.sum(a, axis=1, keepdims=True)
               + w_inter * jnp.sum(qf * n_prev, axis=1, keepdims=True))
        hb = num / jnp.maximum(jnp.abs(den), jnp.exp(-mt))
        m_new = mt[L - 1:L, :]
        b_last = bc[L - 1:L, :]
        g = jnp.exp(b_last - bc + ic - m_new)
        decay = jnp.exp(b_last + m_prev - m_new)
        kg = kf * g
        c_ref[h] = decay * c_prev + lax.dot_general(kg.astype(BF16), v, (((0,), (0,)), ((), ())),
                                                    preferred_element_type=F32)
        n_ref[h:h + 1, :] = decay * n_prev + jnp.sum(kg, axis=0, keepdims=True)
        m_ref[h:h + 1, :] = jnp.broadcast_to(m_new, (1, 128))
        hb = hb * lax.rsqrt(jnp.mean(hb * hb, axis=-1, keepdims=True) + EPS) * gh_ref[:, cols]
        o_ref[0, :, cols] = jax.nn.sigmoid(ob_ref[0, :, cols]) * hb

    @pl.when(ci == pl.num_programs(1) - 1)
    def _():
        c_out_ref[0] = c_ref[...]
        n_out_ref[0] = n_ref[...]
        m_out_ref[0] = m_ref[...]


def _mlstm_prompt(qk, v, ob, gcol, grow, w_conv, b_conv, b_if, g_head):
    b, s_len, _ = qk.shape
    L = min(MLSTM_CHUNK, s_len)
    nc = s_len // L
    bif_col = jnp.zeros((1, 128), F32).at[0, :2 * B_HEADS].set(b_if)
    bif_row = b_if.reshape(2 * B_HEADS, 1)
    seq = lambda w: pl.BlockSpec((1, L, w), lambda bi, ci: (bi, ci, 0))
    full = lambda shape: pl.BlockSpec(shape, lambda bi, ci: (0,) * len(shape))
    return pl.pallas_call(
        _mlstm_kernel,
        grid=(b, nc),
        in_specs=[seq(2 * B_W), seq(B_W), seq(B_W), seq(128),
                  pl.BlockSpec((2 * B_HEADS, L), lambda bi, ci: (0, bi * nc + ci)),
                  full((B_CONV, 2 * B_W)), full((1, 2 * B_W)), full((1, 128)), full((2 * B_HEADS, 1)),
                  full((1, B_W))],
        out_specs=[seq(B_W),
                   pl.BlockSpec((1, B_HEADS, B_HD, B_HD), lambda bi, ci: (bi, 0, 0, 0)),
                   pl.BlockSpec((1, 8, B_HD), lambda bi, ci: (bi, 0, 0)),
                   pl.BlockSpec((1, 8, 128), lambda bi, ci: (bi, 0, 0))],
        out_shape=[jax.ShapeDtypeStruct((b, s_len, B_W), F32),
                   jax.ShapeDtypeStruct((b, B_HEADS, B_HD, B_HD), F32),
                   jax.ShapeDtypeStruct((b, 8, B_HD), F32),
                   jax.ShapeDtypeStruct((b, 8, 128), F32)],
        scratch_shapes=[pltpu.VMEM((L + 8, 2 * B_W), F32), pltpu.VMEM((B_HEADS, B_HD, B_HD), F32),
                        pltpu.VMEM((8, B_HD), F32), pltpu.VMEM((8, 128), F32)],
        compiler_params=_cparams("arbitrary", "arbitrary"),
        name="mlstm_chunkwise",
    )(qk, v, ob, gcol, grow, w_conv, b_conv.reshape(1, -1), bif_col, bif_row, g_head.reshape(1, -1))


def _odd_in_kernel(x_ref, g_ref, w_ref, gcv_ref, bcv_ref, ws_ref, bs_ref, c_ref, xd_ref):
    tm = x_ref.shape[0]
    xn = _rms(x_ref[...], g_ref[...]).astype(BF16)
    u = jnp.dot(xn, w_ref[:, 0:C_W], preferred_element_type=F32)
    v = jnp.dot(xn, w_ref[:, C_W:2 * C_W], preferred_element_type=F32)
    xd_ref[...] = jnp.dot(xn, w_ref[:, 2 * C_W:], preferred_element_type=F32)
    mu = jnp.mean(v, axis=-1, keepdims=True)
    vc = v - mu
    var = jnp.mean(vc * vc, axis=-1, keepdims=True)
    vl = (vc * lax.rsqrt(var + EPS) * gcv_ref[...] + bcv_ref[...]).astype(BF16)
    ri = lax.broadcasted_iota(I32, (C_CHUNK, C_CHUNK), 0)
    cj = lax.broadcasted_iota(I32, (C_CHUNK, C_CHUNK), 1)
    for g in range(C_GROUPS):
        cols = slice(g * C_GD, (g + 1) * C_GD)
        wg = jnp.where(ri >= cj, ws_ref[g], 0.0).astype(BF16)
        for c in range(tm // C_CHUNK):
            rows = slice(c * C_CHUNK, (c + 1) * C_CHUNK)
            s = jnp.dot(wg, vl[rows, cols], preferred_element_type=F32) + bs_ref[:, g:g + 1]
            c_ref[rows, cols] = u[rows, cols] * s


def _odd_in(x, g, w, g_cv, b_cv, w_s, b_s, tm):
    m, d = x.shape
    n = w.shape[1]
    full = lambda shape: pl.BlockSpec(shape, lambda i: (0,) * len(shape))
    return pl.pallas_call(
        _odd_in_kernel,
        grid=(m // tm,),
        in_specs=[pl.BlockSpec((tm, d), lambda i: (i, 0)), full((1, d)), full((d, n)), full((1, C_W)),
                  full((1, C_W)), full((C_GROUPS, C_CHUNK, C_CHUNK)), full((C_CHUNK, C_GROUPS))],
        out_specs=[pl.BlockSpec((tm, C_W), lambda i: (i, 0)), pl.BlockSpec((tm, D_W), lambda i: (i, 0))],
        out_shape=[jax.ShapeDtypeStruct((m, C_W), F32), jax.ShapeDtypeStruct((m, D_W), F32)],
        compiler_params=_cparams("arbitrary"),
        name="odd_in_proj_gate",
    )(x, g.reshape(1, d), w, g_cv.reshape(1, -1), b_cv.reshape(1, -1), w_s, b_s.T)


S5_TT = 128
S5_PAD = 8
S5_HALF = D_GROUPS * D_STATE // 2


def _s5_discretize(a_re, a_im, log_dt, bm_re, bm_im, cm_re, cm_im):
    a_re, a_im = a_re.astype(F32), a_im.astype(F32)
    dt = jnp.exp(log_dt.astype(F32))[:, None]
    mag = jnp.exp(a_re * dt)
    ab_re = mag * jnp.cos(a_im * dt)
    ab_im = mag * jnp.sin(a_im * dt)
    inv = 1.0 / (a_re * a_re + a_im * a_im)
    f_re = ((ab_re - 1.0) * a_re + ab_im * a_im) * inv
    f_im = (ab_im * a_re - (ab_re - 1.0) * a_im) * inv
    bm_re, bm_im = bm_re.astype(F32), bm_im.astype(F32)
    bb_re = f_re[..., None] * bm_re - f_im[..., None] * bm_im
    bb_im = f_re[..., None] * bm_im + f_im[..., None] * bm_re
    return ab_re, ab_im, bb_re, bb_im


def _s5_matrices(bb_re, bb_im, cm_re, cm_im):
    gh = D_GROUPS // 2
    eye = jnp.eye(gh, dtype=F32)

    def in_half(bb):
        return jnp.einsum('gpc,gh->gchp', bb, eye).reshape(gh * D_GCH, gh * D_STATE)

    def out_half(cm):
        return jnp.einsum('gcp,gh->gphc', cm, eye).reshape(gh * D_STATE, gh * D_GCH)

    bd = jnp.stack([jnp.concatenate([in_half(bb_re[h * gh:(h + 1) * gh]), in_half(bb_im[h * gh:(h + 1) * gh])], axis=1)
                    for h in range(2)])
    cm = jnp.stack([jnp.concatenate([out_half(cm_re[h * gh:(h + 1) * gh].astype(F32)),
                                     -out_half(cm_im[h * gh:(h + 1) * gh].astype(F32))], axis=0)
                    for h in range(2)])
    return bd, cm


def _gelu_tanh(x):
    return 0.5 * x * (1.0 + jnp.tanh(math.sqrt(2.0 / math.pi) * (x + 0.044715 * (x * x * x))))


def _s5_kernel(x_ref, bd_ref, cm_ref, ar_ref, ai_ref, dsk_ref, wg_ref, bg_ref, o_ref, hr_out, hi_out,
               bu_ref, hs_ref, hr_ref, hi_ref):
    i = pl.program_id(0)
    nb, tt, _ = x_ref.shape
    stride = tt + S5_PAD
    half_in = D_W // 2
    nct = S5_HALF // 128

    @pl.when(i == 0)
    def _():
        hr_ref[...] = jnp.zeros_like(hr_ref)
        hi_ref[...] = jnp.zeros_like(hi_ref)

    x = x_ref[...].reshape(nb * tt, D_W)
    ys = []
    for h in range(2):
        lanes = slice(h * S5_HALF, (h + 1) * S5_HALF)
        bu = jnp.dot(x[:, h * half_in:(h + 1) * half_in].astype(BF16), bd_ref[h], preferred_element_type=F32)
        for c in range(2 * nct):
            for b in range(nb):
                bu_ref[c, b * stride:b * stride + tt, :] = bu[b * tt:(b + 1) * tt, c * 128:(c + 1) * 128]
        ar = jnp.broadcast_to(ar_ref[:, lanes], (nb, S5_HALF))
        ai = jnp.broadcast_to(ai_ref[:, lanes], (nb, S5_HALF))

        def step(t, carry):
            hr, hi = carry
            rows = pl.ds(t, nb, stride=stride)
            bur = jnp.concatenate([bu_ref[c, rows, :] for c in range(nct)], axis=1)
            bui = jnp.concatenate([bu_ref[nct + c, rows, :] for c in range(nct)], axis=1)
            nhr = ar * hr - ai * hi + bur
            nhi = ar * hi + ai * hr + bui
            for c in range(nct):
                hs_ref[c, rows, :] = nhr[:, c * 128:(c + 1) * 128]
                hs_ref[nct + c, rows, :] = nhi[:, c * 128:(c + 1) * 128]
            return nhr, nhi

        hr, hi = lax.fori_loop(0, tt, step, (hr_ref[:, lanes], hi_ref[:, lanes]), unroll=2)
        hr_ref[:, lanes] = hr
        hi_ref[:, lanes] = hi
        hs = jnp.concatenate(
            [jnp.concatenate([hs_ref[c, b * stride:b * stride + tt, :] for b in range(nb)], axis=0)
             for c in range(2 * nct)], axis=1)
        ys.append(jnp.dot(hs.astype(BF16), cm_ref[h], preferred_element_type=F32))
    y = jnp.concatenate(ys, axis=-1) + dsk_ref[...] * x
    g = _gelu_tanh(y)
    d_out = g * jax.nn.sigmoid(jnp.dot(g.astype(BF16), wg_ref[...], preferred_element_type=F32) + bg_ref[...])
    o_ref[...] = d_out.reshape(nb, tt, D_W)

    @pl.when(i == pl.num_programs(0) - 1)
    def _():
        hr_out[...] = hr_ref[...]
        hi_out[...] = hi_ref[...]


def _s5_prompt(xd, bd, cm, ab_re, ab_im, d_skip, w_glu, b_glu):
    nb, t_len, _ = xd.shape
    tt = min(S5_TT, t_len)
    n_state = D_GROUPS * D_STATE
    full = lambda shape: pl.BlockSpec(shape, lambda i: (0,) * len(shape))
    return pl.pallas_call(
        _s5_kernel,
        grid=(t_len // tt,),
        in_specs=[pl.BlockSpec((nb, tt, D_W), lambda i: (0, i, 0)), full(bd.shape), full(cm.shape),
                  full((1, n_state)), full((1, n_state)), full((1, D_W)), full((D_W, D_W)), full((1, D_W))],
        out_specs=[pl.BlockSpec((nb, tt, D_W), lambda i: (0, i, 0)), full((nb, n_state)), full((nb, n_state))],
        out_shape=[jax.ShapeDtypeStruct(xd.shape, F32), jax.ShapeDtypeStruct((nb, n_state), F32),
                   jax.ShapeDtypeStruct((nb, n_state), F32)],
        scratch_shapes=[pltpu.VMEM((2 * S5_HALF // 128, nb * (tt + S5_PAD), 128), F32),
                        pltpu.VMEM((2 * S5_HALF // 128, nb * (tt + S5_PAD), 128), F32),
                        pltpu.VMEM((nb, n_state), F32), pltpu.VMEM((nb, n_state), F32)],
        compiler_params=_cparams("arbitrary"),
        name="s5_scan_glu",
    )(xd, bd, cm, ab_re.reshape(1, n_state), ab_im.reshape(1, n_state), d_skip.reshape(1, D_W), w_glu,
      b_glu.reshape(1, D_W))


def _hdot(a, b):
    return jnp.dot(a, b, precision=HIGHEST, preferred_element_type=F32)


def _sample_window_kernel(q_ref, kn_ref, vn_ref, tab_ref, tabn_ref, kt_ref, vt_ref, o_ref, okt_ref, ovt_ref,
                          head0=0):
    L = kt_ref.shape[3]
    last = lax.broadcasted_iota(I32, (A_HD, L), 1) == L - 1
    for h in range(kt_ref.shape[1]):
        kh, vh = kt_ref[0, h], vt_ref[0, h]
        kn, vn = kn_ref[0, h], vn_ref[0, h]
        q = q_ref[0, h] * (A_HD ** -0.5)
        s = jnp.sum(kh * q, axis=0, keepdims=True) + tab_ref[head0 + h]
        s_new = jnp.sum(kn * q, axis=0, keepdims=True) + tabn_ref[head0 + h]
        m = jnp.maximum(jnp.max(s, axis=1, keepdims=True), s_new)
        p = jnp.exp(s - m)
        p_new = jnp.exp(s_new - m)
        den = jnp.sum(p, axis=1, keepdims=True) + p_new
        o_ref[0, h] = (jnp.sum(vh * p, axis=1, keepdims=True) + vn * p_new) / den
        okt_ref[0, h] = jnp.where(last, kn, pltpu.roll(kh, L - 1, 1))
        ovt_ref[0, h] = jnp.where(last, vn, pltpu.roll(vh, L - 1, 1))


def _sample_window_attention(q, k_new, v_new, cache_kt, cache_vt, rel_bias):
    operands, in_specs, out_specs, out_shape = _sample_window_specs(q, k_new, v_new, cache_kt, cache_vt, rel_bias,
                                                                    lambda i: (i, 0))
    return pl.pallas_call(
        _sample_window_kernel,
        grid=(q.shape[0],),
        in_specs=in_specs,
        out_specs=out_specs,
        out_shape=out_shape,
        compiler_params=_cparams("arbitrary"),
        name="sample_window_attention",
    )(*operands)


def _sample_window_specs(q, k_new, v_new, cache_kt, cache_vt, rel_bias, block_of_step, heads_per_step=A_HEADS):
    n, _, _, L = cache_kt.shape
    assert L == A_WIN
    tab1 = _distance_logit_table(rel_bias, L)
    tab = tab1[:, :0:-1].reshape(A_HEADS, 1, L)
    tab_new = tab1[:, 0].reshape(A_HEADS, 1, 1)
    col = pl.BlockSpec((1, heads_per_step, A_HD, 1), lambda *g: (*block_of_step(*g), 0, 0))
    win = pl.BlockSpec((1, heads_per_step, A_HD, L), lambda *g: (*block_of_step(*g), 0, 0))
    full = lambda shape: pl.BlockSpec(shape, lambda *g: (0,) * len(shape))
    out_shape = [jax.ShapeDtypeStruct((n, A_HEADS, A_HD, 1), F32),
                 jax.ShapeDtypeStruct(cache_kt.shape, F32), jax.ShapeDtypeStruct(cache_vt.shape, F32)]
    return ([q, k_new, v_new, tab, tab_new, cache_kt, cache_vt],
            [col, col, col, full(tab.shape), full(tab_new.shape), win, win], [col, win, win], out_shape)


def _sample_even_proj(x, g_mix, w_in):
    n_main = 3 * A_W + 4 * B_W
    segs = ((0, A_W), (A_W, A_W), (2 * A_W, A_W), (3 * A_W, 2 * B_W), (3 * A_W + 2 * B_W, B_W),
            (3 * A_W + 3 * B_W, B_W), (n_main, 2 * B_HEADS))
    return _norm_proj(x, g_mix, w_in, jnp.zeros((8, x.shape[1]), F32), segs, (), tm=x.shape[0], precise=True)


def _sample_mlstm_step(qkb, vb, ob, gates, c0, n0, m0, conv0, b_if, w_conv, b_conv, g_head):
    n = qkb.shape[0]
    xp = jnp.concatenate([conv0, qkb[:, None, :]], axis=1)
    qk = _silu(b_conv + jnp.sum(xp * w_conv[None], axis=1))
    q = qk[:, :B_W].reshape(n, B_HEADS, B_HD)
    k = qk[:, B_W:].reshape(n, B_HEADS, B_HD) * (B_HD ** -0.5)
    v = vb.reshape(n, B_HEADS, B_HD)
    ig = gates[:, :B_HEADS] + b_if[:B_HEADS]
    logf = _log_sigmoid(gates[:, B_HEADS:] + b_if[B_HEADS:])
    inter = logf + m0
    mt = jnp.maximum(inter, ig)
    w_intra = jnp.exp(ig - mt)
    w_inter = jnp.exp(inter - mt)
    a = w_intra * jnp.sum(q * k, axis=-1)
    num = a[..., None] * v + w_inter[..., None] * jnp.einsum('nhk,nhkv->nhv', q, c0, precision=HIGHEST)
    den = a + w_inter * jnp.sum(q * n0, axis=-1)
    hb = num / jnp.maximum(jnp.abs(den), jnp.exp(-mt))[..., None]
    g = jnp.exp(ig - mt)
    decay = jnp.exp(inter - mt)
    c_new = decay[..., None, None] * c0 + (g[..., None] * k)[..., :, None] * v[..., None, :]
    n_new = decay[..., None] * n0 + g[..., None] * k
    hb = hb * lax.rsqrt(jnp.mean(hb * hb, axis=-1, keepdims=True) + EPS) * g_head.reshape(B_HEADS, B_HD)
    b_out = jax.nn.sigmoid(ob) * hb.reshape(n, B_W)
    return b_out, c_new, n_new, mt, xp[:, 1:]


def _sample_odd_mixers(x, h_re0, h_im0, g_mix, w_in, g_cv, b_cv, w_s, b_s, ab_re, ab_im, bb_re, bb_im,
                       cm_re, cm_im, d_skip, w_glu, b_glu):
    n = x.shape[0]
    h = _rms(x, g_mix)
    proj = _hdot(h, w_in)
    u, v, xd = proj[:, :C_W], proj[:, C_W:2 * C_W], proj[:, 2 * C_W:]
    mu = jnp.mean(v, axis=-1, keepdims=True)
    vc = v - mu
    v = vc * lax.rsqrt(jnp.mean(vc * vc, axis=-1, keepdims=True) + EPS) * g_cv + b_cv
    s = (w_s[:, 0, 0][None, :, None] * v.reshape(n, C_GROUPS, C_GD) + b_s[:, 0][None, :, None]).reshape(n, C_W)
    c_out = u * s
    xg = xd.reshape(n, D_GROUPS, D_GCH)
    bu_re = jnp.einsum('gpc,ngc->ngp', bb_re, xg, precision=HIGHEST)
    bu_im = jnp.einsum('gpc,ngc->ngp', bb_im, xg, precision=HIGHEST)
    hr = ab_re * h_re0 - ab_im * h_im0 + bu_re
    hi = ab_re * h_im0 + ab_im * h_re0 + bu_im
    y = (jnp.einsum('gcp,ngp->ngc', cm_re, hr, precision=HIGHEST)
         - jnp.einsum('gcp,ngp->ngc', cm_im, hi, precision=HIGHEST) + d_skip * xg)
    g = _gelu_tanh(y.reshape(n, D_W))
    d_out = g * jax.nn.sigmoid(_hdot(g, w_glu) + b_glu)
    return c_out, d_out, v, hr, hi


_EVEN_SEGS = ((0, A_W), (A_W, 2 * B_W), (A_W + 2 * B_W, B_W), (A_W + 3 * B_W, B_W), (A_W + 4 * B_W, 128))


def kernel(x_prompt, x_sample, cache_a_k, cache_a_v, state_b_c, state_b_n, state_b_m, state_b_conv,
           state_d_re, state_d_im, rel_bias, g_mix, g_ffn, g_final, w_in_e, b_if, w_conv_b, b_conv_b,
           g_bhead, w_out_e, w1_e, w3_e, w2_e, w_in_o, g_cv, b_cv, w_s, b_s, a_re, a_im, log_dt,
           bm_re, bm_im, cm_re, cm_im, d_skip, w_glu, b_glu, w_out_o, w_router, b_router,
           w1_m, w3_m, w2_m):
    bp, sp, d = x_prompt.shape
    ns = x_sample.shape[0]
    mp = bp * sp
    xp = x_prompt.reshape(mp, d)
    xs = x_sample.reshape(ns, d)

    w_in = w_in_e[0]
    n_main = 3 * A_W + 4 * B_W
    w_gate = w_in[:, n_main:]
    w_cat = jnp.concatenate([w_in[:, :A_W], w_in[:, 3 * A_W:n_main],
                             jnp.pad(w_gate, ((0, 0), (0, 128 - 2 * B_HEADS)))], axis=1)
    w_t = jnp.concatenate([w_in[:, A_W:3 * A_W], w_gate], axis=1).T
    q, qkb, vb, ob, gcol, kt, vt, grow = _norm_proj(
        xp, g_mix[0], w_cat.astype(BF16), w_t.astype(BF16), _EVEN_SEGS,
        ((0, A_W, True), (A_W, A_W, True), (2 * A_W, 2 * B_HEADS, False)), tm=512, rows_per_seq=sp)
    shp = lambda a: a.reshape(bp, sp, a.shape[-1])
    pos_minor = lambda c: jnp.transpose(c, (0, 2, 3, 1))
    pos_major = lambda c: jnp.transpose(c, (0, 3, 1, 2))
    s_proj = _sample_even_proj(xs, g_mix[0], w_in)
    cols = lambda a: a.reshape(ns, A_HEADS, A_HD, 1)
    s_attn = (cols(s_proj[0]), cols(s_proj[1]), cols(s_proj[2]), pos_minor(cache_a_k[0]), pos_minor(cache_a_v[0]))
    a_out = _attention_prompt(shp(q), kt, vt, rel_bias)
    b_out, pc, pn, pm = _mlstm_prompt(shp(qkb), shp(vb), shp(ob), shp(gcol), grow, w_conv_b[0], b_conv_b[0],
                                      b_if[0], g_bhead[0])
    ffn_args = (a_out.reshape(mp, A_W), b_out.reshape(mp, B_W), xp, w_out_e[0].astype(BF16), g_ffn[0],
                w1_e[0].astype(BF16), w3_e[0].astype(BF16), w2_e[0].astype(BF16))
    ffn_tm, ffn_tf = 1024, 512
    if ns * WINDOW_HEAD_BLOCKS == (mp // ffn_tm) * WINDOW_STEPS:
        x1p, sa, s_kt, s_vt = _mix_ffn(*ffn_args, tm=ffn_tm, tf=ffn_tf, window=s_attn, rel_bias=rel_bias)
    else:
        x1p = _mix_ffn(*ffn_args, tm=ffn_tm, tf=ffn_tf)
        sa, s_kt, s_vt = _sample_window_attention(*s_attn, rel_bias)
    sa = sa.reshape(ns, A_W)
    s_k, s_v = pos_major(s_kt), pos_major(s_vt)

    sb, sc, sn, sm, s_conv = _sample_mlstm_step(
        s_proj[3], s_proj[4], s_proj[5], s_proj[6], state_b_c[0], state_b_n[0], state_b_m[0], state_b_conv[0],
        b_if[0], w_conv_b[0], b_conv_b[0], g_bhead[0])
    x1s = _mix_ffn(sa, sb, xs, w_out_e[0], g_ffn[0], w1_e[0], w3_e[0], w2_e[0], tm=ns, tf=512, precise=True)

    ab_re, ab_im, bb_re, bb_im = _s5_discretize(a_re[0], a_im[0], log_dt[0], bm_re[0], bm_im[0], cm_re[0],
                                                cm_im[0])
    bd, cmat = _s5_matrices(bb_re, bb_im, cm_re[0], cm_im[0])
    c_out, xd = _odd_in(x1p, g_mix[1], w_in_o[0].astype(BF16), g_cv[0], b_cv[0], w_s[0], b_s[0], tm=512)
    d_out, p_hr, p_hi = _s5_prompt(xd.reshape(bp, sp, D_W), bd.astype(BF16), cmat.astype(BF16), ab_re, ab_im,
                                   d_skip[0], w_glu[0].astype(BF16), b_glu[0])
    wr = jnp.pad(w_router[0].astype(F32), ((0, 0), (0, 128 - N_EXPERTS)))
    br = jnp.pad(b_router[0].astype(F32), (0, 128 - N_EXPERTS)).reshape(1, 128)
    x2p, hnp, idp, gtp, rkp, cbp, cap = _mix_router(c_out, d_out.reshape(mp, D_W), x1p, w_out_o[0].astype(BF16),
                                                    g_ffn[1], wr, br, jnp.zeros((1, 128), F32), tm=512)

    sc_out, sd_out, s_cv, s_hr, s_hi = _sample_odd_mixers(
        x1s, state_d_re[0], state_d_im[0], g_mix[1], w_in_o[0], g_cv[0], b_cv[0], w_s[0], b_s[0], ab_re, ab_im,
        bb_re, bb_im, cm_re[0], cm_im[0], d_skip[0], w_glu[0], b_glu[0])
    x2s, hns, ids_s, gts, rks, cbs, cas = _mix_router(sc_out, sd_out, x1s, w_out_o[0], g_ffn[1], wr, br, cap,
                                                      tm=ns, precise=True)

    pad = ((0, ROUTE_TILE - ns), (0, 0))
    cnt_tile = jnp.concatenate([cbp.reshape(-1, 128), cbs.reshape(-1, 128), cas])[:, :N_EXPERTS].astype(I32)
    y_p, y_s = _moe_block(x2p, hnp, idp, gtp, rkp, jnp.pad(x2s, pad), jnp.pad(hns, pad),
                          jnp.pad(ids_s, pad, constant_values=-1), jnp.pad(gts, pad), jnp.pad(rks, pad),
                          cnt_tile, w1_m[0], w3_m[0], w2_m[0], g_final)
    y_prompt = y_p.reshape(bp, sp, d)
    y_sample = y_s[:ns].reshape(ns, 1, d)

    e = lambda a: a[None]
    p_k = pos_major(kt.reshape(bp, A_HEADS, A_HD, sp))
    p_v = pos_major(vt.reshape(bp, A_HEADS, A_HD, sp))
    p_conv = qkb.reshape(bp, sp, 2 * B_W)[:, sp - (B_CONV - 1):]
    return (y_prompt, y_sample, e(p_k), e(p_v), e(s_k), e(s_v), e(pc), e(sc), e(pn[:, :B_HEADS]), e(sn),
            e(pm[:, :B_HEADS, 0]), e(sm), e(p_conv), e(s_conv), e(s_cv.reshape(ns, 1, C_W)),
            e(p_hr.reshape(bp, D_GROUPS, D_STATE)), e(s_hr), e(p_hi.reshape(bp, D_GROUPS, D_STATE)), e(s_hi))
```

```python
import functools
import math

import jax
import jax.numpy as jnp
from jax import lax
from jax.experimental import pallas as pl
from jax.experimental.pallas import tpu as pltpu

F32 = jnp.float32
BF16 = jnp.bfloat16
I32 = jnp.int32
HIGHEST = lax.Precision.HIGHEST

D_MODEL = 1024
A_HEADS, A_HD = 8, 64
A_W = A_HEADS * A_HD
A_BRANCHES = ((128, 1), (512, 4), (2048, 16))
A_WIN = 2048
N_BUCKETS = 32
B_HEADS, B_HD = 4, 128
B_W = B_HEADS * B_HD
B_CONV = 4
C_GROUPS, C_GD = 4, 128
C_W = C_GROUPS * C_GD
C_CHUNK = 128
D_GROUPS, D_GCH, D_STATE = 32, 16, 64
D_W = D_GROUPS * D_GCH
D_FF = 3584
N_EXPERTS = 8
TOP_K = 2
EPS = 1e-6
NEG = -1e30

VMEM_LIMIT = 56 * 1024 * 1024

ROUTE_TILE = 256
FFN_TILE = 1024
FFN_ROW_CHUNK = 256
GATHER_TOKENS = 2 * ROUTE_TILE
COMBINE_ALIGN = 16


def _cparams(*sem):
    return pltpu.CompilerParams(dimension_semantics=sem, vmem_limit_bytes=VMEM_LIMIT)


def _moe_plan(ids, rank, cnt_tile):
    mt = ids.shape[0]
    n_assign = 2 * mt
    n_ffn_tiles = -(-(n_assign + N_EXPERTS * (FFN_TILE - 1)) // FFN_TILE) + 1
    n_sub = n_ffn_tiles * (FFN_TILE // ROUTE_TILE)
    counts = cnt_tile[-1]
    seg = ((counts + FFN_TILE - 1) // FFN_TILE) * FFN_TILE
    seg_end = jnp.cumsum(seg)
    seg_off = seg_end - seg
    off_of = jnp.sum(jnp.where(ids[..., None] == jnp.arange(N_EXPERTS, dtype=I32), seg_off, 0), axis=-1)
    dest = jnp.where(ids >= 0, off_of + rank, -1)
    total = seg_end[-1]
    tile_start = jnp.arange(n_ffn_tiles, dtype=I32) * FFN_TILE
    tile_expert = jnp.minimum(jnp.sum((tile_start[:, None] >= seg_end[None, :]).astype(I32), axis=1),
                              N_EXPERTS - 1).astype(I32)
    rows_left = (seg_off + counts)[tile_expert] - tile_start
    tile_chunks = jnp.where(tile_start < total,
                            jnp.clip(-(-rows_left // FFN_ROW_CHUNK), 0, FFN_TILE // FFN_ROW_CHUNK), 0).astype(I32)
    lo = seg_off[None, :] + cnt_tile[:-1]
    hi = seg_off[None, :] + cnt_tile[1:]
    sub_start = jnp.arange(n_sub, dtype=I32) * ROUTE_TILE
    sub_expert = jnp.minimum(jnp.sum((sub_start[:, None] >= seg_end[None, :]).astype(I32), axis=1),
                             N_EXPERTS - 1)
    lo_e, hi_e = lo.T[sub_expert], hi.T[sub_expert]
    jlo = jnp.sum((hi_e <= sub_start[:, None]).astype(I32), axis=1).astype(I32)
    jhi = (jnp.sum((lo_e < sub_start[:, None] + ROUTE_TILE).astype(I32), axis=1) - 1).astype(I32)
    win = lo // COMBINE_ALIGN
    flat1 = lambda a: a.reshape(-1).astype(I32)
    return dest, tile_expert, tile_chunks, (jlo, jhi), (flat1(win), flat1(lo), flat1(hi)), n_ffn_tiles


def _moe_gather_kernel(jlo_ref, jhi_ref, hp_hbm, hs_hbm, dt_ref, gt_ref, xs_ref, gs_ref,
                       hn_ref, acc_ref, g_ref, sem):
    s = pl.program_id(0)
    mp = hp_hbm.shape[0]

    @pl.when(s == 0)
    def _():
        copies = (pltpu.make_async_copy(hp_hbm, hn_ref.at[0:mp], sem.at[0]),
                  pltpu.make_async_copy(hs_hbm, hn_ref.at[mp:mp + ROUTE_TILE], sem.at[1]))
        for c in copies:
            c.start()
        hn_ref[mp + ROUTE_TILE:, :] = jnp.zeros((ROUTE_TILE, hn_ref.shape[1]), BF16)
        for c in copies:
            c.wait()

    acc_ref[...] = jnp.zeros_like(acc_ref)
    g_ref[...] = jnp.zeros_like(g_ref)
    rows = lax.broadcasted_iota(I32, (ROUTE_TILE, GATHER_TOKENS), 0) + s * ROUTE_TILE
    jlo = jlo_ref[s]

    def body(it, carry):
        off = pl.multiple_of(jlo * ROUTE_TILE + it * GATHER_TOKENS, ROUTE_TILE)
        dt = dt_ref[:, pl.ds(off, GATHER_TOKENS)]
        gt = gt_ref[:, pl.ds(off, GATHER_TOKENS)]
        m1 = dt[0:1, :] == rows
        m2 = dt[1:2, :] == rows
        onehot = jnp.where(m1 | m2, 1.0, 0.0).astype(BF16)
        acc_ref[...] += jnp.dot(onehot, hn_ref[pl.ds(off, GATHER_TOKENS), :], preferred_element_type=F32)
        g_ref[...] += jnp.sum(jnp.where(m1, gt[0:1, :], 0.0) + jnp.where(m2, gt[1:2, :], 0.0),
                              axis=1, keepdims=True)
        return carry

    tiles_per_iter = GATHER_TOKENS // ROUTE_TILE
    lax.fori_loop(0, (jhi_ref[s] - jlo + tiles_per_iter) // tiles_per_iter, body, 0)
    xs_ref[...] = acc_ref[...].astype(BF16)
    gs_ref[...] = g_ref[...]


def _moe_gather(hn_p, hn_s, dest_t, gates_t, glist, n_sub):
    jlo, jhi = glist
    mp, d = hn_p.shape
    mt = dest_t.shape[1]
    grid_spec = pltpu.PrefetchScalarGridSpec(
        num_scalar_prefetch=2,
        grid=(n_sub,),
        in_specs=[
            pl.BlockSpec(memory_space=pl.ANY),
            pl.BlockSpec(memory_space=pl.ANY),
            pl.BlockSpec((2, mt), lambda s, *_: (0, 0)),
            pl.BlockSpec((2, mt), lambda s, *_: (0, 0)),
        ],
        out_specs=[
            pl.BlockSpec((ROUTE_TILE, d), lambda s, *_: (s, 0)),
            pl.BlockSpec((ROUTE_TILE, 1), lambda s, *_: (s, 0)),
        ],
        scratch_shapes=[pltpu.VMEM((mp + 2 * ROUTE_TILE, d), BF16), pltpu.VMEM((ROUTE_TILE, d), F32),
                        pltpu.VMEM((ROUTE_TILE, 1), F32), pltpu.SemaphoreType.DMA((2,))],
    )
    return pl.pallas_call(
        _moe_gather_kernel,
        grid_spec=grid_spec,
        out_shape=[jax.ShapeDtypeStruct((n_sub * ROUTE_TILE, d), BF16),
                   jax.ShapeDtypeStruct((n_sub * ROUTE_TILE, 1), F32)],
        compiler_params=_cparams("arbitrary"),
        name="moe_gather",
    )(jlo, jhi, hn_p, hn_s, dest_t, gates_t)


def _moe_ffn_kernel(te_ref, nc_ref, x_ref, gs_ref, w1_ref, w3_ref, w2_ref, y_ref, acc_ref):
    t, f = pl.program_id(0), pl.program_id(1)
    nf = pl.num_programs(1)
    n_chunks = nc_ref[t]
    full = FFN_TILE // FFN_ROW_CHUNK

    @pl.when(n_chunks > 0)
    def _():
        @pl.when(f == 0)
        def _():
            acc_ref[...] = jnp.zeros_like(acc_ref)

        w1, w3, w2 = w1_ref[0].astype(BF16), w3_ref[0].astype(BF16), w2_ref[0].astype(BF16)

        def ffn(rows):
            x = x_ref[rows, :]
            a = jnp.dot(x, w1, preferred_element_type=F32)
            b = jnp.dot(x, w3, preferred_element_type=F32)
            acc_ref[rows, :] += jnp.dot((_silu(a) * b).astype(BF16), w2, preferred_element_type=F32)

        @pl.when(n_chunks == full)
        def _():
            ffn(slice(0, FFN_TILE))

        for c in range(full - 1):
            @pl.when((n_chunks < full) & (c < n_chunks))
            def _():
                ffn(slice(c * FFN_ROW_CHUNK, (c + 1) * FFN_ROW_CHUNK))

        @pl.when(f == nf - 1)
        def _():
            y_ref[...] = (acc_ref[...] * gs_ref[...]).astype(BF16)

    @pl.when((n_chunks == 0) & (f == nf - 1))
    def _():
        y_ref[...] = jnp.zeros_like(y_ref)


def _moe_ffn(xs, gs, w1, w3, w2, tile_expert, tile_chunks, n_ffn_tiles, tf):
    d = xs.shape[1]
    ff = w1.shape[2]
    nf = ff // tf

    def fidx(t, f, nc):
        return jnp.where(nc[t] > 0, f, nf - 1)

    grid_spec = pltpu.PrefetchScalarGridSpec(
        num_scalar_prefetch=2,
        grid=(n_ffn_tiles, nf),
        in_specs=[
            pl.BlockSpec((FFN_TILE, d), lambda t, f, te, nu: (t, 0)),
            pl.BlockSpec((FFN_TILE, 1), lambda t, f, te, nu: (t, 0)),
            pl.BlockSpec((1, d, tf), lambda t, f, te, nu: (te[t], 0, fidx(t, f, nu))),
            pl.BlockSpec((1, d, tf), lambda t, f, te, nu: (te[t], 0, fidx(t, f, nu))),
            pl.BlockSpec((1, tf, d), lambda t, f, te, nu: (te[t], fidx(t, f, nu), 0)),
        ],
        out_specs=pl.BlockSpec((FFN_TILE, d), lambda t, f, te, nu: (t, 0)),
        scratch_shapes=[pltpu.VMEM((FFN_TILE, d), F32)],
    )
    return pl.pallas_call(
        _moe_ffn_kernel,
        grid_spec=grid_spec,
        out_shape=jax.ShapeDtypeStruct(xs.shape, BF16),
        compiler_params=_cparams("arbitrary", "arbitrary"),
        name="moe_ffn",
    )(tile_expert, tile_chunks, xs, gs, w1, w3, w2)


def _moe_combine_kernel(win_ref, lo_ref, hi_ref, xp_ref, xs_ref, d_ref, g_ref, *rest):
    ys_refs, tail_refs = rest[:N_EXPERTS], rest[N_EXPERTS:2 * N_EXPERTS]
    yp_ref, ysm_ref, acc_ref = rest[2 * N_EXPERTS:]
    j = pl.program_id(0)
    n_prompt_tiles = pl.num_programs(0) - 1

    @pl.when(j < n_prompt_tiles)
    def _():
        acc_ref[...] = xp_ref[...]

    @pl.when(j == n_prompt_tiles)
    def _():
        acc_ref[...] = xs_ref[...]

    for e in range(N_EXPERTS):
        lo, hi = lo_ref[j * N_EXPERTS + e], hi_ref[j * N_EXPERTS + e]

        @pl.when(hi > lo)
        def _():
            d = d_ref[...]
            d = jnp.where((d >= lo) & (d < hi), d, -1)
            start = win_ref[j * N_EXPERTS + e] * COMBINE_ALIGN

            def gathered(ref, first):
                cols = lax.broadcasted_iota(I32, (ROUTE_TILE, ref.shape[0]), 1) + first
                onehot = jnp.where((d[:, 0:1] == cols) | (d[:, 1:2] == cols), 1.0, 0.0).astype(BF16)
                return jnp.dot(onehot, ref[...], preferred_element_type=F32)

            acc_ref[...] += gathered(ys_refs[e], start)

            @pl.when(hi > start + ROUTE_TILE)
            def _():
                acc_ref[...] += gathered(tail_refs[e], start + ROUTE_TILE)

    y = _rms(acc_ref[...], g_ref[...])

    @pl.when(j < n_prompt_tiles)
    def _():
        yp_ref[...] = y

    @pl.when(j == n_prompt_tiles)
    def _():
        ysm_ref[...] = y


def _moe_combine(x2_p, x2_s, dest, ys, g_final, clist):
    win, lo, hi = clist
    mp, d = x2_p.shape
    n_prompt_tiles = mp // ROUTE_TILE

    def ptile(j, *_):
        return (jnp.minimum(j, n_prompt_tiles - 1), 0)

    def window(e, rows, offset):
        return pl.BlockSpec((pl.Element(rows), pl.Element(d)),
                            lambda j, win, lo, hi: ((win[j * N_EXPERTS + e] + offset // COMBINE_ALIGN)
                                                    * COMBINE_ALIGN, 0))

    grid_spec = pltpu.PrefetchScalarGridSpec(
        num_scalar_prefetch=3,
        grid=(n_prompt_tiles + 1,),
        in_specs=[
            pl.BlockSpec((ROUTE_TILE, d), ptile),
            pl.BlockSpec((ROUTE_TILE, d), lambda j, *_: (0, 0)),
            pl.BlockSpec((ROUTE_TILE, 2), lambda j, *_: (j, 0)),
            pl.BlockSpec((1, d), lambda j, *_: (0, 0)),
        ] + [window(e, ROUTE_TILE, 0) for e in range(N_EXPERTS)]
          + [window(e, COMBINE_ALIGN, ROUTE_TILE) for e in range(N_EXPERTS)],
        out_specs=[pl.BlockSpec((ROUTE_TILE, d), ptile), pl.BlockSpec((ROUTE_TILE, d), lambda j, *_: (0, 0))],
        scratch_shapes=[pltpu.VMEM((ROUTE_TILE, d), F32)],
    )
    return pl.pallas_call(
        _moe_combine_kernel,
        grid_spec=grid_spec,
        out_shape=[jax.ShapeDtypeStruct((mp, d), F32), jax.ShapeDtypeStruct((ROUTE_TILE, d), F32)],
        compiler_params=_cparams("arbitrary"),
        name="moe_combine",
    )(win, lo, hi, x2_p, x2_s, dest, g_final.reshape(1, d), *([ys] * (2 * N_EXPERTS)))


def _moe_block(x2_p, hn_p, ids_p, gates_p, rank_p, x2_s, hn_s, ids_s, gates_s, rank_s, cnt_tile, w1, w3, w2,
               g_final, tf=512):
    ids = jnp.concatenate([ids_p, ids_s])
    gates = jnp.concatenate([gates_p, gates_s])
    rank = jnp.concatenate([rank_p, rank_s])
    dest, tile_expert, tile_chunks, glist, clist, n_ffn_tiles = _moe_plan(ids, rank, cnt_tile)
    n_sub = n_ffn_tiles * (FFN_TILE // ROUTE_TILE)
    spare = ((0, 0), (0, GATHER_TOKENS - ROUTE_TILE))
    xs, gs = _moe_gather(hn_p, hn_s, jnp.pad(dest.T, spare, constant_values=-1), jnp.pad(gates.T, spare), glist,
                         n_sub)
    ys = _moe_ffn(xs, gs, w1, w3, w2, tile_expert, tile_chunks, n_ffn_tiles, tf)
    return _moe_combine(x2_p, x2_s, dest, ys, g_final, clist)


def _rms(x, g):
    return x * lax.rsqrt(jnp.mean(x * x, axis=-1, keepdims=True) + EPS) * g


def _mm(a, b, precise):
    if precise:
        return jnp.dot(a.astype(F32), b.astype(F32), preferred_element_type=F32, precision=HIGHEST)
    return jnp.dot(a.astype(BF16), b.astype(BF16), preferred_element_type=F32)


def _silu(x):
    return x * jax.nn.sigmoid(x)


def _norm_proj_kernel(x_ref, g_ref, w_ref, wt_ref, *out_refs, segs, tsegs, precise):
    xn = _rms(x_ref[...], g_ref[...])
    xn = xn if precise else xn.astype(BF16)
    for (start, width), o_ref in zip(segs, out_refs[:len(segs)]):
        for c in range(0, width, 512):
            cw = min(512, width - c)
            o_ref[:, c:c + cw] = _mm(xn, w_ref[:, start + c:start + c + cw], precise)
    for (start, height), o_ref in zip(tsegs, out_refs[len(segs):]):
        res = lax.dot_general(wt_ref[start:start + height, :].astype(xn.dtype), xn, (((1,), (1,)), ((), ())),
                              preferred_element_type=F32, precision=HIGHEST if precise else None)
        o_ref[...] = res.reshape(o_ref.shape)


def _norm_proj(x, g, w, wt, segs, tsegs, tm, rows_per_seq=None, precise=False):
    m, d = x.shape
    n = w.shape[1]
    nt = wt.shape[0]
    out_shape = [jax.ShapeDtypeStruct((m, width), F32) for _, width in segs]
    out_specs = [pl.BlockSpec((tm, width), lambda i: (i, 0)) for _, width in segs]
    for _, height, per_seq in tsegs:
        if per_seq:
            tps = rows_per_seq // tm
            out_shape.append(jax.ShapeDtypeStruct((m // rows_per_seq, height, rows_per_seq), F32))
            out_specs.append(pl.BlockSpec((1, height, tm), lambda i, tps=tps: (i // tps, 0, i % tps)))
        else:
            out_shape.append(jax.ShapeDtypeStruct((height, m), F32))
            out_specs.append(pl.BlockSpec((height, tm), lambda i: (0, i)))
    return pl.pallas_call(
        functools.partial(_norm_proj_kernel, segs=tuple(segs), tsegs=tuple(t[:2] for t in tsegs), precise=precise),
        grid=(m // tm,),
        in_specs=[
            pl.BlockSpec((tm, d), lambda i: (i, 0)),
            pl.BlockSpec((1, d), lambda i: (0, 0)),
            pl.BlockSpec((d, n), lambda i: (0, 0)),
            pl.BlockSpec((nt, d), lambda i: (0, 0)),
        ],
        out_specs=out_specs,
        out_shape=out_shape,
        compiler_params=_cparams("arbitrary"),
        name="norm_proj",
    )(x, g.reshape(1, d), w, wt)


def _mix_ffn_prologue(a_ref, b_ref, x_ref, wo_ref, g_ref, o_ref, hn_ref, precise):
    wa = a_ref.shape[1]

    @pl.when(pl.program_id(1) == 0)
    def _():
        x1 = (x_ref[...] + _mm(a_ref[...], wo_ref[0:wa, :], precise)
              + _mm(b_ref[...], wo_ref[wa:, :], precise))
        o_ref[...] = x1
        hn_ref[...] = _rms(x1, g_ref[...]).astype(hn_ref.dtype)


def _mix_ffn_step(w1_ref, w3_ref, w2_ref, o_ref, hn_ref, precise):
    wdt = F32 if precise else BF16
    w1, w3, w2 = w1_ref[...].astype(wdt), w3_ref[...].astype(wdt), w2_ref[...].astype(wdt)
    hn = hn_ref[...]
    o_ref[...] += _mm(_silu(_mm(hn, w1, precise)) * _mm(hn, w3, precise), w2, precise)


def _mix_ffn_kernel(a_ref, b_ref, x_ref, wo_ref, g_ref, w1_ref, w3_ref, w2_ref, o_ref, hn_ref, *, precise):
    _mix_ffn_prologue(a_ref, b_ref, x_ref, wo_ref, g_ref, o_ref, hn_ref, precise)
    _mix_ffn_step(w1_ref, w3_ref, w2_ref, o_ref, hn_ref, precise)


WINDOW_HEAD_BLOCKS = 2
WINDOW_STEPS = 4
N_MIX_FFN_IN, N_WINDOW_IN, N_WINDOW_OUT = 8, 7, 3


def _mix_ffn_window_kernel(*refs, precise):
    (a_ref, b_ref, x_ref, wo_ref, g_ref, w1_ref, w3_ref, w2_ref), refs = refs[:N_MIX_FFN_IN], refs[N_MIX_FFN_IN:]
    w_ins, refs = refs[:N_WINDOW_IN], refs[N_WINDOW_IN:]
    o_ref, w_outs, hn_ref = refs[0], refs[1:1 + N_WINDOW_OUT], refs[1 + N_WINDOW_OUT]
    f = pl.program_id(1)
    _mix_ffn_prologue(a_ref, b_ref, x_ref, wo_ref, g_ref, o_ref, hn_ref, precise)

    @pl.when(f < WINDOW_STEPS)
    def _():
        _sample_window_kernel(*w_ins, *w_outs, head0=(f % WINDOW_HEAD_BLOCKS) * (A_HEADS // WINDOW_HEAD_BLOCKS))
        _mix_ffn_step(w1_ref, w3_ref, w2_ref, o_ref, hn_ref, precise)

    @pl.when(f >= WINDOW_STEPS)
    def _():
        _mix_ffn_step(w1_ref, w3_ref, w2_ref, o_ref, hn_ref, precise)


def _mix_ffn(a, b, x, w_out, g, w1, w3, w2, tm, tf, precise=False, window=None, rel_bias=None):
    m, d = x.shape
    wa, wb = a.shape[1], b.shape[1]
    ff = w1.shape[1]
    in_specs = [
        pl.BlockSpec((tm, wa), lambda i, f: (i, 0)),
        pl.BlockSpec((tm, wb), lambda i, f: (i, 0)),
        pl.BlockSpec((tm, d), lambda i, f: (i, 0)),
        pl.BlockSpec((wa + wb, d), lambda i, f: (0, 0)),
        pl.BlockSpec((1, d), lambda i, f: (0, 0)),
        pl.BlockSpec((d, tf), lambda i, f: (0, f)),
        pl.BlockSpec((d, tf), lambda i, f: (0, f)),
        pl.BlockSpec((tf, d), lambda i, f: (f, 0)),
    ]
    out_specs = [pl.BlockSpec((tm, d), lambda i, f: (i, 0))]
    out_shape = [jax.ShapeDtypeStruct((m, d), F32)]
    operands = [a, b, x, w_out, g.reshape(1, d), w1, w3, w2]
    body = _mix_ffn_kernel
    if window is not None:
        assert ff // tf >= WINDOW_STEPS and window[0].shape[0] * WINDOW_HEAD_BLOCKS == (m // tm) * WINDOW_STEPS

        def block_of_step(i, f):
            p = i * WINDOW_STEPS + jnp.minimum(f, WINDOW_STEPS - 1)
            return p // WINDOW_HEAD_BLOCKS, p % WINDOW_HEAD_BLOCKS

        w_operands, w_in, w_out_specs, w_shape = _sample_window_specs(*window, rel_bias, block_of_step,
                                                                      A_HEADS // WINDOW_HEAD_BLOCKS)
        in_specs, out_specs, out_shape = in_specs + w_in, out_specs + w_out_specs, out_shape + w_shape
        operands, body = operands + w_operands, _mix_ffn_window_kernel
    res = pl.pallas_call(
        functools.partial(body, precise=precise),
        grid=(m // tm, ff // tf),
        in_specs=in_specs,
        out_specs=out_specs,
        out_shape=out_shape,
        scratch_shapes=[pltpu.VMEM((tm, d), F32 if precise else BF16)],
        compiler_params=_cparams("arbitrary", "arbitrary"),
        name="mix_ffn",
    )(*operands)
    return res[0] if window is None else res


def _mix_router_kernel(a_ref, b_ref, x_ref, wo_ref, g_ref, wr_ref, br_ref, cnt0_ref, x2_ref, hn_ref, ids_ref,
                       gate_ref, rank_ref, cb_ref, ca_ref, cnt_ref, *, precise):
    @pl.when(pl.program_id(0) == 0)
    def _():
        cnt_ref[...] = cnt0_ref[...]

    wa = a_ref.shape[1]
    x2 = (x_ref[...] + _mm(a_ref[...], wo_ref[0:wa, :], precise)
          + _mm(b_ref[...], wo_ref[wa:, :], precise))
    x2_ref[...] = x2
    hn = _rms(x2, g_ref[...])
    hn_ref[...] = hn.astype(BF16)
    wr = wr_ref[...]
    if precise:
        logits = jnp.dot(hn, wr, preferred_element_type=F32, precision=HIGHEST)
    else:
        hn_hi, wr_hi = hn.astype(BF16), wr.astype(BF16)
        hn_lo, wr_lo = (hn - hn_hi.astype(F32)).astype(BF16), (wr - wr_hi.astype(F32)).astype(BF16)
        logits = (jnp.dot(hn_hi, wr_hi, preferred_element_type=F32)
                  + (jnp.dot(hn_lo, wr_hi, preferred_element_type=F32)
                     + jnp.dot(hn_hi, wr_lo, preferred_element_type=F32)))
    lane = lax.broadcasted_iota(I32, logits.shape, 1)
    real = lane < N_EXPERTS
    biased = jnp.where(real, logits + br_ref[...], -jnp.inf)
    m1 = jnp.max(biased, axis=-1, keepdims=True)
    i1 = jnp.min(jnp.where(biased == m1, lane, 128), axis=-1, keepdims=True)
    rest = jnp.where(lane == i1, -jnp.inf, biased)
    m2 = jnp.max(rest, axis=-1, keepdims=True)
    i2 = jnp.min(jnp.where(rest == m2, lane, 128), axis=-1, keepdims=True)
    l1 = jnp.sum(jnp.where(lane == i1, logits, 0.0), axis=-1, keepdims=True)
    l2 = jnp.sum(jnp.where(lane == i2, logits, 0.0), axis=-1, keepdims=True)
    mx = jnp.maximum(l1, l2)
    e1, e2 = jnp.exp(l1 - mx), jnp.exp(l2 - mx)
    two = lax.broadcasted_iota(I32, ids_ref.shape, 1)
    ids_ref[...] = jnp.where(two == 0, i1, i2)
    gate_ref[...] = jnp.where(two == 0, e1, e2) / (e1 + e2)
    tm = logits.shape[0]
    chosen = jnp.where((lane == i1) | (lane == i2), 1.0, 0.0)
    earlier = (lax.broadcasted_iota(I32, (tm, tm), 0) > lax.broadcasted_iota(I32, (tm, tm), 1))
    before = cnt_ref[...] + jnp.dot(jnp.where(earlier, 1.0, 0.0).astype(BF16), chosen.astype(BF16),
                                    preferred_element_type=F32)
    r1 = jnp.sum(jnp.where(lane == i1, before, 0.0), axis=-1, keepdims=True)
    r2 = jnp.sum(jnp.where(lane == i2, before, 0.0), axis=-1, keepdims=True)
    rank_ref[...] = jnp.where(two == 0, r1, r2).astype(I32)
    for c in range(cb_ref.shape[1]):
        cb_ref[0, c:c + 1, :] = before[c * ROUTE_TILE:c * ROUTE_TILE + 1, :]
    cnt_ref[...] += jnp.sum(chosen, axis=0, keepdims=True)
    ca_ref[...] = cnt_ref[...]


def _mix_router(a, b, x, w_out, g, wr, br, cnt0, tm, precise=False):
    m, d = x.shape
    wa, wb = a.shape[1], b.shape[1]
    n_marks = max(1, tm // ROUTE_TILE)
    return pl.pallas_call(
        functools.partial(_mix_router_kernel, precise=precise),
        grid=(m // tm,),
        in_specs=[
            pl.BlockSpec((tm, wa), lambda i: (i, 0)),
            pl.BlockSpec((tm, wb), lambda i: (i, 0)),
            pl.BlockSpec((tm, d), lambda i: (i, 0)),
            pl.BlockSpec((wa + wb, d), lambda i: (0, 0)),
            pl.BlockSpec((1, d), lambda i: (0, 0)),
            pl.BlockSpec((d, 128), lambda i: (0, 0)),
            pl.BlockSpec((1, 128), lambda i: (0, 0)),
            pl.BlockSpec((1, 128), lambda i: (0, 0)),
        ],
        out_specs=[
            pl.BlockSpec((tm, d), lambda i: (i, 0)),
            pl.BlockSpec((tm, d), lambda i: (i, 0)),
            pl.BlockSpec((tm, 2), lambda i: (i, 0)),
            pl.BlockSpec((tm, 2), lambda i: (i, 0)),
            pl.BlockSpec((tm, 2), lambda i: (i, 0)),
            pl.BlockSpec((1, n_marks, 128), lambda i: (i, 0, 0)),
            pl.BlockSpec((1, 128), lambda i: (0, 0)),
        ],
        out_shape=[jax.ShapeDtypeStruct((m, d), F32), jax.ShapeDtypeStruct((m, d), BF16),
                   jax.ShapeDtypeStruct((m, 2), I32), jax.ShapeDtypeStruct((m, 2), F32),
                   jax.ShapeDtypeStruct((m, 2), I32), jax.ShapeDtypeStruct((m // tm, n_marks, 128), F32),
                   jax.ShapeDtypeStruct((1, 128), F32)],
        scratch_shapes=[pltpu.VMEM((1, 128), F32)],
        compiler_params=_cparams("arbitrary"),
        name="mix_router",
    )(a, b, x, w_out, g.reshape(1, d), wr, br, cnt0)


ATT_T = 512


def _t5_bucket(dist):
    max_exact = N_BUCKETS // 2
    d = jnp.maximum(dist, 1).astype(F32)
    large = max_exact + (jnp.log(d / max_exact) / math.log(A_WIN / max_exact)
                         * (N_BUCKETS - max_exact)).astype(I32)
    return jnp.where(dist < max_exact, dist, jnp.minimum(large, N_BUCKETS - 1))


def _distance_logit_table(rel_bias, max_dist):
    dist = jnp.arange(max_dist + 1, dtype=I32)
    mult = jnp.zeros((max_dist + 1,), F32)
    for window, d in A_BRANCHES:
        mult = mult + ((dist % d == 0) & (dist <= window)).astype(F32)
    bias = rel_bias[_t5_bucket(dist)].astype(F32).T
    return jnp.where(mult[None, :] > 0, bias + jnp.log(jnp.maximum(mult, 1.0))[None, :], NEG)


def _attn_kernel(q_ref, kt_ref, vt_ref, row0_ref, o_ref, kb_ref, vb_ref):
    s_len = q_ref.shape[1]
    nb = s_len // ATT_T
    period = row0_ref.shape[2]
    kb_ref[...] = kt_ref[0].astype(BF16)
    vb_ref[...] = vt_ref[0].astype(BF16)
    scale = A_HD ** -0.5
    for hl in range(2):
        hrows = slice(hl * A_HD, (hl + 1) * A_HD)
        table = pltpu.roll(jnp.broadcast_to(row0_ref[0, hl:hl + 1, :], (ATT_T, period)), 0, 1,
                           stride=1, stride_axis=0)
        for i in range(nb):
            rows = slice(i * ATT_T, (i + 1) * ATT_T)
            width = (i + 1) * ATT_T
            q = (q_ref[0, rows, hrows] * scale).astype(BF16)
            s = jnp.dot(q, kb_ref[hrows, 0:width], preferred_element_type=F32)
            s = s + table[:, (nb - 1 - i) * ATT_T:nb * ATT_T]
            m = jnp.max(s, axis=-1, keepdims=True)
            p = jnp.exp(s - m)
            den = jnp.sum(p, axis=-1, keepdims=True)
            o = lax.dot_general(p.astype(BF16), vb_ref[hrows, 0:width], (((1,), (1,)), ((), ())),
                                preferred_element_type=F32)
            o_ref[0, rows, hrows] = o / den


def _attention_prompt(q, kt, vt, rel_bias):
    b, s_len, _ = q.shape
    nb = s_len // ATT_T
    tab1 = _distance_logit_table(rel_bias, s_len)
    period = s_len + ATT_T
    z = jnp.arange(period, dtype=I32)
    delta = jnp.where(z < s_len, (nb - 1) * ATT_T - z, (nb - 1) * ATT_T + period - z)
    row0 = jnp.where(delta[None] >= 0, tab1[:, jnp.clip(delta, 0, s_len)], NEG)
    return pl.pallas_call(
        _attn_kernel,
        grid=(b, A_HEADS // 2),
        in_specs=[pl.BlockSpec((1, s_len, 2 * A_HD), lambda bi, hp: (bi, 0, hp)),
                  pl.BlockSpec((1, 2 * A_HD, s_len), lambda bi, hp: (bi, hp, 0)),
                  pl.BlockSpec((1, 2 * A_HD, s_len), lambda bi, hp: (bi, hp, 0)),
                  pl.BlockSpec((1, 2, period), lambda bi, hp: (hp, 0, 0))],
        out_specs=pl.BlockSpec((1, s_len, 2 * A_HD), lambda bi, hp: (bi, 0, hp)),
        out_shape=jax.ShapeDtypeStruct(q.shape, F32),
        scratch_shapes=[pltpu.VMEM((2 * A_HD, s_len), BF16), pltpu.VMEM((2 * A_HD, s_len), BF16)],
        compiler_params=_cparams("arbitrary", "arbitrary"),
        name="dilated_attention",
    )(q, kt, vt, row0.reshape(A_HEADS // 2, 2, period))


MLSTM_CHUNK = 256


def _log_sigmoid(x):
    return jnp.minimum(x, 0.0) - jnp.log(1.0 + jnp.exp(-jnp.abs(x)))


def _mlstm_kernel(qk_ref, v_ref, ob_ref, gc_ref, gr_ref, wc_ref, bc_ref, bifc_ref, bifr_ref, gh_ref,
                  o_ref, c_out_ref, n_out_ref, m_out_ref, xbuf_ref, c_ref, n_ref, m_ref):
    ci = pl.program_id(1)
    L = qk_ref.shape[1]

    @pl.when(ci == 0)
    def _():
        xbuf_ref[0:8, :] = jnp.zeros((8, 2 * B_W), F32)
        c_ref[...] = jnp.zeros_like(c_ref)
        n_ref[...] = jnp.zeros_like(n_ref)
        m_ref[...] = jnp.zeros_like(m_ref)

    x = qk_ref[0]
    xbuf_ref[8:8 + L, :] = x
    y = bc_ref[...] + wc_ref[3:4, :] * x
    for j in range(B_CONV - 1):
        y = y + wc_ref[j:j + 1, :] * xbuf_ref[5 + j:5 + j + L, :]
    xbuf_ref[0:8, :] = x[L - 8:L, :]
    y = _silu(y)

    gcol = gc_ref[0] + bifc_ref[...]
    grow = gr_ref[...] + bifr_ref[...]
    ri = lax.broadcasted_iota(I32, (L, L), 0)
    cj = lax.broadcasted_iota(I32, (L, L), 1)
    causal = ri >= cj
    lower = jnp.where(causal, 1.0, 0.0)
    b_col = jnp.dot(lower, _log_sigmoid(gcol), preferred_element_type=F32, precision=HIGHEST)
    b_row = lax.dot_general(_log_sigmoid(grow), lower, (((1,), (1,)), ((), ())),
                            preferred_element_type=F32, precision=HIGHEST)

    for h in range(B_HEADS):
        cols = slice(h * B_HD, (h + 1) * B_HD)
        qf = y[:, cols]
        q = qf.astype(BF16)
        kf = y[:, B_W + h * B_HD:B_W + (h + 1) * B_HD] * (B_HD ** -0.5)
        k = kf.astype(BF16)
        v = v_ref[0, :, cols].astype(BF16)
        bc = b_col[:, B_HEADS + h:B_HEADS + h + 1]
        ic = gcol[:, h:h + 1]
        br = b_row[B_HEADS + h:B_HEADS + h + 1, :]
        ir = grow[h:h + 1, :]
        m_prev = m_ref[h:h + 1, 0:1]
        dm = jnp.where(causal, bc - br + ir, NEG)
        inter = bc + m_prev
        mt = jnp.maximum(inter, jnp.max(dm, axis=1, keepdims=True))
        w_intra = jnp.exp(dm - mt)
        w_inter = jnp.exp(inter - mt)
        a = w_intra * lax.dot_general(q, k, (((1,), (1,)), ((), ())), preferred_element_type=F32)
        c_prev = c_ref[h]
        n_prev = n_ref[h:h + 1, :]
        num = (jnp.dot(a.astype(BF16), v, preferred_element_type=F32)
               + w_inter * jnp.dot(q, c_prev.astype(BF16), preferred_element_type=F32))
        den = (jnp.sum(a, axis=1, keepdims=True)
               + w_inter * jnp.sum(qf * n_prev, axis=1, keepdims=True))
        hb = num / jnp.maximum(jnp.abs(den), jnp.exp(-mt))
        m_new = mt[L - 1:L, :]
        b_last = bc[L - 1:L, :]
        g = jnp.exp(b_last - bc + ic - m_new)
        decay = jnp.exp(b_last + m_prev - m_new)
        kg = kf * g
        c_ref[h] = decay * c_prev + lax.dot_general(kg.astype(BF16), v, (((0,), (0,)), ((), ())),
                                                    preferred_element_type=F32)
        n_ref[h:h + 1, :] = decay * n_prev + jnp.sum(kg, axis=0, keepdims=True)
        m_ref[h:h + 1, :] = jnp.broadcast_to(m_new, (1, 128))
        hb = hb * lax.rsqrt(jnp.mean(hb * hb, axis=-1, keepdims=True) + EPS) * gh_ref[:, cols]
        o_ref[0, :, cols] = jax.nn.sigmoid(ob_ref[0, :, cols]) * hb

    @pl.when(ci == pl.num_programs(1) - 1)
    def _():
        c_out_ref[0] = c_ref[...]
        n_out_ref[0] = n_ref[...]
        m_out_ref[0] = m_ref[...]


def _mlstm_prompt(qk, v, ob, gcol, grow, w_conv, b_conv, b_if, g_head):
    b, s_len, _ = qk.shape
    L = min(MLSTM_CHUNK, s_len)
    nc = s_len // L
    bif_col = jnp.zeros((1, 128), F32).at[0, :2 * B_HEADS].set(b_if)
    bif_row = b_if.reshape(2 * B_HEADS, 1)
    seq = lambda w: pl.BlockSpec((1, L, w), lambda bi, ci: (bi, ci, 0))
    full = lambda shape: pl.BlockSpec(shape, lambda bi, ci: (0,) * len(shape))
    return pl.pallas_call(
        _mlstm_kernel,
        grid=(b, nc),
        in_specs=[seq(2 * B_W), seq(B_W), seq(B_W), seq(128),
                  pl.BlockSpec((2 * B_HEADS, L), lambda bi, ci: (0, bi * nc + ci)),
                  full((B_CONV, 2 * B_W)), full((1, 2 * B_W)), full((1, 128)), full((2 * B_HEADS, 1)),
                  full((1, B_W))],
        out_specs=[seq(B_W),
                   pl.BlockSpec((1, B_HEADS, B_HD, B_HD), lambda bi, ci: (bi, 0, 0, 0)),
                   pl.BlockSpec((1, 8, B_HD), lambda bi, ci: (bi, 0, 0)),
                   pl.BlockSpec((1, 8, 128), lambda bi, ci: (bi, 0, 0))],
        out_shape=[jax.ShapeDtypeStruct((b, s_len, B_W), F32),
                   jax.ShapeDtypeStruct((b, B_HEADS, B_HD, B_HD), F32),
                   jax.ShapeDtypeStruct((b, 8, B_HD), F32),
                   jax.ShapeDtypeStruct((b, 8, 128), F32)],
        scratch_shapes=[pltpu.VMEM((L + 8, 2 * B_W), F32), pltpu.VMEM((B_HEADS, B_HD, B_HD), F32),
                        pltpu.VMEM((8, B_HD), F32), pltpu.VMEM((8, 128), F32)],
        compiler_params=_cparams("arbitrary", "arbitrary"),
        name="mlstm_chunkwise",
    )(qk, v, ob, gcol, grow, w_conv, b_conv.reshape(1, -1), bif_col, bif_row, g_head.reshape(1, -1))


def _odd_in_kernel(x_ref, g_ref, w_ref, gcv_ref, bcv_ref, ws_ref, bs_ref, c_ref, xd_ref):
    tm = x_ref.shape[0]
    xn = _rms(x_ref[...], g_ref[...]).astype(BF16)
    u = jnp.dot(xn, w_ref[:, 0:C_W], preferred_element_type=F32)
    v = jnp.dot(xn, w_ref[:, C_W:2 * C_W], preferred_element_type=F32)
    xd_ref[...] = jnp.dot(xn, w_ref[:, 2 * C_W:], preferred_element_type=F32)
    mu = jnp.mean(v, axis=-1, keepdims=True)
    vc = v - mu
    var = jnp.mean(vc * vc, axis=-1, keepdims=True)
    vl = (vc * lax.rsqrt(var + EPS) * gcv_ref[...] + bcv_ref[...]).astype(BF16)
    ri = lax.broadcasted_iota(I32, (C_CHUNK, C_CHUNK), 0)
    cj = lax.broadcasted_iota(I32, (C_CHUNK, C_CHUNK), 1)
    for g in range(C_GROUPS):
        cols = slice(g * C_GD, (g + 1) * C_GD)
        wg = jnp.where(ri >= cj, ws_ref[g], 0.0).astype(BF16)
        for c in range(tm // C_CHUNK):
            rows = slice(c * C_CHUNK, (c + 1) * C_CHUNK)
            s = jnp.dot(wg, vl[rows, cols], preferred_element_type=F32) + bs_ref[:, g:g + 1]
            c_ref[rows, cols] = u[rows, cols] * s


def _odd_in(x, g, w, g_cv, b_cv, w_s, b_s, tm):
    m, d = x.shape
    n = w.shape[1]
    full = lambda shape: pl.BlockSpec(shape, lambda i: (0,) * len(shape))
    return pl.pallas_call(
        _odd_in_kernel,
        grid=(m // tm,),
        in_specs=[pl.BlockSpec((tm, d), lambda i: (i, 0)), full((1, d)), full((d, n)), full((1, C_W)),
                  full((1, C_W)), full((C_GROUPS, C_CHUNK, C_CHUNK)), full((C_CHUNK, C_GROUPS))],
        out_specs=[pl.BlockSpec((tm, C_W), lambda i: (i, 0)), pl.BlockSpec((tm, D_W), lambda i: (i, 0))],
        out_shape=[jax.ShapeDtypeStruct((m, C_W), F32), jax.ShapeDtypeStruct((m, D_W), F32)],
        compiler_params=_cparams("arbitrary"),
        name="odd_in_proj_gate",
    )(x, g.reshape(1, d), w, g_cv.reshape(1, -1), b_cv.reshape(1, -1), w_s, b_s.T)


S5_TT = 128
S5_PAD = 8
S5_HALF = D_GROUPS * D_STATE // 2


def _s5_discretize(a_re, a_im, log_dt, bm_re, bm_im, cm_re, cm_im):
    a_re, a_im = a_re.astype(F32), a_im.astype(F32)
    dt = jnp.exp(log_dt.astype(F32))[:, None]
    mag = jnp.exp(a_re * dt)
    ab_re = mag * jnp.cos(a_im * dt)
    ab_im = mag * jnp.sin(a_im * dt)
    inv = 1.0 / (a_re * a_re + a_im * a_im)
    f_re = ((ab_re - 1.0) * a_re + ab_im * a_im) * inv
    f_im = (ab_im * a_re - (ab_re - 1.0) * a_im) * inv
    bm_re, bm_im = bm_re.astype(F32), bm_im.astype(F32)
    bb_re = f_re[..., None] * bm_re - f_im[..., None] * bm_im
    bb_im = f_re[..., None] * bm_im + f_im[..., None] * bm_re
    return ab_re, ab_im, bb_re, bb_im


def _s5_matrices(bb_re, bb_im, cm_re, cm_im):
    gh = D_GROUPS // 2
    eye = jnp.eye(gh, dtype=F32)

    def in_half(bb):
        return jnp.einsum('gpc,gh->gchp', bb, eye).reshape(gh * D_GCH, gh * D_STATE)

    def out_half(cm):
        return jnp.einsum('gcp,gh->gphc', cm, eye).reshape(gh * D_STATE, gh * D_GCH)

    bd = jnp.stack([jnp.concatenate([in_half(bb_re[h * gh:(h + 1) * gh]), in_half(bb_im[h * gh:(h + 1) * gh])], axis=1)
                    for h in range(2)])
    cm = jnp.stack([jnp.concatenate([out_half(cm_re[h * gh:(h + 1) * gh].astype(F32)),
                                     -out_half(cm_im[h * gh:(h + 1) * gh].astype(F32))], axis=0)
                    for h in range(2)])
    return bd, cm


def _gelu_tanh(x):
    return 0.5 * x * (1.0 + jnp.tanh(math.sqrt(2.0 / math.pi) * (x + 0.044715 * (x * x * x))))


def _s5_kernel(x_ref, bd_ref, cm_ref, ar_ref, ai_ref, dsk_ref, wg_ref, bg_ref, o_ref, hr_out, hi_out,
               bu_ref, hs_ref, hr_ref, hi_ref):
    i = pl.program_id(0)
    nb, tt, _ = x_ref.shape
    stride = tt + S5_PAD
    half_in = D_W // 2
    nct = S5_HALF // 128

    @pl.when(i == 0)
    def _():
        hr_ref[...] = jnp.zeros_like(hr_ref)
        hi_ref[...] = jnp.zeros_like(hi_ref)

    x = x_ref[...].reshape(nb * tt, D_W)
    ys = []
    for h in range(2):
        lanes = slice(h * S5_HALF, (h + 1) * S5_HALF)
        bu = jnp.dot(x[:, h * half_in:(h + 1) * half_in].astype(BF16), bd_ref[h], preferred_element_type=F32)
        for c in range(2 * nct):
            for b in range(nb):
                bu_ref[c, b * stride:b * stride + tt, :] = bu[b * tt:(b + 1) * tt, c * 128:(c + 1) * 128]
        ar = jnp.broadcast_to(ar_ref[:, lanes], (nb, S5_HALF))
        ai = jnp.broadcast_to(ai_ref[:, lanes], (nb, S5_HALF))

        def step(t, carry):
            hr, hi = carry
            rows = pl.ds(t, nb, stride=stride)
            bur = jnp.concatenate([bu_ref[c, rows, :] for c in range(nct)], axis=1)
            bui = jnp.concatenate([bu_ref[nct + c, rows, :] for c in range(nct)], axis=1)
            nhr = ar * hr - ai * hi + bur
            nhi = ar * hi + ai * hr + bui
            for c in range(nct):
                hs_ref[c, rows, :] = nhr[:, c * 128:(c + 1) * 128]
                hs_ref[nct + c, rows, :] = nhi[:, c * 128:(c + 1) * 128]
            return nhr, nhi

        hr, hi = lax.fori_loop(0, tt, step, (hr_ref[:, lanes], hi_ref[:, lanes]), unroll=2)
        hr_ref[:, lanes] = hr
        hi_ref[:, lanes] = hi
        hs = jnp.concatenate(
            [jnp.concatenate([hs_ref[c, b * stride:b * stride + tt, :] for b in range(nb)], axis=0)
             for c in range(2 * nct)], axis=1)
        ys.append(jnp.dot(hs.astype(BF16), cm_ref[h], preferred_element_type=F32))
    y = jnp.concatenate(ys, axis=-1) + dsk_ref[...] * x
    g = _gelu_tanh(y)
    d_out = g * jax.nn.sigmoid(jnp.dot(g.astype(BF16), wg_ref[...], preferred_element_type=F32) + bg_ref[...])
    o_ref[...] = d_out.reshape(nb, tt, D_W)

    @pl.when(i == pl.num_programs(0) - 1)
    def _():
        hr_out[...] = hr_ref[...]
        hi_out[...] = hi_ref[...]


def _s5_prompt(xd, bd, cm, ab_re, ab_im, d_skip, w_glu, b_glu):
    nb, t_len, _ = xd.shape
    tt = min(S5_TT, t_len)
    n_state = D_GROUPS * D_STATE
    full = lambda shape: pl.BlockSpec(shape, lambda i: (0,) * len(shape))
    return pl.pallas_call(
        _s5_kernel,
        grid=(t_len // tt,),
        in_specs=[pl.BlockSpec((nb, tt, D_W), lambda i: (0, i, 0)), full(bd.shape), full(cm.shape),
                  full((1, n_state)), full((1, n_state)), full((1, D_W)), full((D_W, D_W)), full((1, D_W))],
        out_specs=[pl.BlockSpec((nb, tt, D_W), lambda i: (0, i, 0)), full((nb, n_state)), full((nb, n_state))],
        out_shape=[jax.ShapeDtypeStruct(xd.shape, F32), jax.ShapeDtypeStruct((nb, n_state), F32),
                   jax.ShapeDtypeStruct((nb, n_state), F32)],
        scratch_shapes=[pltpu.VMEM((2 * S5_HALF // 128, nb * (tt + S5_PAD), 128), F32),
                        pltpu.VMEM((2 * S5_HALF // 128, nb * (tt + S5_PAD), 128), F32),
                        pltpu.VMEM((nb, n_state), F32), pltpu.VMEM((nb, n_state), F32)],
        compiler_params=_cparams("arbitrary"),
        name="s5_scan_glu",
    )(xd, bd, cm, ab_re.reshape(1, n_state), ab_im.reshape(1, n_state), d_skip.reshape(1, D_W), w_glu,
      b_glu.reshape(1, D_W))


def _hdot(a, b):
    return jnp.dot(a, b, precision=HIGHEST, preferred_element_type=F32)


def _sample_window_kernel(q_ref, kn_ref, vn_ref, tab_ref, tabn_ref, kt_ref, vt_ref, o_ref, okt_ref, ovt_ref,
                          head0=0):
    L = kt_ref.shape[3]
    last = lax.broadcasted_iota(I32, (A_HD, L), 1) == L - 1
    for h in range(kt_ref.shape[1]):
        kh, vh = kt_ref[0, h], vt_ref[0, h]
        kn, vn = kn_ref[0, h], vn_ref[0, h]
        q = q_ref[0, h] * (A_HD ** -0.5)
        s = jnp.sum(kh * q, axis=0, keepdims=True) + tab_ref[head0 + h]
        s_new = jnp.sum(kn * q, axis=0, keepdims=True) + tabn_ref[head0 + h]
        m = jnp.maximum(jnp.max(s, axis=1, keepdims=True), s_new)
        p = jnp.exp(s - m)
        p_new = jnp.exp(s_new - m)
        den = jnp.sum(p, axis=1, keepdims=True) + p_new
        o_ref[0, h] = (jnp.sum(vh * p, axis=1, keepdims=True) + vn * p_new) / den
        okt_ref[0, h] = jnp.where(last, kn, pltpu.roll(kh, L - 1, 1))
        ovt_ref[0, h] = jnp.where(last, vn, pltpu.roll(vh, L - 1, 1))


def _sample_window_attention(q, k_new, v_new, cache_kt, cache_vt, rel_bias):
    operands, in_specs, out_specs, out_shape = _sample_window_specs(q, k_new, v_new, cache_kt, cache_vt, rel_bias,
                                                                    lambda i: (i, 0))
    return pl.pallas_call(
        _sample_window_kernel,
        grid=(q.shape[0],),
        in_specs=in_specs,
        out_specs=out_specs,
        out_shape=out_shape,
        compiler_params=_cparams("arbitrary"),
        name="sample_window_attention",
    )(*operands)


def _sample_window_specs(q, k_new, v_new, cache_kt, cache_vt, rel_bias, block_of_step, heads_per_step=A_HEADS):
    n, _, _, L = cache_kt.shape
    assert L == A_WIN
    tab1 = _distance_logit_table(rel_bias, L)
    tab = tab1[:, :0:-1].reshape(A_HEADS, 1, L)
    tab_new = tab1[:, 0].reshape(A_HEADS, 1, 1)
    col = pl.BlockSpec((1, heads_per_step, A_HD, 1), lambda *g: (*block_of_step(*g), 0, 0))
    win = pl.BlockSpec((1, heads_per_step, A_HD, L), lambda *g: (*block_of_step(*g), 0, 0))
    full = lambda shape: pl.BlockSpec(shape, lambda *g: (0,) * len(shape))
    out_shape = [jax.ShapeDtypeStruct((n, A_HEADS, A_HD, 1), F32),
                 jax.ShapeDtypeStruct(cache_kt.shape, F32), jax.ShapeDtypeStruct(cache_vt.shape, F32)]
    return ([q, k_new, v_new, tab, tab_new, cache_kt, cache_vt],
            [col, col, col, full(tab.shape), full(tab_new.shape), win, win], [col, win, win], out_shape)


def _sample_even_proj(x, g_mix, w_in):
    n_main = 3 * A_W + 4 * B_W
    segs = ((0, A_W), (A_W, A_W), (2 * A_W, A_W), (3 * A_W, 2 * B_W), (3 * A_W + 2 * B_W, B_W),
            (3 * A_W + 3 * B_W, B_W), (n_main, 2 * B_HEADS))
    return _norm_proj(x, g_mix, w_in, jnp.zeros((8, x.shape[1]), F32), segs, (), tm=x.shape[0], precise=True)


def _sample_mlstm_step(qkb, vb, ob, gates, c0, n0, m0, conv0, b_if, w_conv, b_conv, g_head):
    n = qkb.shape[0]
    xp = jnp.concatenate([conv0, qkb[:, None, :]], axis=1)
    qk = _silu(b_conv + jnp.sum(xp * w_conv[None], axis=1))
    q = qk[:, :B_W].reshape(n, B_HEADS, B_HD)
    k = qk[:, B_W:].reshape(n, B_HEADS, B_HD) * (B_HD ** -0.5)
    v = vb.reshape(n, B_HEADS, B_HD)
    ig = gates[:, :B_HEADS] + b_if[:B_HEADS]
    logf = _log_sigmoid(gates[:, B_HEADS:] + b_if[B_HEADS:])
    inter = logf + m0
    mt = jnp.maximum(inter, ig)
    w_intra = jnp.exp(ig - mt)
    w_inter = jnp.exp(inter - mt)
    a = w_intra * jnp.sum(q * k, axis=-1)
    num = a[..., None] * v + w_inter[..., None] * jnp.einsum('nhk,nhkv->nhv', q, c0, precision=HIGHEST)
    den = a + w_inter * jnp.sum(q * n0, axis=-1)
    hb = num / jnp.maximum(jnp.abs(den), jnp.exp(-mt))[..., None]
    g = jnp.exp(ig - mt)
    decay = jnp.exp(inter - mt)
    c_new = decay[..., None, None] * c0 + (g[..., None] * k)[..., :, None] * v[..., None, :]
    n_new = decay[..., None] * n0 + g[..., None] * k
    hb = hb * lax.rsqrt(jnp.mean(hb * hb, axis=-1, keepdims=True) + EPS) * g_head.reshape(B_HEADS, B_HD)
    b_out = jax.nn.sigmoid(ob) * hb.reshape(n, B_W)
    return b_out, c_new, n_new, mt, xp[:, 1:]


def _sample_odd_mixers(x, h_re0, h_im0, g_mix, w_in, g_cv, b_cv, w_s, b_s, ab_re, ab_im, bb_re, bb_im,
                       cm_re, cm_im, d_skip, w_glu, b_glu):
    n = x.shape[0]
    h = _rms(x, g_mix)
    proj = _hdot(h, w_in)
    u, v, xd = proj[:, :C_W], proj[:, C_W:2 * C_W], proj[:, 2 * C_W:]
    mu = jnp.mean(v, axis=-1, keepdims=True)
    vc = v - mu
    v = vc * lax.rsqrt(jnp.mean(vc * vc, axis=-1, keepdims=True) + EPS) * g_cv + b_cv
    s = (w_s[:, 0, 0][None, :, None] * v.reshape(n, C_GROUPS, C_GD) + b_s[:, 0][None, :, None]).reshape(n, C_W)
    c_out = u * s
    xg = xd.reshape(n, D_GROUPS, D_GCH)
    bu_re = jnp.einsum('gpc,ngc->ngp', bb_re, xg, precision=HIGHEST)
    bu_im = jnp.einsum('gpc,ngc->ngp', bb_im, xg, precision=HIGHEST)
    hr = ab_re * h_re0 - ab_im * h_im0 + bu_re
    hi = ab_re * h_im0 + ab_im * h_re0 + bu_im
    y = (jnp.einsum('gcp,ngp->ngc', cm_re, hr, precision=HIGHEST)
         - jnp.einsum('gcp,ngp->ngc', cm_im, hi, precision=HIGHEST) + d_skip * xg)
    g = _gelu_tanh(y.reshape(n, D_W))
    d_out = g * jax.nn.sigmoid(_hdot(g, w_glu) + b_glu)
    return c_out, d_out, v, hr, hi


_EVEN_SEGS = ((0, A_W), (A_W, 2 * B_W), (A_W + 2 * B_W, B_W), (A_W + 3 * B_W, B_W), (A_W + 4 * B_W, 128))


def kernel(x_prompt, x_sample, cache_a_k, cache_a_v, state_b_c, state_b_n, state_b_m, state_b_conv,
           state_d_re, state_d_im, rel_bias, g_mix, g_ffn, g_final, w_in_e, b_if, w_conv_b, b_conv_b,
           g_bhead, w_out_e, w1_e, w3_e, w2_e, w_in_o, g_cv, b_cv, w_s, b_s, a_re, a_im, log_dt,
           bm_re, bm_im, cm_re, cm_im, d_skip, w_glu, b_glu, w_out_o, w_router, b_router,
           w1_m, w3_m, w2_m):
    bp, sp, d = x_prompt.shape
    ns = x_sample.shape[0]
    mp = bp * sp
    xp = x_prompt.reshape(mp, d)
    xs = x_sample.reshape(ns, d)

    w_in = w_in_e[0]
    n_main = 3 * A_W + 4 * B_W
    w_gate = w_in[:, n_main:]
    w_cat = jnp.concatenate([w_in[:, :A_W], w_in[:, 3 * A_W:n_main],
                             jnp.pad(w_gate, ((0, 0), (0, 128 - 2 * B_HEADS)))], axis=1)
    w_t = jnp.concatenate([w_in[:, A_W:3 * A_W], w_gate], axis=1).T
    q, qkb, vb, ob, gcol, kt, vt, grow = _norm_proj(
        xp, g_mix[0], w_cat.astype(BF16), w_t.astype(BF16), _EVEN_SEGS,
        ((0, A_W, True), (A_W, A_W, True), (2 * A_W, 2 * B_HEADS, False)), tm=512, rows_per_seq=sp)
    shp = lambda a: a.reshape(bp, sp, a.shape[-1])
    pos_minor = lambda c: jnp.transpose(c, (0, 2, 3, 1))
    pos_major = lambda c: jnp.transpose(c, (0, 3, 1, 2))
    s_proj = _sample_even_proj(xs, g_mix[0], w_in)
    cols = lambda a: a.reshape(ns, A_HEADS, A_HD, 1)
    s_attn = (cols(s_proj[0]), cols(s_proj[1]), cols(s_proj[2]), pos_minor(cache_a_k[0]), pos_minor(cache_a_v[0]))
    a_out = _attention_prompt(shp(q), kt, vt, rel_bias)
    b_out, pc, pn, pm = _mlstm_prompt(shp(qkb), shp(vb), shp(ob), shp(gcol), grow, w_conv_b[0], b_conv_b[0],
                                      b_if[0], g_bhead[0])
    ffn_args = (a_out.reshape(mp, A_W), b_out.reshape(mp, B_W), xp, w_out_e[0].astype(BF16), g_ffn[0],
                w1_e[0].astype(BF16), w3_e[0].astype(BF16), w2_e[0].astype(BF16))
    ffn_tm, ffn_tf = 1024, 512
    if ns * WINDOW_HEAD_BLOCKS == (mp // ffn_tm) * WINDOW_STEPS:
        x1p, sa, s_kt, s_vt = _mix_ffn(*ffn_args, tm=ffn_tm, tf=ffn_tf, window=s_attn, rel_bias=rel_bias)
    else:
        x1p = _mix_ffn(*ffn_args, tm=ffn_tm, tf=ffn_tf)
        sa, s_kt, s_vt = _sample_window_attention(*s_attn, rel_bias)
    sa = sa.reshape(ns, A_W)
    s_k, s_v = pos_major(s_kt), pos_major(s_vt)

    sb, sc, sn, sm, s_conv = _sample_mlstm_step(
        s_proj[3], s_proj[4], s_proj[5], s_proj[6], state_b_c[0], state_b_n[0], state_b_m[0], state_b_conv[0],
        b_if[0], w_conv_b[0], b_conv_b[0], g_bhead[0])
    x1s = _mix_ffn(sa, sb, xs, w_out_e[0], g_ffn[0], w1_e[0], w3_e[0], w2_e[0], tm=ns, tf=512, precise=True)

    ab_re, ab_im, bb_re, bb_im = _s5_discretize(a_re[0], a_im[0], log_dt[0], bm_re[0], bm_im[0], cm_re[0],
                                                cm_im[0])
    bd, cmat = _s5_matrices(bb_re, bb_im, cm_re[0], cm_im[0])
    c_out, xd = _odd_in(x1p, g_mix[1], w_in_o[0].astype(BF16), g_cv[0], b_cv[0], w_s[0], b_s[0], tm=512)
    d_out, p_hr, p_hi = _s5_prompt(xd.reshape(bp, sp, D_W), bd.astype(BF16), cmat.astype(BF16), ab_re, ab_im,
                                   d_skip[0], w_glu[0].astype(BF16), b_glu[0])
    wr = jnp.pad(w_router[0].astype(F32), ((0, 0), (0, 128 - N_EXPERTS)))
    br = jnp.pad(b_router[0].astype(F32), (0, 128 - N_EXPERTS)).reshape(1, 128)
    x2p, hnp, idp, gtp, rkp, cbp, cap = _mix_router(c_out, d_out.reshape(mp, D_W), x1p, w_out_o[0].astype(BF16),
                                                    g_ffn[1], wr, br, jnp.zeros((1, 128), F32), tm=512)

    sc_out, sd_out, s_cv, s_hr, s_hi = _sample_odd_mixers(
        x1s, state_d_re[0], state_d_im[0], g_mix[1], w_in_o[0], g_cv[0], b_cv[0], w_s[0], b_s[0], ab_re, ab_im,
        bb_re, bb_im, cm_re[0], cm_im[0], d_skip[0], w_glu[0], b_glu[0])
    x2s, hns, ids_s, gts, rks, cbs, cas = _mix_router(sc_out, sd_out, x1s, w_out_o[0], g_ffn[1], wr, br, cap,
                                                      tm=ns, precise=True)

    pad = ((0, ROUTE_TILE - ns), (0, 0))
    cnt_tile = jnp.concatenate([cbp.reshape(-1, 128), cbs.reshape(-1, 128), cas])[:, :N_EXPERTS].astype(I32)
    y_p, y_s = _moe_block(x2p, hnp, idp, gtp, rkp, jnp.pad(x2s, pad), jnp.pad(hns, pad),
                          jnp.pad(ids_s, pad, constant_values=-1), jnp.pad(gts, pad), jnp.pad(rks, pad),
                          cnt_tile, w1_m[0], w3_m[0], w2_m[0], g_final)
    y_prompt = y_p.reshape(bp, sp, d)
    y_sample = y_s[:ns].reshape(ns, 1, d)

    e = lambda a: a[None]
    p_k = pos_major(kt.reshape(bp, A_HEADS, A_HD, sp))
    p_v = pos_major(vt.reshape(bp, A_HEADS, A_HD, sp))
    p_conv = qkb.reshape(bp, sp, 2 * B_W)[:, sp - (B_CONV - 1):]
    return (y_prompt, y_sample, e(p_k), e(p_v), e(s_k), e(s_v), e(pc), e(sc), e(pn[:, :B_HEADS]), e(sn),
            e(pm[:, :B_HEADS, 0]), e(sm), e(p_conv), e(s_conv), e(s_cv.reshape(ns, 1, C_W)),
            e(p_hr.reshape(bp, D_GROUPS, D_STATE)), e(s_hr), e(p_hi.reshape(bp, D_GROUPS, D_STATE)), e(s_hi))
```

```python
import functools
import math

import jax
import jax.numpy as jnp
from jax import lax
from jax.experimental import pallas as pl
from jax.experimental.pallas import tpu as pltpu

F32 = jnp.float32
BF16 = jnp.bfloat16
I32 = jnp.int32
HIGHEST = lax.Precision.HIGHEST

D_MODEL = 1024
A_HEADS, A_HD = 8, 64
A_W = A_HEADS * A_HD
A_BRANCHES = ((128, 1), (512, 4), (2048, 16))
A_WIN = 2048
N_BUCKETS = 32
B_HEADS, B_HD = 4, 128
B_W = B_HEADS * B_HD
B_CONV = 4
C_GROUPS, C_GD = 4, 128
C_W = C_GROUPS * C_GD
C_CHUNK = 128
D_GROUPS, D_GCH, D_STATE = 32, 16, 64
D_W = D_GROUPS * D_GCH
D_FF = 3584
N_EXPERTS = 8
TOP_K = 2
EPS = 1e-6
NEG = -1e30

VMEM_LIMIT = 56 * 1024 * 1024

ROUTE_TILE = 256
FFN_TILE = 1024
FFN_ROW_CHUNK = 256
GATHER_TOKENS = 2 * ROUTE_TILE
COMBINE_ALIGN = 16


def _cparams(*sem):
    return pltpu.CompilerParams(dimension_semantics=sem, vmem_limit_bytes=VMEM_LIMIT)


def _moe_plan(ids, rank, cnt_tile):
    mt = ids.shape[0]
    n_assign = 2 * mt
    n_ffn_tiles = -(-(n_assign + N_EXPERTS * (FFN_TILE - 1)) // FFN_TILE) + 1
    n_sub = n_ffn_tiles * (FFN_TILE // ROUTE_TILE)
    counts = cnt_tile[-1]
    seg = ((counts + FFN_TILE - 1) // FFN_TILE) * FFN_TILE
    seg_end = jnp.cumsum(seg)
    seg_off = seg_end - seg
    off_of = jnp.sum(jnp.where(ids[..., None] == jnp.arange(N_EXPERTS, dtype=I32), seg_off, 0), axis=-1)
    dest = jnp.where(ids >= 0, off_of + rank, -1)
    total = seg_end[-1]
    tile_start = jnp.arange(n_ffn_tiles, dtype=I32) * FFN_TILE
    tile_expert = jnp.minimum(jnp.sum((tile_start[:, None] >= seg_end[None, :]).astype(I32), axis=1),
                              N_EXPERTS - 1).astype(I32)
    rows_left = (seg_off + counts)[tile_expert] - tile_start
    tile_chunks = jnp.where(tile_start < total,
                            jnp.clip(-(-rows_left // FFN_ROW_CHUNK), 0, FFN_TILE // FFN_ROW_CHUNK), 0).astype(I32)
    lo = seg_off[None, :] + cnt_tile[:-1]
    hi = seg_off[None, :] + cnt_tile[1:]
    sub_start = jnp.arange(n_sub, dtype=I32) * ROUTE_TILE
    sub_expert = jnp.minimum(jnp.sum((sub_start[:, None] >= seg_end[None, :]).astype(I32), axis=1),
                             N_EXPERTS - 1)
    lo_e, hi_e = lo.T[sub_expert], hi.T[sub_expert]
    jlo = jnp.sum((hi_e <= sub_start[:, None]).astype(I32), axis=1).astype(I32)
    jhi = (jnp.sum((lo_e < sub_start[:, None] + ROUTE_TILE).astype(I32), axis=1) - 1).astype(I32)
    win = lo // COMBINE_ALIGN
    flat1 = lambda a: a.reshape(-1).astype(I32)
    return dest, tile_expert, tile_chunks, (jlo, jhi), (flat1(win), flat1(lo), flat1(hi)), n_ffn_tiles


def _moe_gather_kernel(jlo_ref, jhi_ref, hp_hbm, hs_hbm, dt_ref, gt_ref, xs_ref, gs_ref,
                       hn_ref, acc_ref, g_ref, sem):
    s = pl.program_id(0)
    mp = hp_hbm.shape[0]

    @pl.when(s == 0)
    def _():
        copies = (pltpu.make_async_copy(hp_hbm, hn_ref.at[0:mp], sem.at[0]),
                  pltpu.make_async_copy(hs_hbm, hn_ref.at[mp:mp + ROUTE_TILE], sem.at[1]))
        for c in copies:
            c.start()
        hn_ref[mp + ROUTE_TILE:, :] = jnp.zeros((ROUTE_TILE, hn_ref.shape[1]), BF16)
        for c in copies:
            c.wait()

    acc_ref[...] = jnp.zeros_like(acc_ref)
    g_ref[...] = jnp.zeros_like(g_ref)
    rows = lax.broadcasted_iota(I32, (ROUTE_TILE, GATHER_TOKENS), 0) + s * ROUTE_TILE
    jlo = jlo_ref[s]

    def body(it, carry):
        off = pl.multiple_of(jlo * ROUTE_TILE + it * GATHER_TOKENS, ROUTE_TILE)
        dt = dt_ref[:, pl.ds(off, GATHER_TOKENS)]
        gt = gt_ref[:, pl.ds(off, GATHER_TOKENS)]
        m1 = dt[0:1, :] == rows
        m2 = dt[1:2, :] == rows
        onehot = jnp.where(m1 | m2, 1.0, 0.0).astype(BF16)
        acc_ref[...] += jnp.dot(onehot, hn_ref[pl.ds(off, GATHER_TOKENS), :], preferred_element_type=F32)
        g_ref[...] += jnp.sum(jnp.where(m1, gt[0:1, :], 0.0) + jnp.where(m2, gt[1:2, :], 0.0),
                              axis=1, keepdims=True)
        return carry

    tiles_per_iter = GATHER_TOKENS // ROUTE_TILE
    lax.fori_loop(0, (jhi_ref[s] - jlo + tiles_per_iter) // tiles_per_iter, body, 0)
    xs_ref[...] = acc_ref[...].astype(BF16)
    gs_ref[...] = g_ref[...]


def _moe_gather(hn_p, hn_s, dest_t, gates_t, glist, n_sub):
    jlo, jhi = glist
    mp, d = hn_p.shape
    mt = dest_t.shape[1]
    grid_spec = pltpu.PrefetchScalarGridSpec(
        num_scalar_prefetch=2,
        grid=(n_sub,),
        in_specs=[
            pl.BlockSpec(memory_space=pl.ANY),
            pl.BlockSpec(memory_space=pl.ANY),
            pl.BlockSpec((2, mt), lambda s, *_: (0, 0)),
            pl.BlockSpec((2, mt), lambda s, *_: (0, 0)),
        ],
        out_specs=[
            pl.BlockSpec((ROUTE_TILE, d), lambda s, *_: (s, 0)),
            pl.BlockSpec((ROUTE_TILE, 1), lambda s, *_: (s, 0)),
        ],
        scratch_shapes=[pltpu.VMEM((mp + 2 * ROUTE_TILE, d), BF16), pltpu.VMEM((ROUTE_TILE, d), F32),
                        pltpu.VMEM((ROUTE_TILE, 1), F32), pltpu.SemaphoreType.DMA((2,))],
    )
    return pl.pallas_call(
        _moe_gather_kernel,
        grid_spec=grid_spec,
        out_shape=[jax.ShapeDtypeStruct((n_sub * ROUTE_TILE, d), BF16),
                   jax.ShapeDtypeStruct((n_sub * ROUTE_TILE, 1), F32)],
        compiler_params=_cparams("arbitrary"),
        name="moe_gather",
    )(jlo, jhi, hn_p, hn_s, dest_t, gates_t)


def _moe_ffn_kernel(te_ref, nc_ref, x_ref, gs_ref, w1_ref, w3_ref, w2_ref, y_ref, acc_ref):
    t, f = pl.program_id(0), pl.program_id(1)
    nf = pl.num_programs(1)
    n_chunks = nc_ref[t]
    full = FFN_TILE // FFN_ROW_CHUNK

    @pl.when(n_chunks > 0)
    def _():
        @pl.when(f == 0)
        def _():
            acc_ref[...] = jnp.zeros_like(acc_ref)

        w1, w3, w2 = w1_ref[0].astype(BF16), w3_ref[0].astype(BF16), w2_ref[0].astype(BF16)

        def ffn(rows):
            x = x_ref[rows, :]
            a = jnp.dot(x, w1, preferred_element_type=F32)
            b = jnp.dot(x, w3, preferred_element_type=F32)
            acc_ref[rows, :] += jnp.dot((_silu(a) * b).astype(BF16), w2, preferred_element_type=F32)

        @pl.when(n_chunks == full)
        def _():
            ffn(slice(0, FFN_TILE))

        for c in range(full - 1):
            @pl.when((n_chunks < full) & (c < n_chunks))
            def _():
                ffn(slice(c * FFN_ROW_CHUNK, (c + 1) * FFN_ROW_CHUNK))

        @pl.when(f == nf - 1)
        def _():
            y_ref[...] = (acc_ref[...] * gs_ref[...]).astype(BF16)

    @pl.when((n_chunks == 0) & (f == nf - 1))
    def _():
        y_ref[...] = jnp.zeros_like(y_ref)


def _moe_ffn(xs, gs, w1, w3, w2, tile_expert, tile_chunks, n_ffn_tiles, tf):
    d = xs.shape[1]
    ff = w1.shape[2]
    nf = ff // tf

    def fidx(t, f, nc):
        return jnp.where(nc[t] > 0, f, nf - 1)

    grid_spec = pltpu.PrefetchScalarGridSpec(
        num_scalar_prefetch=2,
        grid=(n_ffn_tiles, nf),
        in_specs=[
            pl.BlockSpec((FFN_TILE, d), lambda t, f, te, nu: (t, 0)),
            pl.BlockSpec((FFN_TILE, 1), lambda t, f, te, nu: (t, 0)),
            pl.BlockSpec((1, d, tf), lambda t, f, te, nu: (te[t], 0, fidx(t, f, nu))),
            pl.BlockSpec((1, d, tf), lambda t, f, te, nu: (te[t], 0, fidx(t, f, nu))),
            pl.BlockSpec((1, tf, d), lambda t, f, te, nu: (te[t], fidx(t, f, nu), 0)),
        ],
        out_specs=pl.BlockSpec((FFN_TILE, d), lambda t, f, te, nu: (t, 0)),
        scratch_shapes=[pltpu.VMEM((FFN_TILE, d), F32)],
    )
    return pl.pallas_call(
        _moe_ffn_kernel,
        grid_spec=grid_spec,
        out_shape=jax.ShapeDtypeStruct(xs.shape, BF16),
        compiler_params=_cparams("arbitrary", "arbitrary"),
        name="moe_ffn",
    )(tile_expert, tile_chunks, xs, gs, w1, w3, w2)


def _moe_combine_kernel(win_ref, lo_ref, hi_ref, xp_ref, xs_ref, d_ref, g_ref, *rest):
    ys_refs, tail_refs = rest[:N_EXPERTS], rest[N_EXPERTS:2 * N_EXPERTS]
    yp_ref, ysm_ref, acc_ref = rest[2 * N_EXPERTS:]
    j = pl.program_id(0)
    n_prompt_tiles = pl.num_programs(0) - 1

    @pl.when(j < n_prompt_tiles)
    def _():
        acc_ref[...] = xp_ref[...]

    @pl.when(j == n_prompt_tiles)
    def _():
        acc_ref[...] = xs_ref[...]

    for e in range(N_EXPERTS):
        lo, hi = lo_ref[j * N_EXPERTS + e], hi_ref[j * N_EXPERTS + e]

        @pl.when(hi > lo)
        def _():
            d = d_ref[...]
            d = jnp.where((d >= lo) & (d < hi), d, -1)
            start = win_ref[j * N_EXPERTS + e] * COMBINE_ALIGN

            def gathered(ref, first):
                cols = lax.broadcasted_iota(I32, (ROUTE_TILE, ref.shape[0]), 1) + first
                onehot = jnp.where((d[:, 0:1] == cols) | (d[:, 1:2] == cols), 1.0, 0.0).astype(BF16)
                return jnp.dot(onehot, ref[...], preferred_element_type=F32)

            acc_ref[...] += gathered(ys_refs[e], start)

            @pl.when(hi > start + ROUTE_TILE)
            def _():
                acc_ref[...] += gathered(tail_refs[e], start + ROUTE_TILE)

    y = _rms(acc_ref[...], g_ref[...])

    @pl.when(j < n_prompt_tiles)
    def _():
        yp_ref[...] = y

    @pl.when(j == n_prompt_tiles)
    def _():
        ysm_ref[...] = y


def _moe_combine(x2_p, x2_s, dest, ys, g_final, clist):
    win, lo, hi = clist
    mp, d = x2_p.shape
    n_prompt_tiles = mp // ROUTE_TILE

    def ptile(j, *_):
        return (jnp.minimum(j, n_prompt_tiles - 1), 0)

    def window(e, rows, offset):
        return pl.BlockSpec((pl.Element(rows), pl.Element(d)),
                            lambda j, win, lo, hi: ((win[j * N_EXPERTS + e] + offset // COMBINE_ALIGN)
                                                    * COMBINE_ALIGN, 0))

    grid_spec = pltpu.PrefetchScalarGridSpec(
        num_scalar_prefetch=3,
        grid=(n_prompt_tiles + 1,),
        in_specs=[
            pl.BlockSpec((ROUTE_TILE, d), ptile),
            pl.BlockSpec((ROUTE_TILE, d), lambda j, *_: (0, 0)),
            pl.BlockSpec((ROUTE_TILE, 2), lambda j, *_: (j, 0)),
            pl.BlockSpec((1, d), lambda j, *_: (0, 0)),
        ] + [window(e, ROUTE_TILE, 0) for e in range(N_EXPERTS)]
          + [window(e, COMBINE_ALIGN, ROUTE_TILE) for e in range(N_EXPERTS)],
        out_specs=[pl.BlockSpec((ROUTE_TILE, d), ptile), pl.BlockSpec((ROUTE_TILE, d), lambda j, *_: (0, 0))],
        scratch_shapes=[pltpu.VMEM((ROUTE_TILE, d), F32)],
    )
    return pl.pallas_call(
        _moe_combine_kernel,
        grid_spec=grid_spec,
        out_shape=[jax.ShapeDtypeStruct((mp, d), F32), jax.ShapeDtypeStruct((ROUTE_TILE, d), F32)],
        compiler_params=_cparams("arbitrary"),
        name="moe_combine",
    )(win, lo, hi, x2_p, x2_s, dest, g_final.reshape(1, d), *([ys] * (2 * N_EXPERTS)))


def _moe_block(x2_p, hn_p, ids_p, gates_p, rank_p, x2_s, hn_s, ids_s, gates_s, rank_s, cnt_tile, w1, w3, w2,
               g_final, tf=512):
    ids = jnp.concatenate([ids_p, ids_s])
    gates = jnp.concatenate([gates_p, gates_s])
    rank = jnp.concatenate([rank_p, rank_s])
    dest, tile_expert, tile_chunks, glist, clist, n_ffn_tiles = _moe_plan(ids, rank, cnt_tile)
    n_sub = n_ffn_tiles * (FFN_TILE // ROUTE_TILE)
    spare = ((0, 0), (0, GATHER_TOKENS - ROUTE_TILE))
    xs, gs = _moe_gather(hn_p, hn_s, jnp.pad(dest.T, spare, constant_values=-1), jnp.pad(gates.T, spare), glist,
                         n_sub)
    ys = _moe_ffn(xs, gs, w1, w3, w2, tile_expert, tile_chunks, n_ffn_tiles, tf)
    return _moe_combine(x2_p, x2_s, dest, ys, g_final, clist)


def _rms(x, g):
    return x * lax.rsqrt(jnp.mean(x * x, axis=-1, keepdims=True) + EPS) * g


def _mm(a, b, precise):
    if precise:
        return jnp.dot(a.astype(F32), b.astype(F32), preferred_element_type=F32, precision=HIGHEST)
    return jnp.dot(a.astype(BF16), b.astype(BF16), preferred_element_type=F32)


def _silu(x):
    return x * jax.nn.sigmoid(x)


def _norm_proj_kernel(x_ref, g_ref, w_ref, wt_ref, *out_refs, segs, tsegs, precise):
    xn = _rms(x_ref[...], g_ref[...])
    xn = xn if precise else xn.astype(BF16)
    for (start, width), o_ref in zip(segs, out_refs[:len(segs)]):
        for c in range(0, width, 512):
            cw = min(512, width - c)
            o_ref[:, c:c + cw] = _mm(xn, w_ref[:, start + c:start + c + cw], precise)
    for (start, height), o_ref in zip(tsegs, out_refs[len(segs):]):
        res = lax.dot_general(wt_ref[start:start + height, :].astype(xn.dtype), xn, (((1,), (1,)), ((), ())),
                              preferred_element_type=F32, precision=HIGHEST if precise else None)
        o_ref[...] = res.reshape(o_ref.shape)


def _norm_proj(x, g, w, wt, segs, tsegs, tm, rows_per_seq=None, precise=False):
    m, d = x.shape
    n = w.shape[1]
    nt = wt.shape[0]
    out_shape = [jax.ShapeDtypeStruct((m, width), F32) for _, width in segs]
    out_specs = [pl.BlockSpec((tm, width), lambda i: (i, 0)) for _, width in segs]
    for _, height, per_seq in tsegs:
        if per_seq:
            tps = rows_per_seq // tm
            out_shape.append(jax.ShapeDtypeStruct((m // rows_per_seq, height, rows_per_seq), F32))
            out_specs.append(pl.BlockSpec((1, height, tm), lambda i, tps=tps: (i // tps, 0, i % tps)))
        else:
            out_shape.append(jax.ShapeDtypeStruct((height, m), F32))
            out_specs.append(pl.BlockSpec((height, tm), lambda i: (0, i)))
    return pl.pallas_call(
        functools.partial(_norm_proj_kernel, segs=tuple(segs), tsegs=tuple(t[:2] for t in tsegs), precise=precise),
        grid=(m // tm,),
        in_specs=[
            pl.BlockSpec((tm, d), lambda i: (i, 0)),
            pl.BlockSpec((1, d), lambda i: (0, 0)),
            pl.BlockSpec((d, n), lambda i: (0, 0)),
            pl.BlockSpec((nt, d), lambda i: (0, 0)),
        ],
        out_specs=out_specs,
        out_shape=out_shape,
        compiler_params=_cparams("arbitrary"),
        name="norm_proj",
    )(x, g.reshape(1, d), w, wt)


def _mix_ffn_prologue(a_ref, b_ref, x_ref, wo_ref, g_ref, o_ref, hn_ref, precise):
    wa = a_ref.shape[1]

    @pl.when(pl.program_id(1) == 0)
    def _():
        x1 = (x_ref[...] + _mm(a_ref[...], wo_ref[0:wa, :], precise)
              + _mm(b_ref[...], wo_ref[wa:, :], precise))
        o_ref[...] = x1
        hn_ref[...] = _rms(x1, g_ref[...]).astype(hn_ref.dtype)


def _mix_ffn_step(w1_ref, w3_ref, w2_ref, o_ref, hn_ref, precise):
    wdt = F32 if precise else BF16
    w1, w3, w2 = w1_ref[...].astype(wdt), w3_ref[...].astype(wdt), w2_ref[...].astype(wdt)
    hn = hn_ref[...]
    o_ref[...] += _mm(_silu(_mm(hn, w1, precise)) * _mm(hn, w3, precise), w2, precise)


def _mix_ffn_kernel(a_ref, b_ref, x_ref, wo_ref, g_ref, w1_ref, w3_ref, w2_ref, o_ref, hn_ref, *, precise):
    _mix_ffn_prologue(a_ref, b_ref, x_ref, wo_ref, g_ref, o_ref, hn_ref, precise)
    _mix_ffn_step(w1_ref, w3_ref, w2_ref, o_ref, hn_ref, precise)


WINDOW_HEAD_BLOCKS = 2
WINDOW_STEPS = 4
N_MIX_FFN_IN, N_WINDOW_IN, N_WINDOW_OUT = 8, 7, 3


def _mix_ffn_window_kernel(*refs, precise):
    (a_ref, b_ref, x_ref, wo_ref, g_ref, w1_ref, w3_ref, w2_ref), refs = refs[:N_MIX_FFN_IN], refs[N_MIX_FFN_IN:]
    w_ins, refs = refs[:N_WINDOW_IN], refs[N_WINDOW_IN:]
    o_ref, w_outs, hn_ref = refs[0], refs[1:1 + N_WINDOW_OUT], refs[1 + N_WINDOW_OUT]
    f = pl.program_id(1)
    _mix_ffn_prologue(a_ref, b_ref, x_ref, wo_ref, g_ref, o_ref, hn_ref, precise)

    @pl.when(f < WINDOW_STEPS)
    def _():
        _sample_window_kernel(*w_ins, *w_outs, head0=(f % WINDOW_HEAD_BLOCKS) * (A_HEADS // WINDOW_HEAD_BLOCKS))
        _mix_ffn_step(w1_ref, w3_ref, w2_ref, o_ref, hn_ref, precise)

    @pl.when(f >= WINDOW_STEPS)
    def _():
        _mix_ffn_step(w1_ref, w3_ref, w2_ref, o_ref, hn_ref, precise)


def _mix_ffn(a, b, x, w_out, g, w1, w3, w2, tm, tf, precise=False, window=None, rel_bias=None):
    m, d = x.shape
    wa, wb = a.shape[1], b.shape[1]
    ff = w1.shape[1]
    in_specs = [
        pl.BlockSpec((tm, wa), lambda i, f: (i, 0)),
        pl.BlockSpec((tm, wb), lambda i, f: (i, 0)),
        pl.BlockSpec((tm, d), lambda i, f: (i, 0)),
        pl.BlockSpec((wa + wb, d), lambda i, f: (0, 0)),
        pl.BlockSpec((1, d), lambda i, f: (0, 0)),
        pl.BlockSpec((d, tf), lambda i, f: (0, f)),
        pl.BlockSpec((d, tf), lambda i, f: (0, f)),
        pl.BlockSpec((tf, d), lambda i, f: (f, 0)),
    ]
    out_specs = [pl.BlockSpec((tm, d), lambda i, f: (i, 0))]
    out_shape = [jax.ShapeDtypeStruct((m, d), F32)]
    operands = [a, b, x, w_out, g.reshape(1, d), w1, w3, w2]
    body = _mix_ffn_kernel
    if window is not None:
        assert ff // tf >= WINDOW_STEPS and window[0].shape[0] * WINDOW_HEAD_BLOCKS == (m // tm) * WINDOW_STEPS

        def block_of_step(i, f):
            p = i * WINDOW_STEPS + jnp.minimum(f, WINDOW_STEPS - 1)
            return p // WINDOW_HEAD_BLOCKS, p % WINDOW_HEAD_BLOCKS

        w_operands, w_in, w_out_specs, w_shape = _sample_window_specs(*window, rel_bias, block_of_step,
                                                                      A_HEADS // WINDOW_HEAD_BLOCKS)
        in_specs, out_specs, out_shape = in_specs + w_in, out_specs + w_out_specs, out_shape + w_shape
        operands, body = operands + w_operands, _mix_ffn_window_kernel
    res = pl.pallas_call(
        functools.partial(body, precise=precise),
        grid=(m // tm, ff // tf),
        in_specs=in_specs,
        out_specs=out_specs,
        out_shape=out_shape,
        scratch_shapes=[pltpu.VMEM((tm, d), F32 if precise else BF16)],
        compiler_params=_cparams("arbitrary", "arbitrary"),
        name="mix_ffn",
    )(*operands)
    return res[0] if window is None else res


def _mix_router_kernel(a_ref, b_ref, x_ref, wo_ref, g_ref, wr_ref, br_ref, cnt0_ref, x2_ref, hn_ref, ids_ref,
                       gate_ref, rank_ref, cb_ref, ca_ref, cnt_ref, *, precise):
    @pl.when(pl.program_id(0) == 0)
    def _():
        cnt_ref[...] = cnt0_ref[...]

    wa = a_ref.shape[1]
    x2 = (x_ref[...] + _mm(a_ref[...], wo_ref[0:wa, :], precise)
          + _mm(b_ref[...], wo_ref[wa:, :], precise))
    x2_ref[...] = x2
    hn = _rms(x2, g_ref[...])
    hn_ref[...] = hn.astype(BF16)
    wr = wr_ref[...]
    if precise:
        logits = jnp.dot(hn, wr, preferred_element_type=F32, precision=HIGHEST)
    else:
        hn_hi, wr_hi = hn.astype(BF16), wr.astype(BF16)
        hn_lo, wr_lo = (hn - hn_hi.astype(F32)).astype(BF16), (wr - wr_hi.astype(F32)).astype(BF16)
        logits = (jnp.dot(hn_hi, wr_hi, preferred_element_type=F32)
                  + (jnp.dot(hn_lo, wr_hi, preferred_element_type=F32)
                     + jnp.dot(hn_hi, wr_lo, preferred_element_type=F32)))
    lane = lax.broadcasted_iota(I32, logits.shape, 1)
    real = lane < N_EXPERTS
    biased = jnp.where(real, logits + br_ref[...], -jnp.inf)
    m1 = jnp.max(biased, axis=-1, keepdims=True)
    i1 = jnp.min(jnp.where(biased == m1, lane, 128), axis=-1, keepdims=True)
    rest = jnp.where(lane == i1, -jnp.inf, biased)
    m2 = jnp.max(rest, axis=-1, keepdims=True)
    i2 = jnp.min(jnp.where(rest == m2, lane, 128), axis=-1, keepdims=True)
    l1 = jnp.sum(jnp.where(lane == i1, logits, 0.0), axis=-1, keepdims=True)
    l2 = jnp.sum(jnp.where(lane == i2, logits, 0.0), axis=-1, keepdims=True)
    mx = jnp.maximum(l1, l2)
    e1, e2 = jnp.exp(l1 - mx), jnp.exp(l2 - mx)
    two = lax.broadcasted_iota(I32, ids_ref.shape, 1)
    ids_ref[...] = jnp.where(two == 0, i1, i2)
    gate_ref[...] = jnp.where(two == 0, e1, e2) / (e1 + e2)
    tm = logits.shape[0]
    chosen = jnp.where((lane == i1) | (lane == i2), 1.0, 0.0)
    earlier = (lax.broadcasted_iota(I32, (tm, tm), 0) > lax.broadcasted_iota(I32, (tm, tm), 1))
    before = cnt_ref[...] + jnp.dot(jnp.where(earlier, 1.0, 0.0).astype(BF16), chosen.astype(BF16),
                                    preferred_element_type=F32)
    r1 = jnp.sum(jnp.where(lane == i1, before, 0.0), axis=-1, keepdims=True)
    r2 = jnp.sum(jnp.where(lane == i2, before, 0.0), axis=-1, keepdims=True)
    rank_ref[...] = jnp.where(two == 0, r1, r2).astype(I32)
    for c in range(cb_ref.shape[1]):
        cb_ref[0, c:c + 1, :] = before[c * ROUTE_TILE:c * ROUTE_TILE + 1, :]
    cnt_ref[...] += jnp.sum(chosen, axis=0, keepdims=True)
    ca_ref[...] = cnt_ref[...]


def _mix_router(a, b, x, w_out, g, wr, br, cnt0, tm, precise=False):
    m, d = x.shape
    wa, wb = a.shape[1], b.shape[1]
    n_marks = max(1, tm // ROUTE_TILE)
    return pl.pallas_call(
        functools.partial(_mix_router_kernel, precise=precise),
        grid=(m // tm,),
        in_specs=[
            pl.BlockSpec((tm, wa), lambda i: (i, 0)),
            pl.BlockSpec((tm, wb), lambda i: (i, 0)),
            pl.BlockSpec((tm, d), lambda i: (i, 0)),
            pl.BlockSpec((wa + wb, d), lambda i: (0, 0)),
            pl.BlockSpec((1, d), lambda i: (0, 0)),
            pl.BlockSpec((d, 128), lambda i: (0, 0)),
            pl.BlockSpec((1, 128), lambda i: (0, 0)),
            pl.BlockSpec((1, 128), lambda i: (0, 0)),
        ],
        out_specs=[
            pl.BlockSpec((tm, d), lambda i: (i, 0)),
            pl.BlockSpec((tm, d), lambda i: (i, 0)),
            pl.BlockSpec((tm, 2), lambda i: (i, 0)),
            pl.BlockSpec((tm, 2), lambda i: (i, 0)),
            pl.BlockSpec((tm, 2), lambda i: (i, 0)),
            pl.BlockSpec((1, n_marks, 128), lambda i: (i, 0, 0)),
            pl.BlockSpec((1, 128), lambda i: (0, 0)),
        ],
        out_shape=[jax.ShapeDtypeStruct((m, d), F32), jax.ShapeDtypeStruct((m, d), BF16),
                   jax.ShapeDtypeStruct((m, 2), I32), jax.ShapeDtypeStruct((m, 2), F32),
                   jax.ShapeDtypeStruct((m, 2), I32), jax.ShapeDtypeStruct((m // tm, n_marks, 128), F32),
                   jax.ShapeDtypeStruct((1, 128), F32)],
        scratch_shapes=[pltpu.VMEM((1, 128), F32)],
        compiler_params=_cparams("arbitrary"),
        name="mix_router",
    )(a, b, x, w_out, g.reshape(1, d), wr, br, cnt0)


ATT_T = 512


def _t5_bucket(dist):
    max_exact = N_BUCKETS // 2
    d = jnp.maximum(dist, 1).astype(F32)
    large = max_exact + (jnp.log(d / max_exact) / math.log(A_WIN / max_exact)
                         * (N_BUCKETS - max_exact)).astype(I32)
    return jnp.where(dist < max_exact, dist, jnp.minimum(large, N_BUCKETS - 1))


def _distance_logit_table(rel_bias, max_dist):
    dist = jnp.arange(max_dist + 1, dtype=I32)
    mult = jnp.zeros((max_dist + 1,), F32)
    for window, d in A_BRANCHES:
        mult = mult + ((dist % d == 0) & (dist <= window)).astype(F32)
    bias = rel_bias[_t5_bucket(dist)].astype(F32).T
    return jnp.where(mult[None, :] > 0, bias + jnp.log(jnp.maximum(mult, 1.0))[None, :], NEG)


def _attn_kernel(q_ref, kt_ref, vt_ref, row0_ref, o_ref, kb_ref, vb_ref):
    s_len = q_ref.shape[1]
    nb = s_len // ATT_T
    period = row0_ref.shape[2]
    kb_ref[...] = kt_ref[0].astype(BF16)
    vb_ref[...] = vt_ref[0].astype(BF16)
    scale = A_HD ** -0.5
    for hl in range(2):
        hrows = slice(hl * A_HD, (hl + 1) * A_HD)
        table = pltpu.roll(jnp.broadcast_to(row0_ref[0, hl:hl + 1, :], (ATT_T, period)), 0, 1,
                           stride=1, stride_axis=0)
        for i in range(nb):
            rows = slice(i * ATT_T, (i + 1) * ATT_T)
            width = (i + 1) * ATT_T
            q = (q_ref[0, rows, hrows] * scale).astype(BF16)
            s = jnp.dot(q, kb_ref[hrows, 0:width], preferred_element_type=F32)
            s = s + table[:, (nb - 1 - i) * ATT_T:nb * ATT_T]
            m = jnp.max(s, axis=-1, keepdims=True)
            p = jnp.exp(s - m)
            den = jnp.sum(p, axis=-1, keepdims=True)
            o = lax.dot_general(p.astype(BF16), vb_ref[hrows, 0:width], (((1,), (1,)), ((), ())),
                                preferred_element_type=F32)
            o_ref[0, rows, hrows] = o / den


def _attention_prompt(q, kt, vt, rel_bias):
    b, s_len, _ = q.shape
    nb = s_len // ATT_T
    tab1 = _distance_logit_table(rel_bias, s_len)
    period = s_len + ATT_T
    z = jnp.arange(period, dtype=I32)
    delta = jnp.where(z < s_len, (nb - 1) * ATT_T - z, (nb - 1) * ATT_T + period - z)
    row0 = jnp.where(delta[None] >= 0, tab1[:, jnp.clip(delta, 0, s_len)], NEG)
    return pl.pallas_call(
        _attn_kernel,
        grid=(b, A_HEADS // 2),
        in_specs=[pl.BlockSpec((1, s_len, 2 * A_HD), lambda bi, hp: (bi, 0, hp)),
                  pl.BlockSpec((1, 2 * A_HD, s_len), lambda bi, hp: (bi, hp, 0)),
                  pl.BlockSpec((1, 2 * A_HD, s_len), lambda bi, hp: (bi, hp, 0)),
                  pl.BlockSpec((1, 2, period), lambda bi, hp: (hp, 0, 0))],
        out_specs=pl.BlockSpec((1, s_len, 2 * A_HD), lambda bi, hp: (bi, 0, hp)),
        out_shape=jax.ShapeDtypeStruct(q.shape, F32),
        scratch_shapes=[pltpu.VMEM((2 * A_HD, s_len), BF16), pltpu.VMEM((2 * A_HD, s_len), BF16)],
        compiler_params=_cparams("arbitrary", "arbitrary"),
        name="dilated_attention",
    )(q, kt, vt, row0.reshape(A_HEADS // 2, 2, period))


MLSTM_CHUNK = 256


def _log_sigmoid(x):
    return jnp.minimum(x, 0.0) - jnp.log(1.0 + jnp.exp(-jnp.abs(x)))


def _mlstm_kernel(qk_ref, v_ref, ob_ref, gc_ref, gr_ref, wc_ref, bc_ref, bifc_ref, bifr_ref, gh_ref,
                  o_ref, c_out_ref, n_out_ref, m_out_ref, xbuf_ref, c_ref, n_ref, m_ref):
    ci = pl.program_id(1)
    L = qk_ref.shape[1]

    @pl.when(ci == 0)
    def _():
        xbuf_ref[0:8, :] = jnp.zeros((8, 2 * B_W), F32)
        c_ref[...] = jnp.zeros_like(c_ref)
        n_ref[...] = jnp.zeros_like(n_ref)
        m_ref[...] = jnp.zeros_like(m_ref)

    x = qk_ref[0]
    xbuf_ref[8:8 + L, :] = x
    y = bc_ref[...] + wc_ref[3:4, :] * x
    for j in range(B_CONV - 1):
        y = y + wc_ref[j:j + 1, :] * xbuf_ref[5 + j:5 + j + L, :]
    xbuf_ref[0:8, :] = x[L - 8:L, :]
    y = _silu(y)

    gcol = gc_ref[0] + bifc_ref[...]
    grow = gr_ref[...] + bifr_ref[...]
    ri = lax.broadcasted_iota(I32, (L, L), 0)
    cj = lax.broadcasted_iota(I32, (L, L), 1)
    causal = ri >= cj
    lower = jnp.where(causal, 1.0, 0.0)
    b_col = jnp.dot(lower, _log_sigmoid(gcol), preferred_element_type=F32, precision=HIGHEST)
    b_row = lax.dot_general(_log_sigmoid(grow), lower, (((1,), (1,)), ((), ())),
                            preferred_element_type=F32, precision=HIGHEST)

    for h in range(B_HEADS):
        cols = slice(h * B_HD, (h + 1) * B_HD)
        qf = y[:, cols]
        q = qf.astype(BF16)
        kf = y[:, B_W + h * B_HD:B_W + (h + 1) * B_HD] * (B_HD ** -0.5)
        k = kf.astype(BF16)
        v = v_ref[0, :, cols].astype(BF16)
        bc = b_col[:, B_HEADS + h:B_HEADS + h + 1]
        ic = gcol[:, h:h + 1]
        br = b_row[B_HEADS + h:B_HEADS + h + 1, :]
        ir = grow[h:h + 1, :]
        m_prev = m_ref[h:h + 1, 0:1]
        dm = jnp.where(causal, bc - br + ir, NEG)
        inter = bc + m_prev
        mt = jnp.maximum(inter, jnp.max(dm, axis=1, keepdims=True))
        w_intra = jnp.exp(dm - mt)
        w_inter = jnp.exp(inter - mt)
        a = w_intra * lax.dot_general(q, k, (((1,), (1,)), ((), ())), preferred_element_type=F32)
        c_prev = c_ref[h]
        n_prev = n_ref[h:h + 1, :]
        num = (jnp.dot(a.astype(BF16), v, preferred_element_type=F32)
               + w_inter * jnp.dot(q, c_prev.astype(BF16), preferred_element_type=F32))
        den = (jnp.sum(a, axis=1, keepdims=True)
               + w_inter * jnp.sum(qf * n_prev, axis=1, keepdims=True))
        hb = num / jnp.maximum(jnp.abs(den), jnp.exp(-mt))
        m_new = mt[L - 1:L, :]
        b_last = bc[L - 1:L, :]
        g = jnp.exp(b_last - bc + ic - m_new)
        decay = jnp.exp(b_last + m_prev - m_new)
        kg = kf * g
        c_ref[h] = decay * c_prev + lax.dot_general(kg.astype(BF16), v, (((0,), (0,)), ((), ())),
                                                    preferred_element_type=F32)
        n_ref[h:h + 1, :] = decay * n_prev + jnp.sum(kg, axis=0, keepdims=True)
        m_ref[h:h + 1, :] = jnp.broadcast_to(m_new, (1, 128))
        hb = hb * lax.rsqrt(jnp.mean(hb * hb, axis=-1, keepdims=True) + EPS) * gh_ref[:, cols]
        o_ref[0, :, cols] = jax.nn.sigmoid(ob_ref[0, :, cols]) * hb

    @pl.when(ci == pl.num_programs(1) - 1)
    def _():
        c_out_ref[0] = c_ref[...]
        n_out_ref[0] = n_ref[...]
        m_out_ref[0] = m_ref[...]


def _mlstm_prompt(qk, v, ob, gcol, grow, w_conv, b_conv, b_if, g_head):
    b, s_len, _ = qk.shape
    L = min(MLSTM_CHUNK, s_len)
    nc = s_len // L
    bif_col = jnp.zeros((1, 128), F32).at[0, :2 * B_HEADS].set(b_if)
    bif_row = b_if.reshape(2 * B_HEADS, 1)
    seq = lambda w: pl.BlockSpec((1, L, w), lambda bi, ci: (bi, ci, 0))
    full = lambda shape: pl.BlockSpec(shape, lambda bi, ci: (0,) * len(shape))
    return pl.pallas_call(
        _mlstm_kernel,
        grid=(b, nc),
        in_specs=[seq(2 * B_W), seq(B_W), seq(B_W), seq(128),
                  pl.BlockSpec((2 * B_HEADS, L), lambda bi, ci: (0, bi * nc + ci)),
                  full((B_CONV, 2 * B_W)), full((1, 2 * B_W)), full((1, 128)), full((2 * B_HEADS, 1)),
                  full((1, B_W))],
        out_specs=[seq(B_W),
                   pl.BlockSpec((1, B_HEADS, B_HD, B_HD), lambda bi, ci: (bi, 0, 0, 0)),
                   pl.BlockSpec((1, 8, B_HD), lambda bi, ci: (bi, 0, 0)),
                   pl.BlockSpec((1, 8, 128), lambda bi, ci: (bi, 0, 0))],
        out_shape=[jax.ShapeDtypeStruct((b, s_len, B_W), F32),
                   jax.ShapeDtypeStruct((b, B_HEADS, B_HD, B_HD), F32),
                   jax.ShapeDtypeStruct((b, 8, B_HD), F32),
                   jax.ShapeDtypeStruct((b, 8, 128), F32)],
        scratch_shapes=[pltpu.VMEM((L + 8, 2 * B_W), F32), pltpu.VMEM((B_HEADS, B_HD, B_HD), F32),
                        pltpu.VMEM((8, B_HD), F32), pltpu.VMEM((8, 128), F32)],
        compiler_params=_cparams("arbitrary", "arbitrary"),
        name="mlstm_chunkwise",
    )(qk, v, ob, gcol, grow, w_conv, b_conv.reshape(1, -1), bif_col, bif_row, g_head.reshape(1, -1))


def _odd_in_kernel(x_ref, g_ref, w_ref, gcv_ref, bcv_ref, ws_ref, bs_ref, c_ref, xd_ref):
    tm = x_ref.shape[0]
    xn = _rms(x_ref[...], g_ref[...]).astype(BF16)
    u = jnp.dot(xn, w_ref[:, 0:C_W], preferred_element_type=F32)
    v = jnp.dot(xn, w_ref[:, C_W:2 * C_W], preferred_element_type=F32)
    xd_ref[...] = jnp.dot(xn, w_ref[:, 2 * C_W:], preferred_element_type=F32)
    mu = jnp.mean(v, axis=-1, keepdims=True)
    vc = v - mu
    var = jnp.mean(vc * vc, axis=-1, keepdims=True)
    vl = (vc * lax.rsqrt(var + EPS) * gcv_ref[...] + bcv_ref[...]).astype(BF16)
    ri = lax.broadcasted_iota(I32, (C_CHUNK, C_CHUNK), 0)
    cj = lax.broadcasted_iota(I32, (C_CHUNK, C_CHUNK), 1)
    for g in range(C_GROUPS):
        cols = slice(g * C_GD, (g + 1) * C_GD)
        wg = jnp.where(ri >= cj, ws_ref[g], 0.0).astype(BF16)
        for c in range(tm // C_CHUNK):
            rows = slice(c * C_CHUNK, (c + 1) * C_CHUNK)
            s = jnp.dot(wg, vl[rows, cols], preferred_element_type=F32) + bs_ref[:, g:g + 1]
            c_ref[rows, cols] = u[rows, cols] * s


def _odd_in(x, g, w, g_cv, b_cv, w_s, b_s, tm):
    m, d = x.shape
    n = w.shape[1]
    full = lambda shape: pl.BlockSpec(shape, lambda i: (0,) * len(shape))
    return pl.pallas_call(
        _odd_in_kernel,
        grid=(m // tm,),
        in_specs=[pl.BlockSpec((tm, d), lambda i: (i, 0)), full((1, d)), full((d, n)), full((1, C_W)),
                  full((1, C_W)), full((C_GROUPS, C_CHUNK, C_CHUNK)), full((C_CHUNK, C_GROUPS))],
        out_specs=[pl.BlockSpec((tm, C_W), lambda i: (i, 0)), pl.BlockSpec((tm, D_W), lambda i: (i, 0))],
        out_shape=[jax.ShapeDtypeStruct((m, C_W), F32), jax.ShapeDtypeStruct((m, D_W), F32)],
        compiler_params=_cparams("arbitrary"),
        name="odd_in_proj_gate",
    )(x, g.reshape(1, d), w, g_cv.reshape(1, -1), b_cv.reshape(1, -1), w_s, b_s.T)


S5_TT = 128
S5_PAD = 8
S5_HALF = D_GROUPS * D_STATE // 2


def _s5_discretize(a_re, a_im, log_dt, bm_re, bm_im, cm_re, cm_im):
    a_re, a_im = a_re.astype(F32), a_im.astype(F32)
    dt = jnp.exp(log_dt.astype(F32))[:, None]
    mag = jnp.exp(a_re * dt)
    ab_re = mag * jnp.cos(a_im * dt)
    ab_im = mag * jnp.sin(a_im * dt)
    inv = 1.0 / (a_re * a_re + a_im * a_im)
    f_re = ((ab_re - 1.0) * a_re + ab_im * a_im) * inv
    f_im = (ab_im * a_re - (ab_re - 1.0) * a_im) * inv
    bm_re, bm_im = bm_re.astype(F32), bm_im.astype(F32)
    bb_re = f_re[..., None] * bm_re - f_im[..., None] * bm_im
    bb_im = f_re[..., None] * bm_im + f_im[..., None] * bm_re
    return ab_re, ab_im, bb_re, bb_im


def _s5_matrices(bb_re, bb_im, cm_re, cm_im):
    gh = D_GROUPS // 2
    eye = jnp.eye(gh, dtype=F32)

    def in_half(bb):
        return jnp.einsum('gpc,gh->gchp', bb, eye).reshape(gh * D_GCH, gh * D_STATE)

    def out_half(cm):
        return jnp.einsum('gcp,gh->gphc', cm, eye).reshape(gh * D_STATE, gh * D_GCH)

    bd = jnp.stack([jnp.concatenate([in_half(bb_re[h * gh:(h + 1) * gh]), in_half(bb_im[h * gh:(h + 1) * gh])], axis=1)
                    for h in range(2)])
    cm = jnp.stack([jnp.concatenate([out_half(cm_re[h * gh:(h + 1) * gh].astype(F32)),
                                     -out_half(cm_im[h * gh:(h + 1) * gh].astype(F32))], axis=0)
                    for h in range(2)])
    return bd, cm


def _gelu_tanh(x):
    return 0.5 * x * (1.0 + jnp.tanh(math.sqrt(2.0 / math.pi) * (x + 0.044715 * (x * x * x))))


def _s5_kernel(x_ref, bd_ref, cm_ref, ar_ref, ai_ref, dsk_ref, wg_ref, bg_ref, o_ref, hr_out, hi_out,
               bu_ref, hs_ref, hr_ref, hi_ref):
    i = pl.program_id(0)
    nb, tt, _ = x_ref.shape
    stride = tt + S5_PAD
    half_in = D_W // 2
    nct = S5_HALF // 128

    @pl.when(i == 0)
    def _():
        hr_ref[...] = jnp.zeros_like(hr_ref)
        hi_ref[...] = jnp.zeros_like(hi_ref)

    x = x_ref[...].reshape(nb * tt, D_W)
    ys = []
    for h in range(2):
        lanes = slice(h * S5_HALF, (h + 1) * S5_HALF)
        bu = jnp.dot(x[:, h * half_in:(h + 1) * half_in].astype(BF16), bd_ref[h], preferred_element_type=F32)
        for c in range(2 * nct):
            for b in range(nb):
                bu_ref[c, b * stride:b * stride + tt, :] = bu[b * tt:(b + 1) * tt, c * 128:(c + 1) * 128]
        ar = jnp.broadcast_to(ar_ref[:, lanes], (nb, S5_HALF))
        ai = jnp.broadcast_to(ai_ref[:, lanes], (nb, S5_HALF))

        def step(t, carry):
            hr, hi = carry
            rows = pl.ds(t, nb, stride=stride)
            bur = jnp.concatenate([bu_ref[c, rows, :] for c in range(nct)], axis=1)
            bui = jnp.concatenate([bu_ref[nct + c, rows, :] for c in range(nct)], axis=1)
            nhr = ar * hr - ai * hi + bur
            nhi = ar * hi + ai * hr + bui
            for c in range(nct):
                hs_ref[c, rows, :] = nhr[:, c * 128:(c + 1) * 128]
                hs_ref[nct + c, rows, :] = nhi[:, c * 128:(c + 1) * 128]
            return nhr, nhi

        hr, hi = lax.fori_loop(0, tt, step, (hr_ref[:, lanes], hi_ref[:, lanes]), unroll=2)
        hr_ref[:, lanes] = hr
        hi_ref[:, lanes] = hi
        hs = jnp.concatenate(
            [jnp.concatenate([hs_ref[c, b * stride:b * stride + tt, :] for b in range(nb)], axis=0)
             for c in range(2 * nct)], axis=1)
        ys.append(jnp.dot(hs.astype(BF16), cm_ref[h], preferred_element_type=F32))
    y = jnp.concatenate(ys, axis=-1) + dsk_ref[...] * x
    g = _gelu_tanh(y)
    d_out = g * jax.nn.sigmoid(jnp.dot(g.astype(BF16), wg_ref[...], preferred_element_type=F32) + bg_ref[...])
    o_ref[...] = d_out.reshape(nb, tt, D_W)

    @pl.when(i == pl.num_programs(0) - 1)
    def _():
        hr_out[...] = hr_ref[...]
        hi_out[...] = hi_ref[...]


def _s5_prompt(xd, bd, cm, ab_re, ab_im, d_skip, w_glu, b_glu):
    nb, t_len, _ = xd.shape
    tt = min(S5_TT, t_len)
    n_state = D_GROUPS * D_STATE
    full = lambda shape: pl.BlockSpec(shape, lambda i: (0,) * len(shape))
    return pl.pallas_call(
        _s5_kernel,
        grid=(t_len // tt,),
        in_specs=[pl.BlockSpec((nb, tt, D_W), lambda i: (0, i, 0)), full(bd.shape), full(cm.shape),
                  full((1, n_state)), full((1, n_state)), full((1, D_W)), full((D_W, D_W)), full((1, D_W))],
        out_specs=[pl.BlockSpec((nb, tt, D_W), lambda i: (0, i, 0)), full((nb, n_state)), full((nb, n_state))],
        out_shape=[jax.ShapeDtypeStruct(xd.shape, F32), jax.ShapeDtypeStruct((nb, n_state), F32),
                   jax.ShapeDtypeStruct((nb, n_state), F32)],
        scratch_shapes=[pltpu.VMEM((2 * S5_HALF // 128, nb * (tt + S5_PAD), 128), F32),
                        pltpu.VMEM((2 * S5_HALF // 128, nb * (tt + S5_PAD), 128), F32),
                        pltpu.VMEM((nb, n_state), F32), pltpu.VMEM((nb, n_state), F32)],
        compiler_params=_cparams("arbitrary"),
        name="s5_scan_glu",
    )(xd, bd, cm, ab_re.reshape(1, n_state), ab_im.reshape(1, n_state), d_skip.reshape(1, D_W), w_glu,
      b_glu.reshape(1, D_W))


def _hdot(a, b):
    return jnp.dot(a, b, precision=HIGHEST, preferred_element_type=F32)


def _sample_window_kernel(q_ref, kn_ref, vn_ref, tab_ref, tabn_ref, kt_ref, vt_ref, o_ref, okt_ref, ovt_ref,
                          head0=0):
    L = kt_ref.shape[3]
    last = lax.broadcasted_iota(I32, (A_HD, L), 1) == L - 1
    for h in range(kt_ref.shape[1]):
        kh, vh = kt_ref[0, h], vt_ref[0, h]
        kn, vn = kn_ref[0, h], vn_ref[0, h]
        q = q_ref[0, h] * (A_HD ** -0.5)
        s = jnp.sum(kh * q, axis=0, keepdims=True) + tab_ref[head0 + h]
        s_new = jnp.sum(kn * q, axis=0, keepdims=True) + tabn_ref[head0 + h]
        m = jnp.maximum(jnp.max(s, axis=1, keepdims=True), s_new)
        p = jnp.exp(s - m)
        p_new = jnp.exp(s_new - m)
        den = jnp.sum(p, axis=1, keepdims=True) + p_new
        o_ref[0, h] = (jnp.sum(vh * p, axis=1, keepdims=True) + vn * p_new) / den
        okt_ref[0, h] = jnp.where(last, kn, pltpu.roll(kh, L - 1, 1))
        ovt_ref[0, h] = jnp.where(last, vn, pltpu.roll(vh, L - 1, 1))


def _sample_window_attention(q, k_new, v_new, cache_kt, cache_vt, rel_bias):
    operands, in_specs, out_specs, out_shape = _sample_window_specs(q, k_new, v_new, cache_kt, cache_vt, rel_bias,
                                                                    lambda i: (i, 0))
    return pl.pallas_call(
        _sample_window_kernel,
        grid=(q.shape[0],),
        in_specs=in_specs,
        out_specs=out_specs,
        out_shape=out_shape,
        compiler_params=_cparams("arbitrary"),
        name="sample_window_attention",
    )(*operands)


def _sample_window_specs(q, k_new, v_new, cache_kt, cache_vt, rel_bias, block_of_step, heads_per_step=A_HEADS):
    n, _, _, L = cache_kt.shape
    assert L == A_WIN
    tab1 = _distance_logit_table(rel_bias, L)
    tab = tab1[:, :0:-1].reshape(A_HEADS, 1, L)
    tab_new = tab1[:, 0].reshape(A_HEADS, 1, 1)
    col = pl.BlockSpec((1, heads_per_step, A_HD, 1), lambda *g: (*block_of_step(*g), 0, 0))
    win = pl.BlockSpec((1, heads_per_step, A_HD, L), lambda *g: (*block_of_step(*g), 0, 0))
    full = lambda shape: pl.BlockSpec(shape, lambda *g: (0,) * len(shape))
    out_shape = [jax.ShapeDtypeStruct((n, A_HEADS, A_HD, 1), F32),
                 jax.ShapeDtypeStruct(cache_kt.shape, F32), jax.ShapeDtypeStruct(cache_vt.shape, F32)]
    return ([q, k_new, v_new, tab, tab_new, cache_kt, cache_vt],
            [col, col, col, full(tab.shape), full(tab_new.shape), win, win], [col, win, win], out_shape)


def _sample_even_proj(x, g_mix, w_in):
    n_main = 3 * A_W + 4 * B_W
    segs = ((0, A_W), (A_W, A_W), (2 * A_W, A_W), (3 * A_W, 2 * B_W), (3 * A_W + 2 * B_W, B_W),
            (3 * A_W + 3 * B_W, B_W), (n_main, 2 * B_HEADS))
    return _norm_proj(x, g_mix, w_in, jnp.zeros((8, x.shape[1]), F32), segs, (), tm=x.shape[0], precise=True)


def _sample_mlstm_step(qkb, vb, ob, gates, c0, n0, m0, conv0, b_if, w_conv, b_conv, g_head):
    n = qkb.shape[0]
    xp = jnp.concatenate([conv0, qkb[:, None, :]], axis=1)
    qk = _silu(b_conv + jnp.sum(xp * w_conv[None], axis=1))
    q = qk[:, :B_W].reshape(n, B_HEADS, B_HD)
    k = qk[:, B_W:].reshape(n, B_HEADS, B_HD) * (B_HD ** -0.5)
    v = vb.reshape(n, B_HEADS, B_HD)
    ig = gates[:, :B_HEADS] + b_if[:B_HEADS]
    logf = _log_sigmoid(gates[:, B_HEADS:] + b_if[B_HEADS:])
    inter = logf + m0
    mt = jnp.maximum(inter, ig)
    w_intra = jnp.exp(ig - mt)
    w_inter = jnp.exp(inter - mt)
    a = w_intra * jnp.sum(q * k, axis=-1)
    num = a[..., None] * v + w_inter[..., None] * jnp.einsum('nhk,nhkv->nhv', q, c0, precision=HIGHEST)
    den = a + w_inter * jnp.sum(q * n0, axis=-1)
    hb = num / jnp.maximum(jnp.abs(den), jnp.exp(-mt))[..., None]
    g = jnp.exp(ig - mt)
    decay = jnp.exp(inter - mt)
    c_new = decay[..., None, None] * c0 + (g[..., None] * k)[..., :, None] * v[..., None, :]
    n_new = decay[..., None] * n0 + g[..., None] * k
    hb = hb * lax.rsqrt(jnp.mean(hb * hb, axis=-1, keepdims=True) + EPS) * g_head.reshape(B_HEADS, B_HD)
    b_out = jax.nn.sigmoid(ob) * hb.reshape(n, B_W)
    return b_out, c_new, n_new, mt, xp[:, 1:]


def _sample_odd_mixers(x, h_re0, h_im0, g_mix, w_in, g_cv, b_cv, w_s, b_s, ab_re, ab_im, bb_re, bb_im,
                       cm_re, cm_im, d_skip, w_glu, b_glu):
    n = x.shape[0]
    h = _rms(x, g_mix)
    proj = _hdot(h, w_in)
    u, v, xd = proj[:, :C_W], proj[:, C_W:2 * C_W], proj[:, 2 * C_W:]
    mu = jnp.mean(v, axis=-1, keepdims=True)
    vc = v - mu
    v = vc * lax.rsqrt(jnp.mean(vc * vc, axis=-1, keepdims=True) + EPS) * g_cv + b_cv
    s = (w_s[:, 0, 0][None, :, None] * v.reshape(n, C_GROUPS, C_GD) + b_s[:, 0][None, :, None]).reshape(n, C_W)
    c_out = u * s
    xg = xd.reshape(n, D_GROUPS, D_GCH)
    bu_re = jnp.einsum('gpc,ngc->ngp', bb_re, xg, precision=HIGHEST)
    bu_im = jnp.einsum('gpc,ngc->ngp', bb_im, xg, precision=HIGHEST)
    hr = ab_re * h_re0 - ab_im * h_im0 + bu_re
    hi = ab_re * h_im0 + ab_im * h_re0 + bu_im
    y = (jnp.einsum('gcp,ngp->ngc', cm_re, hr, precision=HIGHEST)
         - jnp.einsum('gcp,ngp->ngc', cm_im, hi, precision=HIGHEST) + d_skip * xg)
    g = _gelu_tanh(y.reshape(n, D_W))
    d_out = g * jax.nn.sigmoid(_hdot(g, w_glu) + b_glu)
    return c_out, d_out, v, hr, hi


_EVEN_SEGS = ((0, A_W), (A_W, 2 * B_W), (A_W + 2 * B_W, B_W), (A_W + 3 * B_W, B_W), (A_W + 4 * B_W, 128))


def kernel(x_prompt, x_sample, cache_a_k, cache_a_v, state_b_c, state_b_n, state_b_m, state_b_conv,
           state_d_re, state_d_im, rel_bias, g_mix, g_ffn, g_final, w_in_e, b_if, w_conv_b, b_conv_b,
           g_bhead, w_out_e, w1_e, w3_e, w2_e, w_in_o, g_cv, b_cv, w_s, b_s, a_re, a_im, log_dt,
           bm_re, bm_im, cm_re, cm_im, d_skip, w_glu, b_glu, w_out_o, w_router, b_router,
           w1_m, w3_m, w2_m):
    bp, sp, d = x_prompt.shape
    ns = x_sample.shape[0]
    mp = bp * sp
    xp = x_prompt.reshape(mp, d)
    xs = x_sample.reshape(ns, d)

    w_in = w_in_e[0]
    n_main = 3 * A_W + 4 * B_W
    w_gate = w_in[:, n_main:]
    w_cat = jnp.concatenate([w_in[:, :A_W], w_in[:, 3 * A_W:n_main],
                             jnp.pad(w_gate, ((0, 0), (0, 128 - 2 * B_HEADS)))], axis=1)
    w_t = jnp.concatenate([w_in[:, A_W:3 * A_W], w_gate], axis=1).T
    q, qkb, vb, ob, gcol, kt, vt, grow = _norm_proj(
        xp, g_mix[0], w_cat.astype(BF16), w_t.astype(BF16), _EVEN_SEGS,
        ((0, A_W, True), (A_W, A_W, True), (2 * A_W, 2 * B_HEADS, False)), tm=512, rows_per_seq=sp)
    shp = lambda a: a.reshape(bp, sp, a.shape[-1])
    pos_minor = lambda c: jnp.transpose(c, (0, 2, 3, 1))
    pos_major = lambda c: jnp.transpose(c, (0, 3, 1, 2))
    s_proj = _sample_even_proj(xs, g_mix[0], w_in)
    cols = lambda a: a.reshape(ns, A_HEADS, A_HD, 1)
    s_attn = (cols(s_proj[0]), cols(s_proj[1]), cols(s_proj[2]), pos_minor(cache_a_k[0]), pos_minor(cache_a_v[0]))
    a_out = _attention_prompt(shp(q), kt, vt, rel_bias)
    b_out, pc, pn, pm = _mlstm_prompt(shp(qkb), shp(vb), shp(ob), shp(gcol), grow, w_conv_b[0], b_conv_b[0],
                                      b_if[0], g_bhead[0])
    ffn_args = (a_out.reshape(mp, A_W), b_out.reshape(mp, B_W), xp, w_out_e[0].astype(BF16), g_ffn[0],
                w1_e[0].astype(BF16), w3_e[0].astype(BF16), w2_e[0].astype(BF16))
    ffn_tm, ffn_tf = 1024, 512
    if ns * WINDOW_HEAD_BLOCKS == (mp // ffn_tm) * WINDOW_STEPS:
        x1p, sa, s_kt, s_vt = _mix_ffn(*ffn_args, tm=ffn_tm, tf=ffn_tf, window=s_attn, rel_bias=rel_bias)
    else:
        x1p = _mix_ffn(*ffn_args, tm=ffn_tm, tf=ffn_tf)
        sa, s_kt, s_vt = _sample_window_attention(*s_attn, rel_bias)
    sa = sa.reshape(ns, A_W)
    s_k, s_v = pos_major(s_kt), pos_major(s_vt)

    sb, sc, sn, sm, s_conv = _sample_mlstm_step(
        s_proj[3], s_proj[4], s_proj[5], s_proj[6], state_b_c[0], state_b_n[0], state_b_m[0], state_b_conv[0],
        b_if[0], w_conv_b[0], b_conv_b[0], g_bhead[0])
    x1s = _mix_ffn(sa, sb, xs, w_out_e[0], g_ffn[0], w1_e[0], w3_e[0], w2_e[0], tm=ns, tf=512, precise=True)

    ab_re, ab_im, bb_re, bb_im = _s5_discretize(a_re[0], a_im[0], log_dt[0], bm_re[0], bm_im[0], cm_re[0],
                                                cm_im[0])
    bd, cmat = _s5_matrices(bb_re, bb_im, cm_re[0], cm_im[0])
    c_out, xd = _odd_in(x1p, g_mix[1], w_in_o[0].astype(BF16), g_cv[0], b_cv[0], w_s[0], b_s[0], tm=1024)
    d_out, p_hr, p_hi = _s5_prompt(xd.reshape(bp, sp, D_W), bd.astype(BF16), cmat.astype(BF16), ab_re, ab_im,
                                   d_skip[0], w_glu[0].astype(BF16), b_glu[0])
    wr = jnp.pad(w_router[0].astype(F32), ((0, 0), (0, 128 - N_EXPERTS)))
    br = jnp.pad(b_router[0].astype(F32), (0, 128 - N_EXPERTS)).reshape(1, 128)
    x2p, hnp, idp, gtp, rkp, cbp, cap = _mix_router(c_out, d_out.reshape(mp, D_W), x1p, w_out_o[0].astype(BF16),
                                                    g_ffn[1], wr, br, jnp.zeros((1, 128), F32), tm=1024)

    sc_out, sd_out, s_cv, s_hr, s_hi = _sample_odd_mixers(
        x1s, state_d_re[0], state_d_im[0], g_mix[1], w_in_o[0], g_cv[0], b_cv[0], w_s[0], b_s[0], ab_re, ab_im,
        bb_re, bb_im, cm_re[0], cm_im[0], d_skip[0], w_glu[0], b_glu[0])
    x2s, hns, ids_s, gts, rks, cbs, cas = _mix_router(sc_out, sd_out, x1s, w_out_o[0], g_ffn[1], wr, br, cap,
                                                      tm=ns, precise=True)

    pad = ((0, ROUTE_TILE - ns), (0, 0))
    cnt_tile = jnp.concatenate([cbp.reshape(-1, 128), cbs.reshape(-1, 128), cas])[:, :N_EXPERTS].astype(I32)
    y_p, y_s = _moe_block(x2p, hnp, idp, gtp, rkp, jnp.pad(x2s, pad), jnp.pad(hns, pad),
                          jnp.pad(ids_s, pad, constant_values=-1), jnp.pad(gts, pad), jnp.pad(rks, pad),
                          cnt_tile, w1_m[0], w3_m[0], w2_m[0], g_final)
    y_prompt = y_p.reshape(bp, sp, d)
    y_sample = y_s[:ns].reshape(ns, 1, d)

    e = lambda a: a[None]
    p_k = pos_major(kt.reshape(bp, A_HEADS, A_HD, sp))
    p_v = pos_major(vt.reshape(bp, A_HEADS, A_HD, sp))
    p_conv = qkb.reshape(bp, sp, 2 * B_W)[:, sp - (B_CONV - 1):]
    return (y_prompt, y_sample, e(p_k), e(p_v), e(s_k), e(s_v), e(pc), e(sc), e(pn[:, :B_HEADS]), e(sn),
            e(pm[:, :B_HEADS, 0]), e(sm), e(p_conv), e(s_conv), e(s_cv.reshape(ns, 1, C_W)),
            e(p_hr.reshape(bp, D_GROUPS, D_STATE)), e(s_hr), e(p_hi.reshape(bp, D_GROUPS, D_STATE)), e(s_hi))
```

```python
import functools
import math

import jax
import jax.numpy as jnp
from jax import lax
from jax.experimental import pallas as pl
from jax.experimental.pallas import tpu as pltpu

F32 = jnp.float32
BF16 = jnp.bfloat16
I32 = jnp.int32
HIGHEST = lax.Precision.HIGHEST

D_MODEL = 1024
A_HEADS, A_HD = 8, 64
A_W = A_HEADS * A_HD
A_BRANCHES = ((128, 1), (512, 4), (2048, 16))
A_WIN = 2048
N_BUCKETS = 32
B_HEADS, B_HD = 4, 128
B_W = B_HEADS * B_HD
B_CONV = 4
C_GROUPS, C_GD = 4, 128
C_W = C_GROUPS * C_GD
C_CHUNK = 128
D_GROUPS, D_GCH, D_STATE = 32, 16, 64
D_W = D_GROUPS * D_GCH
D_FF = 3584
N_EXPERTS = 8
TOP_K = 2
EPS = 1e-6
NEG = -1e30

VMEM_LIMIT = 56 * 1024 * 1024

ROUTE_TILE = 256
FFN_TILE = 1024
FFN_ROW_CHUNK = 256
GATHER_TOKENS = 2 * ROUTE_TILE
COMBINE_ALIGN = 16


def _cparams(*sem):
    return pltpu.CompilerParams(dimension_semantics=sem, vmem_limit_bytes=VMEM_LIMIT)


def _moe_plan(ids, rank, cnt_tile):
    mt = ids.shape[0]
    n_assign = 2 * mt
    n_ffn_tiles = -(-(n_assign + N_EXPERTS * (FFN_TILE - 1)) // FFN_TILE) + 1
    n_sub = n_ffn_tiles * (FFN_TILE // ROUTE_TILE)
    counts = cnt_tile[-1]
    seg = ((counts + FFN_TILE - 1) // FFN_TILE) * FFN_TILE
    seg_end = jnp.cumsum(seg)
    seg_off = seg_end - seg
    off_of = jnp.sum(jnp.where(ids[..., None] == jnp.arange(N_EXPERTS, dtype=I32), seg_off, 0), axis=-1)
    dest = jnp.where(ids >= 0, off_of + rank, -1)
    total = seg_end[-1]
    tile_start = jnp.arange(n_ffn_tiles, dtype=I32) * FFN_TILE
    tile_expert = jnp.minimum(jnp.sum((tile_start[:, None] >= seg_end[None, :]).astype(I32), axis=1),
                              N_EXPERTS - 1).astype(I32)
    rows_left = (seg_off + counts)[tile_expert] - tile_start
    tile_chunks = jnp.where(tile_start < total,
                            jnp.clip(-(-rows_left // FFN_ROW_CHUNK), 0, FFN_TILE // FFN_ROW_CHUNK), 0).astype(I32)
    lo = seg_off[None, :] + cnt_tile[:-1]
    hi = seg_off[None, :] + cnt_tile[1:]
    sub_start = jnp.arange(n_sub, dtype=I32) * ROUTE_TILE
    sub_expert = jnp.minimum(jnp.sum((sub_start[:, None] >= seg_end[None, :]).astype(I32), axis=1),
                             N_EXPERTS - 1)
    lo_e, hi_e = lo.T[sub_expert], hi.T[sub_expert]
    jlo = jnp.sum((hi_e <= sub_start[:, None]).astype(I32), axis=1).astype(I32)
    jhi = (jnp.sum((lo_e < sub_start[:, None] + ROUTE_TILE).astype(I32), axis=1) - 1).astype(I32)
    win = lo // COMBINE_ALIGN
    flat1 = lambda a: a.reshape(-1).astype(I32)
    return dest, tile_expert, tile_chunks, (jlo, jhi), (flat1(win), flat1(lo), flat1(hi)), n_ffn_tiles


def _moe_gather_kernel(jlo_ref, jhi_ref, hp_hbm, hs_hbm, dt_ref, gt_ref, xs_ref, gs_ref,
                       hn_ref, acc_ref, g_ref, sem):
    s = pl.program_id(0)
    mp = hp_hbm.shape[0]

    @pl.when(s == 0)
    def _():
        copies = (pltpu.make_async_copy(hp_hbm, hn_ref.at[0:mp], sem.at[0]),
                  pltpu.make_async_copy(hs_hbm, hn_ref.at[mp:mp + ROUTE_TILE], sem.at[1]))
        for c in copies:
            c.start()
        hn_ref[mp + ROUTE_TILE:, :] = jnp.zeros((ROUTE_TILE, hn_ref.shape[1]), BF16)
        for c in copies:
            c.wait()

    acc_ref[...] = jnp.zeros_like(acc_ref)
    g_ref[...] = jnp.zeros_like(g_ref)
    rows = lax.broadcasted_iota(I32, (ROUTE_TILE, GATHER_TOKENS), 0) + s * ROUTE_TILE
    jlo = jlo_ref[s]

    def body(it, carry):
        off = pl.multiple_of(jlo * ROUTE_TILE + it * GATHER_TOKENS, ROUTE_TILE)
        dt = dt_ref[:, pl.ds(off, GATHER_TOKENS)]
        gt = gt_ref[:, pl.ds(off, GATHER_TOKENS)]
        m1 = dt[0:1, :] == rows
        m2 = dt[1:2, :] == rows
        onehot = jnp.where(m1 | m2, 1.0, 0.0).astype(BF16)
        acc_ref[...] += jnp.dot(onehot, hn_ref[pl.ds(off, GATHER_TOKENS), :], preferred_element_type=F32)
        g_ref[...] += jnp.sum(jnp.where(m1, gt[0:1, :], 0.0) + jnp.where(m2, gt[1:2, :], 0.0),
                              axis=1, keepdims=True)
        return carry

    tiles_per_iter = GATHER_TOKENS // ROUTE_TILE
    lax.fori_loop(0, (jhi_ref[s] - jlo + tiles_per_iter) // tiles_per_iter, body, 0)
    xs_ref[...] = acc_ref[...].astype(BF16)
    gs_ref[...] = g_ref[...]


def _moe_gather(hn_p, hn_s, dest_t, gates_t, glist, n_sub):
    jlo, jhi = glist
    mp, d = hn_p.shape
    mt = dest_t.shape[1]
    grid_spec = pltpu.PrefetchScalarGridSpec(
        num_scalar_prefetch=2,
        grid=(n_sub,),
        in_specs=[
            pl.BlockSpec(memory_space=pl.ANY),
            pl.BlockSpec(memory_space=pl.ANY),
            pl.BlockSpec((2, mt), lambda s, *_: (0, 0)),
            pl.BlockSpec((2, mt), lambda s, *_: (0, 0)),
        ],
        out_specs=[
            pl.BlockSpec((ROUTE_TILE, d), lambda s, *_: (s, 0)),
            pl.BlockSpec((ROUTE_TILE, 1), lambda s, *_: (s, 0)),
        ],
        scratch_shapes=[pltpu.VMEM((mp + 2 * ROUTE_TILE, d), BF16), pltpu.VMEM((ROUTE_TILE, d), F32),
                        pltpu.VMEM((ROUTE_TILE, 1), F32), pltpu.SemaphoreType.DMA((2,))],
    )
    return pl.pallas_call(
        _moe_gather_kernel,
        grid_spec=grid_spec,
        out_shape=[jax.ShapeDtypeStruct((n_sub * ROUTE_TILE, d), BF16),
                   jax.ShapeDtypeStruct((n_sub * ROUTE_TILE, 1), F32)],
        compiler_params=_cparams("arbitrary"),
        name="moe_gather",
    )(jlo, jhi, hn_p, hn_s, dest_t, gates_t)


def _moe_ffn_kernel(te_ref, nc_ref, x_ref, gs_ref, w1_ref, w3_ref, w2_ref, y_ref, acc_ref):
    t, f = pl.program_id(0), pl.program_id(1)
    nf = pl.num_programs(1)
    n_chunks = nc_ref[t]
    full = FFN_TILE // FFN_ROW_CHUNK

    @pl.when(n_chunks > 0)
    def _():
        @pl.when(f == 0)
        def _():
            acc_ref[...] = jnp.zeros_like(acc_ref)

        w1, w3, w2 = w1_ref[0].astype(BF16), w3_ref[0].astype(BF16), w2_ref[0].astype(BF16)

        def ffn(rows):
            x = x_ref[rows, :]
            a = jnp.dot(x, w1, preferred_element_type=F32)
            b = jnp.dot(x, w3, preferred_element_type=F32)
            acc_ref[rows, :] += jnp.dot((_silu(a) * b).astype(BF16), w2, preferred_element_type=F32)

        @pl.when(n_chunks == full)
        def _():
            ffn(slice(0, FFN_TILE))

        for c in range(full - 1):
            @pl.when((n_chunks < full) & (c < n_chunks))
            def _():
                ffn(slice(c * FFN_ROW_CHUNK, (c + 1) * FFN_ROW_CHUNK))

        @pl.when(f == nf - 1)
        def _():
            y_ref[...] = (acc_ref[...] * gs_ref[...]).astype(BF16)

    @pl.when((n_chunks == 0) & (f == nf - 1))
    def _():
        y_ref[...] = jnp.zeros_like(y_ref)


def _moe_ffn(xs, gs, w1, w3, w2, tile_expert, tile_chunks, n_ffn_tiles, tf):
    d = xs.shape[1]
    ff = w1.shape[2]
    nf = ff // tf

    def fidx(t, f, nc):
        return jnp.where(nc[t] > 0, f, nf - 1)

    grid_spec = pltpu.PrefetchScalarGridSpec(
        num_scalar_prefetch=2,
        grid=(n_ffn_tiles, nf),
        in_specs=[
            pl.BlockSpec((FFN_TILE, d), lambda t, f, te, nu: (t, 0)),
            pl.BlockSpec((FFN_TILE, 1), lambda t, f, te, nu: (t, 0)),
            pl.BlockSpec((1, d, tf), lambda t, f, te, nu: (te[t], 0, fidx(t, f, nu))),
            pl.BlockSpec((1, d, tf), lambda t, f, te, nu: (te[t], 0, fidx(t, f, nu))),
            pl.BlockSpec((1, tf, d), lambda t, f, te, nu: (te[t], fidx(t, f, nu), 0)),
        ],
        out_specs=pl.BlockSpec((FFN_TILE, d), lambda t, f, te, nu: (t, 0)),
        scratch_shapes=[pltpu.VMEM((FFN_TILE, d), F32)],
    )
    return pl.pallas_call(
        _moe_ffn_kernel,
        grid_spec=grid_spec,
        out_shape=jax.ShapeDtypeStruct(xs.shape, BF16),
        compiler_params=_cparams("arbitrary", "arbitrary"),
        name="moe_ffn",
    )(tile_expert, tile_chunks, xs, gs, w1, w3, w2)


def _moe_combine_kernel(win_ref, lo_ref, hi_ref, xp_ref, xs_ref, d_ref, g_ref, *rest):
    ys_refs, tail_refs = rest[:N_EXPERTS], rest[N_EXPERTS:2 * N_EXPERTS]
    yp_ref, ysm_ref, acc_ref = rest[2 * N_EXPERTS:]
    j = pl.program_id(0)
    n_prompt_tiles = pl.num_programs(0) - 1

    @pl.when(j < n_prompt_tiles)
    def _():
        acc_ref[...] = xp_ref[...]

    @pl.when(j == n_prompt_tiles)
    def _():
        acc_ref[...] = xs_ref[...]

    for e in range(N_EXPERTS):
        lo, hi = lo_ref[j * N_EXPERTS + e], hi_ref[j * N_EXPERTS + e]

        @pl.when(hi > lo)
        def _():
            d = d_ref[...]
            d = jnp.where((d >= lo) & (d < hi), d, -1)
            start = win_ref[j * N_EXPERTS + e] * COMBINE_ALIGN

            def gathered(ref, first):
                cols = lax.broadcasted_iota(I32, (ROUTE_TILE, ref.shape[0]), 1) + first
                onehot = jnp.where((d[:, 0:1] == cols) | (d[:, 1:2] == cols), 1.0, 0.0).astype(BF16)
                return jnp.dot(onehot, ref[...], preferred_element_type=F32)

            acc_ref[...] += gathered(ys_refs[e], start)

            @pl.when(hi > start + ROUTE_TILE)
            def _():
                acc_ref[...] += gathered(tail_refs[e], start + ROUTE_TILE)

    y = _rms(acc_ref[...], g_ref[...])

    @pl.when(j < n_prompt_tiles)
    def _():
        yp_ref[...] = y

    @pl.when(j == n_prompt_tiles)
    def _():
        ysm_ref[...] = y


def _moe_combine(x2_p, x2_s, dest, ys, g_final, clist):
    win, lo, hi = clist
    mp, d = x2_p.shape
    n_prompt_tiles = mp // ROUTE_TILE

    def ptile(j, *_):
        return (jnp.minimum(j, n_prompt_tiles - 1), 0)

    def window(e, rows, offset):
        return pl.BlockSpec((pl.Element(rows), pl.Element(d)),
                            lambda j, win, lo, hi: ((win[j * N_EXPERTS + e] + offset // COMBINE_ALIGN)
                                                    * COMBINE_ALIGN, 0))

    grid_spec = pltpu.PrefetchScalarGridSpec(
        num_scalar_prefetch=3,
        grid=(n_prompt_tiles + 1,),
        in_specs=[
            pl.BlockSpec((ROUTE_TILE, d), ptile),
            pl.BlockSpec((ROUTE_TILE, d), lambda j, *_: (0, 0)),
            pl.BlockSpec((ROUTE_TILE, 2), lambda j, *_: (j, 0)),
            pl.BlockSpec((1, d), lambda j, *_: (0, 0)),
        ] + [window(e, ROUTE_TILE, 0) for e in range(N_EXPERTS)]
          + [window(e, COMBINE_ALIGN, ROUTE_TILE) for e in range(N_EXPERTS)],
        out_specs=[pl.BlockSpec((ROUTE_TILE, d), ptile), pl.BlockSpec((ROUTE_TILE, d), lambda j, *_: (0, 0))],
        scratch_shapes=[pltpu.VMEM((ROUTE_TILE, d), F32)],
    )
    return pl.pallas_call(
        _moe_combine_kernel,
        grid_spec=grid_spec,
        out_shape=[jax.ShapeDtypeStruct((mp, d), F32), jax.ShapeDtypeStruct((ROUTE_TILE, d), F32)],
        compiler_params=_cparams("arbitrary"),
        name="moe_combine",
    )(win, lo, hi, x2_p, x2_s, dest, g_final.reshape(1, d), *([ys] * (2 * N_EXPERTS)))


def _moe_block(x2_p, hn_p, ids_p, gates_p, rank_p, x2_s, hn_s, ids_s, gates_s, rank_s, cnt_tile, w1, w3, w2,
               g_final, tf=512):
    ids = jnp.concatenate([ids_p, ids_s])
    gates = jnp.concatenate([gates_p, gates_s])
    rank = jnp.concatenate([rank_p, rank_s])
    dest, tile_expert, tile_chunks, glist, clist, n_ffn_tiles = _moe_plan(ids, rank, cnt_tile)
    n_sub = n_ffn_tiles * (FFN_TILE // ROUTE_TILE)
    spare = ((0, 0), (0, GATHER_TOKENS - ROUTE_TILE))
    xs, gs = _moe_gather(hn_p, hn_s, jnp.pad(dest.T, spare, constant_values=-1), jnp.pad(gates.T, spare), glist,
                         n_sub)
    ys = _moe_ffn(xs, gs, w1, w3, w2, tile_expert, tile_chunks, n_ffn_tiles, tf)
    return _moe_combine(x2_p, x2_s, dest, ys, g_final, clist)


def _rms(x, g):
    return x * lax.rsqrt(jnp.mean(x * x, axis=-1, keepdims=True) + EPS) * g


def _mm(a, b, precise):
    if precise:
        return jnp.dot(a.astype(F32), b.astype(F32), preferred_element_type=F32, precision=HIGHEST)
    return jnp.dot(a.astype(BF16), b.astype(BF16), preferred_element_type=F32)


def _silu(x):
    return x * jax.nn.sigmoid(x)


def _norm_proj_kernel(x_ref, g_ref, w_ref, wt_ref, *out_refs, segs, tsegs, precise):
    xn = _rms(x_ref[...], g_ref[...])
    xn = xn if precise else xn.astype(BF16)
    for (start, width), o_ref in zip(segs, out_refs[:len(segs)]):
        for c in range(0, width, 512):
            cw = min(512, width - c)
            o_ref[:, c:c + cw] = _mm(xn, w_ref[:, start + c:start + c + cw], precise)
    for (start, height), o_ref in zip(tsegs, out_refs[len(segs):]):
        res = lax.dot_general(wt_ref[start:start + height, :].astype(xn.dtype), xn, (((1,), (1,)), ((), ())),
                              preferred_element_type=F32, precision=HIGHEST if precise else None)
        o_ref[...] = res.reshape(o_ref.shape)


def _norm_proj(x, g, w, wt, segs, tsegs, tm, rows_per_seq=None, precise=False):
    m, d = x.shape
    n = w.shape[1]
    nt = wt.shape[0]
    out_shape = [jax.ShapeDtypeStruct((m, width), F32) for _, width in segs]
    out_specs = [pl.BlockSpec((tm, width), lambda i: (i, 0)) for _, width in segs]
    for _, height, per_seq in tsegs:
        if per_seq:
            tps = rows_per_seq // tm
            out_shape.append(jax.ShapeDtypeStruct((m // rows_per_seq, height, rows_per_seq), F32))
            out_specs.append(pl.BlockSpec((1, height, tm), lambda i, tps=tps: (i // tps, 0, i % tps)))
        else:
            out_shape.append(jax.ShapeDtypeStruct((height, m), F32))
            out_specs.append(pl.BlockSpec((height, tm), lambda i: (0, i)))
    return pl.pallas_call(
        functools.partial(_norm_proj_kernel, segs=tuple(segs), tsegs=tuple(t[:2] for t in tsegs), precise=precise),
        grid=(m // tm,),
        in_specs=[
            pl.BlockSpec((tm, d), lambda i: (i, 0)),
            pl.BlockSpec((1, d), lambda i: (0, 0)),
            pl.BlockSpec((d, n), lambda i: (0, 0)),
            pl.BlockSpec((nt, d), lambda i: (0, 0)),
        ],
        out_specs=out_specs,
        out_shape=out_shape,
        compiler_params=_cparams("arbitrary"),
        name="norm_proj",
    )(x, g.reshape(1, d), w, wt)


def _mix_ffn_prologue(a_ref, b_ref, x_ref, wo_ref, g_ref, o_ref, hn_ref, precise):
    wa = a_ref.shape[1]

    @pl.when(pl.program_id(1) == 0)
    def _():
        x1 = (x_ref[...] + _mm(a_ref[...], wo_ref[0:wa, :], precise)
              + _mm(b_ref[...], wo_ref[wa:, :], precise))
        o_ref[...] = x1
        hn_ref[...] = _rms(x1, g_ref[...]).astype(hn_ref.dtype)


def _mix_ffn_step(w1_ref, w3_ref, w2_ref, o_ref, hn_ref, precise):
    wdt = F32 if precise else BF16
    w1, w3, w2 = w1_ref[...].astype(wdt), w3_ref[...].astype(wdt), w2_ref[...].astype(wdt)
    hn = hn_ref[...]
    o_ref[...] += _mm(_silu(_mm(hn, w1, precise)) * _mm(hn, w3, precise), w2, precise)


def _mix_ffn_kernel(a_ref, b_ref, x_ref, wo_ref, g_ref, w1_ref, w3_ref, w2_ref, o_ref, hn_ref, *, precise):
    _mix_ffn_prologue(a_ref, b_ref, x_ref, wo_ref, g_ref, o_ref, hn_ref, precise)
    _mix_ffn_step(w1_ref, w3_ref, w2_ref, o_ref, hn_ref, precise)


WINDOW_HEAD_BLOCKS = 2
WINDOW_STEPS = 4
WINDOW_STRIDE = 2
N_MIX_FFN_IN, N_WINDOW_IN, N_WINDOW_OUT = 8, 7, 3


def _mix_ffn_window_kernel(*refs, precise):
    (a_ref, b_ref, x_ref, wo_ref, g_ref, w1_ref, w3_ref, w2_ref), refs = refs[:N_MIX_FFN_IN], refs[N_MIX_FFN_IN:]
    w_ins, refs = refs[:N_WINDOW_IN], refs[N_WINDOW_IN:]
    o_ref, w_outs, hn_ref = refs[0], refs[1:1 + N_WINDOW_OUT], refs[1 + N_WINDOW_OUT]
    f = pl.program_id(1)
    _mix_ffn_prologue(a_ref, b_ref, x_ref, wo_ref, g_ref, o_ref, hn_ref, precise)

    slot = f // WINDOW_STRIDE
    hosting = (f % WINDOW_STRIDE == 0) & (slot < WINDOW_STEPS)

    @pl.when(hosting)
    def _():
        _sample_window_kernel(*w_ins, *w_outs, head0=(slot % WINDOW_HEAD_BLOCKS) * (A_HEADS // WINDOW_HEAD_BLOCKS))
        _mix_ffn_step(w1_ref, w3_ref, w2_ref, o_ref, hn_ref, precise)

    @pl.when(jnp.logical_not(hosting))
    def _():
        _mix_ffn_step(w1_ref, w3_ref, w2_ref, o_ref, hn_ref, precise)


def _mix_ffn(a, b, x, w_out, g, w1, w3, w2, tm, tf, precise=False, window=None, rel_bias=None):
    m, d = x.shape
    wa, wb = a.shape[1], b.shape[1]
    ff = w1.shape[1]
    in_specs = [
        pl.BlockSpec((tm, wa), lambda i, f: (i, 0)),
        pl.BlockSpec((tm, wb), lambda i, f: (i, 0)),
        pl.BlockSpec((tm, d), lambda i, f: (i, 0)),
        pl.BlockSpec((wa + wb, d), lambda i, f: (0, 0)),
        pl.BlockSpec((1, d), lambda i, f: (0, 0)),
        pl.BlockSpec((d, tf), lambda i, f: (0, f)),
        pl.BlockSpec((d, tf), lambda i, f: (0, f)),
        pl.BlockSpec((tf, d), lambda i, f: (f, 0)),
    ]
    out_specs = [pl.BlockSpec((tm, d), lambda i, f: (i, 0))]
    out_shape = [jax.ShapeDtypeStruct((m, d), F32)]
    operands = [a, b, x, w_out, g.reshape(1, d), w1, w3, w2]
    body = _mix_ffn_kernel
    if window is not None:
        assert ff // tf > (WINDOW_STEPS - 1) * WINDOW_STRIDE
        assert window[0].shape[0] * WINDOW_HEAD_BLOCKS == (m // tm) * WINDOW_STEPS

        def block_of_step(i, f):
            p = i * WINDOW_STEPS + jnp.minimum(f // WINDOW_STRIDE, WINDOW_STEPS - 1)
            return p // WINDOW_HEAD_BLOCKS, p % WINDOW_HEAD_BLOCKS

        w_operands, w_in, w_out_specs, w_shape = _sample_window_specs(*window, rel_bias, block_of_step,
                                                                      A_HEADS // WINDOW_HEAD_BLOCKS)
        in_specs, out_specs, out_shape = in_specs + w_in, out_specs + w_out_specs, out_shape + w_shape
        operands, body = operands + w_operands, _mix_ffn_window_kernel
    res = pl.pallas_call(
        functools.partial(body, precise=precise),
        grid=(m // tm, ff // tf),
        in_specs=in_specs,
        out_specs=out_specs,
        out_shape=out_shape,
        scratch_shapes=[pltpu.VMEM((tm, d), F32 if precise else BF16)],
        compiler_params=_cparams("arbitrary", "arbitrary"),
        name="mix_ffn",
    )(*operands)
    return res[0] if window is None else res


def _mix_router_kernel(a_ref, b_ref, x_ref, wo_ref, g_ref, wr_ref, br_ref, cnt0_ref, x2_ref, hn_ref, ids_ref,
                       gate_ref, rank_ref, cb_ref, ca_ref, cnt_ref, *, precise):
    @pl.when(pl.program_id(0) == 0)
    def _():
        cnt_ref[...] = cnt0_ref[...]

    wa = a_ref.shape[1]
    x2 = (x_ref[...] + _mm(a_ref[...], wo_ref[0:wa, :], precise)
          + _mm(b_ref[...], wo_ref[wa:, :], precise))
    x2_ref[...] = x2
    hn = _rms(x2, g_ref[...])
    hn_ref[...] = hn.astype(BF16)
    wr = wr_ref[...]
    if precise:
        logits = jnp.dot(hn, wr, preferred_element_type=F32, precision=HIGHEST)
    else:
        hn_hi, wr_hi = hn.astype(BF16), wr.astype(BF16)
        hn_lo, wr_lo = (hn - hn_hi.astype(F32)).astype(BF16), (wr - wr_hi.astype(F32)).astype(BF16)
        logits = (jnp.dot(hn_hi, wr_hi, preferred_element_type=F32)
                  + (jnp.dot(hn_lo, wr_hi, preferred_element_type=F32)
                     + jnp.dot(hn_hi, wr_lo, preferred_element_type=F32)))
    lane = lax.broadcasted_iota(I32, logits.shape, 1)
    real = lane < N_EXPERTS
    biased = jnp.where(real, logits + br_ref[...], -jnp.inf)
    m1 = jnp.max(biased, axis=-1, keepdims=True)
    i1 = jnp.min(jnp.where(biased == m1, lane, 128), axis=-1, keepdims=True)
    rest = jnp.where(lane == i1, -jnp.inf, biased)
    m2 = jnp.max(rest, axis=-1, keepdims=True)
    i2 = jnp.min(jnp.where(rest == m2, lane, 128), axis=-1, keepdims=True)
    l1 = jnp.sum(jnp.where(lane == i1, logits, 0.0), axis=-1, keepdims=True)
    l2 = jnp.sum(jnp.where(lane == i2, logits, 0.0), axis=-1, keepdims=True)
    mx = jnp.maximum(l1, l2)
    e1, e2 = jnp.exp(l1 - mx), jnp.exp(l2 - mx)
    two = lax.broadcasted_iota(I32, ids_ref.shape, 1)
    ids_ref[...] = jnp.where(two == 0, i1, i2)
    gate_ref[...] = jnp.where(two == 0, e1, e2) / (e1 + e2)
    tm = logits.shape[0]
    chosen = jnp.where((lane == i1) | (lane == i2), 1.0, 0.0)
    earlier = (lax.broadcasted_iota(I32, (tm, tm), 0) > lax.broadcasted_iota(I32, (tm, tm), 1))
    before = cnt_ref[...] + jnp.dot(jnp.where(earlier, 1.0, 0.0).astype(BF16), chosen.astype(BF16),
                                    preferred_element_type=F32)
    r1 = jnp.sum(jnp.where(lane == i1, before, 0.0), axis=-1, keepdims=True)
    r2 = jnp.sum(jnp.where(lane == i2, before, 0.0), axis=-1, keepdims=True)
    rank_ref[...] = jnp.where(two == 0, r1, r2).astype(I32)
    for c in range(cb_ref.shape[1]):
        cb_ref[0, c:c + 1, :] = before[c * ROUTE_TILE:c * ROUTE_TILE + 1, :]
    cnt_ref[...] += jnp.sum(chosen, axis=0, keepdims=True)
    ca_ref[...] = cnt_ref[...]


def _mix_router(a, b, x, w_out, g, wr, br, cnt0, tm, precise=False):
    m, d = x.shape
    wa, wb = a.shape[1], b.shape[1]
    n_marks = max(1, tm // ROUTE_TILE)
    return pl.pallas_call(
        functools.partial(_mix_router_kernel, precise=precise),
        grid=(m // tm,),
        in_specs=[
            pl.BlockSpec((tm, wa), lambda i: (i, 0)),
            pl.BlockSpec((tm, wb), lambda i: (i, 0)),
            pl.BlockSpec((tm, d), lambda i: (i, 0)),
            pl.BlockSpec((wa + wb, d), lambda i: (0, 0)),
            pl.BlockSpec((1, d), lambda i: (0, 0)),
            pl.BlockSpec((d, 128), lambda i: (0, 0)),
            pl.BlockSpec((1, 128), lambda i: (0, 0)),
            pl.BlockSpec((1, 128), lambda i: (0, 0)),
        ],
        out_specs=[
            pl.BlockSpec((tm, d), lambda i: (i, 0)),
            pl.BlockSpec((tm, d), lambda i: (i, 0)),
            pl.BlockSpec((tm, 2), lambda i: (i, 0)),
            pl.BlockSpec((tm, 2), lambda i: (i, 0)),
            pl.BlockSpec((tm, 2), lambda i: (i, 0)),
            pl.BlockSpec((1, n_marks, 128), lambda i: (i, 0, 0)),
            pl.BlockSpec((1, 128), lambda i: (0, 0)),
        ],
        out_shape=[jax.ShapeDtypeStruct((m, d), F32), jax.ShapeDtypeStruct((m, d), BF16),
                   jax.ShapeDtypeStruct((m, 2), I32), jax.ShapeDtypeStruct((m, 2), F32),
                   jax.ShapeDtypeStruct((m, 2), I32), jax.ShapeDtypeStruct((m // tm, n_marks, 128), F32),
                   jax.ShapeDtypeStruct((1, 128), F32)],
        scratch_shapes=[pltpu.VMEM((1, 128), F32)],
        compiler_params=_cparams("arbitrary"),
        name="mix_router",
    )(a, b, x, w_out, g.reshape(1, d), wr, br, cnt0)


ATT_T = 512


def _t5_bucket(dist):
    max_exact = N_BUCKETS // 2
    d = jnp.maximum(dist, 1).astype(F32)
    large = max_exact + (jnp.log(d / max_exact) / math.log(A_WIN / max_exact)
                         * (N_BUCKETS - max_exact)).astype(I32)
    return jnp.where(dist < max_exact, dist, jnp.minimum(large, N_BUCKETS - 1))


def _distance_logit_table(rel_bias, max_dist):
    dist = jnp.arange(max_dist + 1, dtype=I32)
    mult = jnp.zeros((max_dist + 1,), F32)
    for window, d in A_BRANCHES:
        mult = mult + ((dist % d == 0) & (dist <= window)).astype(F32)
    bias = rel_bias[_t5_bucket(dist)].astype(F32).T
    return jnp.where(mult[None, :] > 0, bias + jnp.log(jnp.maximum(mult, 1.0))[None, :], NEG)


def _attn_kernel(q_ref, kt_ref, vt_ref, row0_ref, o_ref, kb_ref, vb_ref):
    s_len = q_ref.shape[1]
    nb = s_len // ATT_T
    period = row0_ref.shape[2]
    kb_ref[...] = kt_ref[0].astype(BF16)
    vb_ref[...] = vt_ref[0].astype(BF16)
    scale = A_HD ** -0.5
    for hl in range(2):
        hrows = slice(hl * A_HD, (hl + 1) * A_HD)
        table = pltpu.roll(jnp.broadcast_to(row0_ref[0, hl:hl + 1, :], (ATT_T, period)), 0, 1,
                           stride=1, stride_axis=0)
        for i in range(nb):
            rows = slice(i * ATT_T, (i + 1) * ATT_T)
            width = (i + 1) * ATT_T
            q = (q_ref[0, rows, hrows] * scale).astype(BF16)
            s = jnp.dot(q, kb_ref[hrows, 0:width], preferred_element_type=F32)
            s = s + table[:, (nb - 1 - i) * ATT_T:nb * ATT_T]
            m = jnp.max(s, axis=-1, keepdims=True)
            p = jnp.exp(s - m)
            den = jnp.sum(p, axis=-1, keepdims=True)
            o = lax.dot_general(p.astype(BF16), vb_ref[hrows, 0:width], (((1,), (1,)), ((), ())),
                                preferred_element_type=F32)
            o_ref[0, rows, hrows] = o / den


def _attention_prompt(q, kt, vt, rel_bias):
    b, s_len, _ = q.shape
    nb = s_len // ATT_T
    tab1 = _distance_logit_table(rel_bias, s_len)
    period = s_len + ATT_T
    z = jnp.arange(period, dtype=I32)
    delta = jnp.where(z < s_len, (nb - 1) * ATT_T - z, (nb - 1) * ATT_T + period - z)
    row0 = jnp.where(delta[None] >= 0, tab1[:, jnp.clip(delta, 0, s_len)], NEG)
    return pl.pallas_call(
        _attn_kernel,
        grid=(b, A_HEADS // 2),
        in_specs=[pl.BlockSpec((1, s_len, 2 * A_HD), lambda bi, hp: (bi, 0, hp)),
                  pl.BlockSpec((1, 2 * A_HD, s_len), lambda bi, hp: (bi, hp, 0)),
                  pl.BlockSpec((1, 2 * A_HD, s_len), lambda bi, hp: (bi, hp, 0)),
                  pl.BlockSpec((1, 2, period), lambda bi, hp: (hp, 0, 0))],
        out_specs=pl.BlockSpec((1, s_len, 2 * A_HD), lambda bi, hp: (bi, 0, hp)),
        out_shape=jax.ShapeDtypeStruct(q.shape, F32),
        scratch_shapes=[pltpu.VMEM((2 * A_HD, s_len), BF16), pltpu.VMEM((2 * A_HD, s_len), BF16)],
        compiler_params=_cparams("arbitrary", "arbitrary"),
        name="dilated_attention",
    )(q, kt, vt, row0.reshape(A_HEADS // 2, 2, period))


MLSTM_CHUNK = 256


def _log_sigmoid(x):
    return jnp.minimum(x, 0.0) - jnp.log(1.0 + jnp.exp(-jnp.abs(x)))


def _mlstm_kernel(qk_ref, v_ref, ob_ref, gc_ref, gr_ref, wc_ref, bc_ref, bifc_ref, bifr_ref, gh_ref,
                  o_ref, c_out_ref, n_out_ref, m_out_ref, xbuf_ref, c_ref, n_ref, m_ref):
    ci = pl.program_id(1)
    L = qk_ref.shape[1]

    @pl.when(ci == 0)
    def _():
        xbuf_ref[0:8, :] = jnp.zeros((8, 2 * B_W), F32)
        c_ref[...] = jnp.zeros_like(c_ref)
        n_ref[...] = jnp.zeros_like(n_ref)
        m_ref[...] = jnp.zeros_like(m_ref)

    x = qk_ref[0]
    xbuf_ref[8:8 + L, :] = x
    y = bc_ref[...] + wc_ref[3:4, :] * x
    for j in range(B_CONV - 1):
        y = y + wc_ref[j:j + 1, :] * xbuf_ref[5 + j:5 + j + L, :]
    xbuf_ref[0:8, :] = x[L - 8:L, :]
    y = _silu(y)

    gcol = gc_ref[0] + bifc_ref[...]
    grow = gr_ref[...] + bifr_ref[...]
    ri = lax.broadcasted_iota(I32, (L, L), 0)
    cj = lax.broadcasted_iota(I32, (L, L), 1)
    causal = ri >= cj
    lower = jnp.where(causal, 1.0, 0.0)
    b_col = jnp.dot(lower, _log_sigmoid(gcol), preferred_element_type=F32, precision=HIGHEST)
    b_row = lax.dot_general(_log_sigmoid(grow), lower, (((1,), (1,)), ((), ())),
                            preferred_element_type=F32, precision=HIGHEST)

    for h in range(B_HEADS):
        cols = slice(h * B_HD, (h + 1) * B_HD)
        qf = y[:, cols]
        q = qf.astype(BF16)
        kf = y[:, B_W + h * B_HD:B_W + (h + 1) * B_HD] * (B_HD ** -0.5)
        k = kf.astype(BF16)
        v = v_ref[0, :, cols].astype(BF16)
        bc = b_col[:, B_HEADS + h:B_HEADS + h + 1]
        ic = gcol[:, h:h + 1]
        br = b_row[B_HEADS + h:B_HEADS + h + 1, :]
        ir = grow[h:h + 1, :]
        m_prev = m_ref[h:h + 1, 0:1]
        dm = jnp.where(causal, bc - br + ir, NEG)
        inter = bc + m_prev
        mt = jnp.maximum(inter, jnp.max(dm, axis=1, keepdims=True))
        w_intra = jnp.exp(dm - mt)
        w_inter = jnp.exp(inter - mt)
        a = w_intra * lax.dot_general(q, k, (((1,), (1,)), ((), ())), preferred_element_type=F32)
        c_prev = c_ref[h]
        n_prev = n_ref[h:h + 1, :]
        num = (jnp.dot(a.astype(BF16), v, preferred_element_type=F32)
               + w_inter * jnp.dot(q, c_prev.astype(BF16), preferred_element_type=F32))
        den = (jnp.sum(a, axis=1, keepdims=True)
               + w_inter * jnp.sum(qf * n_prev, axis=1, keepdims=True))
        hb = num / jnp.maximum(jnp.abs(den), jnp.exp(-mt))
        m_new = mt[L - 1:L, :]
        b_last = bc[L - 1:L, :]
        g = jnp.exp(b_last - bc + ic - m_new)
        decay = jnp.exp(b_last + m_prev - m_new)
        kg = kf * g
        c_ref[h] = decay * c_prev + lax.dot_general(kg.astype(BF16), v, (((0,), (0,)), ((), ())),
                                                    preferred_element_type=F32)
        n_ref[h:h + 1, :] = decay * n_prev + jnp.sum(kg, axis=0, keepdims=True)
        m_ref[h:h + 1, :] = jnp.broadcast_to(m_new, (1, 128))
        hb = hb * lax.rsqrt(jnp.mean(hb * hb, axis=-1, keepdims=True) + EPS) * gh_ref[:, cols]
        o_ref[0, :, cols] = jax.nn.sigmoid(ob_ref[0, :, cols]) * hb

    @pl.when(ci == pl.num_programs(1) - 1)
    def _():
        c_out_ref[0] = c_ref[...]
        n_out_ref[0] = n_ref[...]
        m_out_ref[0] = m_ref[...]


def _mlstm_prompt(qk, v, ob, gcol, grow, w_conv, b_conv, b_if, g_head):
    b, s_len, _ = qk.shape
    L = min(MLSTM_CHUNK, s_len)
    nc = s_len // L
    bif_col = jnp.zeros((1, 128), F32).at[0, :2 * B_HEADS].set(b_if)
    bif_row = b_if.reshape(2 * B_HEADS, 1)
    seq = lambda w: pl.BlockSpec((1, L, w), lambda bi, ci: (bi, ci, 0))
    full = lambda shape: pl.BlockSpec(shape, lambda bi, ci: (0,) * len(shape))
    return pl.pallas_call(
        _mlstm_kernel,
        grid=(b, nc),
        in_specs=[seq(2 * B_W), seq(B_W), seq(B_W), seq(128),
                  pl.BlockSpec((2 * B_HEADS, L), lambda bi, ci: (0, bi * nc + ci)),
                  full((B_CONV, 2 * B_W)), full((1, 2 * B_W)), full((1, 128)), full((2 * B_HEADS, 1)),
                  full((1, B_W))],
        out_specs=[seq(B_W),
                   pl.BlockSpec((1, B_HEADS, B_HD, B_HD), lambda bi, ci: (bi, 0, 0, 0)),
                   pl.BlockSpec((1, 8, B_HD), lambda bi, ci: (bi, 0, 0)),
                   pl.BlockSpec((1, 8, 128), lambda bi, ci: (bi, 0, 0))],
        out_shape=[jax.ShapeDtypeStruct((b, s_len, B_W), F32),
                   jax.ShapeDtypeStruct((b, B_HEADS, B_HD, B_HD), F32),
                   jax.ShapeDtypeStruct((b, 8, B_HD), F32),
                   jax.ShapeDtypeStruct((b, 8, 128), F32)],
        scratch_shapes=[pltpu.VMEM((L + 8, 2 * B_W), F32), pltpu.VMEM((B_HEADS, B_HD, B_HD), F32),
                        pltpu.VMEM((8, B_HD), F32), pltpu.VMEM((8, 128), F32)],
        compiler_params=_cparams("arbitrary", "arbitrary"),
        name="mlstm_chunkwise",
    )(qk, v, ob, gcol, grow, w_conv, b_conv.reshape(1, -1), bif_col, bif_row, g_head.reshape(1, -1))


def _odd_in_kernel(x_ref, g_ref, w_ref, gcv_ref, bcv_ref, ws_ref, bs_ref, c_ref, xd_ref):
    tm = x_ref.shape[0]
    xn = _rms(x_ref[...], g_ref[...]).astype(BF16)
    u = jnp.dot(xn, w_ref[:, 0:C_W], preferred_element_type=F32)
    v = jnp.dot(xn, w_ref[:, C_W:2 * C_W], preferred_element_type=F32)
    xd_ref[...] = jnp.dot(xn, w_ref[:, 2 * C_W:], preferred_element_type=F32)
    mu = jnp.mean(v, axis=-1, keepdims=True)
    vc = v - mu
    var = jnp.mean(vc * vc, axis=-1, keepdims=True)
    vl = (vc * lax.rsqrt(var + EPS) * gcv_ref[...] + bcv_ref[...]).astype(BF16)
    ri = lax.broadcasted_iota(I32, (C_CHUNK, C_CHUNK), 0)
    cj = lax.broadcasted_iota(I32, (C_CHUNK, C_CHUNK), 1)
    for g in range(C_GROUPS):
        cols = slice(g * C_GD, (g + 1) * C_GD)
        wg = jnp.where(ri >= cj, ws_ref[g], 0.0).astype(BF16)
        for c in range(tm // C_CHUNK):
            rows = slice(c * C_CHUNK, (c + 1) * C_CHUNK)
            s = jnp.dot(wg, vl[rows, cols], preferred_element_type=F32) + bs_ref[:, g:g + 1]
            c_ref[rows, cols] = u[rows, cols] * s


def _odd_in(x, g, w, g_cv, b_cv, w_s, b_s, tm):
    m, d = x.shape
    n = w.shape[1]
    full = lambda shape: pl.BlockSpec(shape, lambda i: (0,) * len(shape))
    return pl.pallas_call(
        _odd_in_kernel,
        grid=(m // tm,),
        in_specs=[pl.BlockSpec((tm, d), lambda i: (i, 0)), full((1, d)), full((d, n)), full((1, C_W)),
                  full((1, C_W)), full((C_GROUPS, C_CHUNK, C_CHUNK)), full((C_CHUNK, C_GROUPS))],
        out_specs=[pl.BlockSpec((tm, C_W), lambda i: (i, 0)), pl.BlockSpec((tm, D_W), lambda i: (i, 0))],
        out_shape=[jax.ShapeDtypeStruct((m, C_W), F32), jax.ShapeDtypeStruct((m, D_W), F32)],
        compiler_params=_cparams("arbitrary"),
        name="odd_in_proj_gate",
    )(x, g.reshape(1, d), w, g_cv.reshape(1, -1), b_cv.reshape(1, -1), w_s, b_s.T)


S5_TT = 128
S5_PAD = 8
S5_HALF = D_GROUPS * D_STATE // 2


def _s5_discretize(a_re, a_im, log_dt, bm_re, bm_im, cm_re, cm_im):
    a_re, a_im = a_re.astype(F32), a_im.astype(F32)
    dt = jnp.exp(log_dt.astype(F32))[:, None]
    mag = jnp.exp(a_re * dt)
    ab_re = mag * jnp.cos(a_im * dt)
    ab_im = mag * jnp.sin(a_im * dt)
    inv = 1.0 / (a_re * a_re + a_im * a_im)
    f_re = ((ab_re - 1.0) * a_re + ab_im * a_im) * inv
    f_im = (ab_im * a_re - (ab_re - 1.0) * a_im) * inv
    bm_re, bm_im = bm_re.astype(F32), bm_im.astype(F32)
    bb_re = f_re[..., None] * bm_re - f_im[..., None] * bm_im
    bb_im = f_re[..., None] * bm_im + f_im[..., None] * bm_re
    return ab_re, ab_im, bb_re, bb_im


def _s5_matrices(bb_re, bb_im, cm_re, cm_im):
    gh = D_GROUPS // 2
    eye = jnp.eye(gh, dtype=F32)

    def in_half(bb):
        return jnp.einsum('gpc,gh->gchp', bb, eye).reshape(gh * D_GCH, gh * D_STATE)

    def out_half(cm):
        return jnp.einsum('gcp,gh->gphc', cm, eye).reshape(gh * D_STATE, gh * D_GCH)

    bd = jnp.stack([jnp.concatenate([in_half(bb_re[h * gh:(h + 1) * gh]), in_half(bb_im[h * gh:(h + 1) * gh])], axis=1)
                    for h in range(2)])
    cm = jnp.stack([jnp.concatenate([out_half(cm_re[h * gh:(h + 1) * gh].astype(F32)),
                                     -out_half(cm_im[h * gh:(h + 1) * gh].astype(F32))], axis=0)
                    for h in range(2)])
    return bd, cm


def _gelu_tanh(x):
    return 0.5 * x * (1.0 + jnp.tanh(math.sqrt(2.0 / math.pi) * (x + 0.044715 * (x * x * x))))


def _s5_kernel(x_ref, bd_ref, cm_ref, ar_ref, ai_ref, dsk_ref, wg_ref, bg_ref, o_ref, hr_out, hi_out,
               bu_ref, hs_ref, hr_ref, hi_ref):
    i = pl.program_id(0)
    nb, tt, _ = x_ref.shape
    stride = tt + S5_PAD
    half_in = D_W // 2
    nct = S5_HALF // 128

    @pl.when(i == 0)
    def _():
        hr_ref[...] = jnp.zeros_like(hr_ref)
        hi_ref[...] = jnp.zeros_like(hi_ref)

    x = x_ref[...].reshape(nb * tt, D_W)
    ys = []
    for h in range(2):
        lanes = slice(h * S5_HALF, (h + 1) * S5_HALF)
        bu = jnp.dot(x[:, h * half_in:(h + 1) * half_in].astype(BF16), bd_ref[h], preferred_element_type=F32)
        for c in range(2 * nct):
            for b in range(nb):
                bu_ref[c, b * stride:b * stride + tt, :] = bu[b * tt:(b + 1) * tt, c * 128:(c + 1) * 128]
        ar = jnp.broadcast_to(ar_ref[:, lanes], (nb, S5_HALF))
        ai = jnp.broadcast_to(ai_ref[:, lanes], (nb, S5_HALF))

        def step(t, carry):
            hr, hi = carry
            rows = pl.ds(t, nb, stride=stride)
            bur = jnp.concatenate([bu_ref[c, rows, :] for c in range(nct)], axis=1)
            bui = jnp.concatenate([bu_ref[nct + c, rows, :] for c in range(nct)], axis=1)
            nhr = ar * hr - ai * hi + bur
            nhi = ar * hi + ai * hr + bui
            for c in range(nct):
                hs_ref[c, rows, :] = nhr[:, c * 128:(c + 1) * 128]
                hs_ref[nct + c, rows, :] = nhi[:, c * 128:(c + 1) * 128]
            return nhr, nhi

        hr, hi = lax.fori_loop(0, tt, step, (hr_ref[:, lanes], hi_ref[:, lanes]), unroll=2)
        hr_ref[:, lanes] = hr
        hi_ref[:, lanes] = hi
        hs = jnp.concatenate(
            [jnp.concatenate([hs_ref[c, b * stride:b * stride + tt, :] for b in range(nb)], axis=0)
             for c in range(2 * nct)], axis=1)
        ys.append(jnp.dot(hs.astype(BF16), cm_ref[h], preferred_element_type=F32))
    y = jnp.concatenate(ys, axis=-1) + dsk_ref[...] * x
    g = _gelu_tanh(y)
    d_out = g * jax.nn.sigmoid(jnp.dot(g.astype(BF16), wg_ref[...], preferred_element_type=F32) + bg_ref[...])
    o_ref[...] = d_out.reshape(nb, tt, D_W)

    @pl.when(i == pl.num_programs(0) - 1)
    def _():
        hr_out[...] = hr_ref[...]
        hi_out[...] = hi_ref[...]


def _s5_prompt(xd, bd, cm, ab_re, ab_im, d_skip, w_glu, b_glu):
    nb, t_len, _ = xd.shape
    tt = min(S5_TT, t_len)
    n_state = D_GROUPS * D_STATE
    full = lambda shape: pl.BlockSpec(shape, lambda i: (0,) * len(shape))
    return pl.pallas_call(
        _s5_kernel,
        grid=(t_len // tt,),
        in_specs=[pl.BlockSpec((nb, tt, D_W), lambda i: (0, i, 0)), full(bd.shape), full(cm.shape),
                  full((1, n_state)), full((1, n_state)), full((1, D_W)), full((D_W, D_W)), full((1, D_W))],
        out_specs=[pl.BlockSpec((nb, tt, D_W), lambda i: (0, i, 0)), full((nb, n_state)), full((nb, n_state))],
        out_shape=[jax.ShapeDtypeStruct(xd.shape, F32), jax.ShapeDtypeStruct((nb, n_state), F32),
                   jax.ShapeDtypeStruct((nb, n_state), F32)],
        scratch_shapes=[pltpu.VMEM((2 * S5_HALF // 128, nb * (tt + S5_PAD), 128), F32),
                        pltpu.VMEM((2 * S5_HALF // 128, nb * (tt + S5_PAD), 128), F32),
                        pltpu.VMEM((nb, n_state), F32), pltpu.VMEM((nb, n_state), F32)],
        compiler_params=_cparams("arbitrary"),
        name="s5_scan_glu",
    )(xd, bd, cm, ab_re.reshape(1, n_state), ab_im.reshape(1, n_state), d_skip.reshape(1, D_W), w_glu,
      b_glu.reshape(1, D_W))


def _hdot(a, b):
    return jnp.dot(a, b, precision=HIGHEST, preferred_element_type=F32)


def _sample_window_kernel(q_ref, kn_ref, vn_ref, tab_ref, tabn_ref, kt_ref, vt_ref, o_ref, okt_ref, ovt_ref,
                          head0=0):
    L = kt_ref.shape[3]
    last = lax.broadcasted_iota(I32, (A_HD, L), 1) == L - 1
    for h in range(kt_ref.shape[1]):
        kh, vh = kt_ref[0, h], vt_ref[0, h]
        kn, vn = kn_ref[0, h], vn_ref[0, h]
        q = q_ref[0, h] * (A_HD ** -0.5)
        s = jnp.sum(kh * q, axis=0, keepdims=True) + tab_ref[head0 + h]
        s_new = jnp.sum(kn * q, axis=0, keepdims=True) + tabn_ref[head0 + h]
        m = jnp.maximum(jnp.max(s, axis=1, keepdims=True), s_new)
        p = jnp.exp(s - m)
        p_new = jnp.exp(s_new - m)
        den = jnp.sum(p, axis=1, keepdims=True) + p_new
        o_ref[0, h] = (jnp.sum(vh * p, axis=1, keepdims=True) + vn * p_new) / den
        okt_ref[0, h] = jnp.where(last, kn, pltpu.roll(kh, L - 1, 1))
        ovt_ref[0, h] = jnp.where(last, vn, pltpu.roll(vh, L - 1, 1))


def _sample_window_attention(q, k_new, v_new, cache_kt, cache_vt, rel_bias):
    operands, in_specs, out_specs, out_shape = _sample_window_specs(q, k_new, v_new, cache_kt, cache_vt, rel_bias,
                                                                    lambda i: (i, 0))
    return pl.pallas_call(
        _sample_window_kernel,
        grid=(q.shape[0],),
        in_specs=in_specs,
        out_specs=out_specs,
        out_shape=out_shape,
        compiler_params=_cparams("arbitrary"),
        name="sample_window_attention",
    )(*operands)


def _sample_window_specs(q, k_new, v_new, cache_kt, cache_vt, rel_bias, block_of_step, heads_per_step=A_HEADS):
    n, _, _, L = cache_kt.shape
    assert L == A_WIN
    tab1 = _distance_logit_table(rel_bias, L)
    tab = tab1[:, :0:-1].reshape(A_HEADS, 1, L)
    tab_new = tab1[:, 0].reshape(A_HEADS, 1, 1)
    col = pl.BlockSpec((1, heads_per_step, A_HD, 1), lambda *g: (*block_of_step(*g), 0, 0))
    win = pl.BlockSpec((1, heads_per_step, A_HD, L), lambda *g: (*block_of_step(*g), 0, 0))
    full = lambda shape: pl.BlockSpec(shape, lambda *g: (0,) * len(shape))
    out_shape = [jax.ShapeDtypeStruct((n, A_HEADS, A_HD, 1), F32),
                 jax.ShapeDtypeStruct(cache_kt.shape, F32), jax.ShapeDtypeStruct(cache_vt.shape, F32)]
    return ([q, k_new, v_new, tab, tab_new, cache_kt, cache_vt],
            [col, col, col, full(tab.shape), full(tab_new.shape), win, win], [col, win, win], out_shape)


def _sample_even_proj(x, g_mix, w_in):
    n_main = 3 * A_W + 4 * B_W
    segs = ((0, A_W), (A_W, A_W), (2 * A_W, A_W), (3 * A_W, 2 * B_W), (3 * A_W + 2 * B_W, B_W),
            (3 * A_W + 3 * B_W, B_W), (n_main, 2 * B_HEADS))
    return _norm_proj(x, g_mix, w_in, jnp.zeros((8, x.shape[1]), F32), segs, (), tm=x.shape[0], precise=True)


def _sample_mlstm_step(qkb, vb, ob, gates, c0, n0, m0, conv0, b_if, w_conv, b_conv, g_head):
    n = qkb.shape[0]
    xp = jnp.concatenate([conv0, qkb[:, None, :]], axis=1)
    qk = _silu(b_conv + jnp.sum(xp * w_conv[None], axis=1))
    q = qk[:, :B_W].reshape(n, B_HEADS, B_HD)
    k = qk[:, B_W:].reshape(n, B_HEADS, B_HD) * (B_HD ** -0.5)
    v = vb.reshape(n, B_HEADS, B_HD)
    ig = gates[:, :B_HEADS] + b_if[:B_HEADS]
    logf = _log_sigmoid(gates[:, B_HEADS:] + b_if[B_HEADS:])
    inter = logf + m0
    mt = jnp.maximum(inter, ig)
    w_intra = jnp.exp(ig - mt)
    w_inter = jnp.exp(inter - mt)
    a = w_intra * jnp.sum(q * k, axis=-1)
    num = a[..., None] * v + w_inter[..., None] * jnp.einsum('nhk,nhkv->nhv', q, c0, precision=HIGHEST)
    den = a + w_inter * jnp.sum(q * n0, axis=-1)
    hb = num / jnp.maximum(jnp.abs(den), jnp.exp(-mt))[..., None]
    g = jnp.exp(ig - mt)
    decay = jnp.exp(inter - mt)
    c_new = decay[..., None, None] * c0 + (g[..., None] * k)[..., :, None] * v[..., None, :]
    n_new = decay[..., None] * n0 + g[..., None] * k
    hb = hb * lax.rsqrt(jnp.mean(hb * hb, axis=-1, keepdims=True) + EPS) * g_head.reshape(B_HEADS, B_HD)
    b_out = jax.nn.sigmoid(ob) * hb.reshape(n, B_W)
    return b_out, c_new, n_new, mt, xp[:, 1:]


def _sample_odd_mixers(x, h_re0, h_im0, g_mix, w_in, g_cv, b_cv, w_s, b_s, ab_re, ab_im, bb_re, bb_im,
                       cm_re, cm_im, d_skip, w_glu, b_glu):
    n = x.shape[0]
    h = _rms(x, g_mix)
    proj = _hdot(h, w_in)
    u, v, xd = proj[:, :C_W], proj[:, C_W:2 * C_W], proj[:, 2 * C_W:]
    mu = jnp.mean(v, axis=-1, keepdims=True)
    vc = v - mu
    v = vc * lax.rsqrt(jnp.mean(vc * vc, axis=-1, keepdims=True) + EPS) * g_cv + b_cv
    s = (w_s[:, 0, 0][None, :, None] * v.reshape(n, C_GROUPS, C_GD) + b_s[:, 0][None, :, None]).reshape(n, C_W)
    c_out = u * s
    xg = xd.reshape(n, D_GROUPS, D_GCH)
    bu_re = jnp.einsum('gpc,ngc->ngp', bb_re, xg, precision=HIGHEST)
    bu_im = jnp.einsum('gpc,ngc->ngp', bb_im, xg, precision=HIGHEST)
    hr = ab_re * h_re0 - ab_im * h_im0 + bu_re
    hi = ab_re * h_im0 + ab_im * h_re0 + bu_im
    y = (jnp.einsum('gcp,ngp->ngc', cm_re, hr, precision=HIGHEST)
         - jnp.einsum('gcp,ngp->ngc', cm_im, hi, precision=HIGHEST) + d_skip * xg)
    g = _gelu_tanh(y.reshape(n, D_W))
    d_out = g * jax.nn.sigmoid(_hdot(g, w_glu) + b_glu)
    return c_out, d_out, v, hr, hi


_EVEN_SEGS = ((0, A_W), (A_W, 2 * B_W), (A_W + 2 * B_W, B_W), (A_W + 3 * B_W, B_W), (A_W + 4 * B_W, 128))


def kernel(x_prompt, x_sample, cache_a_k, cache_a_v, state_b_c, state_b_n, state_b_m, state_b_conv,
           state_d_re, state_d_im, rel_bias, g_mix, g_ffn, g_final, w_in_e, b_if, w_conv_b, b_conv_b,
           g_bhead, w_out_e, w1_e, w3_e, w2_e, w_in_o, g_cv, b_cv, w_s, b_s, a_re, a_im, log_dt,
           bm_re, bm_im, cm_re, cm_im, d_skip, w_glu, b_glu, w_out_o, w_router, b_router,
           w1_m, w3_m, w2_m):
    bp, sp, d = x_prompt.shape
    ns = x_sample.shape[0]
    mp = bp * sp
    xp = x_prompt.reshape(mp, d)
    xs = x_sample.reshape(ns, d)

    w_in = w_in_e[0]
    n_main = 3 * A_W + 4 * B_W
    w_gate = w_in[:, n_main:]
    w_cat = jnp.concatenate([w_in[:, :A_W], w_in[:, 3 * A_W:n_main],
                             jnp.pad(w_gate, ((0, 0), (0, 128 - 2 * B_HEADS)))], axis=1)
    w_t = jnp.concatenate([w_in[:, A_W:3 * A_W], w_gate], axis=1).T
    q, qkb, vb, ob, gcol, kt, vt, grow = _norm_proj(
        xp, g_mix[0], w_cat.astype(BF16), w_t.astype(BF16), _EVEN_SEGS,
        ((0, A_W, True), (A_W, A_W, True), (2 * A_W, 2 * B_HEADS, False)), tm=512, rows_per_seq=sp)
    shp = lambda a: a.reshape(bp, sp, a.shape[-1])
    pos_minor = lambda c: jnp.transpose(c, (0, 2, 3, 1))
    pos_major = lambda c: jnp.transpose(c, (0, 3, 1, 2))
    s_proj = _sample_even_proj(xs, g_mix[0], w_in)
    cols = lambda a: a.reshape(ns, A_HEADS, A_HD, 1)
    s_attn = (cols(s_proj[0]), cols(s_proj[1]), cols(s_proj[2]), pos_minor(cache_a_k[0]), pos_minor(cache_a_v[0]))
    a_out = _attention_prompt(shp(q), kt, vt, rel_bias)
    b_out, pc, pn, pm = _mlstm_prompt(shp(qkb), shp(vb), shp(ob), shp(gcol), grow, w_conv_b[0], b_conv_b[0],
                                      b_if[0], g_bhead[0])
    ffn_args = (a_out.reshape(mp, A_W), b_out.reshape(mp, B_W), xp, w_out_e[0].astype(BF16), g_ffn[0],
                w1_e[0].astype(BF16), w3_e[0].astype(BF16), w2_e[0].astype(BF16))
    ffn_tm, ffn_tf = 1024, 512
    if ns * WINDOW_HEAD_BLOCKS == (mp // ffn_tm) * WINDOW_STEPS:
        x1p, sa, s_kt, s_vt = _mix_ffn(*ffn_args, tm=ffn_tm, tf=ffn_tf, window=s_attn, rel_bias=rel_bias)
    else:
        x1p = _mix_ffn(*ffn_args, tm=ffn_tm, tf=ffn_tf)
        sa, s_kt, s_vt = _sample_window_attention(*s_attn, rel_bias)
    sa = sa.reshape(ns, A_W)
    s_k, s_v = pos_major(s_kt), pos_major(s_vt)

    sb, sc, sn, sm, s_conv = _sample_mlstm_step(
        s_proj[3], s_proj[4], s_proj[5], s_proj[6], state_b_c[0], state_b_n[0], state_b_m[0], state_b_conv[0],
        b_if[0], w_conv_b[0], b_conv_b[0], g_bhead[0])
    x1s = _mix_ffn(sa, sb, xs, w_out_e[0], g_ffn[0], w1_e[0], w3_e[0], w2_e[0], tm=ns, tf=512, precise=True)

    ab_re, ab_im, bb_re, bb_im = _s5_discretize(a_re[0], a_im[0], log_dt[0], bm_re[0], bm_im[0], cm_re[0],
                                                cm_im[0])
    bd, cmat = _s5_matrices(bb_re, bb_im, cm_re[0], cm_im[0])
    c_out, xd = _odd_in(x1p, g_mix[1], w_in_o[0].astype(BF16), g_cv[0], b_cv[0], w_s[0], b_s[0], tm=1024)
    d_out, p_hr, p_hi = _s5_prompt(xd.reshape(bp, sp, D_W), bd.astype(BF16), cmat.astype(BF16), ab_re, ab_im,
                                   d_skip[0], w_glu[0].astype(BF16), b_glu[0])
    wr = jnp.pad(w_router[0].astype(F32), ((0, 0), (0, 128 - N_EXPERTS)))
    br = jnp.pad(b_router[0].astype(F32), (0, 128 - N_EXPERTS)).reshape(1, 128)
    x2p, hnp, idp, gtp, rkp, cbp, cap = _mix_router(c_out, d_out.reshape(mp, D_W), x1p, w_out_o[0].astype(BF16),
                                                    g_ffn[1], wr, br, jnp.zeros((1, 128), F32), tm=1024)

    sc_out, sd_out, s_cv, s_hr, s_hi = _sample_odd_mixers(
        x1s, state_d_re[0], state_d_im[0], g_mix[1], w_in_o[0], g_cv[0], b_cv[0], w_s[0], b_s[0], ab_re, ab_im,
        bb_re, bb_im, cm_re[0], cm_im[0], d_skip[0], w_glu[0], b_glu[0])
    x2s, hns, ids_s, gts, rks, cbs, cas = _mix_router(sc_out, sd_out, x1s, w_out_o[0], g_ffn[1], wr, br, cap,
                                                      tm=ns, precise=True)

    pad = ((0, ROUTE_TILE - ns), (0, 0))
    cnt_tile = jnp.concatenate([cbp.reshape(-1, 128), cbs.reshape(-1, 128), cas])[:, :N_EXPERTS].astype(I32)
    y_p, y_s = _moe_block(x2p, hnp, idp, gtp, rkp, jnp.pad(x2s, pad), jnp.pad(hns, pad),
                          jnp.pad(ids_s, pad, constant_values=-1), jnp.pad(gts, pad), jnp.pad(rks, pad),
                          cnt_tile, w1_m[0], w3_m[0], w2_m[0], g_final)
    y_prompt = y_p.reshape(bp, sp, d)
    y_sample = y_s[:ns].reshape(ns, 1, d)

    e = lambda a: a[None]
    p_k = pos_major(kt.reshape(bp, A_HEADS, A_HD, sp))
    p_v = pos_major(vt.reshape(bp, A_HEADS, A_HD, sp))
    p_conv = qkb.reshape(bp, sp, 2 * B_W)[:, sp - (B_CONV - 1):]
    return (y_prompt, y_sample, e(p_k), e(p_v), e(s_k), e(s_v), e(pc), e(sc), e(pn[:, :B_HEADS]), e(sn),
            e(pm[:, :B_HEADS, 0]), e(sm), e(p_conv), e(s_conv), e(s_cv.reshape(ns, 1, C_W)),
            e(p_hr.reshape(bp, D_GROUPS, D_STATE)), e(s_hr), e(p_hi.reshape(bp, D_GROUPS, D_STATE)), e(s_hi))
```

```python
import functools
import math

import jax
import jax.numpy as jnp
from jax import lax
from jax.experimental import pallas as pl
from jax.experimental.pallas import tpu as pltpu

F32 = jnp.float32
BF16 = jnp.bfloat16
I32 = jnp.int32
HIGHEST = lax.Precision.HIGHEST

D_MODEL = 1024
A_HEADS, A_HD = 8, 64
A_W = A_HEADS * A_HD
A_BRANCHES = ((128, 1), (512, 4), (2048, 16))
A_WIN = 2048
N_BUCKETS = 32
B_HEADS, B_HD = 4, 128
B_W = B_HEADS * B_HD
B_CONV = 4
C_GROUPS, C_GD = 4, 128
C_W = C_GROUPS * C_GD
C_CHUNK = 128
D_GROUPS, D_GCH, D_STATE = 32, 16, 64
D_W = D_GROUPS * D_GCH
D_FF = 3584
N_EXPERTS = 8
TOP_K = 2
EPS = 1e-6
NEG = -1e30

VMEM_LIMIT = 56 * 1024 * 1024

ROUTE_TILE = 256
FFN_TILE = 1024
FFN_ROW_CHUNK = 256
GATHER_TOKENS = 2 * ROUTE_TILE
COMBINE_ALIGN = 16


def _cparams(*sem):
    return pltpu.CompilerParams(dimension_semantics=sem, vmem_limit_bytes=VMEM_LIMIT)


def _moe_plan(ids, rank, cnt_tile):
    mt = ids.shape[0]
    n_assign = 2 * mt
    n_ffn_tiles = -(-(n_assign + N_EXPERTS * (FFN_TILE - 1)) // FFN_TILE) + 1
    n_sub = n_ffn_tiles * (FFN_TILE // ROUTE_TILE)
    counts = cnt_tile[-1]
    seg = ((counts + FFN_TILE - 1) // FFN_TILE) * FFN_TILE
    seg_end = jnp.cumsum(seg)
    seg_off = seg_end - seg
    off_of = jnp.sum(jnp.where(ids[..., None] == jnp.arange(N_EXPERTS, dtype=I32), seg_off, 0), axis=-1)
    dest = jnp.where(ids >= 0, off_of + rank, -1)
    total = seg_end[-1]
    tile_start = jnp.arange(n_ffn_tiles, dtype=I32) * FFN_TILE
    tile_expert = jnp.minimum(jnp.sum((tile_start[:, None] >= seg_end[None, :]).astype(I32), axis=1),
                              N_EXPERTS - 1).astype(I32)
    rows_left = (seg_off + counts)[tile_expert] - tile_start
    tile_chunks = jnp.where(tile_start < total,
                            jnp.clip(-(-rows_left // FFN_ROW_CHUNK), 0, FFN_TILE // FFN_ROW_CHUNK), 0).astype(I32)
    lo = seg_off[None, :] + cnt_tile[:-1]
    hi = seg_off[None, :] + cnt_tile[1:]
    sub_start = jnp.arange(n_sub, dtype=I32) * ROUTE_TILE
    sub_expert = jnp.minimum(jnp.sum((sub_start[:, None] >= seg_end[None, :]).astype(I32), axis=1),
                             N_EXPERTS - 1)
    lo_e, hi_e = lo.T[sub_expert], hi.T[sub_expert]
    jlo = jnp.sum((hi_e <= sub_start[:, None]).astype(I32), axis=1).astype(I32)
    jhi = (jnp.sum((lo_e < sub_start[:, None] + ROUTE_TILE).astype(I32), axis=1) - 1).astype(I32)
    win = lo // COMBINE_ALIGN
    flat1 = lambda a: a.reshape(-1).astype(I32)
    return dest, tile_expert, tile_chunks, (jlo, jhi), (flat1(win), flat1(lo), flat1(hi)), n_ffn_tiles


def _moe_gather_kernel(jlo_ref, jhi_ref, hp_hbm, hs_hbm, dt_ref, gt_ref, xs_ref, gs_ref,
                       hn_ref, acc_ref, g_ref, sem):
    s = pl.program_id(0)
    mp = hp_hbm.shape[0]

    @pl.when(s == 0)
    def _():
        copies = (pltpu.make_async_copy(hp_hbm, hn_ref.at[0:mp], sem.at[0]),
                  pltpu.make_async_copy(hs_hbm, hn_ref.at[mp:mp + ROUTE_TILE], sem.at[1]))
        for c in copies:
            c.start()
        hn_ref[mp + ROUTE_TILE:, :] = jnp.zeros((ROUTE_TILE, hn_ref.shape[1]), BF16)
        for c in copies:
            c.wait()

    acc_ref[...] = jnp.zeros_like(acc_ref)
    g_ref[...] = jnp.zeros_like(g_ref)
    rows = lax.broadcasted_iota(I32, (ROUTE_TILE, GATHER_TOKENS), 0) + s * ROUTE_TILE
    jlo = jlo_ref[s]

    def body(it, carry):
        off = pl.multiple_of(jlo * ROUTE_TILE + it * GATHER_TOKENS, ROUTE_TILE)
        dt = dt_ref[:, pl.ds(off, GATHER_TOKENS)]
        gt = gt_ref[:, pl.ds(off, GATHER_TOKENS)]
        m1 = dt[0:1, :] == rows
        m2 = dt[1:2, :] == rows
        onehot = jnp.where(m1 | m2, 1.0, 0.0).astype(BF16)
        acc_ref[...] += jnp.dot(onehot, hn_ref[pl.ds(off, GATHER_TOKENS), :], preferred_element_type=F32)
        g_ref[...] += jnp.sum(jnp.where(m1, gt[0:1, :], 0.0) + jnp.where(m2, gt[1:2, :], 0.0),
                              axis=1, keepdims=True)
        return carry

    tiles_per_iter = GATHER_TOKENS // ROUTE_TILE
    lax.fori_loop(0, (jhi_ref[s] - jlo + tiles_per_iter) // tiles_per_iter, body, 0)
    xs_ref[...] = acc_ref[...].astype(BF16)
    gs_ref[...] = g_ref[...]


def _moe_gather(hn_p, hn_s, dest_t, gates_t, glist, n_sub):
    jlo, jhi = glist
    mp, d = hn_p.shape
    mt = dest_t.shape[1]
    grid_spec = pltpu.PrefetchScalarGridSpec(
        num_scalar_prefetch=2,
        grid=(n_sub,),
        in_specs=[
            pl.BlockSpec(memory_space=pl.ANY),
            pl.BlockSpec(memory_space=pl.ANY),
            pl.BlockSpec((2, mt), lambda s, *_: (0, 0)),
            pl.BlockSpec((2, mt), lambda s, *_: (0, 0)),
        ],
        out_specs=[
            pl.BlockSpec((ROUTE_TILE, d), lambda s, *_: (s, 0)),
            pl.BlockSpec((ROUTE_TILE, 1), lambda s, *_: (s, 0)),
        ],
        scratch_shapes=[pltpu.VMEM((mp + 2 * ROUTE_TILE, d), BF16), pltpu.VMEM((ROUTE_TILE, d), F32),
                        pltpu.VMEM((ROUTE_TILE, 1), F32), pltpu.SemaphoreType.DMA((2,))],
    )
    return pl.pallas_call(
        _moe_gather_kernel,
        grid_spec=grid_spec,
        out_shape=[jax.ShapeDtypeStruct((n_sub * ROUTE_TILE, d), BF16),
                   jax.ShapeDtypeStruct((n_sub * ROUTE_TILE, 1), F32)],
        compiler_params=_cparams("arbitrary"),
        name="moe_gather",
    )(jlo, jhi, hn_p, hn_s, dest_t, gates_t)


def _moe_ffn_kernel(te_ref, nc_ref, x_ref, gs_ref, w1_ref, w3_ref, w2_ref, y_ref, acc_ref):
    t, f = pl.program_id(0), pl.program_id(1)
    nf = pl.num_programs(1)
    n_chunks = nc_ref[t]
    full = FFN_TILE // FFN_ROW_CHUNK

    @pl.when(n_chunks > 0)
    def _():
        @pl.when(f == 0)
        def _():
            acc_ref[...] = jnp.zeros_like(acc_ref)

        w1, w3, w2 = w1_ref[0].astype(BF16), w3_ref[0].astype(BF16), w2_ref[0].astype(BF16)

        def ffn(rows):
            x = x_ref[rows, :]
            a = jnp.dot(x, w1, preferred_element_type=F32)
            b = jnp.dot(x, w3, preferred_element_type=F32)
            acc_ref[rows, :] += jnp.dot((_silu(a) * b).astype(BF16), w2, preferred_element_type=F32)

        @pl.when(n_chunks == full)
        def _():
            ffn(slice(0, FFN_TILE))

        for c in range(full - 1):
            @pl.when((n_chunks < full) & (c < n_chunks))
            def _():
                ffn(slice(c * FFN_ROW_CHUNK, (c + 1) * FFN_ROW_CHUNK))

        @pl.when(f == nf - 1)
        def _():
            y_ref[...] = (acc_ref[...] * gs_ref[...]).astype(BF16)

    @pl.when((n_chunks == 0) & (f == nf - 1))
    def _():
        y_ref[...] = jnp.zeros_like(y_ref)


def _moe_ffn(xs, gs, w1, w3, w2, tile_expert, tile_chunks, n_ffn_tiles, tf):
    d = xs.shape[1]
    ff = w1.shape[2]
    nf = ff // tf

    def fidx(t, f, nc):
        return jnp.where(nc[t] > 0, f, nf - 1)

    grid_spec = pltpu.PrefetchScalarGridSpec(
        num_scalar_prefetch=2,
        grid=(n_ffn_tiles, nf),
        in_specs=[
            pl.BlockSpec((FFN_TILE, d), lambda t, f, te, nu: (t, 0)),
            pl.BlockSpec((FFN_TILE, 1), lambda t, f, te, nu: (t, 0)),
            pl.BlockSpec((1, d, tf), lambda t, f, te, nu: (te[t], 0, fidx(t, f, nu))),
            pl.BlockSpec((1, d, tf), lambda t, f, te, nu: (te[t], 0, fidx(t, f, nu))),
            pl.BlockSpec((1, tf, d), lambda t, f, te, nu: (te[t], fidx(t, f, nu), 0)),
        ],
        out_specs=pl.BlockSpec((FFN_TILE, d), lambda t, f, te, nu: (t, 0)),
        scratch_shapes=[pltpu.VMEM((FFN_TILE, d), F32)],
    )
    return pl.pallas_call(
        _moe_ffn_kernel,
        grid_spec=grid_spec,
        out_shape=jax.ShapeDtypeStruct(xs.shape, BF16),
        compiler_params=_cparams("arbitrary", "arbitrary"),
        name="moe_ffn",
    )(tile_expert, tile_chunks, xs, gs, w1, w3, w2)


def _moe_combine_kernel(win_ref, lo_ref, hi_ref, xp_ref, xs_ref, d_ref, g_ref, *rest):
    ys_refs, tail_refs = rest[:N_EXPERTS], rest[N_EXPERTS:2 * N_EXPERTS]
    yp_ref, ysm_ref, acc_ref = rest[2 * N_EXPERTS:]
    j = pl.program_id(0)
    n_prompt_tiles = pl.num_programs(0) - 1

    @pl.when(j < n_prompt_tiles)
    def _():
        acc_ref[...] = xp_ref[...]

    @pl.when(j == n_prompt_tiles)
    def _():
        acc_ref[...] = xs_ref[...]

    for e in range(N_EXPERTS):
        lo, hi = lo_ref[j * N_EXPERTS + e], hi_ref[j * N_EXPERTS + e]

        @pl.when(hi > lo)
        def _():
            d = d_ref[...]
            d = jnp.where((d >= lo) & (d < hi), d, -1)
            start = win_ref[j * N_EXPERTS + e] * COMBINE_ALIGN

            def gathered(ref, first):
                cols = lax.broadcasted_iota(I32, (ROUTE_TILE, ref.shape[0]), 1) + first
                onehot = jnp.where((d[:, 0:1] == cols) | (d[:, 1:2] == cols), 1.0, 0.0).astype(BF16)
                return jnp.dot(onehot, ref[...], preferred_element_type=F32)

            acc_ref[...] += gathered(ys_refs[e], start)

            @pl.when(hi > start + ROUTE_TILE)
            def _():
                acc_ref[...] += gathered(tail_refs[e], start + ROUTE_TILE)

    y = _rms(acc_ref[...], g_ref[...])

    @pl.when(j < n_prompt_tiles)
    def _():
        yp_ref[...] = y

    @pl.when(j == n_prompt_tiles)
    def _():
        ysm_ref[...] = y


def _moe_combine(x2_p, x2_s, dest, ys, g_final, clist):
    win, lo, hi = clist
    mp, d = x2_p.shape
    n_prompt_tiles = mp // ROUTE_TILE

    def ptile(j, *_):
        return (jnp.minimum(j, n_prompt_tiles - 1), 0)

    def window(e, rows, offset):
        return pl.BlockSpec((pl.Element(rows), pl.Element(d)),
                            lambda j, win, lo, hi: ((win[j * N_EXPERTS + e] + offset // COMBINE_ALIGN)
                                                    * COMBINE_ALIGN, 0))

    grid_spec = pltpu.PrefetchScalarGridSpec(
        num_scalar_prefetch=3,
        grid=(n_prompt_tiles + 1,),
        in_specs=[
            pl.BlockSpec((ROUTE_TILE, d), ptile),
            pl.BlockSpec((ROUTE_TILE, d), lambda j, *_: (0, 0)),
            pl.BlockSpec((ROUTE_TILE, 2), lambda j, *_: (j, 0)),
            pl.BlockSpec((1, d), lambda j, *_: (0, 0)),
        ] + [window(e, ROUTE_TILE, 0) for e in range(N_EXPERTS)]
          + [window(e, COMBINE_ALIGN, ROUTE_TILE) for e in range(N_EXPERTS)],
        out_specs=[pl.BlockSpec((ROUTE_TILE, d), ptile), pl.BlockSpec((ROUTE_TILE, d), lambda j, *_: (0, 0))],
        scratch_shapes=[pltpu.VMEM((ROUTE_TILE, d), F32)],
    )
    return pl.pallas_call(
        _moe_combine_kernel,
        grid_spec=grid_spec,
        out_shape=[jax.ShapeDtypeStruct((mp, d), F32), jax.ShapeDtypeStruct((ROUTE_TILE, d), F32)],
        compiler_params=_cparams("arbitrary"),
        name="moe_combine",
    )(win, lo, hi, x2_p, x2_s, dest, g_final.reshape(1, d), *([ys] * (2 * N_EXPERTS)))


def _moe_block(x2_p, hn_p, ids_p, gates_p, rank_p, x2_s, hn_s, ids_s, gates_s, rank_s, cnt_tile, w1, w3, w2,
               g_final, tf=512):
    ids = jnp.concatenate([ids_p, ids_s])
    gates = jnp.concatenate([gates_p, gates_s])
    rank = jnp.concatenate([rank_p, rank_s])
    dest, tile_expert, tile_chunks, glist, clist, n_ffn_tiles = _moe_plan(ids, rank, cnt_tile)
    n_sub = n_ffn_tiles * (FFN_TILE // ROUTE_TILE)
    spare = ((0, 0), (0, GATHER_TOKENS - ROUTE_TILE))
    xs, gs = _moe_gather(hn_p, hn_s, jnp.pad(dest.T, spare, constant_values=-1), jnp.pad(gates.T, spare), glist,
                         n_sub)
    ys = _moe_ffn(xs, gs, w1, w3, w2, tile_expert, tile_chunks, n_ffn_tiles, tf)
    return _moe_combine(x2_p, x2_s, dest, ys, g_final, clist)


def _rms(x, g):
    return x * lax.rsqrt(jnp.mean(x * x, axis=-1, keepdims=True) + EPS) * g


def _mm(a, b, precise):
    if precise:
        return jnp.dot(a.astype(F32), b.astype(F32), preferred_element_type=F32, precision=HIGHEST)
    return jnp.dot(a.astype(BF16), b.astype(BF16), preferred_element_type=F32)


def _silu(x):
    return x * jax.nn.sigmoid(x)


def _norm_proj_kernel(x_ref, g_ref, w_ref, wt_ref, *out_refs, segs, tsegs, precise):
    xn = _rms(x_ref[...], g_ref[...])
    xn = xn if precise else xn.astype(BF16)
    for (start, width), o_ref in zip(segs, out_refs[:len(segs)]):
        for c in range(0, width, 512):
            cw = min(512, width - c)
            o_ref[:, c:c + cw] = _mm(xn, w_ref[:, start + c:start + c + cw], precise)
    for (start, height), o_ref in zip(tsegs, out_refs[len(segs):]):
        res = lax.dot_general(wt_ref[start:start + height, :].astype(xn.dtype), xn, (((1,), (1,)), ((), ())),
                              preferred_element_type=F32, precision=HIGHEST if precise else None)
        o_ref[...] = res.reshape(o_ref.shape)


def _norm_proj(x, g, w, wt, segs, tsegs, tm, rows_per_seq=None, precise=False):
    m, d = x.shape
    n = w.shape[1]
    nt = wt.shape[0]
    out_shape = [jax.ShapeDtypeStruct((m, width), F32) for _, width in segs]
    out_specs = [pl.BlockSpec((tm, width), lambda i: (i, 0)) for _, width in segs]
    for _, height, per_seq in tsegs:
        if per_seq:
            tps = rows_per_seq // tm
            out_shape.append(jax.ShapeDtypeStruct((m // rows_per_seq, height, rows_per_seq), F32))
            out_specs.append(pl.BlockSpec((1, height, tm), lambda i, tps=tps: (i // tps, 0, i % tps)))
        else:
            out_shape.append(jax.ShapeDtypeStruct((height, m), F32))
            out_specs.append(pl.BlockSpec((height, tm), lambda i: (0, i)))
    return pl.pallas_call(
        functools.partial(_norm_proj_kernel, segs=tuple(segs), tsegs=tuple(t[:2] for t in tsegs), precise=precise),
        grid=(m // tm,),
        in_specs=[
            pl.BlockSpec((tm, d), lambda i: (i, 0)),
            pl.BlockSpec((1, d), lambda i: (0, 0)),
            pl.BlockSpec((d, n), lambda i: (0, 0)),
            pl.BlockSpec((nt, d), lambda i: (0, 0)),
        ],
        out_specs=out_specs,
        out_shape=out_shape,
        compiler_params=_cparams("arbitrary"),
        name="norm_proj",
    )(x, g.reshape(1, d), w, wt)


def _mix_ffn_prologue(a_ref, b_ref, x_ref, wo_ref, g_ref, o_ref, hn_ref, precise):
    wa = a_ref.shape[1]

    @pl.when(pl.program_id(1) == 0)
    def _():
        x1 = (x_ref[...] + _mm(a_ref[...], wo_ref[0:wa, :], precise)
              + _mm(b_ref[...], wo_ref[wa:, :], precise))
        o_ref[...] = x1
        hn_ref[...] = _rms(x1, g_ref[...]).astype(hn_ref.dtype)


def _mix_ffn_step(w1_ref, w3_ref, w2_ref, o_ref, hn_ref, precise):
    wdt = F32 if precise else BF16
    w1, w3, w2 = w1_ref[...].astype(wdt), w3_ref[...].astype(wdt), w2_ref[...].astype(wdt)
    hn = hn_ref[...]
    o_ref[...] += _mm(_silu(_mm(hn, w1, precise)) * _mm(hn, w3, precise), w2, precise)


def _mix_ffn_kernel(a_ref, b_ref, x_ref, wo_ref, g_ref, w1_ref, w3_ref, w2_ref, o_ref, hn_ref, *, precise):
    _mix_ffn_prologue(a_ref, b_ref, x_ref, wo_ref, g_ref, o_ref, hn_ref, precise)
    _mix_ffn_step(w1_ref, w3_ref, w2_ref, o_ref, hn_ref, precise)


WINDOW_HEAD_BLOCKS = 2
WINDOW_STEPS = 4
N_MIX_FFN_IN, N_WINDOW_IN, N_WINDOW_OUT = 8, 7, 3


def _mix_ffn_window_kernel(*refs, precise):
    (a_ref, b_ref, x_ref, wo_ref, g_ref, w1_ref, w3_ref, w2_ref), refs = refs[:N_MIX_FFN_IN], refs[N_MIX_FFN_IN:]
    w_ins, refs = refs[:N_WINDOW_IN], refs[N_WINDOW_IN:]
    o_ref, w_outs, hn_ref = refs[0], refs[1:1 + N_WINDOW_OUT], refs[1 + N_WINDOW_OUT]
    f = pl.program_id(1)
    _mix_ffn_prologue(a_ref, b_ref, x_ref, wo_ref, g_ref, o_ref, hn_ref, precise)

    @pl.when(f < WINDOW_STEPS)
    def _():
        _sample_window_kernel(*w_ins, *w_outs, head0=(f % WINDOW_HEAD_BLOCKS) * (A_HEADS // WINDOW_HEAD_BLOCKS))
        _mix_ffn_step(w1_ref, w3_ref, w2_ref, o_ref, hn_ref, precise)

    @pl.when(f >= WINDOW_STEPS)
    def _():
        _mix_ffn_step(w1_ref, w3_ref, w2_ref, o_ref, hn_ref, precise)


def _mix_ffn(a, b, x, w_out, g, w1, w3, w2, tm, tf, precise=False, window=None, rel_bias=None):
    m, d = x.shape
    wa, wb = a.shape[1], b.shape[1]
    ff = w1.shape[1]
    in_specs = [
        pl.BlockSpec((tm, wa), lambda i, f: (i, 0)),
        pl.BlockSpec((tm, wb), lambda i, f: (i, 0)),
        pl.BlockSpec((tm, d), lambda i, f: (i, 0)),
        pl.BlockSpec((wa + wb, d), lambda i, f: (0, 0)),
        pl.BlockSpec((1, d), lambda i, f: (0, 0)),
        pl.BlockSpec((d, tf), lambda i, f: (0, f)),
        pl.BlockSpec((d, tf), lambda i, f: (0, f)),
        pl.BlockSpec((tf, d), lambda i, f: (f, 0)),
    ]
    out_specs = [pl.BlockSpec((tm, d), lambda i, f: (i, 0))]
    out_shape = [jax.ShapeDtypeStruct((m, d), F32)]
    operands = [a, b, x, w_out, g.reshape(1, d), w1, w3, w2]
    body = _mix_ffn_kernel
    if window is not None:
        assert ff // tf >= WINDOW_STEPS and window[0].shape[0] * WINDOW_HEAD_BLOCKS == (m // tm) * WINDOW_STEPS

        def block_of_step(i, f):
            p = i * WINDOW_STEPS + jnp.minimum(f, WINDOW_STEPS - 1)
            return p // WINDOW_HEAD_BLOCKS, p % WINDOW_HEAD_BLOCKS

        w_operands, w_in, w_out_specs, w_shape = _sample_window_specs(*window, rel_bias, block_of_step,
                                                                      A_HEADS // WINDOW_HEAD_BLOCKS)
        in_specs, out_specs, out_shape = in_specs + w_in, out_specs + w_out_specs, out_shape + w_shape
        operands, body = operands + w_operands, _mix_ffn_window_kernel
    res = pl.pallas_call(
        functools.partial(body, precise=precise),
        grid=(m // tm, ff // tf),
        in_specs=in_specs,
        out_specs=out_specs,
        out_shape=out_shape,
        scratch_shapes=[pltpu.VMEM((tm, d), F32 if precise else BF16)],
        compiler_params=_cparams("arbitrary", "arbitrary"),
        name="mix_ffn",
    )(*operands)
    return res[0] if window is None else res


def _mix_router_kernel(a_ref, b_ref, x_ref, wo_ref, g_ref, wr_ref, br_ref, cnt0_ref, x2_ref, hn_ref, ids_ref,
                       gate_ref, rank_ref, cb_ref, ca_ref, cnt_ref, *, precise):
    @pl.when(pl.program_id(0) == 0)
    def _():
        cnt_ref[...] = cnt0_ref[...]

    wa = a_ref.shape[1]
    x2 = (x_ref[...] + _mm(a_ref[...], wo_ref[0:wa, :], precise)
          + _mm(b_ref[...], wo_ref[wa:, :], precise))
    x2_ref[...] = x2
    hn = _rms(x2, g_ref[...])
    hn_ref[...] = hn.astype(BF16)
    wr = wr_ref[...]
    if precise:
        logits = jnp.dot(hn, wr, preferred_element_type=F32, precision=HIGHEST)
    else:
        hn_hi, wr_hi = hn.astype(BF16), wr.astype(BF16)
        hn_lo, wr_lo = (hn - hn_hi.astype(F32)).astype(BF16), (wr - wr_hi.astype(F32)).astype(BF16)
        logits = (jnp.dot(hn_hi, wr_hi, preferred_element_type=F32)
                  + (jnp.dot(hn_lo, wr_hi, preferred_element_type=F32)
                     + jnp.dot(hn_hi, wr_lo, preferred_element_type=F32)))
    lane = lax.broadcasted_iota(I32, logits.shape, 1)
    real = lane < N_EXPERTS
    biased = jnp.where(real, logits + br_ref[...], -jnp.inf)
    m1 = jnp.max(biased, axis=-1, keepdims=True)
    i1 = jnp.min(jnp.where(biased == m1, lane, 128), axis=-1, keepdims=True)
    rest = jnp.where(lane == i1, -jnp.inf, biased)
    m2 = jnp.max(rest, axis=-1, keepdims=True)
    i2 = jnp.min(jnp.where(rest == m2, lane, 128), axis=-1, keepdims=True)
    l1 = jnp.sum(jnp.where(lane == i1, logits, 0.0), axis=-1, keepdims=True)
    l2 = jnp.sum(jnp.where(lane == i2, logits, 0.0), axis=-1, keepdims=True)
    mx = jnp.maximum(l1, l2)
    e1, e2 = jnp.exp(l1 - mx), jnp.exp(l2 - mx)
    two = lax.broadcasted_iota(I32, ids_ref.shape, 1)
    ids_ref[...] = jnp.where(two == 0, i1, i2)
    gate_ref[...] = jnp.where(two == 0, e1, e2) / (e1 + e2)
    tm = logits.shape[0]
    chosen = jnp.where((lane == i1) | (lane == i2), 1.0, 0.0)
    earlier = (lax.broadcasted_iota(I32, (tm, tm), 0) > lax.broadcasted_iota(I32, (tm, tm), 1))
    before = cnt_ref[...] + jnp.dot(jnp.where(earlier, 1.0, 0.0).astype(BF16), chosen.astype(BF16),
                                    preferred_element_type=F32)
    r1 = jnp.sum(jnp.where(lane == i1, before, 0.0), axis=-1, keepdims=True)
    r2 = jnp.sum(jnp.where(lane == i2, before, 0.0), axis=-1, keepdims=True)
    rank_ref[...] = jnp.where(two == 0, r1, r2).astype(I32)
    for c in range(cb_ref.shape[1]):
        cb_ref[0, c:c + 1, :] = before[c * ROUTE_TILE:c * ROUTE_TILE + 1, :]
    cnt_ref[...] += jnp.sum(chosen, axis=0, keepdims=True)
    ca_ref[...] = cnt_ref[...]


def _mix_router(a, b, x, w_out, g, wr, br, cnt0, tm, precise=False):
    m, d = x.shape
    wa, wb = a.shape[1], b.shape[1]
    n_marks = max(1, tm // ROUTE_TILE)
    return pl.pallas_call(
        functools.partial(_mix_router_kernel, precise=precise),
        grid=(m // tm,),
        in_specs=[
            pl.BlockSpec((tm, wa), lambda i: (i, 0)),
            pl.BlockSpec((tm, wb), lambda i: (i, 0)),
            pl.BlockSpec((tm, d), lambda i: (i, 0)),
            pl.BlockSpec((wa + wb, d), lambda i: (0, 0)),
            pl.BlockSpec((1, d), lambda i: (0, 0)),
            pl.BlockSpec((d, 128), lambda i: (0, 0)),
            pl.BlockSpec((1, 128), lambda i: (0, 0)),
            pl.BlockSpec((1, 128), lambda i: (0, 0)),
        ],
        out_specs=[
            pl.BlockSpec((tm, d), lambda i: (i, 0)),
            pl.BlockSpec((tm, d), lambda i: (i, 0)),
            pl.BlockSpec((tm, 2), lambda i: (i, 0)),
            pl.BlockSpec((tm, 2), lambda i: (i, 0)),
            pl.BlockSpec((tm, 2), lambda i: (i, 0)),
            pl.BlockSpec((1, n_marks, 128), lambda i: (i, 0, 0)),
            pl.BlockSpec((1, 128), lambda i: (0, 0)),
        ],
        out_shape=[jax.ShapeDtypeStruct((m, d), F32), jax.ShapeDtypeStruct((m, d), BF16),
                   jax.ShapeDtypeStruct((m, 2), I32), jax.ShapeDtypeStruct((m, 2), F32),
                   jax.ShapeDtypeStruct((m, 2), I32), jax.ShapeDtypeStruct((m // tm, n_marks, 128), F32),
                   jax.ShapeDtypeStruct((1, 128), F32)],
        scratch_shapes=[pltpu.VMEM((1, 128), F32)],
        compiler_params=_cparams("arbitrary"),
        name="mix_router",
    )(a, b, x, w_out, g.reshape(1, d), wr, br, cnt0)


ATT_T = 512


def _t5_bucket(dist):
    max_exact = N_BUCKETS // 2
    d = jnp.maximum(dist, 1).astype(F32)
    large = max_exact + (jnp.log(d / max_exact) / math.log(A_WIN / max_exact)
                         * (N_BUCKETS - max_exact)).astype(I32)
    return jnp.where(dist < max_exact, dist, jnp.minimum(large, N_BUCKETS - 1))


def _distance_logit_table(rel_bias, max_dist):
    dist = jnp.arange(max_dist + 1, dtype=I32)
    mult = jnp.zeros((max_dist + 1,), F32)
    for window, d in A_BRANCHES:
        mult = mult + ((dist % d == 0) & (dist <= window)).astype(F32)
    bias = rel_bias[_t5_bucket(dist)].astype(F32).T
    return jnp.where(mult[None, :] > 0, bias + jnp.log(jnp.maximum(mult, 1.0))[None, :], NEG)


def _attn_kernel(q_ref, kt_ref, vt_ref, row0_ref, o_ref, kb_ref, vb_ref):
    s_len = q_ref.shape[1]
    nb = s_len // ATT_T
    period = row0_ref.shape[2]
    kb_ref[...] = kt_ref[0].astype(BF16)
    vb_ref[...] = vt_ref[0].astype(BF16)
    scale = A_HD ** -0.5
    for hl in range(2):
        hrows = slice(hl * A_HD, (hl + 1) * A_HD)
        table = pltpu.roll(jnp.broadcast_to(row0_ref[0, hl:hl + 1, :], (ATT_T, period)), 0, 1,
                           stride=1, stride_axis=0)
        for i in range(nb):
            rows = slice(i * ATT_T, (i + 1) * ATT_T)
            width = (i + 1) * ATT_T
            q = (q_ref[0, rows, hrows] * scale).astype(BF16)
            s = jnp.dot(q, kb_ref[hrows, 0:width], preferred_element_type=F32)
            s = s + table[:, (nb - 1 - i) * ATT_T:nb * ATT_T]
            m = jnp.max(s, axis=-1, keepdims=True)
            p = jnp.exp(s - m)
            den = jnp.sum(p, axis=-1, keepdims=True)
            o = lax.dot_general(p.astype(BF16), vb_ref[hrows, 0:width], (((1,), (1,)), ((), ())),
                                preferred_element_type=F32)
            o_ref[0, rows, hrows] = o / den


def _attention_prompt(q, kt, vt, rel_bias):
    b, s_len, _ = q.shape
    nb = s_len // ATT_T
    tab1 = _distance_logit_table(rel_bias, s_len)
    period = s_len + ATT_T
    z = jnp.arange(period, dtype=I32)
    delta = jnp.where(z < s_len, (nb - 1) * ATT_T - z, (nb - 1) * ATT_T + period - z)
    row0 = jnp.where(delta[None] >= 0, tab1[:, jnp.clip(delta, 0, s_len)], NEG)
    return pl.pallas_call(
        _attn_kernel,
        grid=(b, A_HEADS // 2),
        in_specs=[pl.BlockSpec((1, s_len, 2 * A_HD), lambda bi, hp: (bi, 0, hp)),
                  pl.BlockSpec((1, 2 * A_HD, s_len), lambda bi, hp: (bi, hp, 0)),
                  pl.BlockSpec((1, 2 * A_HD, s_len), lambda bi, hp: (bi, hp, 0)),
                  pl.BlockSpec((1, 2, period), lambda bi, hp: (hp, 0, 0))],
        out_specs=pl.BlockSpec((1, s_len, 2 * A_HD), lambda bi, hp: (bi, 0, hp)),
        out_shape=jax.ShapeDtypeStruct(q.shape, F32),
        scratch_shapes=[pltpu.VMEM((2 * A_HD, s_len), BF16), pltpu.VMEM((2 * A_HD, s_len), BF16)],
        compiler_params=_cparams("arbitrary", "arbitrary"),
        name="dilated_attention",
    )(q, kt, vt, row0.reshape(A_HEADS // 2, 2, period))


MLSTM_CHUNK = 256


def _log_sigmoid(x):
    return jnp.minimum(x, 0.0) - jnp.log(1.0 + jnp.exp(-jnp.abs(x)))


def _mlstm_kernel(qk_ref, v_ref, ob_ref, gc_ref, gr_ref, wc_ref, bc_ref, bifc_ref, bifr_ref, gh_ref,
                  o_ref, c_out_ref, n_out_ref, m_out_ref, xbuf_ref, c_ref, n_ref, m_ref):
    ci = pl.program_id(1)
    L = qk_ref.shape[1]

    @pl.when(ci == 0)
    def _():
        xbuf_ref[0:8, :] = jnp.zeros((8, 2 * B_W), F32)
        c_ref[...] = jnp.zeros_like(c_ref)
        n_ref[...] = jnp.zeros_like(n_ref)
        m_ref[...] = jnp.zeros_like(m_ref)

    x = qk_ref[0]
    xbuf_ref[8:8 + L, :] = x
    y = bc_ref[...] + wc_ref[3:4, :] * x
    for j in range(B_CONV - 1):
        y = y + wc_ref[j:j + 1, :] * xbuf_ref[5 + j:5 + j + L, :]
    xbuf_ref[0:8, :] = x[L - 8:L, :]
    y = _silu(y)

    gcol = gc_ref[0] + bifc_ref[...]
    grow = gr_ref[...] + bifr_ref[...]
    ri = lax.broadcasted_iota(I32, (L, L), 0)
    cj = lax.broadcasted_iota(I32, (L, L), 1)
    causal = ri >= cj
    lower = jnp.where(causal, 1.0, 0.0)
    b_col = jnp.dot(lower, _log_sigmoid(gcol), preferred_element_type=F32, precision=HIGHEST)
    b_row = lax.dot_general(_log_sigmoid(grow), lower, (((1,), (1,)), ((), ())),
                            preferred_element_type=F32, precision=HIGHEST)

    for h in range(B_HEADS):
        cols = slice(h * B_HD, (h + 1) * B_HD)
        qf = y[:, cols]
        q = qf.astype(BF16)
        kf = y[:, B_W + h * B_HD:B_W + (h + 1) * B_HD] * (B_HD ** -0.5)
        k = kf.astype(BF16)
        v = v_ref[0, :, cols].astype(BF16)
        bc = b_col[:, B_HEADS + h:B_HEADS + h + 1]
        ic = gcol[:, h:h + 1]
        br = b_row[B_HEADS + h:B_HEADS + h + 1, :]
        ir = grow[h:h + 1, :]
        m_prev = m_ref[h:h + 1, 0:1]
        dm = jnp.where(causal, bc - br + ir, NEG)
        inter = bc + m_prev
        mt = jnp.maximum(inter, jnp.max(dm, axis=1, keepdims=True))
        w_intra = jnp.exp(dm - mt)
        w_inter = jnp.exp(inter - mt)
        a = w_intra * lax.dot_general(q, k, (((1,), (1,)), ((), ())), preferred_element_type=F32)
        c_prev = c_ref[h]
        n_prev = n_ref[h:h + 1, :]
        num = (jnp.dot(a.astype(BF16), v, preferred_element_type=F32)
               + w_inter * jnp.dot(q, c_prev.astype(BF16), preferred_element_type=F32))
        den = (jnp.sum(a, axis=1, keepdims=True)
               + w_inter * jnp.sum(qf * n_prev, axis=1, keepdims=True))
        hb = num / jnp.maximum(jnp.abs(den), jnp.exp(-mt))
        m_new = mt[L - 1:L, :]
        b_last = bc[L - 1:L, :]
        g = jnp.exp(b_last - bc + ic - m_new)
        decay = jnp.exp(b_last + m_prev - m_new)
        kg = kf * g
        c_ref[h] = decay * c_prev + lax.dot_general(kg.astype(BF16), v, (((0,), (0,)), ((), ())),
                                                    preferred_element_type=F32)
        n_ref[h:h + 1, :] = decay * n_prev + jnp.sum(kg, axis=0, keepdims=True)
        m_ref[h:h + 1, :] = jnp.broadcast_to(m_new, (1, 128))
        hb = hb * lax.rsqrt(jnp.mean(hb * hb, axis=-1, keepdims=True) + EPS) * gh_ref[:, cols]
        o_ref[0, :, cols] = jax.nn.sigmoid(ob_ref[0, :, cols]) * hb

    @pl.when(ci == pl.num_programs(1) - 1)
    def _():
        c_out_ref[0] = c_ref[...]
        n_out_ref[0] = n_ref[...]
        m_out_ref[0] = m_ref[...]


def _mlstm_prompt(qk, v, ob, gcol, grow, w_conv, b_conv, b_if, g_head):
    b, s_len, _ = qk.shape
    L = min(MLSTM_CHUNK, s_len)
    nc = s_len // L
    bif_col = jnp.zeros((1, 128), F32).at[0, :2 * B_HEADS].set(b_if)
    bif_row = b_if.reshape(2 * B_HEADS, 1)
    seq = lambda w: pl.BlockSpec((1, L, w), lambda bi, ci: (bi, ci, 0))
    full = lambda shape: pl.BlockSpec(shape, lambda bi, ci: (0,) * len(shape))
    return pl.pallas_call(
        _mlstm_kernel,
        grid=(b, nc),
        in_specs=[seq(2 * B_W), seq(B_W), seq(B_W), seq(128),
                  pl.BlockSpec((2 * B_HEADS, L), lambda bi, ci: (0, bi * nc + ci)),
                  full((B_CONV, 2 * B_W)), full((1, 2 * B_W)), full((1, 128)), full((2 * B_HEADS, 1)),
                  full((1, B_W))],
        out_specs=[seq(B_W),
                   pl.BlockSpec((1, B_HEADS, B_HD, B_HD), lambda bi, ci: (bi, 0, 0, 0)),
                   pl.BlockSpec((1, 8, B_HD), lambda bi, ci: (bi, 0, 0)),
                   pl.BlockSpec((1, 8, 128), lambda bi, ci: (bi, 0, 0))],
        out_shape=[jax.ShapeDtypeStruct((b, s_len, B_W), F32),
                   jax.ShapeDtypeStruct((b, B_HEADS, B_HD, B_HD), F32),
                   jax.ShapeDtypeStruct((b, 8, B_HD), F32),
                   jax.ShapeDtypeStruct((b, 8, 128), F32)],
        scratch_shapes=[pltpu.VMEM((L + 8, 2 * B_W), F32), pltpu.VMEM((B_HEADS, B_HD, B_HD), F32),
                        pltpu.VMEM((8, B_HD), F32), pltpu.VMEM((8, 128), F32)],
        compiler_params=_cparams("arbitrary", "arbitrary"),
        name="mlstm_chunkwise",
    )(qk, v, ob, gcol, grow, w_conv, b_conv.reshape(1, -1), bif_col, bif_row, g_head.reshape(1, -1))


def _odd_in_kernel(x_ref, g_ref, w_ref, gcv_ref, bcv_ref, ws_ref, bs_ref, c_ref, xd_ref):
    tm = x_ref.shape[0]
    xn = _rms(x_ref[...], g_ref[...]).astype(BF16)
    u = jnp.dot(xn, w_ref[:, 0:C_W], preferred_element_type=F32)
    v = jnp.dot(xn, w_ref[:, C_W:2 * C_W], preferred_element_type=F32)
    xd_ref[...] = jnp.dot(xn, w_ref[:, 2 * C_W:], preferred_element_type=F32)
    mu = jnp.mean(v, axis=-1, keepdims=True)
    vc = v - mu
    var = jnp.mean(vc * vc, axis=-1, keepdims=True)
    vl = (vc * lax.rsqrt(var + EPS) * gcv_ref[...] + bcv_ref[...]).astype(BF16)
    ri = lax.broadcasted_iota(I32, (C_CHUNK, C_CHUNK), 0)
    cj = lax.broadcasted_iota(I32, (C_CHUNK, C_CHUNK), 1)
    for g in range(C_GROUPS):
        cols = slice(g * C_GD, (g + 1) * C_GD)
        wg = jnp.where(ri >= cj, ws_ref[g], 0.0).astype(BF16)
        for c in range(tm // C_CHUNK):
            rows = slice(c * C_CHUNK, (c + 1) * C_CHUNK)
            s = jnp.dot(wg, vl[rows, cols], preferred_element_type=F32) + bs_ref[:, g:g + 1]
            c_ref[rows, cols] = u[rows, cols] * s


def _odd_in(x, g, w, g_cv, b_cv, w_s, b_s, tm):
    m, d = x.shape
    n = w.shape[1]
    full = lambda shape: pl.BlockSpec(shape, lambda i: (0,) * len(shape))
    return pl.pallas_call(
        _odd_in_kernel,
        grid=(m // tm,),
        in_specs=[pl.BlockSpec((tm, d), lambda i: (i, 0)), full((1, d)), full((d, n)), full((1, C_W)),
                  full((1, C_W)), full((C_GROUPS, C_CHUNK, C_CHUNK)), full((C_CHUNK, C_GROUPS))],
        out_specs=[pl.BlockSpec((tm, C_W), lambda i: (i, 0)), pl.BlockSpec((tm, D_W), lambda i: (i, 0))],
        out_shape=[jax.ShapeDtypeStruct((m, C_W), F32), jax.ShapeDtypeStruct((m, D_W), F32)],
        compiler_params=_cparams("arbitrary"),
        name="odd_in_proj_gate",
    )(x, g.reshape(1, d), w, g_cv.reshape(1, -1), b_cv.reshape(1, -1), w_s, b_s.T)


S5_TT = 128
S5_PAD = 8
S5_HALF = D_GROUPS * D_STATE // 2


def _s5_discretize(a_re, a_im, log_dt, bm_re, bm_im, cm_re, cm_im):
    a_re, a_im = a_re.astype(F32), a_im.astype(F32)
    dt = jnp.exp(log_dt.astype(F32))[:, None]
    mag = jnp.exp(a_re * dt)
    ab_re = mag * jnp.cos(a_im * dt)
    ab_im = mag * jnp.sin(a_im * dt)
    inv = 1.0 / (a_re * a_re + a_im * a_im)
    f_re = ((ab_re - 1.0) * a_re + ab_im * a_im) * inv
    f_im = (ab_im * a_re - (ab_re - 1.0) * a_im) * inv
    bm_re, bm_im = bm_re.astype(F32), bm_im.astype(F32)
    bb_re = f_re[..., None] * bm_re - f_im[..., None] * bm_im
    bb_im = f_re[..., None] * bm_im + f_im[..., None] * bm_re
    return ab_re, ab_im, bb_re, bb_im


def _s5_matrices(bb_re, bb_im, cm_re, cm_im):
    gh = D_GROUPS // 2
    eye = jnp.eye(gh, dtype=F32)

    def in_half(bb):
        return jnp.einsum('gpc,gh->gchp', bb, eye).reshape(gh * D_GCH, gh * D_STATE)

    def out_half(cm):
        return jnp.einsum('gcp,gh->gphc', cm, eye).reshape(gh * D_STATE, gh * D_GCH)

    bd = jnp.stack([jnp.concatenate([in_half(bb_re[h * gh:(h + 1) * gh]), in_half(bb_im[h * gh:(h + 1) * gh])], axis=1)
                    for h in range(2)])
    cm = jnp.stack([jnp.concatenate([out_half(cm_re[h * gh:(h + 1) * gh].astype(F32)),
                                     -out_half(cm_im[h * gh:(h + 1) * gh].astype(F32))], axis=0)
                    for h in range(2)])
    return bd, cm


def _gelu_tanh(x):
    return 0.5 * x * (1.0 + jnp.tanh(math.sqrt(2.0 / math.pi) * (x + 0.044715 * (x * x * x))))


def _s5_kernel(x_ref, bd_ref, cm_ref, ar_ref, ai_ref, dsk_ref, wg_ref, bg_ref, o_ref, hr_out, hi_out,
               bu_ref, hs_ref, hr_ref, hi_ref):
    i = pl.program_id(0)
    nb, tt, _ = x_ref.shape
    stride = tt + S5_PAD
    half_in = D_W // 2
    nct = S5_HALF // 128

    @pl.when(i == 0)
    def _():
        hr_ref[...] = jnp.zeros_like(hr_ref)
        hi_ref[...] = jnp.zeros_like(hi_ref)

    x = x_ref[...].reshape(nb * tt, D_W)
    ys = []
    for h in range(2):
        lanes = slice(h * S5_HALF, (h + 1) * S5_HALF)
        bu = jnp.dot(x[:, h * half_in:(h + 1) * half_in].astype(BF16), bd_ref[h], preferred_element_type=F32)
        for c in range(2 * nct):
            for b in range(nb):
                bu_ref[c, b * stride:b * stride + tt, :] = bu[b * tt:(b + 1) * tt, c * 128:(c + 1) * 128]
        ar = jnp.broadcast_to(ar_ref[:, lanes], (nb, S5_HALF))
        ai = jnp.broadcast_to(ai_ref[:, lanes], (nb, S5_HALF))

        def step(t, carry):
            hr, hi = carry
            rows = pl.ds(t, nb, stride=stride)
            bur = jnp.concatenate([bu_ref[c, rows, :] for c in range(nct)], axis=1)
            bui = jnp.concatenate([bu_ref[nct + c, rows, :] for c in range(nct)], axis=1)
            nhr = ar * hr - ai * hi + bur
            nhi = ar * hi + ai * hr + bui
            for c in range(nct):
                hs_ref[c, rows, :] = nhr[:, c * 128:(c + 1) * 128]
                hs_ref[nct + c, rows, :] = nhi[:, c * 128:(c + 1) * 128]
            return nhr, nhi

        hr, hi = lax.fori_loop(0, tt, step, (hr_ref[:, lanes], hi_ref[:, lanes]), unroll=2)
        hr_ref[:, lanes] = hr
        hi_ref[:, lanes] = hi
        hs = jnp.concatenate(
            [jnp.concatenate([hs_ref[c, b * stride:b * stride + tt, :] for b in range(nb)], axis=0)
             for c in range(2 * nct)], axis=1)
        ys.append(jnp.dot(hs.astype(BF16), cm_ref[h], preferred_element_type=F32))
    y = jnp.concatenate(ys, axis=-1) + dsk_ref[...] * x
    g = _gelu_tanh(y)
    d_out = g * jax.nn.sigmoid(jnp.dot(g.astype(BF16), wg_ref[...], preferred_element_type=F32) + bg_ref[...])
    o_ref[...] = d_out.reshape(nb, tt, D_W)

    @pl.when(i == pl.num_programs(0) - 1)
    def _():
        hr_out[...] = hr_ref[...]
        hi_out[...] = hi_ref[...]


def _s5_prompt(xd, bd, cm, ab_re, ab_im, d_skip, w_glu, b_glu):
    nb, t_len, _ = xd.shape
    tt = min(S5_TT, t_len)
    n_state = D_GROUPS * D_STATE
    full = lambda shape: pl.BlockSpec(shape, lambda i: (0,) * len(shape))
    return pl.pallas_call(
        _s5_kernel,
        grid=(t_len // tt,),
        in_specs=[pl.BlockSpec((nb, tt, D_W), lambda i: (0, i, 0)), full(bd.shape), full(cm.shape),
                  full((1, n_state)), full((1, n_state)), full((1, D_W)), full((D_W, D_W)), full((1, D_W))],
        out_specs=[pl.BlockSpec((nb, tt, D_W), lambda i: (0, i, 0)), full((nb, n_state)), full((nb, n_state))],
        out_shape=[jax.ShapeDtypeStruct(xd.shape, F32), jax.ShapeDtypeStruct((nb, n_state), F32),
                   jax.ShapeDtypeStruct((nb, n_state), F32)],
        scratch_shapes=[pltpu.VMEM((2 * S5_HALF // 128, nb * (tt + S5_PAD), 128), F32),
                        pltpu.VMEM((2 * S5_HALF // 128, nb * (tt + S5_PAD), 128), F32),
                        pltpu.VMEM((nb, n_state), F32), pltpu.VMEM((nb, n_state), F32)],
        compiler_params=_cparams("arbitrary"),
        name="s5_scan_glu",
    )(xd, bd, cm, ab_re.reshape(1, n_state), ab_im.reshape(1, n_state), d_skip.reshape(1, D_W), w_glu,
      b_glu.reshape(1, D_W))


def _hdot(a, b):
    return jnp.dot(a, b, precision=HIGHEST, preferred_element_type=F32)


def _sample_window_kernel(q_ref, kn_ref, vn_ref, tab_ref, tabn_ref, kt_ref, vt_ref, o_ref, okt_ref, ovt_ref,
                          head0=0):
    L = kt_ref.shape[3]
    last = lax.broadcasted_iota(I32, (A_HD, L), 1) == L - 1
    for h in range(kt_ref.shape[1]):
        kh, vh = kt_ref[0, h], vt_ref[0, h]
        kn, vn = kn_ref[0, h], vn_ref[0, h]
        q = q_ref[0, h] * (A_HD ** -0.5)
        s = jnp.sum(kh * q, axis=0, keepdims=True) + tab_ref[head0 + h]
        s_new = jnp.sum(kn * q, axis=0, keepdims=True) + tabn_ref[head0 + h]
        m = jnp.maximum(jnp.max(s, axis=1, keepdims=True), s_new)
        p = jnp.exp(s - m)
        p_new = jnp.exp(s_new - m)
        den = jnp.sum(p, axis=1, keepdims=True) + p_new
        o_ref[0, h] = (jnp.sum(vh * p, axis=1, keepdims=True) + vn * p_new) / den
        okt_ref[0, h] = jnp.where(last, kn, pltpu.roll(kh, L - 1, 1))
        ovt_ref[0, h] = jnp.where(last, vn, pltpu.roll(vh, L - 1, 1))


def _sample_window_attention(q, k_new, v_new, cache_kt, cache_vt, rel_bias):
    operands, in_specs, out_specs, out_shape = _sample_window_specs(q, k_new, v_new, cache_kt, cache_vt, rel_bias,
                                                                    lambda i: (i, 0))
    return pl.pallas_call(
        _sample_window_kernel,
        grid=(q.shape[0],),
        in_specs=in_specs,
        out_specs=out_specs,
        out_shape=out_shape,
        compiler_params=_cparams("arbitrary"),
        name="sample_window_attention",
    )(*operands)


def _sample_window_specs(q, k_new, v_new, cache_kt, cache_vt, rel_bias, block_of_step, heads_per_step=A_HEADS):
    n, _, _, L = cache_kt.shape
    assert L == A_WIN
    tab1 = _distance_logit_table(rel_bias, L)
    tab = tab1[:, :0:-1].reshape(A_HEADS, 1, L)
    tab_new = tab1[:, 0].reshape(A_HEADS, 1, 1)
    col = pl.BlockSpec((1, heads_per_step, A_HD, 1), lambda *g: (*block_of_step(*g), 0, 0))
    win = pl.BlockSpec((1, heads_per_step, A_HD, L), lambda *g: (*block_of_step(*g), 0, 0))
    full = lambda shape: pl.BlockSpec(shape, lambda *g: (0,) * len(shape))
    out_shape = [jax.ShapeDtypeStruct((n, A_HEADS, A_HD, 1), F32),
                 jax.ShapeDtypeStruct(cache_kt.shape, F32), jax.ShapeDtypeStruct(cache_vt.shape, F32)]
    return ([q, k_new, v_new, tab, tab_new, cache_kt, cache_vt],
            [col, col, col, full(tab.shape), full(tab_new.shape), win, win], [col, win, win], out_shape)


def _sample_even_proj(x, g_mix, w_in):
    n_main = 3 * A_W + 4 * B_W
    segs = ((0, A_W), (A_W, A_W), (2 * A_W, A_W), (3 * A_W, 2 * B_W), (3 * A_W + 2 * B_W, B_W),
            (3 * A_W + 3 * B_W, B_W), (n_main, 2 * B_HEADS))
    return _norm_proj(x, g_mix, w_in, jnp.zeros((8, x.shape[1]), F32), segs, (), tm=x.shape[0], precise=True)


def _sample_mlstm_step(qkb, vb, ob, gates, c0, n0, m0, conv0, b_if, w_conv, b_conv, g_head):
    n = qkb.shape[0]
    xp = jnp.concatenate([conv0, qkb[:, None, :]], axis=1)
    qk = _silu(b_conv + jnp.sum(xp * w_conv[None], axis=1))
    q = qk[:, :B_W].reshape(n, B_HEADS, B_HD)
    k = qk[:, B_W:].reshape(n, B_HEADS, B_HD) * (B_HD ** -0.5)
    v = vb.reshape(n, B_HEADS, B_HD)
    ig = gates[:, :B_HEADS] + b_if[:B_HEADS]
    logf = _log_sigmoid(gates[:, B_HEADS:] + b_if[B_HEADS:])
    inter = logf + m0
    mt = jnp.maximum(inter, ig)
    w_intra = jnp.exp(ig - mt)
    w_inter = jnp.exp(inter - mt)
    a = w_intra * jnp.sum(q * k, axis=-1)
    num = a[..., None] * v + w_inter[..., None] * jnp.einsum('nhk,nhkv->nhv', q, c0, precision=HIGHEST)
    den = a + w_inter * jnp.sum(q * n0, axis=-1)
    hb = num / jnp.maximum(jnp.abs(den), jnp.exp(-mt))[..., None]
    g = jnp.exp(ig - mt)
    decay = jnp.exp(inter - mt)
    c_new = decay[..., None, None] * c0 + (g[..., None] * k)[..., :, None] * v[..., None, :]
    n_new = decay[..., None] * n0 + g[..., None] * k
    hb = hb * lax.rsqrt(jnp.mean(hb * hb, axis=-1, keepdims=True) + EPS) * g_head.reshape(B_HEADS, B_HD)
    b_out = jax.nn.sigmoid(ob) * hb.reshape(n, B_W)
    return b_out, c_new, n_new, mt, xp[:, 1:]


def _sample_odd_mixers(x, h_re0, h_im0, g_mix, w_in, g_cv, b_cv, w_s, b_s, ab_re, ab_im, bb_re, bb_im,
                       cm_re, cm_im, d_skip, w_glu, b_glu):
    n = x.shape[0]
    h = _rms(x, g_mix)
    proj = _hdot(h, w_in)
    u, v, xd = proj[:, :C_W], proj[:, C_W:2 * C_W], proj[:, 2 * C_W:]
    mu = jnp.mean(v, axis=-1, keepdims=True)
    vc = v - mu
    v = vc * lax.rsqrt(jnp.mean(vc * vc, axis=-1, keepdims=True) + EPS) * g_cv + b_cv
    s = (w_s[:, 0, 0][None, :, None] * v.reshape(n, C_GROUPS, C_GD) + b_s[:, 0][None, :, None]).reshape(n, C_W)
    c_out = u * s
    xg = xd.reshape(n, D_GROUPS, D_GCH)
    bu_re = jnp.einsum('gpc,ngc->ngp', bb_re, xg, precision=HIGHEST)
    bu_im = jnp.einsum('gpc,ngc->ngp', bb_im, xg, precision=HIGHEST)
    hr = ab_re * h_re0 - ab_im * h_im0 + bu_re
    hi = ab_re * h_im0 + ab_im * h_re0 + bu_im
    y = (jnp.einsum('gcp,ngp->ngc', cm_re, hr, precision=HIGHEST)
         - jnp.einsum('gcp,ngp->ngc', cm_im, hi, precision=HIGHEST) + d_skip * xg)
    g = _gelu_tanh(y.reshape(n, D_W))
    d_out = g * jax.nn.sigmoid(_hdot(g, w_glu) + b_glu)
    return c_out, d_out, v, hr, hi


_EVEN_SEGS = ((0, A_W), (A_W, 2 * B_W), (A_W + 2 * B_W, B_W), (A_W + 3 * B_W, B_W), (A_W + 4 * B_W, 128))


def kernel(x_prompt, x_sample, cache_a_k, cache_a_v, state_b_c, state_b_n, state_b_m, state_b_conv,
           state_d_re, state_d_im, rel_bias, g_mix, g_ffn, g_final, w_in_e, b_if, w_conv_b, b_conv_b,
           g_bhead, w_out_e, w1_e, w3_e, w2_e, w_in_o, g_cv, b_cv, w_s, b_s, a_re, a_im, log_dt,
           bm_re, bm_im, cm_re, cm_im, d_skip, w_glu, b_glu, w_out_o, w_router, b_router,
           w1_m, w3_m, w2_m):
    bp, sp, d = x_prompt.shape
    ns = x_sample.shape[0]
    mp = bp * sp
    xp = x_prompt.reshape(mp, d)
    xs = x_sample.reshape(ns, d)

    w_in = w_in_e[0]
    n_main = 3 * A_W + 4 * B_W
    w_gate = w_in[:, n_main:]
    w_cat = jnp.concatenate([w_in[:, :A_W], w_in[:, 3 * A_W:n_main],
                             jnp.pad(w_gate, ((0, 0), (0, 128 - 2 * B_HEADS)))], axis=1)
    w_t = jnp.concatenate([w_in[:, A_W:3 * A_W], w_gate], axis=1).T
    q, qkb, vb, ob, gcol, kt, vt, grow = _norm_proj(
        xp, g_mix[0], w_cat.astype(BF16), w_t.astype(BF16), _EVEN_SEGS,
        ((0, A_W, True), (A_W, A_W, True), (2 * A_W, 2 * B_HEADS, False)), tm=1024, rows_per_seq=sp)
    shp = lambda a: a.reshape(bp, sp, a.shape[-1])
    pos_minor = lambda c: jnp.transpose(c, (0, 2, 3, 1))
    pos_major = lambda c: jnp.transpose(c, (0, 3, 1, 2))
    s_proj = _sample_even_proj(xs, g_mix[0], w_in)
    cols = lambda a: a.reshape(ns, A_HEADS, A_HD, 1)
    s_attn = (cols(s_proj[0]), cols(s_proj[1]), cols(s_proj[2]), pos_minor(cache_a_k[0]), pos_minor(cache_a_v[0]))
    a_out = _attention_prompt(shp(q), kt, vt, rel_bias)
    b_out, pc, pn, pm = _mlstm_prompt(shp(qkb), shp(vb), shp(ob), shp(gcol), grow, w_conv_b[0], b_conv_b[0],
                                      b_if[0], g_bhead[0])
    ffn_args = (a_out.reshape(mp, A_W), b_out.reshape(mp, B_W), xp, w_out_e[0].astype(BF16), g_ffn[0],
                w1_e[0].astype(BF16), w3_e[0].astype(BF16), w2_e[0].astype(BF16))
    ffn_tm, ffn_tf = 1024, 512
    if ns * WINDOW_HEAD_BLOCKS == (mp // ffn_tm) * WINDOW_STEPS:
        x1p, sa, s_kt, s_vt = _mix_ffn(*ffn_args, tm=ffn_tm, tf=ffn_tf, window=s_attn, rel_bias=rel_bias)
    else:
        x1p = _mix_ffn(*ffn_args, tm=ffn_tm, tf=ffn_tf)
        sa, s_kt, s_vt = _sample_window_attention(*s_attn, rel_bias)
    sa = sa.reshape(ns, A_W)
    s_k, s_v = pos_major(s_kt), pos_major(s_vt)

    sb, sc, sn, sm, s_conv = _sample_mlstm_step(
        s_proj[3], s_proj[4], s_proj[5], s_proj[6], state_b_c[0], state_b_n[0], state_b_m[0], state_b_conv[0],
        b_if[0], w_conv_b[0], b_conv_b[0], g_bhead[0])
    x1s = _mix_ffn(sa, sb, xs, w_out_e[0], g_ffn[0], w1_e[0], w3_e[0], w2_e[0], tm=ns, tf=512, precise=True)

    ab_re, ab_im, bb_re, bb_im = _s5_discretize(a_re[0], a_im[0], log_dt[0], bm_re[0], bm_im[0], cm_re[0],
                                                cm_im[0])
    bd, cmat = _s5_matrices(bb_re, bb_im, cm_re[0], cm_im[0])
    c_out, xd = _odd_in(x1p, g_mix[1], w_in_o[0].astype(BF16), g_cv[0], b_cv[0], w_s[0], b_s[0], tm=1024)
    d_out, p_hr, p_hi = _s5_prompt(xd.reshape(bp, sp, D_W), bd.astype(BF16), cmat.astype(BF16), ab_re, ab_im,
                                   d_skip[0], w_glu[0].astype(BF16), b_glu[0])
    wr = jnp.pad(w_router[0].astype(F32), ((0, 0), (0, 128 - N_EXPERTS)))
    br = jnp.pad(b_router[0].astype(F32), (0, 128 - N_EXPERTS)).reshape(1, 128)
    x2p, hnp, idp, gtp, rkp, cbp, cap = _mix_router(c_out, d_out.reshape(mp, D_W), x1p, w_out_o[0].astype(BF16),
                                                    g_ffn[1], wr, br, jnp.zeros((1, 128), F32), tm=1024)

    sc_out, sd_out, s_cv, s_hr, s_hi = _sample_odd_mixers(
        x1s, state_d_re[0], state_d_im[0], g_mix[1], w_in_o[0], g_cv[0], b_cv[0], w_s[0], b_s[0], ab_re, ab_im,
        bb_re, bb_im, cm_re[0], cm_im[0], d_skip[0], w_glu[0], b_glu[0])
    x2s, hns, ids_s, gts, rks, cbs, cas = _mix_router(sc_out, sd_out, x1s, w_out_o[0], g_ffn[1], wr, br, cap,
                                                      tm=ns, precise=True)

    pad = ((0, ROUTE_TILE - ns), (0, 0))
    cnt_tile = jnp.concatenate([cbp.reshape(-1, 128), cbs.reshape(-1, 128), cas])[:, :N_EXPERTS].astype(I32)
    y_p, y_s = _moe_block(x2p, hnp, idp, gtp, rkp, jnp.pad(x2s, pad), jnp.pad(hns, pad),
                          jnp.pad(ids_s, pad, constant_values=-1), jnp.pad(gts, pad), jnp.pad(rks, pad),
                          cnt_tile, w1_m[0], w3_m[0], w2_m[0], g_final)
    y_prompt = y_p.reshape(bp, sp, d)
    y_sample = y_s[:ns].reshape(ns, 1, d)

    e = lambda a: a[None]
    p_k = pos_major(kt.reshape(bp, A_HEADS, A_HD, sp))
    p_v = pos_major(vt.reshape(bp, A_HEADS, A_HD, sp))
    p_conv = qkb.reshape(bp, sp, 2 * B_W)[:, sp - (B_CONV - 1):]
    return (y_prompt, y_sample, e(p_k), e(p_v), e(s_k), e(s_v), e(pc), e(sc), e(pn[:, :B_HEADS]), e(sn),
            e(pm[:, :B_HEADS, 0]), e(sm), e(p_conv), e(s_conv), e(s_cv.reshape(ns, 1, C_W)),
            e(p_hr.reshape(bp, D_GROUPS, D_STATE)), e(s_hr), e(p_hi.reshape(bp, D_GROUPS, D_STATE)), e(s_hi))
```

```python
import functools
import math

import jax
import jax.numpy as jnp
from jax import lax
from jax.experimental import pallas as pl
from jax.experimental.pallas import tpu as pltpu

F32 = jnp.float32
BF16 = jnp.bfloat16
I32 = jnp.int32
HIGHEST = lax.Precision.HIGHEST

D_MODEL = 1024
A_HEADS, A_HD = 8, 64
A_W = A_HEADS * A_HD
A_BRANCHES = ((128, 1), (512, 4), (2048, 16))
A_WIN = 2048
N_BUCKETS = 32
B_HEADS, B_HD = 4, 128
B_W = B_HEADS * B_HD
B_CONV = 4
C_GROUPS, C_GD = 4, 128
C_W = C_GROUPS * C_GD
C_CHUNK = 128
D_GROUPS, D_GCH, D_STATE = 32, 16, 64
D_W = D_GROUPS * D_GCH
D_FF = 3584
N_EXPERTS = 8
TOP_K = 2
EPS = 1e-6
NEG = -1e30

VMEM_LIMIT = 62 * 1024 * 1024

ROUTE_TILE = 256
FFN_TILE = 1024
FFN_ROW_CHUNK = 256
GATHER_TOKENS = 2 * ROUTE_TILE
COMBINE_ALIGN = 16


def _cparams(*sem):
    return pltpu.CompilerParams(dimension_semantics=sem, vmem_limit_bytes=VMEM_LIMIT)


def _moe_plan(ids, rank, cnt_tile):
    mt = ids.shape[0]
    n_assign = 2 * mt
    n_ffn_tiles = -(-(n_assign + N_EXPERTS * (FFN_TILE - 1)) // FFN_TILE) + 1
    n_sub = n_ffn_tiles * (FFN_TILE // ROUTE_TILE)
    counts = cnt_tile[-1]
    seg = ((counts + FFN_TILE - 1) // FFN_TILE) * FFN_TILE
    seg_end = jnp.cumsum(seg)
    seg_off = seg_end - seg
    off_of = jnp.sum(jnp.where(ids[..., None] == jnp.arange(N_EXPERTS, dtype=I32), seg_off, 0), axis=-1)
    dest = jnp.where(ids >= 0, off_of + rank, -1)
    total = seg_end[-1]
    tile_start = jnp.arange(n_ffn_tiles, dtype=I32) * FFN_TILE
    tile_expert = jnp.minimum(jnp.sum((tile_start[:, None] >= seg_end[None, :]).astype(I32), axis=1),
                              N_EXPERTS - 1).astype(I32)
    rows_left = (seg_off + counts)[tile_expert] - tile_start
    tile_chunks = jnp.where(tile_start < total,
                            jnp.clip(-(-rows_left // FFN_ROW_CHUNK), 0, FFN_TILE // FFN_ROW_CHUNK), 0).astype(I32)
    lo = seg_off[None, :] + cnt_tile[:-1]
    hi = seg_off[None, :] + cnt_tile[1:]
    sub_start = jnp.arange(n_sub, dtype=I32) * ROUTE_TILE
    sub_expert = jnp.minimum(jnp.sum((sub_start[:, None] >= seg_end[None, :]).astype(I32), axis=1),
                             N_EXPERTS - 1)
    lo_e, hi_e = lo.T[sub_expert], hi.T[sub_expert]
    jlo = jnp.sum((hi_e <= sub_start[:, None]).astype(I32), axis=1).astype(I32)
    jhi = (jnp.sum((lo_e < sub_start[:, None] + ROUTE_TILE).astype(I32), axis=1) - 1).astype(I32)
    win = lo // COMBINE_ALIGN
    flat1 = lambda a: a.reshape(-1).astype(I32)
    return dest, tile_expert, tile_chunks, (jlo, jhi), (flat1(win), flat1(lo), flat1(hi)), n_ffn_tiles


def _moe_gather_kernel(jlo_ref, jhi_ref, hp_hbm, hs_hbm, dt_ref, gt_ref, xs_ref, gs_ref,
                       hn_ref, acc_ref, g_ref, sem):
    s = pl.program_id(0)
    mp = hp_hbm.shape[0]

    @pl.when(s == 0)
    def _():
        copies = (pltpu.make_async_copy(hp_hbm, hn_ref.at[0:mp], sem.at[0]),
                  pltpu.make_async_copy(hs_hbm, hn_ref.at[mp:mp + ROUTE_TILE], sem.at[1]))
        for c in copies:
            c.start()
        hn_ref[mp + ROUTE_TILE:, :] = jnp.zeros((ROUTE_TILE, hn_ref.shape[1]), BF16)
        for c in copies:
            c.wait()

    acc_ref[...] = jnp.zeros_like(acc_ref)
    g_ref[...] = jnp.zeros_like(g_ref)
    rows = lax.broadcasted_iota(I32, (ROUTE_TILE, GATHER_TOKENS), 0) + s * ROUTE_TILE
    jlo = jlo_ref[s]

    def body(it, carry):
        off = pl.multiple_of(jlo * ROUTE_TILE + it * GATHER_TOKENS, ROUTE_TILE)
        dt = dt_ref[:, pl.ds(off, GATHER_TOKENS)]
        gt = gt_ref[:, pl.ds(off, GATHER_TOKENS)]
        m1 = dt[0:1, :] == rows
        m2 = dt[1:2, :] == rows
        onehot = jnp.where(m1 | m2, 1.0, 0.0).astype(BF16)
        acc_ref[...] += jnp.dot(onehot, hn_ref[pl.ds(off, GATHER_TOKENS), :], preferred_element_type=F32)
        g_ref[...] += jnp.sum(jnp.where(m1, gt[0:1, :], 0.0) + jnp.where(m2, gt[1:2, :], 0.0),
                              axis=1, keepdims=True)
        return carry

    tiles_per_iter = GATHER_TOKENS // ROUTE_TILE
    lax.fori_loop(0, (jhi_ref[s] - jlo + tiles_per_iter) // tiles_per_iter, body, 0)
    xs_ref[...] = acc_ref[...].astype(BF16)
    gs_ref[...] = g_ref[...]


def _moe_gather(hn_p, hn_s, dest_t, gates_t, glist, n_sub):
    jlo, jhi = glist
    mp, d = hn_p.shape
    mt = dest_t.shape[1]
    grid_spec = pltpu.PrefetchScalarGridSpec(
        num_scalar_prefetch=2,
        grid=(n_sub,),
        in_specs=[
            pl.BlockSpec(memory_space=pl.ANY),
            pl.BlockSpec(memory_space=pl.ANY),
            pl.BlockSpec((2, mt), lambda s, *_: (0, 0)),
            pl.BlockSpec((2, mt), lambda s, *_: (0, 0)),
        ],
        out_specs=[
            pl.BlockSpec((ROUTE_TILE, d), lambda s, *_: (s, 0)),
            pl.BlockSpec((ROUTE_TILE, 1), lambda s, *_: (s, 0)),
        ],
        scratch_shapes=[pltpu.VMEM((mp + 2 * ROUTE_TILE, d), BF16), pltpu.VMEM((ROUTE_TILE, d), F32),
                        pltpu.VMEM((ROUTE_TILE, 1), F32), pltpu.SemaphoreType.DMA((2,))],
    )
    return pl.pallas_call(
        _moe_gather_kernel,
        grid_spec=grid_spec,
        out_shape=[jax.ShapeDtypeStruct((n_sub * ROUTE_TILE, d), BF16),
                   jax.ShapeDtypeStruct((n_sub * ROUTE_TILE, 1), F32)],
        compiler_params=_cparams("arbitrary"),
        name="moe_gather",
    )(jlo, jhi, hn_p, hn_s, dest_t, gates_t)


def _moe_ffn_kernel(te_ref, nc_ref, x_ref, gs_ref, w1_ref, w3_ref, w2_ref, y_ref, acc_ref):
    t, f = pl.program_id(0), pl.program_id(1)
    nf = pl.num_programs(1)
    n_chunks = nc_ref[t]
    full = FFN_TILE // FFN_ROW_CHUNK

    @pl.when(n_chunks > 0)
    def _():
        @pl.when(f == 0)
        def _():
            acc_ref[...] = jnp.zeros_like(acc_ref)

        w1, w3, w2 = w1_ref[0].astype(BF16), w3_ref[0].astype(BF16), w2_ref[0].astype(BF16)

        def ffn(rows):
            x = x_ref[rows, :]
            a = jnp.dot(x, w1, preferred_element_type=F32)
            b = jnp.dot(x, w3, preferred_element_type=F32)
            acc_ref[rows, :] += jnp.dot((_silu(a) * b).astype(BF16), w2, preferred_element_type=F32)

        @pl.when(n_chunks == full)
        def _():
            ffn(slice(0, FFN_TILE))

        for c in range(full - 1):
            @pl.when((n_chunks < full) & (c < n_chunks))
            def _():
                ffn(slice(c * FFN_ROW_CHUNK, (c + 1) * FFN_ROW_CHUNK))

        @pl.when(f == nf - 1)
        def _():
            y_ref[...] = (acc_ref[...] * gs_ref[...]).astype(BF16)

    @pl.when((n_chunks == 0) & (f == nf - 1))
    def _():
        y_ref[...] = jnp.zeros_like(y_ref)


def _moe_ffn(xs, gs, w1, w3, w2, tile_expert, tile_chunks, n_ffn_tiles, tf):
    d = xs.shape[1]
    ff = w1.shape[2]
    nf = ff // tf

    def fidx(t, f, nc):
        return jnp.where(nc[t] > 0, f, nf - 1)

    grid_spec = pltpu.PrefetchScalarGridSpec(
        num_scalar_prefetch=2,
        grid=(n_ffn_tiles, nf),
        in_specs=[
            pl.BlockSpec((FFN_TILE, d), lambda t, f, te, nu: (t, 0)),
            pl.BlockSpec((FFN_TILE, 1), lambda t, f, te, nu: (t, 0)),
            pl.BlockSpec((1, d, tf), lambda t, f, te, nu: (te[t], 0, fidx(t, f, nu))),
            pl.BlockSpec((1, d, tf), lambda t, f, te, nu: (te[t], 0, fidx(t, f, nu))),
            pl.BlockSpec((1, tf, d), lambda t, f, te, nu: (te[t], fidx(t, f, nu), 0)),
        ],
        out_specs=pl.BlockSpec((FFN_TILE, d), lambda t, f, te, nu: (t, 0)),
        scratch_shapes=[pltpu.VMEM((FFN_TILE, d), F32)],
    )
    return pl.pallas_call(
        _moe_ffn_kernel,
        grid_spec=grid_spec,
        out_shape=jax.ShapeDtypeStruct(xs.shape, BF16),
        compiler_params=_cparams("arbitrary", "arbitrary"),
        name="moe_ffn",
    )(tile_expert, tile_chunks, xs, gs, w1, w3, w2)


def _moe_combine_kernel(win_ref, lo_ref, hi_ref, xp_ref, xs_ref, d_ref, g_ref, *rest):
    ys_refs, tail_refs = rest[:N_EXPERTS], rest[N_EXPERTS:2 * N_EXPERTS]
    yp_ref, ysm_ref, acc_ref = rest[2 * N_EXPERTS:]
    j = pl.program_id(0)
    n_prompt_tiles = pl.num_programs(0) - 1

    @pl.when(j < n_prompt_tiles)
    def _():
        acc_ref[...] = xp_ref[...]

    @pl.when(j == n_prompt_tiles)
    def _():
        acc_ref[...] = xs_ref[...]

    for e in range(N_EXPERTS):
        lo, hi = lo_ref[j * N_EXPERTS + e], hi_ref[j * N_EXPERTS + e]

        @pl.when(hi > lo)
        def _():
            d = d_ref[...]
            d = jnp.where((d >= lo) & (d < hi), d, -1)
            start = win_ref[j * N_EXPERTS + e] * COMBINE_ALIGN

            def gathered(ref, first):
                cols = lax.broadcasted_iota(I32, (ROUTE_TILE, ref.shape[0]), 1) + first
                onehot = jnp.where((d[:, 0:1] == cols) | (d[:, 1:2] == cols), 1.0, 0.0).astype(BF16)
                return jnp.dot(onehot, ref[...], preferred_element_type=F32)

            acc_ref[...] += gathered(ys_refs[e], start)

            @pl.when(hi > start + ROUTE_TILE)
            def _():
                acc_ref[...] += gathered(tail_refs[e], start + ROUTE_TILE)

    y = _rms(acc_ref[...], g_ref[...])

    @pl.when(j < n_prompt_tiles)
    def _():
        yp_ref[...] = y

    @pl.when(j == n_prompt_tiles)
    def _():
        ysm_ref[...] = y


def _moe_combine(x2_p, x2_s, dest, ys, g_final, clist):
    win, lo, hi = clist
    mp, d = x2_p.shape
    n_prompt_tiles = mp // ROUTE_TILE

    def ptile(j, *_):
        return (jnp.minimum(j, n_prompt_tiles - 1), 0)

    def window(e, rows, offset):
        return pl.BlockSpec((pl.Element(rows), pl.Element(d)),
                            lambda j, win, lo, hi: ((win[j * N_EXPERTS + e] + offset // COMBINE_ALIGN)
                                                    * COMBINE_ALIGN, 0))

    grid_spec = pltpu.PrefetchScalarGridSpec(
        num_scalar_prefetch=3,
        grid=(n_prompt_tiles + 1,),
        in_specs=[
            pl.BlockSpec((ROUTE_TILE, d), ptile),
            pl.BlockSpec((ROUTE_TILE, d), lambda j, *_: (0, 0)),
            pl.BlockSpec((ROUTE_TILE, 2), lambda j, *_: (j, 0)),
            pl.BlockSpec((1, d), lambda j, *_: (0, 0)),
        ] + [window(e, ROUTE_TILE, 0) for e in range(N_EXPERTS)]
          + [window(e, COMBINE_ALIGN, ROUTE_TILE) for e in range(N_EXPERTS)],
        out_specs=[pl.BlockSpec((ROUTE_TILE, d), ptile), pl.BlockSpec((ROUTE_TILE, d), lambda j, *_: (0, 0))],
        scratch_shapes=[pltpu.VMEM((ROUTE_TILE, d), F32)],
    )
    return pl.pallas_call(
        _moe_combine_kernel,
        grid_spec=grid_spec,
        out_shape=[jax.ShapeDtypeStruct((mp, d), F32), jax.ShapeDtypeStruct((ROUTE_TILE, d), F32)],
        compiler_params=_cparams("arbitrary"),
        name="moe_combine",
    )(win, lo, hi, x2_p, x2_s, dest, g_final.reshape(1, d), *([ys] * (2 * N_EXPERTS)))


def _moe_block(x2_p, hn_p, ids_p, gates_p, rank_p, x2_s, hn_s, ids_s, gates_s, rank_s, cnt_tile, w1, w3, w2,
               g_final, tf=512):
    ids = jnp.concatenate([ids_p, ids_s])
    gates = jnp.concatenate([gates_p, gates_s])
    rank = jnp.concatenate([rank_p, rank_s])
    dest, tile_expert, tile_chunks, glist, clist, n_ffn_tiles = _moe_plan(ids, rank, cnt_tile)
    n_sub = n_ffn_tiles * (FFN_TILE // ROUTE_TILE)
    spare = ((0, 0), (0, GATHER_TOKENS - ROUTE_TILE))
    xs, gs = _moe_gather(hn_p, hn_s, jnp.pad(dest.T, spare, constant_values=-1), jnp.pad(gates.T, spare), glist,
                         n_sub)
    ys = _moe_ffn(xs, gs, w1, w3, w2, tile_expert, tile_chunks, n_ffn_tiles, tf)
    return _moe_combine(x2_p, x2_s, dest, ys, g_final, clist)


def _rms(x, g):
    return x * lax.rsqrt(jnp.mean(x * x, axis=-1, keepdims=True) + EPS) * g


def _mm(a, b, precise):
    if precise:
        return jnp.dot(a.astype(F32), b.astype(F32), preferred_element_type=F32, precision=HIGHEST)
    return jnp.dot(a.astype(BF16), b.astype(BF16), preferred_element_type=F32)


def _silu(x):
    return x * jax.nn.sigmoid(x)


def _norm_proj_kernel(x_ref, g_ref, w_ref, wt_ref, *out_refs, segs, tsegs, precise):
    xn = _rms(x_ref[...], g_ref[...])
    xn = xn if precise else xn.astype(BF16)
    for (start, width), o_ref in zip(segs, out_refs[:len(segs)]):
        for c in range(0, width, 512):
            cw = min(512, width - c)
            o_ref[:, c:c + cw] = _mm(xn, w_ref[:, start + c:start + c + cw], precise)
    for (start, height), o_ref in zip(tsegs, out_refs[len(segs):]):
        res = lax.dot_general(wt_ref[start:start + height, :].astype(xn.dtype), xn, (((1,), (1,)), ((), ())),
                              preferred_element_type=F32, precision=HIGHEST if precise else None)
        o_ref[...] = res.reshape(o_ref.shape)


def _norm_proj(x, g, w, wt, segs, tsegs, tm, rows_per_seq=None, precise=False):
    m, d = x.shape
    n = w.shape[1]
    nt = wt.shape[0]
    out_shape = [jax.ShapeDtypeStruct((m, width), F32) for _, width in segs]
    out_specs = [pl.BlockSpec((tm, width), lambda i: (i, 0)) for _, width in segs]
    for _, height, per_seq in tsegs:
        if per_seq:
            tps = rows_per_seq // tm
            out_shape.append(jax.ShapeDtypeStruct((m // rows_per_seq, height, rows_per_seq), F32))
            out_specs.append(pl.BlockSpec((1, height, tm), lambda i, tps=tps: (i // tps, 0, i % tps)))
        else:
            out_shape.append(jax.ShapeDtypeStruct((height, m), F32))
            out_specs.append(pl.BlockSpec((height, tm), lambda i: (0, i)))
    return pl.pallas_call(
        functools.partial(_norm_proj_kernel, segs=tuple(segs), tsegs=tuple(t[:2] for t in tsegs), precise=precise),
        grid=(m // tm,),
        in_specs=[
            pl.BlockSpec((tm, d), lambda i: (i, 0)),
            pl.BlockSpec((1, d), lambda i: (0, 0)),
            pl.BlockSpec((d, n), lambda i: (0, 0)),
            pl.BlockSpec((nt, d), lambda i: (0, 0)),
        ],
        out_specs=out_specs,
        out_shape=out_shape,
        compiler_params=_cparams("arbitrary"),
        name="norm_proj",
    )(x, g.reshape(1, d), w, wt)


def _mix_ffn_prologue(a_ref, b_ref, x_ref, wo_ref, g_ref, o_ref, hn_ref, precise):
    wa = a_ref.shape[1]

    @pl.when(pl.program_id(1) == 0)
    def _():
        x1 = (x_ref[...] + _mm(a_ref[...], wo_ref[0:wa, :], precise)
              + _mm(b_ref[...], wo_ref[wa:, :], precise))
        o_ref[...] = x1
        hn_ref[...] = _rms(x1, g_ref[...]).astype(hn_ref.dtype)


def _mix_ffn_step(w1_ref, w3_ref, w2_ref, o_ref, hn_ref, precise):
    wdt = F32 if precise else BF16
    w1, w3, w2 = w1_ref[...].astype(wdt), w3_ref[...].astype(wdt), w2_ref[...].astype(wdt)
    hn = hn_ref[...]
    o_ref[...] += _mm(_silu(_mm(hn, w1, precise)) * _mm(hn, w3, precise), w2, precise)


def _mix_ffn_kernel(a_ref, b_ref, x_ref, wo_ref, g_ref, w1_ref, w3_ref, w2_ref, o_ref, hn_ref, *, precise):
    _mix_ffn_prologue(a_ref, b_ref, x_ref, wo_ref, g_ref, o_ref, hn_ref, precise)
    _mix_ffn_step(w1_ref, w3_ref, w2_ref, o_ref, hn_ref, precise)


WINDOW_HEAD_BLOCKS = 2
WINDOW_STEPS = 4
WINDOW_RING = 3
N_MIX_FFN_IN, N_WINDOW_IN, N_WINDOW_OUT = 8, 7, 3


def _mix_ffn_window_kernel(*refs, precise):
    (a_ref, b_ref, x_ref, wo_ref, g_ref, w1_ref, w3_ref, w2_ref), refs = refs[:N_MIX_FFN_IN], refs[N_MIX_FFN_IN:]
    w_ins, refs = refs[:N_WINDOW_IN], refs[N_WINDOW_IN:]
    o_ref, w_outs, hn_ref = refs[0], refs[1:1 + N_WINDOW_OUT], refs[1 + N_WINDOW_OUT]
    kbuf_ref, vbuf_ref, sem = refs[2 + N_WINDOW_OUT:]
    q_ref, kn_ref, vn_ref, tab_ref, tabn_ref, kt_hbm, vt_hbm = w_ins
    i, f = pl.program_id(0), pl.program_id(1)
    heads = A_HEADS // WINDOW_HEAD_BLOCKS
    n_slots = pl.num_programs(0) * WINDOW_STEPS
    _mix_ffn_prologue(a_ref, b_ref, x_ref, wo_ref, g_ref, o_ref, hn_ref, precise)

    def window_copies(slot):
        seq, blk, buf = slot // WINDOW_HEAD_BLOCKS, slot % WINDOW_HEAD_BLOCKS, slot % WINDOW_RING
        src = lambda c: c.at[pl.ds(seq, 1), pl.ds(blk * heads, heads)]
        return (pltpu.make_async_copy(src(kt_hbm), kbuf_ref.at[buf], sem.at[buf, 0]),
                pltpu.make_async_copy(src(vt_hbm), vbuf_ref.at[buf], sem.at[buf, 1]))

    @pl.when((i == 0) & (f == 0))
    def _():
        for s in range(WINDOW_RING - 1):
            for c in window_copies(s):
                c.start()

    @pl.when(f < WINDOW_STEPS)
    def _():
        slot = i * WINDOW_STEPS + f

        @pl.when(slot + WINDOW_RING - 1 < n_slots)
        def _():
            for c in window_copies(slot + WINDOW_RING - 1):
                c.start()

        for c in window_copies(slot):
            c.wait()
        buf = slot % WINDOW_RING
        _sample_window_kernel(q_ref, kn_ref, vn_ref, tab_ref, tabn_ref, kbuf_ref.at[buf], vbuf_ref.at[buf], *w_outs,
                              head0=(f % WINDOW_HEAD_BLOCKS) * heads)
        _mix_ffn_step(w1_ref, w3_ref, w2_ref, o_ref, hn_ref, precise)

    @pl.when(f >= WINDOW_STEPS)
    def _():
        _mix_ffn_step(w1_ref, w3_ref, w2_ref, o_ref, hn_ref, precise)


def _mix_ffn(a, b, x, w_out, g, w1, w3, w2, tm, tf, precise=False, window=None, rel_bias=None):
    m, d = x.shape
    wa, wb = a.shape[1], b.shape[1]
    ff = w1.shape[1]
    in_specs = [
        pl.BlockSpec((tm, wa), lambda i, f: (i, 0)),
        pl.BlockSpec((tm, wb), lambda i, f: (i, 0)),
        pl.BlockSpec((tm, d), lambda i, f: (i, 0)),
        pl.BlockSpec((wa + wb, d), lambda i, f: (0, 0)),
        pl.BlockSpec((1, d), lambda i, f: (0, 0)),
        pl.BlockSpec((d, tf), lambda i, f: (0, f)),
        pl.BlockSpec((d, tf), lambda i, f: (0, f)),
        pl.BlockSpec((tf, d), lambda i, f: (f, 0)),
    ]
    out_specs = [pl.BlockSpec((tm, d), lambda i, f: (i, 0))]
    out_shape = [jax.ShapeDtypeStruct((m, d), F32)]
    operands = [a, b, x, w_out, g.reshape(1, d), w1, w3, w2]
    scratch = [pltpu.VMEM((tm, d), F32 if precise else BF16)]
    body = _mix_ffn_kernel
    if window is not None:
        assert ff // tf >= WINDOW_STEPS and window[0].shape[0] * WINDOW_HEAD_BLOCKS == (m // tm) * WINDOW_STEPS

        def block_of_step(i, f):
            p = i * WINDOW_STEPS + jnp.minimum(f, WINDOW_STEPS - 1)
            return p // WINDOW_HEAD_BLOCKS, p % WINDOW_HEAD_BLOCKS

        w_operands, w_in, w_out_specs, w_shape = _sample_window_specs(*window, rel_bias, block_of_step,
                                                                      A_HEADS // WINDOW_HEAD_BLOCKS, manual_inputs=True)
        ring = (WINDOW_RING, 1, A_HEADS // WINDOW_HEAD_BLOCKS, A_HD, window[3].shape[3])
        scratch += [pltpu.VMEM(ring, F32), pltpu.VMEM(ring, F32), pltpu.SemaphoreType.DMA((WINDOW_RING, 2))]
        in_specs, out_specs, out_shape = in_specs + w_in, out_specs + w_out_specs, out_shape + w_shape
        operands, body = operands + w_operands, _mix_ffn_window_kernel
    res = pl.pallas_call(
        functools.partial(body, precise=precise),
        grid=(m // tm, ff // tf),
        in_specs=in_specs,
        out_specs=out_specs,
        out_shape=out_shape,
        scratch_shapes=scratch,
        compiler_params=_cparams("arbitrary", "arbitrary"),
        name="mix_ffn",
    )(*operands)
    return res[0] if window is None else res


def _mix_router_kernel(a_ref, b_ref, x_ref, wo_ref, g_ref, wr_ref, br_ref, cnt0_ref, x2_ref, hn_ref, ids_ref,
                       gate_ref, rank_ref, cb_ref, ca_ref, cnt_ref, *, precise):
    @pl.when(pl.program_id(0) == 0)
    def _():
        cnt_ref[...] = cnt0_ref[...]

    wa = a_ref.shape[1]
    x2 = (x_ref[...] + _mm(a_ref[...], wo_ref[0:wa, :], precise)
          + _mm(b_ref[...], wo_ref[wa:, :], precise))
    x2_ref[...] = x2
    hn = _rms(x2, g_ref[...])
    hn_ref[...] = hn.astype(BF16)
    wr = wr_ref[...]
    if precise:
        logits = jnp.dot(hn, wr, preferred_element_type=F32, precision=HIGHEST)
    else:
        hn_hi, wr_hi = hn.astype(BF16), wr.astype(BF16)
        hn_lo, wr_lo = (hn - hn_hi.astype(F32)).astype(BF16), (wr - wr_hi.astype(F32)).astype(BF16)
        logits = (jnp.dot(hn_hi, wr_hi, preferred_element_type=F32)
                  + (jnp.dot(hn_lo, wr_hi, preferred_element_type=F32)
                     + jnp.dot(hn_hi, wr_lo, preferred_element_type=F32)))
    lane = lax.broadcasted_iota(I32, logits.shape, 1)
    real = lane < N_EXPERTS
    biased = jnp.where(real, logits + br_ref[...], -jnp.inf)
    m1 = jnp.max(biased, axis=-1, keepdims=True)
    i1 = jnp.min(jnp.where(biased == m1, lane, 128), axis=-1, keepdims=True)
    rest = jnp.where(lane == i1, -jnp.inf, biased)
    m2 = jnp.max(rest, axis=-1, keepdims=True)
    i2 = jnp.min(jnp.where(rest == m2, lane, 128), axis=-1, keepdims=True)
    l1 = jnp.sum(jnp.where(lane == i1, logits, 0.0), axis=-1, keepdims=True)
    l2 = jnp.sum(jnp.where(lane == i2, logits, 0.0), axis=-1, keepdims=True)
    mx = jnp.maximum(l1, l2)
    e1, e2 = jnp.exp(l1 - mx), jnp.exp(l2 - mx)
    two = lax.broadcasted_iota(I32, ids_ref.shape, 1)
    ids_ref[...] = jnp.where(two == 0, i1, i2)
    gate_ref[...] = jnp.where(two == 0, e1, e2) / (e1 + e2)
    tm = logits.shape[0]
    chosen = jnp.where((lane == i1) | (lane == i2), 1.0, 0.0)
    earlier = (lax.broadcasted_iota(I32, (tm, tm), 0) > lax.broadcasted_iota(I32, (tm, tm), 1))
    before = cnt_ref[...] + jnp.dot(jnp.where(earlier, 1.0, 0.0).astype(BF16), chosen.astype(BF16),
                                    preferred_element_type=F32)
    r1 = jnp.sum(jnp.where(lane == i1, before, 0.0), axis=-1, keepdims=True)
    r2 = jnp.sum(jnp.where(lane == i2, before, 0.0), axis=-1, keepdims=True)
    rank_ref[...] = jnp.where(two == 0, r1, r2).astype(I32)
    for c in range(cb_ref.shape[1]):
        cb_ref[0, c:c + 1, :] = before[c * ROUTE_TILE:c * ROUTE_TILE + 1, :]
    cnt_ref[...] += jnp.sum(chosen, axis=0, keepdims=True)
    ca_ref[...] = cnt_ref[...]


def _mix_router(a, b, x, w_out, g, wr, br, cnt0, tm, precise=False):
    m, d = x.shape
    wa, wb = a.shape[1], b.shape[1]
    n_marks = max(1, tm // ROUTE_TILE)
    return pl.pallas_call(
        functools.partial(_mix_router_kernel, precise=precise),
        grid=(m // tm,),
        in_specs=[
            pl.BlockSpec((tm, wa), lambda i: (i, 0)),
            pl.BlockSpec((tm, wb), lambda i: (i, 0)),
            pl.BlockSpec((tm, d), lambda i: (i, 0)),
            pl.BlockSpec((wa + wb, d), lambda i: (0, 0)),
            pl.BlockSpec((1, d), lambda i: (0, 0)),
            pl.BlockSpec((d, 128), lambda i: (0, 0)),
            pl.BlockSpec((1, 128), lambda i: (0, 0)),
            pl.BlockSpec((1, 128), lambda i: (0, 0)),
        ],
        out_specs=[
            pl.BlockSpec((tm, d), lambda i: (i, 0)),
            pl.BlockSpec((tm, d), lambda i: (i, 0)),
            pl.BlockSpec((tm, 2), lambda i: (i, 0)),
            pl.BlockSpec((tm, 2), lambda i: (i, 0)),
            pl.BlockSpec((tm, 2), lambda i: (i, 0)),
            pl.BlockSpec((1, n_marks, 128), lambda i: (i, 0, 0)),
            pl.BlockSpec((1, 128), lambda i: (0, 0)),
        ],
        out_shape=[jax.ShapeDtypeStruct((m, d), F32), jax.ShapeDtypeStruct((m, d), BF16),
                   jax.ShapeDtypeStruct((m, 2), I32), jax.ShapeDtypeStruct((m, 2), F32),
                   jax.ShapeDtypeStruct((m, 2), I32), jax.ShapeDtypeStruct((m // tm, n_marks, 128), F32),
                   jax.ShapeDtypeStruct((1, 128), F32)],
        scratch_shapes=[pltpu.VMEM((1, 128), F32)],
        compiler_params=_cparams("arbitrary"),
        name="mix_router",
    )(a, b, x, w_out, g.reshape(1, d), wr, br, cnt0)


ATT_T = 512


def _t5_bucket(dist):
    max_exact = N_BUCKETS // 2
    d = jnp.maximum(dist, 1).astype(F32)
    large = max_exact + (jnp.log(d / max_exact) / math.log(A_WIN / max_exact)
                         * (N_BUCKETS - max_exact)).astype(I32)
    return jnp.where(dist < max_exact, dist, jnp.minimum(large, N_BUCKETS - 1))


def _distance_logit_table(rel_bias, max_dist):
    dist = jnp.arange(max_dist + 1, dtype=I32)
    mult = jnp.zeros((max_dist + 1,), F32)
    for window, d in A_BRANCHES:
        mult = mult + ((dist % d == 0) & (dist <= window)).astype(F32)
    bias = rel_bias[_t5_bucket(dist)].astype(F32).T
    return jnp.where(mult[None, :] > 0, bias + jnp.log(jnp.maximum(mult, 1.0))[None, :], NEG)


def _attn_kernel(q_ref, kt_ref, vt_ref, row0_ref, o_ref, kb_ref, vb_ref):
    s_len = q_ref.shape[1]
    nb = s_len // ATT_T
    period = row0_ref.shape[2]
    kb_ref[...] = kt_ref[0].astype(BF16)
    vb_ref[...] = vt_ref[0].astype(BF16)
    scale = A_HD ** -0.5
    for hl in range(2):
        hrows = slice(hl * A_HD, (hl + 1) * A_HD)
        table = pltpu.roll(jnp.broadcast_to(row0_ref[0, hl:hl + 1, :], (ATT_T, period)), 0, 1,
                           stride=1, stride_axis=0)
        for i in range(nb):
            rows = slice(i * ATT_T, (i + 1) * ATT_T)
            width = (i + 1) * ATT_T
            q = (q_ref[0, rows, hrows] * scale).astype(BF16)
            s = jnp.dot(q, kb_ref[hrows, 0:width], preferred_element_type=F32)
            s = s + table[:, (nb - 1 - i) * ATT_T:nb * ATT_T]
            m = jnp.max(s, axis=-1, keepdims=True)
            p = jnp.exp(s - m)
            den = jnp.sum(p, axis=-1, keepdims=True)
            o = lax.dot_general(p.astype(BF16), vb_ref[hrows, 0:width], (((1,), (1,)), ((), ())),
                                preferred_element_type=F32)
            o_ref[0, rows, hrows] = o / den


def _attention_prompt(q, kt, vt, rel_bias):
    b, s_len, _ = q.shape
    nb = s_len // ATT_T
    tab1 = _distance_logit_table(rel_bias, s_len)
    period = s_len + ATT_T
    z = jnp.arange(period, dtype=I32)
    delta = jnp.where(z < s_len, (nb - 1) * ATT_T - z, (nb - 1) * ATT_T + period - z)
    row0 = jnp.where(delta[None] >= 0, tab1[:, jnp.clip(delta, 0, s_len)], NEG)
    return pl.pallas_call(
        _attn_kernel,
        grid=(b, A_HEADS // 2),
        in_specs=[pl.BlockSpec((1, s_len, 2 * A_HD), lambda bi, hp: (bi, 0, hp)),
                  pl.BlockSpec((1, 2 * A_HD, s_len), lambda bi, hp: (bi, hp, 0)),
                  pl.BlockSpec((1, 2 * A_HD, s_len), lambda bi, hp: (bi, hp, 0)),
                  pl.BlockSpec((1, 2, period), lambda bi, hp: (hp, 0, 0))],
        out_specs=pl.BlockSpec((1, s_len, 2 * A_HD), lambda bi, hp: (bi, 0, hp)),
        out_shape=jax.ShapeDtypeStruct(q.shape, F32),
        scratch_shapes=[pltpu.VMEM((2 * A_HD, s_len), BF16), pltpu.VMEM((2 * A_HD, s_len), BF16)],
        compiler_params=_cparams("arbitrary", "arbitrary"),
        name="dilated_attention",
    )(q, kt, vt, row0.reshape(A_HEADS // 2, 2, period))


MLSTM_CHUNK = 256


def _log_sigmoid(x):
    return jnp.minimum(x, 0.0) - jnp.log(1.0 + jnp.exp(-jnp.abs(x)))


def _mlstm_kernel(qk_ref, v_ref, ob_ref, gc_ref, gr_ref, wc_ref, bc_ref, bifc_ref, bifr_ref, gh_ref,
                  o_ref, c_out_ref, n_out_ref, m_out_ref, xbuf_ref, c_ref, n_ref, m_ref):
    ci = pl.program_id(1)
    L = qk_ref.shape[1]

    @pl.when(ci == 0)
    def _():
        xbuf_ref[0:8, :] = jnp.zeros((8, 2 * B_W), F32)
        c_ref[...] = jnp.zeros_like(c_ref)
        n_ref[...] = jnp.zeros_like(n_ref)
        m_ref[...] = jnp.zeros_like(m_ref)

    x = qk_ref[0]
    xbuf_ref[8:8 + L, :] = x
    y = bc_ref[...] + wc_ref[3:4, :] * x
    for j in range(B_CONV - 1):
        y = y + wc_ref[j:j + 1, :] * xbuf_ref[5 + j:5 + j + L, :]
    xbuf_ref[0:8, :] = x[L - 8:L, :]
    y = _silu(y)

    gcol = gc_ref[0] + bifc_ref[...]
    grow = gr_ref[...] + bifr_ref[...]
    ri = lax.broadcasted_iota(I32, (L, L), 0)
    cj = lax.broadcasted_iota(I32, (L, L), 1)
    causal = ri >= cj
    lower = jnp.where(causal, 1.0, 0.0)
    b_col = jnp.dot(lower, _log_sigmoid(gcol), preferred_element_type=F32, precision=HIGHEST)
    b_row = lax.dot_general(_log_sigmoid(grow), lower, (((1,), (1,)), ((), ())),
                            preferred_element_type=F32, precision=HIGHEST)

    for h in range(B_HEADS):
        cols = slice(h * B_HD, (h + 1) * B_HD)
        qf = y[:, cols]
        q = qf.astype(BF16)
        kf = y[:, B_W + h * B_HD:B_W + (h + 1) * B_HD] * (B_HD ** -0.5)
        k = kf.astype(BF16)
        v = v_ref[0, :, cols].astype(BF16)
        bc = b_col[:, B_HEADS + h:B_HEADS + h + 1]
        ic = gcol[:, h:h + 1]
        br = b_row[B_HEADS + h:B_HEADS + h + 1, :]
        ir = grow[h:h + 1, :]
        m_prev = m_ref[h:h + 1, 0:1]
        dm = jnp.where(causal, bc - br + ir, NEG)
        inter = bc + m_prev
        mt = jnp.maximum(inter, jnp.max(dm, axis=1, keepdims=True))
        w_intra = jnp.exp(dm - mt)
        w_inter = jnp.exp(inter - mt)
        a = w_intra * lax.dot_general(q, k, (((1,), (1,)), ((), ())), preferred_element_type=F32)
        c_prev = c_ref[h]
        n_prev = n_ref[h:h + 1, :]
        num = (jnp.dot(a.astype(BF16), v, preferred_element_type=F32)
               + w_inter * jnp.dot(q, c_prev.astype(BF16), preferred_element_type=F32))
        den = (jnp.sum(a, axis=1, keepdims=True)
               + w_inter * jnp.sum(qf * n_prev, axis=1, keepdims=True))
        hb = num / jnp.maximum(jnp.abs(den), jnp.exp(-mt))
        m_new = mt[L - 1:L, :]
        b_last = bc[L - 1:L, :]
        g = jnp.exp(b_last - bc + ic - m_new)
        decay = jnp.exp(b_last + m_prev - m_new)
        kg = kf * g
        c_ref[h] = decay * c_prev + lax.dot_general(kg.astype(BF16), v, (((0,), (0,)), ((), ())),
                                                    preferred_element_type=F32)
        n_ref[h:h + 1, :] = decay * n_prev + jnp.sum(kg, axis=0, keepdims=True)
        m_ref[h:h + 1, :] = jnp.broadcast_to(m_new, (1, 128))
        hb = hb * lax.rsqrt(jnp.mean(hb * hb, axis=-1, keepdims=True) + EPS) * gh_ref[:, cols]
        o_ref[0, :, cols] = jax.nn.sigmoid(ob_ref[0, :, cols]) * hb

    @pl.when(ci == pl.num_programs(1) - 1)
    def _():
        c_out_ref[0] = c_ref[...]
        n_out_ref[0] = n_ref[...]
        m_out_ref[0] = m_ref[...]


def _mlstm_prompt(qk, v, ob, gcol, grow, w_conv, b_conv, b_if, g_head):
    b, s_len, _ = qk.shape
    L = min(MLSTM_CHUNK, s_len)
    nc = s_len // L
    bif_col = jnp.zeros((1, 128), F32).at[0, :2 * B_HEADS].set(b_if)
    bif_row = b_if.reshape(2 * B_HEADS, 1)
    seq = lambda w: pl.BlockSpec((1, L, w), lambda bi, ci: (bi, ci, 0))
    full = lambda shape: pl.BlockSpec(shape, lambda bi, ci: (0,) * len(shape))
    return pl.pallas_call(
        _mlstm_kernel,
        grid=(b, nc),
        in_specs=[seq(2 * B_W), seq(B_W), seq(B_W), seq(128),
                  pl.BlockSpec((2 * B_HEADS, L), lambda bi, ci: (0, bi * nc + ci)),
                  full((B_CONV, 2 * B_W)), full((1, 2 * B_W)), full((1, 128)), full((2 * B_HEADS, 1)),
                  full((1, B_W))],
        out_specs=[seq(B_W),
                   pl.BlockSpec((1, B_HEADS, B_HD, B_HD), lambda bi, ci: (bi, 0, 0, 0)),
                   pl.BlockSpec((1, 8, B_HD), lambda bi, ci: (bi, 0, 0)),
                   pl.BlockSpec((1, 8, 128), lambda bi, ci: (bi, 0, 0))],
        out_shape=[jax.ShapeDtypeStruct((b, s_len, B_W), F32),
                   jax.ShapeDtypeStruct((b, B_HEADS, B_HD, B_HD), F32),
                   jax.ShapeDtypeStruct((b, 8, B_HD), F32),
                   jax.ShapeDtypeStruct((b, 8, 128), F32)],
        scratch_shapes=[pltpu.VMEM((L + 8, 2 * B_W), F32), pltpu.VMEM((B_HEADS, B_HD, B_HD), F32),
                        pltpu.VMEM((8, B_HD), F32), pltpu.VMEM((8, 128), F32)],
        compiler_params=_cparams("arbitrary", "arbitrary"),
        name="mlstm_chunkwise",
    )(qk, v, ob, gcol, grow, w_conv, b_conv.reshape(1, -1), bif_col, bif_row, g_head.reshape(1, -1))


def _odd_in_kernel(x_ref, g_ref, w_ref, gcv_ref, bcv_ref, ws_ref, bs_ref, c_ref, xd_ref):
    tm = x_ref.shape[0]
    xn = _rms(x_ref[...], g_ref[...]).astype(BF16)
    u = jnp.dot(xn, w_ref[:, 0:C_W], preferred_element_type=F32)
    v = jnp.dot(xn, w_ref[:, C_W:2 * C_W], preferred_element_type=F32)
    xd_ref[...] = jnp.dot(xn, w_ref[:, 2 * C_W:], preferred_element_type=F32)
    mu = jnp.mean(v, axis=-1, keepdims=True)
    vc = v - mu
    var = jnp.mean(vc * vc, axis=-1, keepdims=True)
    vl = (vc * lax.rsqrt(var + EPS) * gcv_ref[...] + bcv_ref[...]).astype(BF16)
    ri = lax.broadcasted_iota(I32, (C_CHUNK, C_CHUNK), 0)
    cj = lax.broadcasted_iota(I32, (C_CHUNK, C_CHUNK), 1)
    for g in range(C_GROUPS):
        cols = slice(g * C_GD, (g + 1) * C_GD)
        wg = jnp.where(ri >= cj, ws_ref[g], 0.0).astype(BF16)
        for c in range(tm // C_CHUNK):
            rows = slice(c * C_CHUNK, (c + 1) * C_CHUNK)
            s = jnp.dot(wg, vl[rows, cols], preferred_element_type=F32) + bs_ref[:, g:g + 1]
            c_ref[rows, cols] = u[rows, cols] * s


def _odd_in(x, g, w, g_cv, b_cv, w_s, b_s, tm):
    m, d = x.shape
    n = w.shape[1]
    full = lambda shape: pl.BlockSpec(shape, lambda i: (0,) * len(shape))
    return pl.pallas_call(
        _odd_in_kernel,
        grid=(m // tm,),
        in_specs=[pl.BlockSpec((tm, d), lambda i: (i, 0)), full((1, d)), full((d, n)), full((1, C_W)),
                  full((1, C_W)), full((C_GROUPS, C_CHUNK, C_CHUNK)), full((C_CHUNK, C_GROUPS))],
        out_specs=[pl.BlockSpec((tm, C_W), lambda i: (i, 0)), pl.BlockSpec((tm, D_W), lambda i: (i, 0))],
        out_shape=[jax.ShapeDtypeStruct((m, C_W), F32), jax.ShapeDtypeStruct((m, D_W), F32)],
        compiler_params=_cparams("arbitrary"),
        name="odd_in_proj_gate",
    )(x, g.reshape(1, d), w, g_cv.reshape(1, -1), b_cv.reshape(1, -1), w_s, b_s.T)


S5_TT = 128
S5_PAD = 8
S5_HALF = D_GROUPS * D_STATE // 2


def _s5_discretize(a_re, a_im, log_dt, bm_re, bm_im, cm_re, cm_im):
    a_re, a_im = a_re.astype(F32), a_im.astype(F32)
    dt = jnp.exp(log_dt.astype(F32))[:, None]
    mag = jnp.exp(a_re * dt)
    ab_re = mag * jnp.cos(a_im * dt)
    ab_im = mag * jnp.sin(a_im * dt)
    inv = 1.0 / (a_re * a_re + a_im * a_im)
    f_re = ((ab_re - 1.0) * a_re + ab_im * a_im) * inv
    f_im = (ab_im * a_re - (ab_re - 1.0) * a_im) * inv
    bm_re, bm_im = bm_re.astype(F32), bm_im.astype(F32)
    bb_re = f_re[..., None] * bm_re - f_im[..., None] * bm_im
    bb_im = f_re[..., None] * bm_im + f_im[..., None] * bm_re
    return ab_re, ab_im, bb_re, bb_im


def _s5_matrices(bb_re, bb_im, cm_re, cm_im):
    gh = D_GROUPS // 2
    eye = jnp.eye(gh, dtype=F32)

    def in_half(bb):
        return jnp.einsum('gpc,gh->gchp', bb, eye).reshape(gh * D_GCH, gh * D_STATE)

    def out_half(cm):
        return jnp.einsum('gcp,gh->gphc', cm, eye).reshape(gh * D_STATE, gh * D_GCH)

    bd = jnp.stack([jnp.concatenate([in_half(bb_re[h * gh:(h + 1) * gh]), in_half(bb_im[h * gh:(h + 1) * gh])], axis=1)
                    for h in range(2)])
    cm = jnp.stack([jnp.concatenate([out_half(cm_re[h * gh:(h + 1) * gh].astype(F32)),
                                     -out_half(cm_im[h * gh:(h + 1) * gh].astype(F32))], axis=0)
                    for h in range(2)])
    return bd, cm


def _gelu_tanh(x):
    return 0.5 * x * (1.0 + jnp.tanh(math.sqrt(2.0 / math.pi) * (x + 0.044715 * (x * x * x))))


def _s5_kernel(x_ref, bd_ref, cm_ref, ar_ref, ai_ref, dsk_ref, wg_ref, bg_ref, o_ref, hr_out, hi_out,
               bu_ref, hs_ref, hr_ref, hi_ref):
    i = pl.program_id(0)
    nb, tt, _ = x_ref.shape
    stride = tt + S5_PAD
    half_in = D_W // 2
    nct = S5_HALF // 128

    @pl.when(i == 0)
    def _():
        hr_ref[...] = jnp.zeros_like(hr_ref)
        hi_ref[...] = jnp.zeros_like(hi_ref)

    x = x_ref[...].reshape(nb * tt, D_W)
    ys = []
    for h in range(2):
        lanes = slice(h * S5_HALF, (h + 1) * S5_HALF)
        bu = jnp.dot(x[:, h * half_in:(h + 1) * half_in].astype(BF16), bd_ref[h], preferred_element_type=F32)
        for c in range(2 * nct):
            for b in range(nb):
                bu_ref[c, b * stride:b * stride + tt, :] = bu[b * tt:(b + 1) * tt, c * 128:(c + 1) * 128]
        ar = jnp.broadcast_to(ar_ref[:, lanes], (nb, S5_HALF))
        ai = jnp.broadcast_to(ai_ref[:, lanes], (nb, S5_HALF))

        def step(t, carry):
            hr, hi = carry
            rows = pl.ds(t, nb, stride=stride)
            bur = jnp.concatenate([bu_ref[c, rows, :] for c in range(nct)], axis=1)
            bui = jnp.concatenate([bu_ref[nct + c, rows, :] for c in range(nct)], axis=1)
            nhr = ar * hr - ai * hi + bur
            nhi = ar * hi + ai * hr + bui
            for c in range(nct):
                hs_ref[c, rows, :] = nhr[:, c * 128:(c + 1) * 128]
                hs_ref[nct + c, rows, :] = nhi[:, c * 128:(c + 1) * 128]
            return nhr, nhi

        hr, hi = lax.fori_loop(0, tt, step, (hr_ref[:, lanes], hi_ref[:, lanes]), unroll=2)
        hr_ref[:, lanes] = hr
        hi_ref[:, lanes] = hi
        hs = jnp.concatenate(
            [jnp.concatenate([hs_ref[c, b * stride:b * stride + tt, :] for b in range(nb)], axis=0)
             for c in range(2 * nct)], axis=1)
        ys.append(jnp.dot(hs.astype(BF16), cm_ref[h], preferred_element_type=F32))
    y = jnp.concatenate(ys, axis=-1) + dsk_ref[...] * x
    g = _gelu_tanh(y)
    d_out = g * jax.nn.sigmoid(jnp.dot(g.astype(BF16), wg_ref[...], preferred_element_type=F32) + bg_ref[...])
    o_ref[...] = d_out.reshape(nb, tt, D_W)

    @pl.when(i == pl.num_programs(0) - 1)
    def _():
        hr_out[...] = hr_ref[...]
        hi_out[...] = hi_ref[...]


def _s5_prompt(xd, bd, cm, ab_re, ab_im, d_skip, w_glu, b_glu):
    nb, t_len, _ = xd.shape
    tt = min(S5_TT, t_len)
    n_state = D_GROUPS * D_STATE
    full = lambda shape: pl.BlockSpec(shape, lambda i: (0,) * len(shape))
    return pl.pallas_call(
        _s5_kernel,
        grid=(t_len // tt,),
        in_specs=[pl.BlockSpec((nb, tt, D_W), lambda i: (0, i, 0)), full(bd.shape), full(cm.shape),
                  full((1, n_state)), full((1, n_state)), full((1, D_W)), full((D_W, D_W)), full((1, D_W))],
        out_specs=[pl.BlockSpec((nb, tt, D_W), lambda i: (0, i, 0)), full((nb, n_state)), full((nb, n_state))],
        out_shape=[jax.ShapeDtypeStruct(xd.shape, F32), jax.ShapeDtypeStruct((nb, n_state), F32),
                   jax.ShapeDtypeStruct((nb, n_state), F32)],
        scratch_shapes=[pltpu.VMEM((2 * S5_HALF // 128, nb * (tt + S5_PAD), 128), F32),
                        pltpu.VMEM((2 * S5_HALF // 128, nb * (tt + S5_PAD), 128), F32),
                        pltpu.VMEM((nb, n_state), F32), pltpu.VMEM((nb, n_state), F32)],
        compiler_params=_cparams("arbitrary"),
        name="s5_scan_glu",
    )(xd, bd, cm, ab_re.reshape(1, n_state), ab_im.reshape(1, n_state), d_skip.reshape(1, D_W), w_glu,
      b_glu.reshape(1, D_W))


def _hdot(a, b):
    return jnp.dot(a, b, precision=HIGHEST, preferred_element_type=F32)


def _sample_window_kernel(q_ref, kn_ref, vn_ref, tab_ref, tabn_ref, kt_ref, vt_ref, o_ref, okt_ref, ovt_ref,
                          head0=0):
    L = kt_ref.shape[3]
    last = lax.broadcasted_iota(I32, (A_HD, L), 1) == L - 1
    for h in range(kt_ref.shape[1]):
        kh, vh = kt_ref[0, h], vt_ref[0, h]
        kn, vn = kn_ref[0, h], vn_ref[0, h]
        q = q_ref[0, h] * (A_HD ** -0.5)
        s = jnp.sum(kh * q, axis=0, keepdims=True) + tab_ref[head0 + h]
        s_new = jnp.sum(kn * q, axis=0, keepdims=True) + tabn_ref[head0 + h]
        m = jnp.maximum(jnp.max(s, axis=1, keepdims=True), s_new)
        p = jnp.exp(s - m)
        p_new = jnp.exp(s_new - m)
        den = jnp.sum(p, axis=1, keepdims=True) + p_new
        o_ref[0, h] = (jnp.sum(vh * p, axis=1, keepdims=True) + vn * p_new) / den
        okt_ref[0, h] = jnp.where(last, kn, pltpu.roll(kh, L - 1, 1))
        ovt_ref[0, h] = jnp.where(last, vn, pltpu.roll(vh, L - 1, 1))


def _sample_window_attention(q, k_new, v_new, cache_kt, cache_vt, rel_bias):
    operands, in_specs, out_specs, out_shape = _sample_window_specs(q, k_new, v_new, cache_kt, cache_vt, rel_bias,
                                                                    lambda i: (i, 0))
    return pl.pallas_call(
        _sample_window_kernel,
        grid=(q.shape[0],),
        in_specs=in_specs,
        out_specs=out_specs,
        out_shape=out_shape,
        compiler_params=_cparams("arbitrary"),
        name="sample_window_attention",
    )(*operands)


def _sample_window_specs(q, k_new, v_new, cache_kt, cache_vt, rel_bias, block_of_step, heads_per_step=A_HEADS,
                         manual_inputs=False):
    n, _, _, L = cache_kt.shape
    assert L == A_WIN
    tab1 = _distance_logit_table(rel_bias, L)
    tab = tab1[:, :0:-1].reshape(A_HEADS, 1, L)
    tab_new = tab1[:, 0].reshape(A_HEADS, 1, 1)
    col = pl.BlockSpec((1, heads_per_step, A_HD, 1), lambda *g: (*block_of_step(*g), 0, 0))
    win = pl.BlockSpec((1, heads_per_step, A_HD, L), lambda *g: (*block_of_step(*g), 0, 0))
    win_in = pl.BlockSpec(memory_space=pl.ANY) if manual_inputs else win
    full = lambda shape: pl.BlockSpec(shape, lambda *g: (0,) * len(shape))
    out_shape = [jax.ShapeDtypeStruct((n, A_HEADS, A_HD, 1), F32),
                 jax.ShapeDtypeStruct(cache_kt.shape, F32), jax.ShapeDtypeStruct(cache_vt.shape, F32)]
    return ([q, k_new, v_new, tab, tab_new, cache_kt, cache_vt],
            [col, col, col, full(tab.shape), full(tab_new.shape), win_in, win_in], [col, win, win], out_shape)


def _sample_even_proj(x, g_mix, w_in):
    n_main = 3 * A_W + 4 * B_W
    segs = ((0, A_W), (A_W, A_W), (2 * A_W, A_W), (3 * A_W, 2 * B_W), (3 * A_W + 2 * B_W, B_W),
            (3 * A_W + 3 * B_W, B_W), (n_main, 2 * B_HEADS))
    return _norm_proj(x, g_mix, w_in, jnp.zeros((8, x.shape[1]), F32), segs, (), tm=x.shape[0], precise=True)


def _sample_mlstm_step(qkb, vb, ob, gates, c0, n0, m0, conv0, b_if, w_conv, b_conv, g_head):
    n = qkb.shape[0]
    xp = jnp.concatenate([conv0, qkb[:, None, :]], axis=1)
    qk = _silu(b_conv + jnp.sum(xp * w_conv[None], axis=1))
    q = qk[:, :B_W].reshape(n, B_HEADS, B_HD)
    k = qk[:, B_W:].reshape(n, B_HEADS, B_HD) * (B_HD ** -0.5)
    v = vb.reshape(n, B_HEADS, B_HD)
    ig = gates[:, :B_HEADS] + b_if[:B_HEADS]
    logf = _log_sigmoid(gates[:, B_HEADS:] + b_if[B_HEADS:])
    inter = logf + m0
    mt = jnp.maximum(inter, ig)
    w_intra = jnp.exp(ig - mt)
    w_inter = jnp.exp(inter - mt)
    a = w_intra * jnp.sum(q * k, axis=-1)
    num = a[..., None] * v + w_inter[..., None] * jnp.einsum('nhk,nhkv->nhv', q, c0, precision=HIGHEST)
    den = a + w_inter * jnp.sum(q * n0, axis=-1)
    hb = num / jnp.maximum(jnp.abs(den), jnp.exp(-mt))[..., None]
    g = jnp.exp(ig - mt)
    decay = jnp.exp(inter - mt)
    c_new = decay[..., None, None] * c0 + (g[..., None] * k)[..., :, None] * v[..., None, :]
    n_new = decay[..., None] * n0 + g[..., None] * k
    hb = hb * lax.rsqrt(jnp.mean(hb * hb, axis=-1, keepdims=True) + EPS) * g_head.reshape(B_HEADS, B_HD)
    b_out = jax.nn.sigmoid(ob) * hb.reshape(n, B_W)
    return b_out, c_new, n_new, mt, xp[:, 1:]


def _sample_odd_mixers(x, h_re0, h_im0, g_mix, w_in, g_cv, b_cv, w_s, b_s, ab_re, ab_im, bb_re, bb_im,
                       cm_re, cm_im, d_skip, w_glu, b_glu):
    n = x.shape[0]
    h = _rms(x, g_mix)
    proj = _hdot(h, w_in)
    u, v, xd = proj[:, :C_W], proj[:, C_W:2 * C_W], proj[:, 2 * C_W:]
    mu = jnp.mean(v, axis=-1, keepdims=True)
    vc = v - mu
    v = vc * lax.rsqrt(jnp.mean(vc * vc, axis=-1, keepdims=True) + EPS) * g_cv + b_cv
    s = (w_s[:, 0, 0][None, :, None] * v.reshape(n, C_GROUPS, C_GD) + b_s[:, 0][None, :, None]).reshape(n, C_W)
    c_out = u * s
    xg = xd.reshape(n, D_GROUPS, D_GCH)
    bu_re = jnp.einsum('gpc,ngc->ngp', bb_re, xg, precision=HIGHEST)
    bu_im = jnp.einsum('gpc,ngc->ngp', bb_im, xg, precision=HIGHEST)
    hr = ab_re * h_re0 - ab_im * h_im0 + bu_re
    hi = ab_re * h_im0 + ab_im * h_re0 + bu_im
    y = (jnp.einsum('gcp,ngp->ngc', cm_re, hr, precision=HIGHEST)
         - jnp.einsum('gcp,ngp->ngc', cm_im, hi, precision=HIGHEST) + d_skip * xg)
    g = _gelu_tanh(y.reshape(n, D_W))
    d_out = g * jax.nn.sigmoid(_hdot(g, w_glu) + b_glu)
    return c_out, d_out, v, hr, hi


_EVEN_SEGS = ((0, A_W), (A_W, 2 * B_W), (A_W + 2 * B_W, B_W), (A_W + 3 * B_W, B_W), (A_W + 4 * B_W, 128))


def kernel(x_prompt, x_sample, cache_a_k, cache_a_v, state_b_c, state_b_n, state_b_m, state_b_conv,
           state_d_re, state_d_im, rel_bias, g_mix, g_ffn, g_final, w_in_e, b_if, w_conv_b, b_conv_b,
           g_bhead, w_out_e, w1_e, w3_e, w2_e, w_in_o, g_cv, b_cv, w_s, b_s, a_re, a_im, log_dt,
           bm_re, bm_im, cm_re, cm_im, d_skip, w_glu, b_glu, w_out_o, w_router, b_router,
           w1_m, w3_m, w2_m):
    bp, sp, d = x_prompt.shape
    ns = x_sample.shape[0]
    mp = bp * sp
    xp = x_prompt.reshape(mp, d)
    xs = x_sample.reshape(ns, d)

    w_in = w_in_e[0]
    n_main = 3 * A_W + 4 * B_W
    w_gate = w_in[:, n_main:]
    w_cat = jnp.concatenate([w_in[:, :A_W], w_in[:, 3 * A_W:n_main],
                             jnp.pad(w_gate, ((0, 0), (0, 128 - 2 * B_HEADS)))], axis=1)
    w_t = jnp.concatenate([w_in[:, A_W:3 * A_W], w_gate], axis=1).T
    q, qkb, vb, ob, gcol, kt, vt, grow = _norm_proj(
        xp, g_mix[0], w_cat.astype(BF16), w_t.astype(BF16), _EVEN_SEGS,
        ((0, A_W, True), (A_W, A_W, True), (2 * A_W, 2 * B_HEADS, False)), tm=1024, rows_per_seq=sp)
    shp = lambda a: a.reshape(bp, sp, a.shape[-1])
    pos_minor = lambda c: jnp.transpose(c, (0, 2, 3, 1))
    pos_major = lambda c: jnp.transpose(c, (0, 3, 1, 2))
    s_proj = _sample_even_proj(xs, g_mix[0], w_in)
    cols = lambda a: a.reshape(ns, A_HEADS, A_HD, 1)
    s_attn = (cols(s_proj[0]), cols(s_proj[1]), cols(s_proj[2]), pos_minor(cache_a_k[0]), pos_minor(cache_a_v[0]))
    a_out = _attention_prompt(shp(q), kt, vt, rel_bias)
    b_out, pc, pn, pm = _mlstm_prompt(shp(qkb), shp(vb), shp(ob), shp(gcol), grow, w_conv_b[0], b_conv_b[0],
                                      b_if[0], g_bhead[0])
    ffn_args = (a_out.reshape(mp, A_W), b_out.reshape(mp, B_W), xp, w_out_e[0].astype(BF16), g_ffn[0],
                w1_e[0].astype(BF16), w3_e[0].astype(BF16), w2_e[0].astype(BF16))
    ffn_tm, ffn_tf = 1024, 512
    if ns * WINDOW_HEAD_BLOCKS == (mp // ffn_tm) * WINDOW_STEPS:
        x1p, sa, s_kt, s_vt = _mix_ffn(*ffn_args, tm=ffn_tm, tf=ffn_tf, window=s_attn, rel_bias=rel_bias)
    else:
        x1p = _mix_ffn(*ffn_args, tm=ffn_tm, tf=ffn_tf)
        sa, s_kt, s_vt = _sample_window_attention(*s_attn, rel_bias)
    sa = sa.reshape(ns, A_W)
    s_k, s_v = pos_major(s_kt), pos_major(s_vt)

    sb, sc, sn, sm, s_conv = _sample_mlstm_step(
        s_proj[3], s_proj[4], s_proj[5], s_proj[6], state_b_c[0], state_b_n[0], state_b_m[0], state_b_conv[0],
        b_if[0], w_conv_b[0], b_conv_b[0], g_bhead[0])
    x1s = _mix_ffn(sa, sb, xs, w_out_e[0], g_ffn[0], w1_e[0], w3_e[0], w2_e[0], tm=ns, tf=512, precise=True)

    ab_re, ab_im, bb_re, bb_im = _s5_discretize(a_re[0], a_im[0], log_dt[0], bm_re[0], bm_im[0], cm_re[0],
                                                cm_im[0])
    bd, cmat = _s5_matrices(bb_re, bb_im, cm_re[0], cm_im[0])
    c_out, xd = _odd_in(x1p, g_mix[1], w_in_o[0].astype(BF16), g_cv[0], b_cv[0], w_s[0], b_s[0], tm=1024)
    d_out, p_hr, p_hi = _s5_prompt(xd.reshape(bp, sp, D_W), bd.astype(BF16), cmat.astype(BF16), ab_re, ab_im,
                                   d_skip[0], w_glu[0].astype(BF16), b_glu[0])
    wr = jnp.pad(w_router[0].astype(F32), ((0, 0), (0, 128 - N_EXPERTS)))
    br = jnp.pad(b_router[0].astype(F32), (0, 128 - N_EXPERTS)).reshape(1, 128)
    x2p, hnp, idp, gtp, rkp, cbp, cap = _mix_router(c_out, d_out.reshape(mp, D_W), x1p, w_out_o[0].astype(BF16),
                                                    g_ffn[1], wr, br, jnp.zeros((1, 128), F32), tm=1024)

    sc_out, sd_out, s_cv, s_hr, s_hi = _sample_odd_mixers(
        x1s, state_d_re[0], state_d_im[0], g_mix[1], w_in_o[0], g_cv[0], b_cv[0], w_s[0], b_s[0], ab_re, ab_im,
        bb_re, bb_im, cm_re[0], cm_im[0], d_skip[0], w_glu[0], b_glu[0])
    x2s, hns, ids_s, gts, rks, cbs, cas = _mix_router(sc_out, sd_out, x1s, w_out_o[0], g_ffn[1], wr, br, cap,
                                                      tm=ns, precise=True)

    pad = ((0, ROUTE_TILE - ns), (0, 0))
    cnt_tile = jnp.concatenate([cbp.reshape(-1, 128), cbs.reshape(-1, 128), cas])[:, :N_EXPERTS].astype(I32)
    y_p, y_s = _moe_block(x2p, hnp, idp, gtp, rkp, jnp.pad(x2s, pad), jnp.pad(hns, pad),
                          jnp.pad(ids_s, pad, constant_values=-1), jnp.pad(gts, pad), jnp.pad(rks, pad),
                          cnt_tile, w1_m[0], w3_m[0], w2_m[0], g_final)
    y_prompt = y_p.reshape(bp, sp, d)
    y_sample = y_s[:ns].reshape(ns, 1, d)

    e = lambda a: a[None]
    p_k = pos_major(kt.reshape(bp, A_HEADS, A_HD, sp))
    p_v = pos_major(vt.reshape(bp, A_HEADS, A_HD, sp))
    p_conv = qkb.reshape(bp, sp, 2 * B_W)[:, sp - (B_CONV - 1):]
    return (y_prompt, y_sample, e(p_k), e(p_v), e(s_k), e(s_v), e(pc), e(sc), e(pn[:, :B_HEADS]), e(sn),
            e(pm[:, :B_HEADS, 0]), e(sm), e(p_conv), e(s_conv), e(s_cv.reshape(ns, 1, C_W)),
            e(p_hr.reshape(bp, D_GROUPS, D_STATE)), e(s_hr), e(p_hi.reshape(bp, D_GROUPS, D_STATE)), e(s_hi))
```
